```python
import jax, jax.numpy as jnp
from jax import lax
import numpy as np

D_MODEL = 1024
BATCH = 16
SEQ = 2048
DEPTH = 2

GRID_W = 64
CTX_LEN = 256

HEAD_DIM = 64
A_HEADS = 8
A_KV_HEADS = 2
A_GROUP = A_HEADS // A_KV_HEADS
A_WINDOW = 128
BLOCK = 128
B_CH = 256
B_WIDTH = 31
C_HEADS = 4
C_Q_RANK = 384
C_KV_RANK = 256
C_NOPE = 64
C_ROPE = 32
C_V = 64

A_Q = A_HEADS * HEAD_DIM
A_KV = A_KV_HEADS * HEAD_DIM
MIX_WIDTH = A_Q + B_CH + C_HEADS * C_V
IN_SPLITS = (A_Q, A_Q + A_KV, A_Q + 2 * A_KV, A_Q + 2 * A_KV + 2 * B_CH,
             A_Q + 2 * A_KV + 2 * B_CH + C_Q_RANK, A_Q + 2 * A_KV + 2 * B_CH + C_Q_RANK + C_KV_RANK)
IN_COLS = IN_SPLITS[-1] + C_ROPE

D_FF = ((8 * D_MODEL // 3 + 127) // 128) * 128
N_EXPERTS = 8
TOP_K = 2

ROPE_BASE = 10000.0
EPS = 1e-6
NEG = -1e30
A_SCALE = HEAD_DIM ** -0.5
MLA_SCALE = (C_NOPE + C_ROPE) ** -0.5

kernel_name = "hybrid_parallel_heads_dit_block"


def rms_norm(x, g):
    xf = x.astype(jnp.float32)
    y = xf * lax.rsqrt(jnp.mean(xf * xf, axis=-1, keepdims=True) + EPS)
    return y.astype(x.dtype) * g


def layer_norm(x, g, b):
    xf = x.astype(jnp.float32)
    mu = jnp.mean(xf, axis=-1, keepdims=True)
    var = jnp.mean(jnp.square(xf - mu), axis=-1, keepdims=True)
    return ((xf - mu) * lax.rsqrt(var + EPS)).astype(x.dtype) * g + b


def modulate(h, shift, scale):
    return h * (1 + scale) + shift


def axial_tables(rows, cols, rot_dim, dtype):
    a = rot_dim // 2
    inv = 1.0 / (ROPE_BASE ** (jnp.arange(0, a, 2, dtype=jnp.float32) / a))
    ar = rows.astype(jnp.float32)[:, None] * inv
    ac = cols.astype(jnp.float32)[:, None] * inv
    return (jnp.cos(ar).astype(dtype), jnp.sin(ar).astype(dtype),
            jnp.cos(ac).astype(dtype), jnp.sin(ac).astype(dtype))


def rope_half(x, cos, sin):
    d2 = x.shape[-1] // 2
    x1, x2 = x[..., :d2], x[..., d2:]
    cs, sn = cos[:, None, :], sin[:, None, :]
    return jnp.concatenate([x1 * cs - x2 * sn, x1 * sn + x2 * cs], axis=-1)


def rope_2d(x, tab):
    cr, sr, cc, sc = tab
    a = x.shape[-1] // 2
    return jnp.concatenate([rope_half(x[..., :a], cr, sr), rope_half(x[..., a:], cc, sc)], axis=-1)


def softmax_with_sink(s, sink):
    m = jnp.maximum(jnp.max(s, axis=-1, keepdims=True), sink)
    e = jnp.exp(s - m)
    return e / (jnp.sum(e, axis=-1, keepdims=True) + jnp.exp(sink - m))


def mixer_inputs(p, a_qg, a_kg, c_qrank_g, c_kvrank_g, w_uq, w_ukv, c_qn_g, c_kn_g, c_qr_g, c_kr_g):
    bsz, n = p.shape[:2]
    qa, ka, va, ub, cq, ckv, kr = jnp.split(p, IN_SPLITS, axis=-1)
    qa = rms_norm(qa.reshape(bsz, n, A_HEADS, HEAD_DIM), a_qg)
    ka = rms_norm(ka.reshape(bsz, n, A_KV_HEADS, HEAD_DIM), a_kg)
    va = va.reshape(bsz, n, A_KV_HEADS, HEAD_DIM)
    q_c = (rms_norm(cq, c_qrank_g) @ w_uq).reshape(bsz, n, C_HEADS, C_NOPE + C_ROPE)
    kv_c = (rms_norm(ckv, c_kvrank_g) @ w_ukv).reshape(bsz, n, C_HEADS, C_NOPE + C_V)
    qn = rms_norm(q_c[..., :C_NOPE], c_qn_g)
    qr = rms_norm(q_c[..., C_NOPE:], c_qr_g)
    kn = rms_norm(kv_c[..., :C_NOPE], c_kn_g)
    vc = kv_c[..., C_NOPE:]
    kr = rms_norm(kr, c_kr_g)
    return qa, ka, va, ub, qn, qr, kn, kr, vc


def window_attention(q, k, v, k_ctx, v_ctx, sink):
    bsz, t = q.shape[:2]
    qg = q.reshape(bsz, t, A_KV_HEADS, A_GROUP, HEAD_DIM)
    pad = ((0, 0), (BLOCK, BLOCK), (0, 0), (0, 0))
    kp, vp = jnp.pad(k, pad), jnp.pad(v, pad)
    sink_b = sink.astype(jnp.float32).reshape(A_KV_HEADS, A_GROUP)[None, :, :, None, None]

    def block(n):
        st = n * BLOCK
        qb = lax.dynamic_slice_in_dim(qg, st, BLOCK, axis=1)
        kb = lax.dynamic_slice_in_dim(kp, st, 3 * BLOCK, axis=1)
        vb = lax.dynamic_slice_in_dim(vp, st, 3 * BLOCK, axis=1)
        s_loc = jnp.einsum('bqhgd,bkhd->bhgqk', qb, kb).astype(jnp.float32) * A_SCALE
        qpos = st + jnp.arange(BLOCK)
        kpos = st - BLOCK + jnp.arange(3 * BLOCK)
        valid = ((kpos[None, :] >= 0) & (kpos[None, :] < t)
                 & (jnp.abs(qpos[:, None] - kpos[None, :]) <= A_WINDOW))
        s_loc = jnp.where(valid, s_loc, NEG)
        s_ctx = jnp.einsum('bqhgd,blhd->bhgql', qb, k_ctx).astype(jnp.float32) * A_SCALE
        p = softmax_with_sink(jnp.concatenate([s_loc, s_ctx], axis=-1), sink_b).astype(v.dtype)
        return (jnp.einsum('bhgqk,bkhd->bqhgd', p[..., :3 * BLOCK], vb)
                + jnp.einsum('bhgql,blhd->bqhgd', p[..., 3 * BLOCK:], v_ctx))

    o = lax.map(block, jnp.arange(t // BLOCK))
    return o.transpose(1, 0, 2, 3, 4, 5).reshape(bsz, t, A_Q)


def ctx_gqa(q, k, v, sink):
    bsz, n = q.shape[:2]
    qg = q.reshape(bsz, n, A_KV_HEADS, A_GROUP, HEAD_DIM)
    sink_b = sink.astype(jnp.float32).reshape(A_KV_HEADS, A_GROUP)[None, :, :, None, None]
    s = jnp.einsum('bqhgd,bkhd->bhgqk', qg, k).astype(jnp.float32) * A_SCALE
    p = softmax_with_sink(s, sink_b).astype(v.dtype)
    return jnp.einsum('bhgqk,bkhd->bqhgd', p, v).reshape(bsz, n, A_Q)


def conv_module(u2, w, b, ln_g, ln_b):
    a, g = jnp.split(u2, 2, axis=-1)
    u = a * jax.nn.sigmoid(g)
    y = lax.conv_general_dilated(u, w[:, None, :].astype(u.dtype), (1,), [(B_WIDTH // 2, B_WIDTH // 2)],
                                 dimension_numbers=('NWC', 'WIO', 'NWC'), feature_group_count=B_CH) + b
    return jax.nn.silu(layer_norm(y, ln_g, ln_b))


def mla_core(qn, qr, kn, kr, v):
    s = (jnp.einsum('bqhd,bkhd->bhqk', qn, kn)
         + jnp.einsum('bqhd,bkd->bhqk', qr, kr)).astype(jnp.float32) * MLA_SCALE
    p = jax.nn.softmax(s, axis=-1).astype(v.dtype)
    o = jnp.einsum('bhqk,bkhd->bqhd', p, v)
    return o.reshape(o.shape[0], o.shape[1], C_HEADS * C_V)


def mla_blocks(qn, qr, kn, kr, v):
    bsz, t = qn.shape[:2]

    def block(n):
        st = n * BLOCK
        return mla_core(lax.dynamic_slice_in_dim(qn, st, BLOCK, axis=1),
                        lax.dynamic_slice_in_dim(qr, st, BLOCK, axis=1), kn, kr, v)

    o = lax.map(block, jnp.arange(t // BLOCK))
    return o.transpose(1, 0, 2, 3).reshape(bsz, t, C_HEADS * C_V)


def swiglu(h, w1, w3, w2):
    return (jax.nn.silu(h @ w1) * (h @ w3)) @ w2


def moe_ffn(h, wr, br, w1, w3, w2):
    logits = (h @ wr).astype(jnp.float32) + br.astype(jnp.float32)
    top_v, top_i = lax.top_k(logits, TOP_K)
    top_w = jax.nn.softmax(top_v, axis=-1)
    gates = jnp.sum(jax.nn.one_hot(top_i, N_EXPERTS, dtype=jnp.float32) * top_w[..., None], axis=-2).astype(h.dtype)
    y = jnp.zeros_like(h)
    for e in range(N_EXPERTS):
        y = y + gates[..., e:e + 1] * swiglu(h, w1[e], w3[e], w2[e])
    return y


def setup_inputs(seed: int = 0) -> dict:
    key = jax.random.key(seed)
    ks = iter(jax.random.split(key, 40))

    def nrm(shape, scale):
        return jax.random.normal(next(ks), shape, jnp.float32) * scale

    def gain(shape):
        return 1.0 + nrm(shape, 0.05)

    D = D_MODEL
    n_dense = (DEPTH + 1) // 2
    n_moe = DEPTH // 2
    return {
        "x": nrm((BATCH, SEQ, D), 1.0),
        "c": nrm((BATCH, D), 1.0),
        "ctx": nrm((BATCH, CTX_LEN, D), 1.0),
        "c_ctx": nrm((D,), 1.0),
        "ada_w": nrm((DEPTH, D, 6 * D), 0.5 * D ** -0.5),
        "ada_b": nrm((DEPTH, 6 * D), 0.02),
        "mix_norm_g": gain((DEPTH, D)),
        "ffn_norm_g": gain((DEPTH, D)),
        "w_in": nrm((DEPTH, D, IN_COLS), D ** -0.5),
        "w_out": nrm((DEPTH, MIX_WIDTH, D), MIX_WIDTH ** -0.5),
        "a_q_norm_g": gain((DEPTH, HEAD_DIM)),
        "a_k_norm_g": gain((DEPTH, HEAD_DIM)),
        "a_sink": nrm((DEPTH, A_HEADS), 1.0),
        "b_conv_w": nrm((DEPTH, B_WIDTH, B_CH), B_WIDTH ** -0.5),
        "b_conv_b": nrm((DEPTH, B_CH), 0.02),
        "b_ln_g": gain((DEPTH, B_CH)),
        "b_ln_b": nrm((DEPTH, B_CH), 0.02),
        "c_q_rank_norm_g": gain((DEPTH, C_Q_RANK)),
        "c_kv_rank_norm_g": gain((DEPTH, C_KV_RANK)),
        "c_w_uq": nrm((DEPTH, C_Q_RANK, C_HEADS * (C_NOPE + C_ROPE)), C_Q_RANK ** -0.5),
        "c_w_ukv": nrm((DEPTH, C_KV_RANK, C_HEADS * (C_NOPE + C_V)), C_KV_RANK ** -0.5),
        "c_q_nope_norm_g": gain((DEPTH, C_NOPE)),
        "c_k_nope_norm_g": gain((DEPTH, C_NOPE)),
        "c_q_rope_norm_g": gain((DEPTH, C_ROPE)),
        "c_k_rope_norm_g": gain((DEPTH, C_ROPE)),
        "dense_w1": nrm((n_dense, D, D_FF), D ** -0.5),
        "dense_w3": nrm((n_dense, D, D_FF), D ** -0.5),
        "dense_w2": nrm((n_dense, D_FF, D), D_FF ** -0.5),
        "moe_router_w": nrm((n_moe, D, N_EXPERTS), D ** -0.5),
        "moe_router_b": nrm((n_moe, N_EXPERTS), 0.01),
        "moe_w1": nrm((n_moe, N_EXPERTS, D, D_FF), D ** -0.5),
        "moe_w3": nrm((n_moe, N_EXPERTS, D, D_FF), D ** -0.5),
        "moe_w2": nrm((n_moe, N_EXPERTS, D_FF, D), D_FF ** -0.5),
    }


def reference(x, c, ctx, c_ctx, ada_w, ada_b, mix_norm_g, ffn_norm_g, w_in, w_out,
              a_q_norm_g, a_k_norm_g, a_sink, b_conv_w, b_conv_b, b_ln_g, b_ln_b,
              c_q_rank_norm_g, c_kv_rank_norm_g, c_w_uq, c_w_ukv,
              c_q_nope_norm_g, c_k_nope_norm_g, c_q_rope_norm_g, c_k_rope_norm_g,
              dense_w1, dense_w3, dense_w2, moe_router_w, moe_router_b, moe_w1, moe_w3, moe_w2):
    bsz, t = x.shape[:2]
    ROWS = t // GRID_W
    rows = jnp.repeat(jnp.arange(ROWS), GRID_W)
    cols = jnp.arange(ROWS * GRID_W) % GRID_W
    tab_a = axial_tables(rows, cols, HEAD_DIM, x.dtype)
    tab_c = axial_tables(rows, cols, C_ROPE, x.dtype)
    c_act = jax.nn.silu(c)
    cc_act = jax.nn.silu(c_ctx)

    for i in range(DEPTH):
        last = i == DEPTH - 1
        mod_x = (c_act @ ada_w[i] + ada_b[i]).reshape(bsz, 6, 1, D_MODEL)
        mod_c = (cc_act @ ada_w[i] + ada_b[i]).reshape(6, 1, 1, D_MODEL)

        hx = modulate(rms_norm(x, mix_norm_g[i]), mod_x[:, 0], mod_x[:, 1])
        hc = modulate(rms_norm(ctx, mix_norm_g[i]), mod_c[0], mod_c[1])
        gparams = (a_q_norm_g[i], a_k_norm_g[i], c_q_rank_norm_g[i], c_kv_rank_norm_g[i], c_w_uq[i], c_w_ukv[i],
                   c_q_nope_norm_g[i], c_k_nope_norm_g[i], c_q_rope_norm_g[i], c_k_rope_norm_g[i])
        qa_x, ka_x, va_x, ub_x, qn_x, qr_x, kn_x, kr_x, vc_x = mixer_inputs(hx @ w_in[i], *gparams)
        qa_c, ka_c, va_c, ub_c, qn_c, qr_c, kn_c, kr_c, vc_c = mixer_inputs(hc @ w_in[i], *gparams)

        qa_x = rope_2d(qa_x, tab_a)
        ka_x = rope_2d(ka_x, tab_a)
        qr_x = rope_2d(qr_x, tab_c)
        kr_x = rope_2d(kr_x[:, :, None, :], tab_c)[:, :, 0, :]

        o_a = window_attention(qa_x, ka_x, va_x, ka_c, va_c, a_sink[i])
        o_b = conv_module(ub_x, b_conv_w[i], b_conv_b[i], b_ln_g[i], b_ln_b[i])
        o_c = mla_blocks(qn_x, qr_x, jnp.concatenate([kn_x, kn_c], axis=1),
                         jnp.concatenate([kr_x, kr_c], axis=1),
                         jnp.concatenate([vc_x, vc_c], axis=1))
        x = x + mod_x[:, 2] * (jnp.concatenate([o_a, o_b, o_c], axis=-1) @ w_out[i])
        if not last:
            oc_a = ctx_gqa(qa_c, ka_c, va_c, a_sink[i])
            oc_b = conv_module(ub_c, b_conv_w[i], b_conv_b[i], b_ln_g[i], b_ln_b[i])
            oc_c = mla_core(qn_c, qr_c, kn_c, kr_c, vc_c)
            ctx = ctx + mod_c[2] * (jnp.concatenate([oc_a, oc_b, oc_c], axis=-1) @ w_out[i])

        j = i // 2
        if i % 2 == 0:
            ffn = lambda h, j=j: swiglu(h, dense_w1[j], dense_w3[j], dense_w2[j])
        else:
            ffn = lambda h, j=j: moe_ffn(h, moe_router_w[j], moe_router_b[j], moe_w1[j], moe_w3[j], moe_w2[j])
        x = x + mod_x[:, 5] * ffn(modulate(rms_norm(x, ffn_norm_g[i]), mod_x[:, 3], mod_x[:, 4]))
        if not last:
            ctx = ctx + mod_c[5] * ffn(modulate(rms_norm(ctx, ffn_norm_g[i]), mod_c[3], mod_c[4]))

    return x
```

```python
import functools

import jax
import jax.numpy as jnp
import numpy as np
from jax import lax
from jax.experimental import pallas as pl
from jax.experimental.pallas import tpu as pltpu

F32 = jnp.float32
BF16 = jnp.bfloat16

D_MODEL = 1024
GRID_W = 64
HEAD_DIM = 64
A_HEADS = 8
A_KV_HEADS = 2
A_WINDOW = 128
BLOCK = 128
B_CH = 256
B_WIDTH = 31
C_HEADS = 4
C_Q_RANK = 384
C_KV_RANK = 256
C_NOPE = 64
C_ROPE = 32
C_V = 64
A_Q = A_HEADS * HEAD_DIM
A_KV = A_KV_HEADS * HEAD_DIM
IN_COLS_PAD = 2048
D_FF = 2816
N_EXPERTS = 8
ROPE_BASE = 10000.0
EPS = 1e-6
NEG = -1e30
A_SCALE = HEAD_DIM ** -0.5
MLA_SCALE = (C_NOPE + C_ROPE) ** -0.5

LANES = 128
MOD_ROWS = 8
C_SLOT = 128
VMEM_LIMIT = 56 * 1024 * 1024


def _cparams(sem):
    return pltpu.CompilerParams(dimension_semantics=sem, vmem_limit_bytes=VMEM_LIMIT)


def _dot(a, b):
    return jnp.dot(a, b, preferred_element_type=F32)


def _dot_nt(a, b):
    return lax.dot_general(a, b, (((1,), (1,)), ((), ())), preferred_element_type=F32)


def _ada_kernel(c_ref, w_ref, b_ref, o_ref):
    c = c_ref[...]
    a = c * jax.nn.sigmoid(c)
    o_ref[0] = _dot(a.astype(BF16), w_ref[0].astype(BF16)) + b_ref[0]


def _ada_call(c_pad, ada_w, ada_b):
    depth, d, n6 = ada_w.shape
    rows = c_pad.shape[0]
    tn = 1536
    return pl.pallas_call(
        _ada_kernel,
        grid=(depth, n6 // tn),
        in_specs=[
            pl.BlockSpec((rows, d), lambda i, j: (0, 0)),
            pl.BlockSpec((1, d, tn), lambda i, j: (i, 0, j)),
            pl.BlockSpec((1, 1, tn), lambda i, j: (i, 0, j)),
        ],
        out_specs=pl.BlockSpec((1, rows, tn), lambda i, j: (i, 0, j)),
        out_shape=jax.ShapeDtypeStruct((depth, rows, n6), F32),
        compiler_params=_cparams(("parallel", "parallel")),
        name="ada_proj",
    )(c_pad, ada_w, ada_b.reshape(depth, 1, n6))


def _rope_chunk(c, cos, sin, half):
    lane = lax.broadcasted_iota(jnp.int32, c.shape, 1)
    lo = (lane & (2 * half - 1)) < half
    partner = jnp.where(lo, pltpu.roll(c, LANES - half, 1), pltpu.roll(c, half, 1))
    return c * cos + partner * sin


def _pre_kernel(x_ref, mod_ref, gmix_ref, win_ref, cosa_ref, sina_ref, cosc_ref, sinc_ref,
                sa_ref, sc_ref, gq_ref, gk_ref, gcq_ref, gckv_ref, wuq_ref, wukvk_ref, wukvv_ref,
                gqc_ref, gkn_ref, gkr_ref,
                qa_ref, ka_ref, va_ref, u_ref, qc_ref, kc_ref, vc_ref):
    x = x_ref[...]
    ms = jnp.mean(x * x, axis=-1, keepdims=True)
    y = x * lax.rsqrt(ms + EPS)
    shift = mod_ref[0, 0:1, :]
    scale = mod_ref[0, 1:2, :]
    h = (y * gmix_ref[...]) * (1.0 + scale) + shift
    p = _dot(h.astype(BF16), win_ref[...])

    cosa, sina = cosa_ref[...], sina_ref[...]
    cosc, sinc = cosc_ref[...], sinc_ref[...]

    qa = p[:, 0:A_Q]
    ssq = _dot((qa * qa).astype(BF16), sa_ref[...])
    qa = qa * lax.rsqrt(ssq + EPS) * gq_ref[...]
    for j in range(A_Q // LANES):
        sl = slice(j * LANES, (j + 1) * LANES)
        qa_ref[:, sl] = _rope_chunk(qa[:, sl], cosa, sina, HEAD_DIM // 4).astype(BF16)

    ka = p[:, A_Q:A_Q + A_KV]
    ssk = _dot((ka * ka).astype(BF16), sa_ref[0:A_KV, 0:A_KV])
    ka = ka * lax.rsqrt(ssk + EPS) * gk_ref[...]
    ka_ref[...] = _rope_chunk(ka, cosa, sina, HEAD_DIM // 4).astype(BF16)
    va_ref[...] = p[:, A_Q + A_KV:A_Q + 2 * A_KV].astype(BF16)

    o = A_Q + 2 * A_KV
    u_ref[...] = p[:, o:o + B_CH] * jax.nn.sigmoid(p[:, o + B_CH:o + 2 * B_CH])

    o = o + 2 * B_CH
    cq = p[:, o:o + C_Q_RANK]
    cq = cq * lax.rsqrt(jnp.mean(cq * cq, axis=-1, keepdims=True) + EPS) * gcq_ref[...]
    qc = _dot(cq.astype(BF16), wuq_ref[...])
    ssq = _dot((qc * qc).astype(BF16), sc_ref[...])
    qc = qc * lax.rsqrt(ssq + EPS) * gqc_ref[...]
    for j in range(C_HEADS):
        sl = slice(j * C_SLOT, (j + 1) * C_SLOT)
        qc_ref[:, sl] = _rope_chunk(qc[:, sl], cosc, sinc, C_ROPE // 4).astype(BF16)

    o = o + C_Q_RANK
    ckv = p[:, o:o + C_KV_RANK]
    ckv = (ckv * lax.rsqrt(jnp.mean(ckv * ckv, axis=-1, keepdims=True) + EPS) * gckv_ref[...]).astype(BF16)
    kn = _dot(ckv, wukvk_ref[...])
    vc_ref[...] = _dot(ckv, wukvv_ref[...]).astype(BF16)
    ssk = _dot((kn * kn).astype(BF16), sc_ref[...])
    kn = kn * lax.rsqrt(ssk + EPS) * gkn_ref[...]
    o = o + C_KV_RANK
    kr = p[:, o:o + C_SLOT]
    kr = kr * lax.rsqrt(jnp.sum(kr * kr, axis=-1, keepdims=True) * (1.0 / C_ROPE) + EPS) * gkr_ref[...]
    kr = _rope_chunk(kr, cosc, sinc, C_ROPE // 4)
    for j in range(C_HEADS):
        sl = slice(j * C_SLOT, (j + 1) * C_SLOT)
        kc_ref[:, sl] = (kn[:, sl] + kr).astype(BF16)


def _pre_call(x2d, mod, tiles_per_mod, tab_tiles, consts, tm):
    n, d = x2d.shape
    (gmix, win, cosa, sina, cosc, sinc, sa, sc, gq, gk, gcq, gckv, wuq, wukvk, wukvv, gqc, gkn, gkr) = consts

    def const(a):
        return pl.BlockSpec(a.shape, lambda i: (0,) * a.ndim)

    def tab(a):
        return pl.BlockSpec((tm, LANES), lambda i: (i % tab_tiles, 0))

    in_specs = [
        pl.BlockSpec((tm, d), lambda i: (i, 0)),
        pl.BlockSpec((1, MOD_ROWS, d), lambda i: (i // tiles_per_mod, 0, 0)),
        const(gmix), const(win), tab(cosa), tab(sina), tab(cosc), tab(sinc),
        const(sa), const(sc), const(gq), const(gk), const(gcq), const(gckv),
        const(wuq), const(wukvk), const(wukvv), const(gqc), const(gkn), const(gkr),
    ]
    widths = (A_Q, A_KV, A_KV, B_CH, C_HEADS * C_SLOT, C_HEADS * C_SLOT, C_HEADS * C_V)
    dtypes = (BF16, BF16, BF16, F32, BF16, BF16, BF16)
    return pl.pallas_call(
        _pre_kernel,
        grid=(n // tm,),
        in_specs=in_specs,
        out_specs=[pl.BlockSpec((tm, w), lambda i: (i, 0)) for w in widths],
        out_shape=[jax.ShapeDtypeStruct((n, w), dt) for w, dt in zip(widths, dtypes)],
        compiler_params=_cparams(("parallel",)),
        name="pre_attn",
    )(x2d, mod, gmix, win, cosa, sina, cosc, sinc, sa, sc, gq, gk, gcq, gckv, wuq, wukvk, wukvv,
      gqc, gkn, gkr)


def _attn_a_kernel(*refs, t, has_local):
    if has_local:
        q_ref, k_ref, v_ref, kc_ref, vc_ref, sink_ref, o_ref = refs
    else:
        q_ref, kc_ref, vc_ref, sink_ref, o_ref = refs
    nchunk = A_Q // LANES
    rows = 2 * nchunk * BLOCK
    q = q_ref[...]
    lane = lax.broadcasted_iota(jnp.int32, (BLOCK, LANES), 1)
    zero = jnp.zeros((BLOCK, LANES), BF16)
    parts = [jnp.where(lane < HEAD_DIM, q[:, j * LANES:(j + 1) * LANES], zero) for j in range(nchunk)]
    parts += [jnp.where(lane >= HEAD_DIM, q[:, j * LANES:(j + 1) * LANES], zero) for j in range(nchunk)]
    qs = jnp.concatenate(parts, axis=0)

    sink = sink_ref[:, 0:1]
    kc = kc_ref[0]
    s_ctx = _dot_nt(qs, kc)
    m = jnp.maximum(jnp.max(s_ctx, axis=-1, keepdims=True), sink)
    if has_local:
        n = pl.program_id(1)
        span = 3 * BLOCK
        start = pl.multiple_of(jnp.clip((n - 1) * BLOCK, 0, t - span), BLOCK)
        kl = k_ref[0, pl.ds(start, span), :]
        vl = v_ref[0, pl.ds(start, span), :]
        s_loc = _dot_nt(qs, kl)
        qpos = n * BLOCK + (lax.broadcasted_iota(jnp.int32, (rows, span), 0) & (BLOCK - 1))
        kpos = start + lax.broadcasted_iota(jnp.int32, (rows, span), 1)
        s_loc = jnp.where(jnp.abs(qpos - kpos) <= A_WINDOW, s_loc, NEG)
        m = jnp.maximum(m, jnp.max(s_loc, axis=-1, keepdims=True))
    e_ctx = jnp.exp(s_ctx - m)
    den = jnp.sum(e_ctx, axis=-1, keepdims=True) + jnp.exp(sink - m)
    acc = _dot(e_ctx.astype(BF16), vc_ref[0])
    if has_local:
        e_loc = jnp.exp(s_loc - m)
        den = den + jnp.sum(e_loc, axis=-1, keepdims=True)
        acc = acc + _dot(e_loc.astype(BF16), vl)
    acc = acc / den
    half = nchunk * BLOCK
    for j in range(nchunk):
        o_ref[:, j * LANES:(j + 1) * LANES] = jnp.where(
            lane < HEAD_DIM, acc[j * BLOCK:(j + 1) * BLOCK], acc[half + j * BLOCK:half + (j + 1) * BLOCK]
        ).astype(BF16)


def _attn_a_call(qa, ka, va, kac, vac, sink_rows, bsz, t):
    has_local = ka is not None
    nb = t // BLOCK
    ctx_len = kac.shape[1]
    in_specs = [pl.BlockSpec((BLOCK, A_Q), lambda b, n: (b * nb + n, 0))]
    args = [qa]
    if has_local:
        in_specs += [pl.BlockSpec((1, t, A_KV), lambda b, n: (b, 0, 0))] * 2
        args += [ka, va]
    in_specs += [pl.BlockSpec((1, ctx_len, A_KV), lambda b, n: (b, 0, 0))] * 2
    in_specs += [pl.BlockSpec(sink_rows.shape, lambda b, n: (0, 0))]
    args += [kac, vac, sink_rows]
    return pl.pallas_call(
        functools.partial(_attn_a_kernel, t=t, has_local=has_local),
        grid=(bsz, nb),
        in_specs=in_specs,
        out_specs=pl.BlockSpec((BLOCK, A_Q), lambda b, n: (b * nb + n, 0)),
        out_shape=jax.ShapeDtypeStruct((bsz * t, A_Q), BF16),
        compiler_params=_cparams(("parallel", "parallel")),
        name="attn_a_local" if has_local else "attn_a_ctx",
    )(*args)


CONV_PAD = 16
CONV_CHUNK = 128


def _conv_kernel(u_ref, w_ref, b_ref, g_ref, beta_ref, o_ref, pad_ref, *, t):
    zeros = jnp.zeros((CONV_PAD, B_CH), F32)
    pad_ref[0:CONV_PAD, :] = zeros
    pad_ref[CONV_PAD + t:CONV_PAD + t + CONV_PAD, :] = zeros
    pad_ref[CONV_PAD:CONV_PAD + t, :] = u_ref[0]
    off = CONV_PAD - B_WIDTH // 2
    for c in range(t // CONV_CHUNK):
        base = c * CONV_CHUNK + off
        acc = jnp.zeros((CONV_CHUNK, B_CH), F32)
        for k in range(B_WIDTH):
            acc = acc + pad_ref[base + k:base + k + CONV_CHUNK, :] * w_ref[k:k + 1, :]
        y = acc + b_ref[...]
        mu = jnp.mean(y, axis=-1, keepdims=True)
        yc = y - mu
        var = jnp.mean(yc * yc, axis=-1, keepdims=True)
        z = yc * lax.rsqrt(var + EPS) * g_ref[...] + beta_ref[...]
        o_ref[0, c * CONV_CHUNK:(c + 1) * CONV_CHUNK, :] = (z * jax.nn.sigmoid(z)).astype(BF16)


def _conv_call(u3, w_pad, b, g, beta):
    bsz, t, ch = u3.shape

    def const(a):
        return pl.BlockSpec(a.shape, lambda i: (0, 0))

    return pl.pallas_call(
        functools.partial(_conv_kernel, t=t),
        grid=(bsz,),
        in_specs=[pl.BlockSpec((1, t, ch), lambda i: (i, 0, 0)), const(w_pad), const(b), const(g), const(beta)],
        out_specs=pl.BlockSpec((1, t, ch), lambda i: (i, 0, 0)),
        out_shape=jax.ShapeDtypeStruct((bsz, t, ch), BF16),
        scratch_shapes=[pltpu.VMEM((t + 2 * CONV_PAD, ch), F32)],
        compiler_params=_cparams(("parallel",)),
        name="conv_module",
    )(u3, w_pad, b, g, beta)


def _mla_kernel(*refs, has_local):
    if has_local:
        q_ref, kx_ref, vx_ref, kc_ref, vc_ref, o_ref = refs
    else:
        q_ref, kc_ref, vc_ref, o_ref = refs
    tq = q_ref.shape[0]
    width = C_HEADS * C_V
    lane = lax.broadcasted_iota(jnp.int32, (tq, width), 1)
    out = jnp.zeros((tq, width), F32)
    for h in range(C_HEADS):
        sl = slice(h * C_SLOT, (h + 1) * C_SLOT)
        q = q_ref[:, sl]
        s_c = _dot_nt(q, kc_ref[0, :, sl])
        m = jnp.max(s_c, axis=-1, keepdims=True)
        if has_local:
            s_x = _dot_nt(q, kx_ref[0, :, sl])
            m = jnp.maximum(m, jnp.max(s_x, axis=-1, keepdims=True))
        e_c = jnp.exp(s_c - m)
        den = jnp.sum(e_c, axis=-1, keepdims=True)
        acc = _dot(e_c.astype(BF16), vc_ref[0])
        if has_local:
            e_x = jnp.exp(s_x - m)
            den = den + jnp.sum(e_x, axis=-1, keepdims=True)
            acc = acc + _dot(e_x.astype(BF16), vx_ref[0])
        out = jnp.where((lane >= h * C_V) & (lane < (h + 1) * C_V), acc / den, out)
    o_ref[...] = out.astype(BF16)


def _mla_call(qc, kx, vx, kcc, vcc, bsz, t, tq):
    has_local = kx is not None
    nq = t // tq
    ctx_len = kcc.shape[1]
    wq = C_HEADS * C_SLOT
    wv = C_HEADS * C_V
    in_specs = [pl.BlockSpec((tq, wq), lambda b, n: (b * nq + n, 0))]
    args = [qc]
    if has_local:
        in_specs += [pl.BlockSpec((1, t, wq), lambda b, n: (b, 0, 0)),
                     pl.BlockSpec((1, t, wv), lambda b, n: (b, 0, 0))]
        args += [kx, vx]
    in_specs += [pl.BlockSpec((1, ctx_len, wq), lambda b, n: (b, 0, 0)),
                 pl.BlockSpec((1, ctx_len, wv), lambda b, n: (b, 0, 0))]
    args += [kcc, vcc]
    return pl.pallas_call(
        functools.partial(_mla_kernel, has_local=has_local),
        grid=(bsz, nq),
        in_specs=in_specs,
        out_specs=pl.BlockSpec((tq, wv), lambda b, n: (b * nq + n, 0)),
        out_shape=jax.ShapeDtypeStruct((bsz * t, wv), BF16),
        compiler_params=_cparams(("parallel", "parallel")),
        name="mla_local" if has_local else "mla_ctx",
    )(*args)


def _out_kernel(x_ref, mod_ref, oa_ref, ob_ref, oc_ref, w_ref, o_ref):
    y = _dot(oa_ref[...], w_ref[0:A_Q, :])
    y = y + _dot(ob_ref[...], w_ref[A_Q:A_Q + B_CH, :])
    y = y + _dot(oc_ref[...], w_ref[A_Q + B_CH:, :])
    o_ref[...] = x_ref[...] + mod_ref[0, 2:3, :] * y


def _out_call(x2d, mod, tiles_per_mod, oa, ob, oc, w_out, tm):
    n, d = x2d.shape
    return pl.pallas_call(
        _out_kernel,
        grid=(n // tm,),
        in_specs=[
            pl.BlockSpec((tm, d), lambda i: (i, 0)),
            pl.BlockSpec((1, MOD_ROWS, d), lambda i: (i // tiles_per_mod, 0, 0)),
            pl.BlockSpec((tm, oa.shape[1]), lambda i: (i, 0)),
            pl.BlockSpec((tm, ob.shape[1]), lambda i: (i, 0)),
            pl.BlockSpec((tm, oc.shape[1]), lambda i: (i, 0)),
            pl.BlockSpec(w_out.shape, lambda i: (0, 0)),
        ],
        out_specs=pl.BlockSpec((tm, d), lambda i: (i, 0)),
        out_shape=jax.ShapeDtypeStruct((n, d), F32),
        compiler_params=_cparams(("parallel",)),
        name="out_proj",
    )(x2d, mod, oa, ob, oc, w_out)


def _split_bf16(a):
    hi = a.astype(BF16)
    lo = (a - hi.astype(F32)).astype(BF16)
    return hi, lo


def _ffn_kernel(*refs, moe):
    if moe:
        (x_ref, mod_ref, g_ref, wr_ref, br_ref, w1_ref, w3_ref, w2_ref, o_ref, h_ref, gate_ref, acc_ref) = refs
    else:
        (x_ref, mod_ref, g_ref, w1_ref, w3_ref, w2_ref, o_ref, h_ref, acc_ref) = refs
    e = pl.program_id(1)
    f = pl.program_id(2)

    @pl.when((e == 0) & (f == 0))
    def _():
        x = x_ref[...]
        ms = jnp.mean(x * x, axis=-1, keepdims=True)
        y = x * lax.rsqrt(ms + EPS)
        h = (y * g_ref[...]) * (1.0 + mod_ref[0, 4:5, :]) + mod_ref[0, 3:4, :]
        h_ref[...] = h.astype(BF16)
        acc_ref[...] = jnp.zeros_like(acc_ref)
        if moe:
            h_hi, h_lo = _split_bf16(h)
            w_hi, w_lo = _split_bf16(wr_ref[...])
            logits = _dot(h_hi, w_hi) + (_dot(h_lo, w_hi) + _dot(h_hi, w_lo)) + br_ref[...]
            lane = lax.broadcasted_iota(jnp.int32, logits.shape, 1).astype(F32)
            logits = jnp.where(lane < N_EXPERTS, logits, NEG)
            m1 = jnp.max(logits, axis=-1, keepdims=True)
            i1 = jnp.min(jnp.where(logits == m1, lane, float(LANES)), axis=-1, keepdims=True)
            rest = jnp.where(lane == i1, NEG, logits)
            m2 = jnp.max(rest, axis=-1, keepdims=True)
            i2 = jnp.min(jnp.where(rest == m2, lane, float(LANES)), axis=-1, keepdims=True)
            e2 = jnp.exp(m2 - m1)
            den = 1.0 + e2
            gate_ref[...] = jnp.where(lane == i1, 1.0 / den, 0.0) + jnp.where(lane == i2, e2 / den, 0.0)

    h = h_ref[...]
    a = _dot(h, w1_ref[0])
    b = _dot(h, w3_ref[0])
    g = (a * jax.nn.sigmoid(a) * b).astype(BF16)
    y = _dot(g, w2_ref[0])
    if moe:
        gates = gate_ref[...]
        lane = lax.broadcasted_iota(jnp.int32, gates.shape, 1)
        y = y * jnp.sum(jnp.where(lane == e, gates, 0.0), axis=-1, keepdims=True)
    acc_ref[...] += y

    @pl.when((e == pl.num_programs(1) - 1) & (f == pl.num_programs(2) - 1))
    def _():
        o_ref[...] = x_ref[...] + mod_ref[0, 5:6, :] * acc_ref[...]


def _ffn_call(x2d, mod, tiles_per_mod, g, w1, w3, w2, router, tm, nf):
    n, d = x2d.shape
    ne, _, ff = w1.shape
    tf = ff // nf
    moe = router is not None
    in_specs = [
        pl.BlockSpec((tm, d), lambda i, e, f: (i, 0)),
        pl.BlockSpec((1, MOD_ROWS, d), lambda i, e, f: (i // tiles_per_mod, 0, 0)),
        pl.BlockSpec(g.shape, lambda i, e, f: (0, 0)),
    ]
    args = [x2d, mod, g]
    scratch = [pltpu.VMEM((tm, d), BF16)]
    if moe:
        in_specs += [pl.BlockSpec(router[0].shape, lambda i, e, f: (0, 0)),
                     pl.BlockSpec(router[1].shape, lambda i, e, f: (0, 0))]
        args += list(router)
        scratch += [pltpu.VMEM((tm, LANES), F32)]
    in_specs += [
        pl.BlockSpec((1, d, tf), lambda i, e, f: (e, 0, f)),
        pl.BlockSpec((1, d, tf), lambda i, e, f: (e, 0, f)),
        pl.BlockSpec((1, tf, d), lambda i, e, f: (e, f, 0)),
    ]
    args += [w1, w3, w2]
    scratch += [pltpu.VMEM((tm, d), F32)]
    return pl.pallas_call(
        functools.partial(_ffn_kernel, moe=moe),
        grid=(n // tm, ne, nf),
        in_specs=in_specs,
        out_specs=pl.BlockSpec((tm, d), lambda i, e, f: (i, 0)),
        out_shape=jax.ShapeDtypeStruct((n, d), F32),
        scratch_shapes=scratch,
        compiler_params=_cparams(("parallel", "arbitrary", "arbitrary")),
        name="ffn_moe" if moe else "ffn_dense",
    )(*args)


def _rope_tables(t):
    rows = jnp.arange(t, dtype=F32) // GRID_W
    cols = jnp.arange(t, dtype=F32) % GRID_W

    def tables(rot_dim):
        a = rot_dim // 2
        inv = 1.0 / (ROPE_BASE ** (jnp.arange(0, a, 2, dtype=F32) / a))
        ar = rows[:, None] * inv
        ac = cols[:, None] * inv
        cos = jnp.concatenate([jnp.cos(ar), jnp.cos(ar), jnp.cos(ac), jnp.cos(ac)], axis=-1)
        sin = jnp.concatenate([-jnp.sin(ar), jnp.sin(ar), -jnp.sin(ac), jnp.sin(ac)], axis=-1)
        return cos, sin

    ca, sa = tables(HEAD_DIM)
    cos_a = jnp.tile(ca, (1, LANES // HEAD_DIM))
    sin_a = jnp.tile(sa, (1, LANES // HEAD_DIM))
    cc, sc = tables(C_ROPE)
    ones = jnp.ones((t, C_NOPE), F32)
    tail = C_SLOT - C_NOPE - C_ROPE
    cos_c = jnp.concatenate([ones, cc, jnp.ones((t, tail), F32)], axis=-1)
    sin_c = jnp.concatenate([0 * ones, sc, jnp.zeros((t, tail), F32)], axis=-1)
    return cos_a, sin_a, cos_c, sin_c


def _head_perm():
    order = []
    for j in range(A_HEADS // 2):
        order += [j, A_HEADS // 2 + j]
    return np.concatenate([np.arange(h * HEAD_DIM, (h + 1) * HEAD_DIM) for h in order])


def _segment_mean_matrix(widths, total):
    m = np.zeros((total, total), np.float32)
    o = 0
    while o < total:
        for w, used in widths:
            if used:
                m[o:o + w, o:o + w] = 1.0 / w
            o += w
    return jnp.asarray(m, BF16)


def _slot_vec(nope, rope):
    z = jnp.zeros((C_SLOT - C_NOPE - C_ROPE,), F32)
    n = jnp.zeros((C_NOPE,), F32) if nope is None else nope
    r = jnp.zeros((C_ROPE,), F32) if rope is None else rope
    return jnp.tile(jnp.concatenate([n, r, z]), C_HEADS)[None, :]


def _layer_consts(i, p, perm, tabs):
    w_in = p["w_in"][i]
    o_kr = A_Q + 2 * A_KV + 2 * B_CH + C_Q_RANK + C_KV_RANK
    d = w_in.shape[0]
    kr_cols = jnp.concatenate([jnp.zeros((d, C_NOPE), F32), w_in[:, o_kr:o_kr + C_ROPE],
                               jnp.zeros((d, C_SLOT - C_NOPE - C_ROPE), F32)], axis=1)
    win = jnp.concatenate([w_in[:, :A_Q][:, perm], w_in[:, A_Q:o_kr], kr_cols], axis=1).astype(BF16)

    w_uq = p["c_w_uq"][i].reshape(C_Q_RANK, C_HEADS, C_NOPE + C_ROPE)
    wuq = jnp.pad(w_uq, ((0, 0), (0, 0), (0, C_SLOT - C_NOPE - C_ROPE))).reshape(C_Q_RANK, C_HEADS * C_SLOT)
    w_ukv = p["c_w_ukv"][i].reshape(C_KV_RANK, C_HEADS, C_NOPE + C_V)
    wukvk = jnp.pad(w_ukv[:, :, :C_NOPE], ((0, 0), (0, 0), (0, C_SLOT - C_NOPE))).reshape(C_KV_RANK, -1)
    wukvv = w_ukv[:, :, C_NOPE:].reshape(C_KV_RANK, C_HEADS * C_V)

    sa = _segment_mean_matrix([(HEAD_DIM, True)], A_Q)
    sc = _segment_mean_matrix([(C_NOPE, True), (C_ROPE, True), (C_SLOT - C_NOPE - C_ROPE, False)], C_HEADS * C_SLOT)
    gq = jnp.tile(p["a_q_norm_g"][i] * A_SCALE, A_HEADS)[None, :]
    gk = jnp.tile(p["a_k_norm_g"][i], A_KV_HEADS)[None, :]
    gqc = _slot_vec(p["c_q_nope_norm_g"][i], p["c_q_rope_norm_g"][i]) * MLA_SCALE
    gkn = _slot_vec(p["c_k_nope_norm_g"][i], None)
    gkr = _slot_vec(None, p["c_k_rope_norm_g"][i])[:, :C_SLOT]
    return (p["mix_norm_g"][i][None, :], win) + tabs + (
        sa, sc, gq, gk, p["c_q_rank_norm_g"][i][None, :], p["c_kv_rank_norm_g"][i][None, :],
        wuq.astype(BF16), wukvk.astype(BF16), wukvv.astype(BF16), gqc, gkn, gkr)


def kernel(x, c, ctx, c_ctx, ada_w, ada_b, mix_norm_g, ffn_norm_g, w_in, w_out, a_q_norm_g, a_k_norm_g, a_sink, b_conv_w, b_conv_b, b_ln_g, b_ln_b, c_q_rank_norm_g, c_kv_rank_norm_g, c_w_uq, c_w_ukv, c_q_nope_norm_g, c_k_nope_norm_g, c_q_rope_norm_g, c_k_rope_norm_g, dense_w1, dense_w3, dense_w2, moe_router_w, moe_router_b, moe_w1, moe_w3, moe_w2):
    p = dict(w_in=w_in, c_w_uq=c_w_uq, c_w_ukv=c_w_ukv, a_q_norm_g=a_q_norm_g, a_k_norm_g=a_k_norm_g,
             c_q_nope_norm_g=c_q_nope_norm_g, c_k_nope_norm_g=c_k_nope_norm_g,
             c_q_rope_norm_g=c_q_rope_norm_g, c_k_rope_norm_g=c_k_rope_norm_g,
             c_q_rank_norm_g=c_q_rank_norm_g, c_kv_rank_norm_g=c_kv_rank_norm_g, mix_norm_g=mix_norm_g)
    bsz, t, d = x.shape
    ctx_len = ctx.shape[1]
    depth = ada_w.shape[0]
    n_x, n_c = bsz * t, bsz * ctx_len
    tm_pre = 256
    tm_tok = 512
    tq = 256

    ada_rows = ((bsz + 1 + 7) // 8) * 8
    c_pad = jnp.concatenate([c, c_ctx[None, :], jnp.zeros((ada_rows - bsz - 1, d), F32)], axis=0)
    mods = _ada_call(c_pad, ada_w, ada_b).reshape(depth, ada_rows, 6, d)
    mods = jnp.pad(mods, ((0, 0), (0, 0), (0, MOD_ROWS - 6), (0, 0)))

    tabs_x = _rope_tables(t)
    ones = jnp.ones((tm_pre, LANES), F32)
    tabs_c = (ones, 0 * ones, ones, 0 * ones)
    perm = _head_perm()

    x2 = x.reshape(n_x, d)
    c2 = ctx.reshape(n_c, d)
    for i in range(depth):
        last = i == depth - 1
        mod_x = mods[i, :bsz]
        mod_c = mods[i, bsz:bsz + 1]
        consts_x = _layer_consts(i, p, perm, tabs_x)
        consts_c = _layer_consts(i, p, perm, tabs_c)

        qa_x, ka_x, va_x, u_x, qc_x, kc_x, vc_x = _pre_call(x2, mod_x, t // tm_pre, t // tm_pre, consts_x, tm_pre)
        qa_c, ka_c, va_c, u_c, qc_c, kc_c, vc_c = _pre_call(c2, mod_c, n_c // tm_pre, 1, consts_c, tm_pre)

        def r3(a, length):
            return a.reshape(bsz, length, a.shape[-1])

        sink_rows = jnp.broadcast_to(a_sink[i][:, None, None], (A_HEADS, BLOCK, LANES)).reshape(-1, LANES)
        conv_w = jnp.pad(b_conv_w[i], ((0, 32 - B_WIDTH), (0, 0)))
        conv_p = (conv_w, b_conv_b[i][None, :], b_ln_g[i][None, :], b_ln_b[i][None, :])
        w_o = jnp.concatenate([w_out[i][:A_Q][perm], w_out[i][A_Q:]], axis=0).astype(BF16)

        o_a = _attn_a_call(qa_x, r3(ka_x, t), r3(va_x, t), r3(ka_c, ctx_len), r3(va_c, ctx_len), sink_rows, bsz, t)
        o_b = _conv_call(r3(u_x, t), *conv_p).reshape(n_x, B_CH)
        o_c = _mla_call(qc_x, r3(kc_x, t), r3(vc_x, t), r3(kc_c, ctx_len), r3(vc_c, ctx_len), bsz, t, tq)
        x2 = _out_call(x2, mod_x, t // tm_tok, o_a, o_b, o_c, w_o, tm_tok)
        if not last:
            oc_a = _attn_a_call(qa_c, None, None, r3(ka_c, ctx_len), r3(va_c, ctx_len), sink_rows, bsz, ctx_len)
            oc_b = _conv_call(r3(u_c, ctx_len), *conv_p).reshape(n_c, B_CH)
            oc_c = _mla_call(qc_c, None, None, r3(kc_c, ctx_len), r3(vc_c, ctx_len), bsz, ctx_len, ctx_len)
            c2 = _out_call(c2, mod_c, n_c // tm_tok, oc_a, oc_b, oc_c, w_o, tm_tok)

        j = i // 2
        g_ffn = ffn_norm_g[i][None, :]
        if i % 2 == 0:
            w1 = dense_w1[j][None].astype(BF16)
            w3 = dense_w3[j][None].astype(BF16)
            w2 = dense_w2[j][None].astype(BF16)
            router = None
        else:
            w1 = moe_w1[j].astype(BF16)
            w3 = moe_w3[j].astype(BF16)
            w2 = moe_w2[j].astype(BF16)
            router = (jnp.pad(moe_router_w[j], ((0, 0), (0, LANES - N_EXPERTS))),
                      jnp.pad(moe_router_b[j], (0, LANES - N_EXPERTS))[None, :])
        x2 = _ffn_call(x2, mod_x, t // tm_tok, g_ffn, w1, w3, w2, router, tm_tok, 2)
        if not last:
            c2 = _ffn_call(c2, mod_c, n_c // tm_tok, g_ffn, w1, w3, w2, router, tm_tok, 2)
    return x2.reshape(bsz, t, d)
```

```python
import functools

import jax
import jax.numpy as jnp
import numpy as np
from jax import lax
from jax.experimental import pallas as pl
from jax.experimental.pallas import tpu as pltpu

F32 = jnp.float32
BF16 = jnp.bfloat16

D_MODEL = 1024
GRID_W = 64
HEAD_DIM = 64
A_HEADS = 8
A_KV_HEADS = 2
A_WINDOW = 128
BLOCK = 128
B_CH = 256
B_WIDTH = 31
C_HEADS = 4
C_Q_RANK = 384
C_KV_RANK = 256
C_NOPE = 64
C_ROPE = 32
C_V = 64
A_Q = A_HEADS * HEAD_DIM
A_KV = A_KV_HEADS * HEAD_DIM
IN_COLS_PAD = 2048
D_FF = 2816
N_EXPERTS = 8
ROPE_BASE = 10000.0
EPS = 1e-6
NEG = -1e30
A_SCALE = HEAD_DIM ** -0.5
MLA_SCALE = (C_NOPE + C_ROPE) ** -0.5

LANES = 128
MOD_ROWS = 8
C_SLOT = 128
VMEM_LIMIT = 56 * 1024 * 1024


def _cparams(sem):
    return pltpu.CompilerParams(dimension_semantics=sem, vmem_limit_bytes=VMEM_LIMIT)


def _dot(a, b):
    return jnp.dot(a, b, preferred_element_type=F32)


def _dot_nt(a, b):
    return lax.dot_general(a, b, (((1,), (1,)), ((), ())), preferred_element_type=F32)


def _ada_kernel(c_ref, w_ref, b_ref, o_ref):
    c = c_ref[...]
    a = c * jax.nn.sigmoid(c)
    o_ref[0] = _dot(a.astype(BF16), w_ref[0].astype(BF16)) + b_ref[0]


def _ada_call(c_pad, ada_w, ada_b):
    depth, d, n6 = ada_w.shape
    rows = c_pad.shape[0]
    tn = 1536
    return pl.pallas_call(
        _ada_kernel,
        grid=(depth, n6 // tn),
        in_specs=[
            pl.BlockSpec((rows, d), lambda i, j: (0, 0)),
            pl.BlockSpec((1, d, tn), lambda i, j: (i, 0, j)),
            pl.BlockSpec((1, 1, tn), lambda i, j: (i, 0, j)),
        ],
        out_specs=pl.BlockSpec((1, rows, tn), lambda i, j: (i, 0, j)),
        out_shape=jax.ShapeDtypeStruct((depth, rows, n6), F32),
        compiler_params=_cparams(("parallel", "parallel")),
        name="ada_proj",
    )(c_pad, ada_w, ada_b.reshape(depth, 1, n6))


def _rope_chunk(c, cos, sin, half):
    lane = lax.broadcasted_iota(jnp.int32, c.shape, 1)
    lo = (lane & (2 * half - 1)) < half
    partner = jnp.where(lo, pltpu.roll(c, LANES - half, 1), pltpu.roll(c, half, 1))
    return c * cos + partner * sin


def _pre_kernel(x_ref, mod_ref, gmix_ref, win_ref, cosa_ref, sina_ref, cosc_ref, sinc_ref,
                sa_ref, sc_ref, gq_ref, gk_ref, gcq_ref, gckv_ref, wuq_ref, wukvk_ref, wukvv_ref,
                gqc_ref, gkn_ref, gkr_ref,
                qa_ref, ka_ref, va_ref, u_ref, qc_ref, kc_ref, vc_ref):
    x = x_ref[...]
    ms = jnp.mean(x * x, axis=-1, keepdims=True)
    y = x * lax.rsqrt(ms + EPS)
    shift = mod_ref[0, 0:1, :]
    scale = mod_ref[0, 1:2, :]
    h = (y * gmix_ref[...]) * (1.0 + scale) + shift
    p = _dot(h.astype(BF16), win_ref[...])

    cosa, sina = cosa_ref[...], sina_ref[...]
    cosc, sinc = cosc_ref[...], sinc_ref[...]

    qa = p[:, 0:A_Q]
    ssq = _dot((qa * qa).astype(BF16), sa_ref[...])
    qa = qa * lax.rsqrt(ssq + EPS) * gq_ref[...]
    for j in range(A_Q // LANES):
        sl = slice(j * LANES, (j + 1) * LANES)
        qa_ref[:, sl] = _rope_chunk(qa[:, sl], cosa, sina, HEAD_DIM // 4).astype(BF16)

    ka = p[:, A_Q:A_Q + A_KV]
    ssk = _dot((ka * ka).astype(BF16), sa_ref[0:A_KV, 0:A_KV])
    ka = ka * lax.rsqrt(ssk + EPS) * gk_ref[...]
    ka_ref[...] = _rope_chunk(ka, cosa, sina, HEAD_DIM // 4).astype(BF16)
    va_ref[...] = p[:, A_Q + A_KV:A_Q + 2 * A_KV].astype(BF16)

    o = A_Q + 2 * A_KV
    u_ref[...] = p[:, o:o + B_CH] * jax.nn.sigmoid(p[:, o + B_CH:o + 2 * B_CH])

    o = o + 2 * B_CH
    cq = p[:, o:o + C_Q_RANK]
    cq = cq * lax.rsqrt(jnp.mean(cq * cq, axis=-1, keepdims=True) + EPS) * gcq_ref[...]
    qc = _dot(cq.astype(BF16), wuq_ref[...])
    ssq = _dot((qc * qc).astype(BF16), sc_ref[...])
    qc = qc * lax.rsqrt(ssq + EPS) * gqc_ref[...]
    for j in range(C_HEADS):
        sl = slice(j * C_SLOT, (j + 1) * C_SLOT)
        qc_ref[:, sl] = _rope_chunk(qc[:, sl], cosc, sinc, C_ROPE // 4).astype(BF16)

    o = o + C_Q_RANK
    ckv = p[:, o:o + C_KV_RANK]
    ckv = (ckv * lax.rsqrt(jnp.mean(ckv * ckv, axis=-1, keepdims=True) + EPS) * gckv_ref[...]).astype(BF16)
    kn = _dot(ckv, wukvk_ref[...])
    vc_ref[...] = _dot(ckv, wukvv_ref[...]).astype(BF16)
    ssk = _dot((kn * kn).astype(BF16), sc_ref[...])
    kn = kn * lax.rsqrt(ssk + EPS) * gkn_ref[...]
    o = o + C_KV_RANK
    kr = p[:, o:o + C_SLOT]
    kr = kr * lax.rsqrt(jnp.sum(kr * kr, axis=-1, keepdims=True) * (1.0 / C_ROPE) + EPS) * gkr_ref[...]
    kr = _rope_chunk(kr, cosc, sinc, C_ROPE // 4)
    for j in range(C_HEADS):
        sl = slice(j * C_SLOT, (j + 1) * C_SLOT)
        kc_ref[:, sl] = (kn[:, sl] + kr).astype(BF16)


def _pre_call(x2d, mod, tiles_per_mod, tab_tiles, consts, tm):
    n, d = x2d.shape
    (gmix, win, cosa, sina, cosc, sinc, sa, sc, gq, gk, gcq, gckv, wuq, wukvk, wukvv, gqc, gkn, gkr) = consts

    def const(a):
        return pl.BlockSpec(a.shape, lambda i: (0,) * a.ndim)

    def tab(a):
        return pl.BlockSpec((tm, LANES), lambda i: (i % tab_tiles, 0))

    in_specs = [
        pl.BlockSpec((tm, d), lambda i: (i, 0)),
        pl.BlockSpec((1, MOD_ROWS, d), lambda i: (i // tiles_per_mod, 0, 0)),
        const(gmix), const(win), tab(cosa), tab(sina), tab(cosc), tab(sinc),
        const(sa), const(sc), const(gq), const(gk), const(gcq), const(gckv),
        const(wuq), const(wukvk), const(wukvv), const(gqc), const(gkn), const(gkr),
    ]
    widths = (A_Q, A_KV, A_KV, B_CH, C_HEADS * C_SLOT, C_HEADS * C_SLOT, C_HEADS * C_V)
    dtypes = (BF16, BF16, BF16, F32, BF16, BF16, BF16)
    return pl.pallas_call(
        _pre_kernel,
        grid=(n // tm,),
        in_specs=in_specs,
        out_specs=[pl.BlockSpec((tm, w), lambda i: (i, 0)) for w in widths],
        out_shape=[jax.ShapeDtypeStruct((n, w), dt) for w, dt in zip(widths, dtypes)],
        compiler_params=_cparams(("parallel",)),
        name="pre_attn",
    )(x2d, mod, gmix, win, cosa, sina, cosc, sinc, sa, sc, gq, gk, gcq, gckv, wuq, wukvk, wukvv,
      gqc, gkn, gkr)


def _attn_a_kernel(*refs, t, has_local):
    if has_local:
        q_ref, k_ref, v_ref, kc_ref, vc_ref, sink_ref, o_ref = refs
    else:
        q_ref, kc_ref, vc_ref, sink_ref, o_ref = refs
    nchunk = A_Q // LANES
    rows = 2 * nchunk * BLOCK
    q = q_ref[...]
    lane = lax.broadcasted_iota(jnp.int32, (BLOCK, LANES), 1)
    zero = jnp.zeros((BLOCK, LANES), BF16)
    parts = [jnp.where(lane < HEAD_DIM, q[:, j * LANES:(j + 1) * LANES], zero) for j in range(nchunk)]
    parts += [jnp.where(lane >= HEAD_DIM, q[:, j * LANES:(j + 1) * LANES], zero) for j in range(nchunk)]
    qs = jnp.concatenate(parts, axis=0)

    sink = sink_ref[:, 0:1]
    kc = kc_ref[0]
    s_ctx = _dot_nt(qs, kc)
    m = jnp.maximum(jnp.max(s_ctx, axis=-1, keepdims=True), sink)
    if has_local:
        n = pl.program_id(1)
        span = 3 * BLOCK
        start = pl.multiple_of(jnp.clip((n - 1) * BLOCK, 0, t - span), BLOCK)
        kl = k_ref[0, pl.ds(start, span), :]
        vl = v_ref[0, pl.ds(start, span), :]
        s_loc = _dot_nt(qs, kl)
        qpos = n * BLOCK + (lax.broadcasted_iota(jnp.int32, (rows, span), 0) & (BLOCK - 1))
        kpos = start + lax.broadcasted_iota(jnp.int32, (rows, span), 1)
        s_loc = jnp.where(jnp.abs(qpos - kpos) <= A_WINDOW, s_loc, NEG)
        m = jnp.maximum(m, jnp.max(s_loc, axis=-1, keepdims=True))
    e_ctx = jnp.exp(s_ctx - m)
    den = jnp.sum(e_ctx, axis=-1, keepdims=True) + jnp.exp(sink - m)
    acc = _dot(e_ctx.astype(BF16), vc_ref[0])
    if has_local:
        e_loc = jnp.exp(s_loc - m)
        den = den + jnp.sum(e_loc, axis=-1, keepdims=True)
        acc = acc + _dot(e_loc.astype(BF16), vl)
    acc = acc / den
    half = nchunk * BLOCK
    for j in range(nchunk):
        o_ref[:, j * LANES:(j + 1) * LANES] = jnp.where(
            lane < HEAD_DIM, acc[j * BLOCK:(j + 1) * BLOCK], acc[half + j * BLOCK:half + (j + 1) * BLOCK]
        ).astype(BF16)


def _attn_a_call(qa, ka, va, kac, vac, sink_rows, bsz, t):
    has_local = ka is not None
    nb = t // BLOCK
    ctx_len = kac.shape[1]
    in_specs = [pl.BlockSpec((BLOCK, A_Q), lambda b, n: (b * nb + n, 0))]
    args = [qa]
    if has_local:
        in_specs += [pl.BlockSpec((1, t, A_KV), lambda b, n: (b, 0, 0))] * 2
        args += [ka, va]
    in_specs += [pl.BlockSpec((1, ctx_len, A_KV), lambda b, n: (b, 0, 0))] * 2
    in_specs += [pl.BlockSpec(sink_rows.shape, lambda b, n: (0, 0))]
    args += [kac, vac, sink_rows]
    return pl.pallas_call(
        functools.partial(_attn_a_kernel, t=t, has_local=has_local),
        grid=(bsz, nb),
        in_specs=in_specs,
        out_specs=pl.BlockSpec((BLOCK, A_Q), lambda b, n: (b * nb + n, 0)),
        out_shape=jax.ShapeDtypeStruct((bsz * t, A_Q), BF16),
        compiler_params=_cparams(("parallel", "parallel")),
        name="attn_a_local" if has_local else "attn_a_ctx",
    )(*args)


CONV_PAD = 16
CONV_CHUNK = 128


def _conv_kernel(u_ref, w_ref, b_ref, g_ref, beta_ref, o_ref, pad_ref, *, t):
    zeros = jnp.zeros((CONV_PAD, B_CH), F32)
    pad_ref[0:CONV_PAD, :] = zeros
    pad_ref[CONV_PAD + t:CONV_PAD + t + CONV_PAD, :] = zeros
    pad_ref[CONV_PAD:CONV_PAD + t, :] = u_ref[0]
    off = CONV_PAD - B_WIDTH // 2
    for c in range(t // CONV_CHUNK):
        base = c * CONV_CHUNK + off
        acc = jnp.zeros((CONV_CHUNK, B_CH), F32)
        for k in range(B_WIDTH):
            acc = acc + pad_ref[base + k:base + k + CONV_CHUNK, :] * w_ref[k:k + 1, :]
        y = acc + b_ref[...]
        mu = jnp.mean(y, axis=-1, keepdims=True)
        yc = y - mu
        var = jnp.mean(yc * yc, axis=-1, keepdims=True)
        z = yc * lax.rsqrt(var + EPS) * g_ref[...] + beta_ref[...]
        o_ref[0, c * CONV_CHUNK:(c + 1) * CONV_CHUNK, :] = (z * jax.nn.sigmoid(z)).astype(BF16)


def _conv_call(u3, w_pad, b, g, beta):
    bsz, t, ch = u3.shape

    def const(a):
        return pl.BlockSpec(a.shape, lambda i: (0, 0))

    return pl.pallas_call(
        functools.partial(_conv_kernel, t=t),
        grid=(bsz,),
        in_specs=[pl.BlockSpec((1, t, ch), lambda i: (i, 0, 0)), const(w_pad), const(b), const(g), const(beta)],
        out_specs=pl.BlockSpec((1, t, ch), lambda i: (i, 0, 0)),
        out_shape=jax.ShapeDtypeStruct((bsz, t, ch), BF16),
        scratch_shapes=[pltpu.VMEM((t + 2 * CONV_PAD, ch), F32)],
        compiler_params=_cparams(("parallel",)),
        name="conv_module",
    )(u3, w_pad, b, g, beta)


def _mla_kernel(*refs, has_local):
    if has_local:
        q_ref, kx_ref, vx_ref, kc_ref, vc_ref, o_ref = refs
    else:
        q_ref, kc_ref, vc_ref, o_ref = refs
    tq = q_ref.shape[0]
    width = C_HEADS * C_V
    lane = lax.broadcasted_iota(jnp.int32, (tq, width), 1)
    out = jnp.zeros((tq, width), F32)
    for h in range(C_HEADS):
        sl = slice(h * C_SLOT, (h + 1) * C_SLOT)
        q = q_ref[:, sl]
        s_c = _dot_nt(q, kc_ref[0, :, sl])
        m = jnp.max(s_c, axis=-1, keepdims=True)
        if has_local:
            s_x = _dot_nt(q, kx_ref[0, :, sl])
            m = jnp.maximum(m, jnp.max(s_x, axis=-1, keepdims=True))
        e_c = jnp.exp(s_c - m)
        den = jnp.sum(e_c, axis=-1, keepdims=True)
        acc = _dot(e_c.astype(BF16), vc_ref[0])
        if has_local:
            e_x = jnp.exp(s_x - m)
            den = den + jnp.sum(e_x, axis=-1, keepdims=True)
            acc = acc + _dot(e_x.astype(BF16), vx_ref[0])
        out = jnp.where((lane >= h * C_V) & (lane < (h + 1) * C_V), acc / den, out)
    o_ref[...] = out.astype(BF16)


def _mla_call(qc, kx, vx, kcc, vcc, bsz, t, tq):
    has_local = kx is not None
    nq = t // tq
    ctx_len = kcc.shape[1]
    wq = C_HEADS * C_SLOT
    wv = C_HEADS * C_V
    in_specs = [pl.BlockSpec((tq, wq), lambda b, n: (b * nq + n, 0))]
    args = [qc]
    if has_local:
        in_specs += [pl.BlockSpec((1, t, wq), lambda b, n: (b, 0, 0)),
                     pl.BlockSpec((1, t, wv), lambda b, n: (b, 0, 0))]
        args += [kx, vx]
    in_specs += [pl.BlockSpec((1, ctx_len, wq), lambda b, n: (b, 0, 0)),
                 pl.BlockSpec((1, ctx_len, wv), lambda b, n: (b, 0, 0))]
    args += [kcc, vcc]
    return pl.pallas_call(
        functools.partial(_mla_kernel, has_local=has_local),
        grid=(bsz, nq),
        in_specs=in_specs,
        out_specs=pl.BlockSpec((tq, wv), lambda b, n: (b * nq + n, 0)),
        out_shape=jax.ShapeDtypeStruct((bsz * t, wv), BF16),
        compiler_params=_cparams(("parallel", "parallel")),
        name="mla_local" if has_local else "mla_ctx",
    )(*args)


def _out_kernel(x_ref, mod_ref, oa_ref, ob_ref, oc_ref, w_ref, o_ref):
    y = _dot(oa_ref[...], w_ref[0:A_Q, :])
    y = y + _dot(ob_ref[...], w_ref[A_Q:A_Q + B_CH, :])
    y = y + _dot(oc_ref[...], w_ref[A_Q + B_CH:, :])
    o_ref[...] = x_ref[...] + mod_ref[0, 2:3, :] * y


def _out_call(x2d, mod, tiles_per_mod, oa, ob, oc, w_out, tm):
    n, d = x2d.shape
    return pl.pallas_call(
        _out_kernel,
        grid=(n // tm,),
        in_specs=[
            pl.BlockSpec((tm, d), lambda i: (i, 0)),
            pl.BlockSpec((1, MOD_ROWS, d), lambda i: (i // tiles_per_mod, 0, 0)),
            pl.BlockSpec((tm, oa.shape[1]), lambda i: (i, 0)),
            pl.BlockSpec((tm, ob.shape[1]), lambda i: (i, 0)),
            pl.BlockSpec((tm, oc.shape[1]), lambda i: (i, 0)),
            pl.BlockSpec(w_out.shape, lambda i: (0, 0)),
        ],
        out_specs=pl.BlockSpec((tm, d), lambda i: (i, 0)),
        out_shape=jax.ShapeDtypeStruct((n, d), F32),
        compiler_params=_cparams(("parallel",)),
        name="out_proj",
    )(x2d, mod, oa, ob, oc, w_out)


def _split_bf16(a):
    hi = a.astype(BF16)
    lo = (a - hi.astype(F32)).astype(BF16)
    return hi, lo


def _ffn_input(x_ref, mod_ref, g_ref):
    x = x_ref[...]
    ms = jnp.mean(x * x, axis=-1, keepdims=True)
    y = x * lax.rsqrt(ms + EPS)
    return (y * g_ref[...]) * (1.0 + mod_ref[0, 4:5, :]) + mod_ref[0, 3:4, :]


def _swiglu_partial(h, w1, w3, w2):
    a = _dot(h, w1)
    b = _dot(h, w3)
    g = (a * jax.nn.sigmoid(a) * b).astype(BF16)
    return _dot(g, w2)


def _ffn_kernel(x_ref, mod_ref, g_ref, w1_ref, w3_ref, w2_ref, o_ref, h_ref, acc_ref):
    f = pl.program_id(1)

    @pl.when(f == 0)
    def _():
        h_ref[...] = _ffn_input(x_ref, mod_ref, g_ref).astype(BF16)
        acc_ref[...] = jnp.zeros_like(acc_ref)

    acc_ref[...] += _swiglu_partial(h_ref[...], w1_ref[...], w3_ref[...], w2_ref[...])

    @pl.when(f == pl.num_programs(1) - 1)
    def _():
        o_ref[...] = x_ref[...] + mod_ref[0, 5:6, :] * acc_ref[...]


def _ffn_call(x2d, mod, tiles_per_mod, g, w1, w3, w2, tm, nf):
    n, d = x2d.shape
    ff = w1.shape[1]
    tf = ff // nf
    return pl.pallas_call(
        _ffn_kernel,
        grid=(n // tm, nf),
        in_specs=[
            pl.BlockSpec((tm, d), lambda i, f: (i, 0)),
            pl.BlockSpec((1, MOD_ROWS, d), lambda i, f: (i // tiles_per_mod, 0, 0)),
            pl.BlockSpec(g.shape, lambda i, f: (0, 0)),
            pl.BlockSpec((d, tf), lambda i, f: (0, f)),
            pl.BlockSpec((d, tf), lambda i, f: (0, f)),
            pl.BlockSpec((tf, d), lambda i, f: (f, 0)),
        ],
        out_specs=pl.BlockSpec((tm, d), lambda i, f: (i, 0)),
        out_shape=jax.ShapeDtypeStruct((n, d), F32),
        scratch_shapes=[pltpu.VMEM((tm, d), BF16), pltpu.VMEM((tm, d), F32)],
        compiler_params=_cparams(("parallel", "arbitrary")),
        name="ffn_dense",
    )(x2d, mod, g, w1, w3, w2)


MOE_TM = 512
SEG_ALIGN = 16
SEG_PIECES = (512, 256, 128, 64, 32, 16)
CBUF_ROWS = 2 * MOE_TM + N_EXPERTS * SEG_ALIGN
META_I1, META_I2, META_G1, META_G2, META_R1, META_R2 = range(6)


def _router_kernel(x_ref, mod_ref, g_ref, wr_ref, br_ref, ltri_ref, h_ref, meta_ref, cnt_ref):
    h = _ffn_input(x_ref, mod_ref, g_ref)
    h_ref[...] = h.astype(BF16)
    h_hi, h_lo = _split_bf16(h)
    w_hi, w_lo = _split_bf16(wr_ref[...])
    logits = _dot(h_hi, w_hi) + (_dot(h_lo, w_hi) + _dot(h_hi, w_lo)) + br_ref[...]
    lane = lax.broadcasted_iota(jnp.int32, logits.shape, 1).astype(F32)
    logits = jnp.where(lane < N_EXPERTS, logits, NEG)
    m1 = jnp.max(logits, axis=-1, keepdims=True)
    i1 = jnp.min(jnp.where(logits == m1, lane, float(LANES)), axis=-1, keepdims=True)
    rest = jnp.where(lane == i1, NEG, logits)
    m2 = jnp.max(rest, axis=-1, keepdims=True)
    i2 = jnp.min(jnp.where(rest == m2, lane, float(LANES)), axis=-1, keepdims=True)
    e2 = jnp.exp(m2 - m1)
    den = 1.0 + e2
    sel1 = jnp.where(lane == i1, 1.0, 0.0)
    sel2 = jnp.where(lane == i2, 1.0, 0.0)
    sel = sel1 + sel2
    before = _dot(ltri_ref[...], sel.astype(BF16))
    r1 = jnp.sum(before * sel1, axis=-1, keepdims=True)
    r2 = jnp.sum(before * sel2, axis=-1, keepdims=True)
    cnt_ref[0] = jnp.sum(sel, axis=0, keepdims=True)
    cols = (i1, i2, 1.0 / den, e2 / den, r1, r2)
    meta = jnp.zeros_like(logits)
    for k, col in enumerate(cols):
        meta = jnp.where(lane == k, col, meta)
    meta_ref[...] = meta


def _router_call(x2d, mod, tiles_per_mod, g, wr, br, ltri):
    n, d = x2d.shape
    tm = MOE_TM
    nt = n // tm
    return pl.pallas_call(
        _router_kernel,
        grid=(nt,),
        in_specs=[
            pl.BlockSpec((tm, d), lambda i: (i, 0)),
            pl.BlockSpec((1, MOD_ROWS, d), lambda i: (i // tiles_per_mod, 0, 0)),
            pl.BlockSpec(g.shape, lambda i: (0, 0)),
            pl.BlockSpec(wr.shape, lambda i: (0, 0)),
            pl.BlockSpec(br.shape, lambda i: (0, 0)),
            pl.BlockSpec(ltri.shape, lambda i: (0, 0)),
        ],
        out_specs=[
            pl.BlockSpec((tm, d), lambda i: (i, 0)),
            pl.BlockSpec((tm, LANES), lambda i: (i, 0)),
            pl.BlockSpec((1, 1, LANES), lambda i: (i, 0, 0)),
        ],
        out_shape=[
            jax.ShapeDtypeStruct((n, d), BF16),
            jax.ShapeDtypeStruct((n, LANES), F32),
            jax.ShapeDtypeStruct((nt, 1, LANES), F32),
        ],
        compiler_params=_cparams(("parallel",)),
        name="moe_router",
    )(x2d, mod, g, wr, br, ltri)


def _pair_slots(meta, segoff_row):
    lane = lax.broadcasted_iota(jnp.int32, meta.shape, 1).astype(F32)
    i1 = meta[:, META_I1:META_I1 + 1]
    i2 = meta[:, META_I2:META_I2 + 1]
    s1 = jnp.sum(jnp.where(lane == i1, segoff_row, 0.0), axis=-1, keepdims=True) + meta[:, META_R1:META_R1 + 1]
    s2 = jnp.sum(jnp.where(lane == i2, segoff_row, 0.0), axis=-1, keepdims=True) + meta[:, META_R2:META_R2 + 1]
    return s1, s2


def _segment_copies(src, dst, src_off, dst_off, length, sem):
    out = []
    for size in SEG_PIECES:
        done = (length // (2 * size)) * (2 * size)
        s = pl.multiple_of(src_off + done, SEG_ALIGN)
        t = pl.multiple_of(dst_off + done, SEG_ALIGN)
        cp = pltpu.make_async_copy(src.at[pl.ds(s, size)], dst.at[pl.ds(t, size)], sem)
        out.append(((length & size) != 0, cp))
    return out


def _run_copies(copies):
    for pred, cp in copies:
        @pl.when(pred)
        def _(cp=cp):
            cp.start()
    for pred, cp in copies:
        @pl.when(pred)
        def _(cp=cp):
            cp.wait()


def _compact_kernel(segoff_s, base_s, len_s, h_ref, meta_ref, segoff_ref, xs_in, gs_in, xs_out, gs_out,
                    cbuf, gbuf, sems):
    del xs_in, gs_in
    i = pl.program_id(0)
    meta = meta_ref[...]
    s1, s2 = _pair_slots(meta, segoff_ref[0])
    slot = lax.broadcasted_iota(jnp.int32, (MOE_TM, CBUF_ROWS), 1).astype(F32)
    p1 = jnp.where(slot == s1, 1.0, 0.0)
    p2 = jnp.where(slot == s2, 1.0, 0.0)
    tn = (((0,), (0,)), ((), ()))
    perm = (p1 + p2).astype(BF16)
    cbuf[...] = lax.dot_general(perm, h_ref[...], tn, preferred_element_type=F32).astype(BF16)
    lane = lax.broadcasted_iota(jnp.int32, meta.shape, 1)
    g1 = meta[:, META_G1:META_G1 + 1]
    g2 = meta[:, META_G2:META_G2 + 1]
    g1_hi = g1.astype(BF16).astype(F32)
    g2_hi = g2.astype(BF16).astype(F32)
    a1 = jnp.where(lane == 0, g1_hi, jnp.where(lane == 1, g1 - g1_hi, 0.0)).astype(BF16)
    a2 = jnp.where(lane == 0, g2_hi, jnp.where(lane == 1, g2 - g2_hi, 0.0)).astype(BF16)
    gbuf[...] = (lax.dot_general(p1.astype(BF16), a1, tn, preferred_element_type=F32)
                 + lax.dot_general(p2.astype(BF16), a2, tn, preferred_element_type=F32))
    copies = []
    for e in range(N_EXPERTS):
        k = i * N_EXPERTS + e
        copies += _segment_copies(cbuf, xs_out, segoff_s[k], base_s[k], len_s[k], sems.at[0])
        copies += _segment_copies(gbuf, gs_out, segoff_s[k], base_s[k], len_s[k], sems.at[1])
    _run_copies(copies)


def _compact_call(sched, h, meta, segoff_v, xs0, gs0):
    n, d = h.shape
    nt = n // MOE_TM
    grid_spec = pltpu.PrefetchScalarGridSpec(
        num_scalar_prefetch=3,
        grid=(nt,),
        in_specs=[
            pl.BlockSpec((MOE_TM, d), lambda i, *_: (i, 0)),
            pl.BlockSpec((MOE_TM, LANES), lambda i, *_: (i, 0)),
            pl.BlockSpec((1, 1, LANES), lambda i, *_: (i, 0, 0)),
            pl.BlockSpec(memory_space=pl.ANY),
            pl.BlockSpec(memory_space=pl.ANY),
        ],
        out_specs=[pl.BlockSpec(memory_space=pl.ANY), pl.BlockSpec(memory_space=pl.ANY)],
        scratch_shapes=[pltpu.VMEM((CBUF_ROWS, d), BF16), pltpu.VMEM((CBUF_ROWS, LANES), F32),
                        pltpu.SemaphoreType.DMA((2,))],
    )
    return pl.pallas_call(
        _compact_kernel,
        grid_spec=grid_spec,
        out_shape=[jax.ShapeDtypeStruct(xs0.shape, xs0.dtype), jax.ShapeDtypeStruct(gs0.shape, gs0.dtype)],
        input_output_aliases={6: 0, 7: 1},
        compiler_params=_cparams(("arbitrary",)),
        name="moe_compact",
    )(*sched, h, meta, segoff_v, xs0, gs0)


def _expert_kernel(exp_s, blk_s, valid_s, xs_ref, gs_ref, w1_ref, w3_ref, w2_ref, y_ref, acc_ref):
    del exp_s, blk_s
    j = pl.program_id(0)
    f = pl.program_id(1)

    @pl.when(valid_s[j] != 0)
    def _():
        @pl.when(f == 0)
        def _():
            acc_ref[...] = jnp.zeros_like(acc_ref)

        acc_ref[...] += _swiglu_partial(xs_ref[...], w1_ref[0], w3_ref[0], w2_ref[0])

        @pl.when(f == pl.num_programs(1) - 1)
        def _():
            gate = gs_ref[:, 0:1] + gs_ref[:, 1:2]
            y_ref[...] = (acc_ref[...] * gate).astype(BF16)

    @pl.when(valid_s[j] == 0)
    def _():
        y_ref[...] = jnp.zeros_like(y_ref)


def _expert_call(tile_sched, xs, gs, w1, w3, w2, nf):
    rows, d = xs.shape
    ff = w1.shape[2]
    tf = ff // nf

    def f_eff(j, f, valid_s):
        return jnp.where(valid_s[j] != 0, f, nf - 1)

    grid_spec = pltpu.PrefetchScalarGridSpec(
        num_scalar_prefetch=3,
        grid=(rows // MOE_TM, nf),
        in_specs=[
            pl.BlockSpec((MOE_TM, d), lambda j, f, e_s, b_s, v_s: (b_s[j], 0)),
            pl.BlockSpec((MOE_TM, LANES), lambda j, f, e_s, b_s, v_s: (b_s[j], 0)),
            pl.BlockSpec((1, d, tf), lambda j, f, e_s, b_s, v_s: (e_s[j], 0, f_eff(j, f, v_s))),
            pl.BlockSpec((1, d, tf), lambda j, f, e_s, b_s, v_s: (e_s[j], 0, f_eff(j, f, v_s))),
            pl.BlockSpec((1, tf, d), lambda j, f, e_s, b_s, v_s: (e_s[j], f_eff(j, f, v_s), 0)),
        ],
        out_specs=pl.BlockSpec((MOE_TM, d), lambda j, f, e_s, b_s, v_s: (j, 0)),
        scratch_shapes=[pltpu.VMEM((MOE_TM, d), F32)],
    )
    return pl.pallas_call(
        _expert_kernel,
        grid_spec=grid_spec,
        out_shape=jax.ShapeDtypeStruct((rows, d), BF16),
        compiler_params=_cparams(("arbitrary", "arbitrary")),
        name="moe_experts",
    )(*tile_sched, xs, gs, w1, w3, w2)


def _combine_kernel(segoff_s, base_s, len_s, x_ref, mod_ref, meta_ref, segoff_ref, y_hbm, o_ref, ybuf, sems):
    i = pl.program_id(0)

    @pl.when(i == 0)
    def _():
        ybuf[...] = jnp.zeros_like(ybuf)

    copies = []
    for e in range(N_EXPERTS):
        k = i * N_EXPERTS + e
        copies += _segment_copies(y_hbm, ybuf, base_s[k], segoff_s[k], len_s[k], sems.at[0])
    _run_copies(copies)
    s1, s2 = _pair_slots(meta_ref[...], segoff_ref[0])
    slot = lax.broadcasted_iota(jnp.int32, (MOE_TM, CBUF_ROWS), 1).astype(F32)
    pick = (jnp.where(slot == s1, 1.0, 0.0) + jnp.where(slot == s2, 1.0, 0.0)).astype(BF16)
    o_ref[...] = x_ref[...] + mod_ref[0, 5:6, :] * _dot(pick, ybuf[...])


def _combine_call(sched, x2d, mod, tiles_per_mod, meta, segoff_v, y):
    n, d = x2d.shape
    grid_spec = pltpu.PrefetchScalarGridSpec(
        num_scalar_prefetch=3,
        grid=(n // MOE_TM,),
        in_specs=[
            pl.BlockSpec((MOE_TM, d), lambda i, *_: (i, 0)),
            pl.BlockSpec((1, MOD_ROWS, d), lambda i, *_: (i // tiles_per_mod, 0, 0)),
            pl.BlockSpec((MOE_TM, LANES), lambda i, *_: (i, 0)),
            pl.BlockSpec((1, 1, LANES), lambda i, *_: (i, 0, 0)),
            pl.BlockSpec(memory_space=pl.ANY),
        ],
        out_specs=pl.BlockSpec((MOE_TM, d), lambda i, *_: (i, 0)),
        scratch_shapes=[pltpu.VMEM((CBUF_ROWS, d), BF16), pltpu.SemaphoreType.DMA((1,))],
    )
    return pl.pallas_call(
        _combine_kernel,
        grid_spec=grid_spec,
        out_shape=jax.ShapeDtypeStruct((n, d), F32),
        compiler_params=_cparams(("arbitrary",)),
        name="moe_combine",
    )(*sched, x2d, mod, meta, segoff_v, y)


def _moe_call(x2d, mod, tiles_per_mod, g, wr, br, w1, w3, w2, nf):
    n, d = x2d.shape
    nt = n // MOE_TM
    ltri = jnp.asarray(np.tril(np.ones((MOE_TM, MOE_TM), np.float32), -1), BF16)
    h, meta, counts = _router_call(x2d, mod, tiles_per_mod, g, wr, br, ltri)

    cnt = counts[:, 0, :N_EXPERTS].astype(jnp.int32)
    seg_len = (cnt + SEG_ALIGN - 1) // SEG_ALIGN * SEG_ALIGN
    segoff = jnp.cumsum(seg_len, axis=1) - seg_len
    region = (jnp.sum(seg_len, axis=0) + MOE_TM - 1) // MOE_TM * MOE_TM
    region_start = jnp.cumsum(region) - region
    base = region_start[None, :] + jnp.cumsum(seg_len, axis=0) - seg_len
    rows_max = 2 * n + nt * N_EXPERTS * (SEG_ALIGN - 1) + N_EXPERTS * (MOE_TM - 1)
    n_sorted_tiles = (rows_max + MOE_TM - 1) // MOE_TM
    tile_end = jnp.cumsum(region // MOE_TM)
    total_tiles = tile_end[-1]
    jt = jnp.arange(n_sorted_tiles, dtype=jnp.int32)
    blk = jnp.minimum(jt, total_tiles - 1)
    tile_expert = jnp.sum((blk[:, None] >= tile_end[None, :]).astype(jnp.int32), axis=1)
    tile_sched = (tile_expert.astype(jnp.int32), blk.astype(jnp.int32), (jt < total_tiles).astype(jnp.int32))
    sched = tuple(a.reshape(-1).astype(jnp.int32) for a in (segoff, base, seg_len))
    segoff_v = jnp.pad(segoff.astype(F32), ((0, 0), (0, LANES - N_EXPERTS)))[:, None, :]

    rows = n_sorted_tiles * MOE_TM
    xs, gs = _compact_call(sched, h, meta, segoff_v, jnp.zeros((rows, d), BF16), jnp.zeros((rows, LANES), F32))
    y = _expert_call(tile_sched, xs, gs, w1, w3, w2, nf)
    return _combine_call(sched, x2d, mod, tiles_per_mod, meta, segoff_v, y)


def _rope_tables(t):
    rows = jnp.arange(t, dtype=F32) // GRID_W
    cols = jnp.arange(t, dtype=F32) % GRID_W

    def tables(rot_dim):
        a = rot_dim // 2
        inv = 1.0 / (ROPE_BASE ** (jnp.arange(0, a, 2, dtype=F32) / a))
        ar = rows[:, None] * inv
        ac = cols[:, None] * inv
        cos = jnp.concatenate([jnp.cos(ar), jnp.cos(ar), jnp.cos(ac), jnp.cos(ac)], axis=-1)
        sin = jnp.concatenate([-jnp.sin(ar), jnp.sin(ar), -jnp.sin(ac), jnp.sin(ac)], axis=-1)
        return cos, sin

    ca, sa = tables(HEAD_DIM)
    cos_a = jnp.tile(ca, (1, LANES // HEAD_DIM))
    sin_a = jnp.tile(sa, (1, LANES // HEAD_DIM))
    cc, sc = tables(C_ROPE)
    ones = jnp.ones((t, C_NOPE), F32)
    tail = C_SLOT - C_NOPE - C_ROPE
    cos_c = jnp.concatenate([ones, cc, jnp.ones((t, tail), F32)], axis=-1)
    sin_c = jnp.concatenate([0 * ones, sc, jnp.zeros((t, tail), F32)], axis=-1)
    return cos_a, sin_a, cos_c, sin_c


def _head_perm():
    order = []
    for j in range(A_HEADS // 2):
        order += [j, A_HEADS // 2 + j]
    return np.concatenate([np.arange(h * HEAD_DIM, (h + 1) * HEAD_DIM) for h in order])


def _segment_mean_matrix(widths, total):
    m = np.zeros((total, total), np.float32)
    o = 0
    while o < total:
        for w, used in widths:
            if used:
                m[o:o + w, o:o + w] = 1.0 / w
            o += w
    return jnp.asarray(m, BF16)


def _slot_vec(nope, rope):
    z = jnp.zeros((C_SLOT - C_NOPE - C_ROPE,), F32)
    n = jnp.zeros((C_NOPE,), F32) if nope is None else nope
    r = jnp.zeros((C_ROPE,), F32) if rope is None else rope
    return jnp.tile(jnp.concatenate([n, r, z]), C_HEADS)[None, :]


def _layer_consts(i, p, perm, tabs):
    w_in = p["w_in"][i]
    o_kr = A_Q + 2 * A_KV + 2 * B_CH + C_Q_RANK + C_KV_RANK
    d = w_in.shape[0]
    kr_cols = jnp.concatenate([jnp.zeros((d, C_NOPE), F32), w_in[:, o_kr:o_kr + C_ROPE],
                               jnp.zeros((d, C_SLOT - C_NOPE - C_ROPE), F32)], axis=1)
    win = jnp.concatenate([w_in[:, :A_Q][:, perm], w_in[:, A_Q:o_kr], kr_cols], axis=1).astype(BF16)

    w_uq = p["c_w_uq"][i].reshape(C_Q_RANK, C_HEADS, C_NOPE + C_ROPE)
    wuq = jnp.pad(w_uq, ((0, 0), (0, 0), (0, C_SLOT - C_NOPE - C_ROPE))).reshape(C_Q_RANK, C_HEADS * C_SLOT)
    w_ukv = p["c_w_ukv"][i].reshape(C_KV_RANK, C_HEADS, C_NOPE + C_V)
    wukvk = jnp.pad(w_ukv[:, :, :C_NOPE], ((0, 0), (0, 0), (0, C_SLOT - C_NOPE))).reshape(C_KV_RANK, -1)
    wukvv = w_ukv[:, :, C_NOPE:].reshape(C_KV_RANK, C_HEADS * C_V)

    sa = _segment_mean_matrix([(HEAD_DIM, True)], A_Q)
    sc = _segment_mean_matrix([(C_NOPE, True), (C_ROPE, True), (C_SLOT - C_NOPE - C_ROPE, False)], C_HEADS * C_SLOT)
    gq = jnp.tile(p["a_q_norm_g"][i] * A_SCALE, A_HEADS)[None, :]
    gk = jnp.tile(p["a_k_norm_g"][i], A_KV_HEADS)[None, :]
    gqc = _slot_vec(p["c_q_nope_norm_g"][i], p["c_q_rope_norm_g"][i]) * MLA_SCALE
    gkn = _slot_vec(p["c_k_nope_norm_g"][i], None)
    gkr = _slot_vec(None, p["c_k_rope_norm_g"][i])[:, :C_SLOT]
    return (p["mix_norm_g"][i][None, :], win) + tabs + (
        sa, sc, gq, gk, p["c_q_rank_norm_g"][i][None, :], p["c_kv_rank_norm_g"][i][None, :],
        wuq.astype(BF16), wukvk.astype(BF16), wukvv.astype(BF16), gqc, gkn, gkr)


def kernel(x, c, ctx, c_ctx, ada_w, ada_b, mix_norm_g, ffn_norm_g, w_in, w_out, a_q_norm_g, a_k_norm_g, a_sink, b_conv_w, b_conv_b, b_ln_g, b_ln_b, c_q_rank_norm_g, c_kv_rank_norm_g, c_w_uq, c_w_ukv, c_q_nope_norm_g, c_k_nope_norm_g, c_q_rope_norm_g, c_k_rope_norm_g, dense_w1, dense_w3, dense_w2, moe_router_w, moe_router_b, moe_w1, moe_w3, moe_w2):
    p = dict(w_in=w_in, c_w_uq=c_w_uq, c_w_ukv=c_w_ukv, a_q_norm_g=a_q_norm_g, a_k_norm_g=a_k_norm_g,
             c_q_nope_norm_g=c_q_nope_norm_g, c_k_nope_norm_g=c_k_nope_norm_g,
             c_q_rope_norm_g=c_q_rope_norm_g, c_k_rope_norm_g=c_k_rope_norm_g,
             c_q_rank_norm_g=c_q_rank_norm_g, c_kv_rank_norm_g=c_kv_rank_norm_g, mix_norm_g=mix_norm_g)
    bsz, t, d = x.shape
    ctx_len = ctx.shape[1]
    depth = ada_w.shape[0]
    n_x, n_c = bsz * t, bsz * ctx_len
    tm_pre = 256
    tm_tok = 512
    tq = 256

    ada_rows = ((bsz + 1 + 7) // 8) * 8
    c_pad = jnp.concatenate([c, c_ctx[None, :], jnp.zeros((ada_rows - bsz - 1, d), F32)], axis=0)
    mods = _ada_call(c_pad, ada_w, ada_b).reshape(depth, ada_rows, 6, d)
    mods = jnp.pad(mods, ((0, 0), (0, 0), (0, MOD_ROWS - 6), (0, 0)))

    tabs_x = _rope_tables(t)
    ones = jnp.ones((tm_pre, LANES), F32)
    tabs_c = (ones, 0 * ones, ones, 0 * ones)
    perm = _head_perm()

    x2 = x.reshape(n_x, d)
    c2 = ctx.reshape(n_c, d)
    for i in range(depth):
        last = i == depth - 1
        mod_x = mods[i, :bsz]
        mod_c = mods[i, bsz:bsz + 1]
        consts_x = _layer_consts(i, p, perm, tabs_x)
        consts_c = _layer_consts(i, p, perm, tabs_c)

        qa_x, ka_x, va_x, u_x, qc_x, kc_x, vc_x = _pre_call(x2, mod_x, t // tm_pre, t // tm_pre, consts_x, tm_pre)
        qa_c, ka_c, va_c, u_c, qc_c, kc_c, vc_c = _pre_call(c2, mod_c, n_c // tm_pre, 1, consts_c, tm_pre)

        def r3(a, length):
            return a.reshape(bsz, length, a.shape[-1])

        sink_rows = jnp.broadcast_to(a_sink[i][:, None, None], (A_HEADS, BLOCK, LANES)).reshape(-1, LANES)
        conv_w = jnp.pad(b_conv_w[i], ((0, 32 - B_WIDTH), (0, 0)))
        conv_p = (conv_w, b_conv_b[i][None, :], b_ln_g[i][None, :], b_ln_b[i][None, :])
        w_o = jnp.concatenate([w_out[i][:A_Q][perm], w_out[i][A_Q:]], axis=0).astype(BF16)

        o_a = _attn_a_call(qa_x, r3(ka_x, t), r3(va_x, t), r3(ka_c, ctx_len), r3(va_c, ctx_len), sink_rows, bsz, t)
        o_b = _conv_call(r3(u_x, t), *conv_p).reshape(n_x, B_CH)
        o_c = _mla_call(qc_x, r3(kc_x, t), r3(vc_x, t), r3(kc_c, ctx_len), r3(vc_c, ctx_len), bsz, t, tq)
        x2 = _out_call(x2, mod_x, t // tm_tok, o_a, o_b, o_c, w_o, tm_tok)
        if not last:
            oc_a = _attn_a_call(qa_c, None, None, r3(ka_c, ctx_len), r3(va_c, ctx_len), sink_rows, bsz, ctx_len)
            oc_b = _conv_call(r3(u_c, ctx_len), *conv_p).reshape(n_c, B_CH)
            oc_c = _mla_call(qc_c, None, None, r3(kc_c, ctx_len), r3(vc_c, ctx_len), bsz, ctx_len, ctx_len)
            c2 = _out_call(c2, mod_c, n_c // tm_tok, oc_a, oc_b, oc_c, w_o, tm_tok)

        j = i // 2
        g_ffn = ffn_norm_g[i][None, :]
        if i % 2 == 0:
            w = (dense_w1[j].astype(BF16), dense_w3[j].astype(BF16), dense_w2[j].astype(BF16))

            def ffn(a2, mod, tiles_per_mod, w=w, g_ffn=g_ffn):
                return _ffn_call(a2, mod, tiles_per_mod, g_ffn, *w, tm_tok, 2)
        else:
            w = (moe_w1[j].astype(BF16), moe_w3[j].astype(BF16), moe_w2[j].astype(BF16))
            wr = jnp.pad(moe_router_w[j], ((0, 0), (0, LANES - N_EXPERTS)))
            br = jnp.pad(moe_router_b[j], (0, LANES - N_EXPERTS))[None, :]

            def ffn(a2, mod, tiles_per_mod, w=w, g_ffn=g_ffn, wr=wr, br=br):
                return _moe_call(a2, mod, tiles_per_mod * (tm_tok // MOE_TM), g_ffn, wr, br, *w, 2)
        x2 = ffn(x2, mod_x, t // tm_tok)
        if not last:
            c2 = ffn(c2, mod_c, n_c // tm_tok)
    return x2.reshape(bsz, t, d)
```

```python
import functools

import jax
import jax.numpy as jnp
import numpy as np
from jax import lax
from jax.experimental import pallas as pl
from jax.experimental.pallas import tpu as pltpu

F32 = jnp.float32
BF16 = jnp.bfloat16

D_MODEL = 1024
GRID_W = 64
HEAD_DIM = 64
A_HEADS = 8
A_KV_HEADS = 2
A_WINDOW = 128
BLOCK = 128
B_CH = 256
B_WIDTH = 31
C_HEADS = 4
C_Q_RANK = 384
C_KV_RANK = 256
C_NOPE = 64
C_ROPE = 32
C_V = 64
A_Q = A_HEADS * HEAD_DIM
A_KV = A_KV_HEADS * HEAD_DIM
IN_COLS_PAD = 2048
D_FF = 2816
N_EXPERTS = 8
ROPE_BASE = 10000.0
EPS = 1e-6
NEG = -1e30
A_SCALE = HEAD_DIM ** -0.5
MLA_SCALE = (C_NOPE + C_ROPE) ** -0.5
LOG2E = 1.4426950408889634

LANES = 128
MOD_ROWS = 8
C_SLOT = 128
VMEM_LIMIT = 56 * 1024 * 1024


def _cparams(sem):
    return pltpu.CompilerParams(dimension_semantics=sem, vmem_limit_bytes=VMEM_LIMIT)


def _dot(a, b):
    return jnp.dot(a, b, preferred_element_type=F32)


def _dot_nt(a, b):
    return lax.dot_general(a, b, (((1,), (1,)), ((), ())), preferred_element_type=F32)


def _ada_kernel(c_ref, w_ref, b_ref, o_ref):
    c = c_ref[...]
    a = c * jax.nn.sigmoid(c)
    o_ref[0] = _dot(a.astype(BF16), w_ref[0].astype(BF16)) + b_ref[0]


def _ada_call(c_pad, ada_w, ada_b):
    depth, d, n6 = ada_w.shape
    rows = c_pad.shape[0]
    tn = 1536
    return pl.pallas_call(
        _ada_kernel,
        grid=(depth, n6 // tn),
        in_specs=[
            pl.BlockSpec((rows, d), lambda i, j: (0, 0)),
            pl.BlockSpec((1, d, tn), lambda i, j: (i, 0, j)),
            pl.BlockSpec((1, 1, tn), lambda i, j: (i, 0, j)),
        ],
        out_specs=pl.BlockSpec((1, rows, tn), lambda i, j: (i, 0, j)),
        out_shape=jax.ShapeDtypeStruct((depth, rows, n6), F32),
        compiler_params=_cparams(("parallel", "parallel")),
        name="ada_proj",
    )(c_pad, ada_w, ada_b.reshape(depth, 1, n6))


def _rope_chunk(c, cos, sin, half):
    lane = lax.broadcasted_iota(jnp.int32, c.shape, 1)
    lo = (lane & (2 * half - 1)) < half
    partner = jnp.where(lo, pltpu.roll(c, LANES - half, 1), pltpu.roll(c, half, 1))
    return c * cos + partner * sin


def _pre_kernel(x_ref, mod_ref, gmix_ref, win_ref, cosa_ref, sina_ref, cosc_ref, sinc_ref,
                sa_ref, sc_ref, gq_ref, gk_ref, gcq_ref, gckv_ref, wuq_ref, wukvk_ref, wukvv_ref,
                gqc_ref, gkn_ref, gkr_ref,
                qa_ref, ka_ref, va_ref, u_ref, qc_ref, kc_ref, vc_ref):
    x = x_ref[...]
    ms = jnp.mean(x * x, axis=-1, keepdims=True)
    y = x * lax.rsqrt(ms + EPS)
    shift = mod_ref[0, 0:1, :]
    scale = mod_ref[0, 1:2, :]
    h = (y * gmix_ref[...]) * (1.0 + scale) + shift
    p = _dot(h.astype(BF16), win_ref[...])

    cosa, sina = cosa_ref[...], sina_ref[...]
    cosc, sinc = cosc_ref[...], sinc_ref[...]

    qa = p[:, 0:A_Q]
    ssq = _dot((qa * qa).astype(BF16), sa_ref[...])
    qa = qa * lax.rsqrt(ssq + EPS) * gq_ref[...]
    for j in range(A_Q // LANES):
        sl = slice(j * LANES, (j + 1) * LANES)
        qa_ref[:, sl] = _rope_chunk(qa[:, sl], cosa, sina, HEAD_DIM // 4).astype(BF16)

    ka = p[:, A_Q:A_Q + A_KV]
    ssk = _dot((ka * ka).astype(BF16), sa_ref[0:A_KV, 0:A_KV])
    ka = ka * lax.rsqrt(ssk + EPS) * gk_ref[...]
    ka_ref[...] = _rope_chunk(ka, cosa, sina, HEAD_DIM // 4).astype(BF16)
    va_ref[...] = p[:, A_Q + A_KV:A_Q + 2 * A_KV].T.astype(BF16)

    o = A_Q + 2 * A_KV
    u_ref[...] = p[:, o:o + B_CH] * jax.nn.sigmoid(p[:, o + B_CH:o + 2 * B_CH])

    o = o + 2 * B_CH
    cq = p[:, o:o + C_Q_RANK]
    cq = cq * lax.rsqrt(jnp.mean(cq * cq, axis=-1, keepdims=True) + EPS) * gcq_ref[...]
    qc = _dot(cq.astype(BF16), wuq_ref[...])
    ssq = _dot((qc * qc).astype(BF16), sc_ref[...])
    qc = qc * lax.rsqrt(ssq + EPS) * gqc_ref[...]
    for j in range(C_HEADS):
        sl = slice(j * C_SLOT, (j + 1) * C_SLOT)
        qc_ref[:, sl] = _rope_chunk(qc[:, sl], cosc, sinc, C_ROPE // 4).astype(BF16)

    o = o + C_Q_RANK
    ckv = p[:, o:o + C_KV_RANK]
    ckv = (ckv * lax.rsqrt(jnp.mean(ckv * ckv, axis=-1, keepdims=True) + EPS) * gckv_ref[...]).astype(BF16)
    kn = _dot(ckv, wukvk_ref[...])
    vc_ref[...] = _dot(ckv, wukvv_ref[...]).T.astype(BF16)
    ssk = _dot((kn * kn).astype(BF16), sc_ref[...])
    kn = kn * lax.rsqrt(ssk + EPS) * gkn_ref[...]
    o = o + C_KV_RANK
    kr = p[:, o:o + C_SLOT]
    kr = kr * lax.rsqrt(jnp.sum(kr * kr, axis=-1, keepdims=True) * (1.0 / C_ROPE) + EPS) * gkr_ref[...]
    kr = _rope_chunk(kr, cosc, sinc, C_ROPE // 4)
    for j in range(C_HEADS):
        sl = slice(j * C_SLOT, (j + 1) * C_SLOT)
        kc_ref[:, sl] = (kn[:, sl] + kr).astype(BF16)


def _pre_call(x2d, mod, tiles_per_mod, tab_tiles, consts, tm):
    n, d = x2d.shape
    (gmix, win, cosa, sina, cosc, sinc, sa, sc, gq, gk, gcq, gckv, wuq, wukvk, wukvv, gqc, gkn, gkr) = consts

    def const(a):
        return pl.BlockSpec(a.shape, lambda i: (0,) * a.ndim)

    def tab(a):
        return pl.BlockSpec((tm, LANES), lambda i: (i % tab_tiles, 0))

    in_specs = [
        pl.BlockSpec((tm, d), lambda i: (i, 0)),
        pl.BlockSpec((1, MOD_ROWS, d), lambda i: (i // tiles_per_mod, 0, 0)),
        const(gmix), const(win), tab(cosa), tab(sina), tab(cosc), tab(sinc),
        const(sa), const(sc), const(gq), const(gk), const(gcq), const(gckv),
        const(wuq), const(wukvk), const(wukvv), const(gqc), const(gkn), const(gkr),
    ]
    widths = (A_Q, A_KV, A_KV, B_CH, C_HEADS * C_SLOT, C_HEADS * C_SLOT, C_HEADS * C_V)
    dtypes = (BF16, BF16, BF16, F32, BF16, BF16, BF16)
    out_specs = [pl.BlockSpec((tm, w), lambda i: (i, 0)) for w in widths]
    out_shape = [jax.ShapeDtypeStruct((n, w), dt) for w, dt in zip(widths, dtypes)]
    out_specs[2] = pl.BlockSpec((A_KV, tm), lambda i: (0, i))
    out_shape[2] = jax.ShapeDtypeStruct((A_KV, n), BF16)
    out_specs[6] = pl.BlockSpec((C_HEADS * C_V, tm), lambda i: (0, i))
    out_shape[6] = jax.ShapeDtypeStruct((C_HEADS * C_V, n), BF16)
    return pl.pallas_call(
        _pre_kernel,
        grid=(n // tm,),
        in_specs=in_specs,
        out_specs=out_specs,
        out_shape=out_shape,
        compiler_params=_cparams(("parallel",)),
        name="pre_attn",
    )(x2d, mod, gmix, win, cosa, sina, cosc, sinc, sa, sc, gq, gk, gcq, gckv, wuq, wukvk, wukvv,
      gqc, gkn, gkr)


ATTN_A_QBLOCKS = 4


ONES_ROWS = 16


def _attn_a_kernel(*refs, t, has_local, qblocks):
    if has_local:
        q_ref, k_ref, vt_ref, kc_ref, vct_ref, sink_ref, bias_ref, o_ref = refs
    else:
        q_ref, kc_ref, vct_ref, sink_ref, o_ref = refs
    nchunk = A_Q // LANES
    span = 3 * BLOCK
    lane = lax.broadcasted_iota(jnp.int32, (BLOCK, LANES), 1)
    row = lax.broadcasted_iota(jnp.int32, (A_KV, BLOCK), 0)
    zero = jnp.zeros((BLOCK, LANES), BF16)
    sink = sink_ref[...]
    kc = kc_ref[0]
    vct = vct_ref[...]
    for blk in range(qblocks):
        rows = slice(blk * BLOCK, (blk + 1) * BLOCK)
        chunks = [q_ref[rows, j * LANES:(j + 1) * LANES] for j in range(nchunk)]
        qs = jnp.concatenate([jnp.where(lane < HEAD_DIM, c, zero) for c in chunks]
                             + [jnp.where(lane >= HEAD_DIM, c, zero) for c in chunks], axis=0)
        if has_local:
            n = pl.program_id(1) * qblocks + blk
            start = pl.multiple_of(jnp.clip((n - 1) * BLOCK, 0, t - span), BLOCK)
            keys = jnp.concatenate([k_ref[0, pl.ds(start, span), :], kc], axis=0)
            vt = jnp.concatenate([vt_ref[:, pl.ds(start, span)], vct], axis=1)
        else:
            keys, vt = kc, vct
        s = _dot_nt(keys, qs)
        if has_local:
            s = jnp.concatenate([s[:span] + bias_ref[n - start // BLOCK], s[span:]], axis=0)
        m = jnp.maximum(jnp.max(s, axis=0, keepdims=True), sink)
        e = jnp.exp2(s - m).astype(BF16)
        vte = jnp.concatenate([vt, jnp.ones((ONES_ROWS, vt.shape[1]), BF16)], axis=0)
        acc = _dot(vte, e)
        den = acc[A_KV:A_KV + 1] + jnp.exp2(sink - m)
        out = acc[:A_KV] * (1.0 / den)
        for j in range(nchunk):
            x = jnp.where(row < HEAD_DIM, out[:, j * BLOCK:(j + 1) * BLOCK],
                          out[:, (nchunk + j) * BLOCK:(nchunk + j + 1) * BLOCK])
            o_ref[rows, j * LANES:(j + 1) * LANES] = x.T.astype(BF16)


def _window_bias():
    r = np.arange(BLOCK)[None, :]
    c = np.arange(3 * BLOCK)[:, None]
    pats = [np.where(np.abs(c - p * BLOCK - r) <= A_WINDOW, 0.0, NEG) for p in range(3)]
    return jnp.asarray(np.stack([np.tile(p, (1, A_HEADS)) for p in pats]), F32)


def _attn_a_call(qa, ka, vat, kac, vact, sink_row, bsz, t):
    has_local = ka is not None
    qblocks = min(ATTN_A_QBLOCKS, t // BLOCK)
    tq = qblocks * BLOCK
    nq = t // tq
    ctx_len = kac.shape[1]
    in_specs = [pl.BlockSpec((tq, A_Q), lambda b, n: (b * nq + n, 0))]
    args = [qa]
    if has_local:
        in_specs += [pl.BlockSpec((1, t, A_KV), lambda b, n: (b, 0, 0)),
                     pl.BlockSpec((A_KV, t), lambda b, n: (0, b))]
        args += [ka, vat]
    in_specs += [pl.BlockSpec((1, ctx_len, A_KV), lambda b, n: (b, 0, 0)),
                 pl.BlockSpec((A_KV, ctx_len), lambda b, n: (0, b))]
    in_specs += [pl.BlockSpec(sink_row.shape, lambda b, n: (0, 0))]
    args += [kac, vact, sink_row]
    if has_local:
        bias = _window_bias()
        in_specs += [pl.BlockSpec(bias.shape, lambda b, n: (0, 0, 0))]
        args += [bias]
    return pl.pallas_call(
        functools.partial(_attn_a_kernel, t=t, has_local=has_local, qblocks=qblocks),
        grid=(bsz, nq),
        in_specs=in_specs,
        out_specs=pl.BlockSpec((tq, A_Q), lambda b, n: (b * nq + n, 0)),
        out_shape=jax.ShapeDtypeStruct((bsz * t, A_Q), BF16),
        compiler_params=_cparams(("parallel", "parallel")),
        name="attn_a_local" if has_local else "attn_a_ctx",
    )(*args)


CONV_PAD = 16
CONV_CHUNK = 128
SUBLANES = 8


def _conv_kernel(u_ref, w_ref, b_ref, g_ref, beta_ref, o_ref, pad_ref, *, t):
    zeros = jnp.zeros((CONV_PAD, B_CH), F32)
    pad_ref[0:CONV_PAD, :] = zeros
    pad_ref[CONV_PAD + t:CONV_PAD + t + CONV_PAD, :] = zeros
    pad_ref[CONV_PAD:CONV_PAD + t, :] = u_ref[0]
    off = CONV_PAD - B_WIDTH // 2
    nq = (off + B_WIDTH - 1) // SUBLANES + 1
    win = CONV_CHUNK + (nq - 1) * SUBLANES
    for c in range(t // CONV_CHUNK):
        base = c * CONV_CHUNK
        acc = jnp.zeros((CONV_CHUNK, B_CH), F32)
        for r in range(SUBLANES):
            taps = [k for k in range(B_WIDTH) if (off + k) % SUBLANES == r]
            if not taps:
                continue
            w_r = pad_ref[base + r:base + r + win, :]
            part = None
            for k in taps:
                q = (off + k) // SUBLANES
                term = w_r[q * SUBLANES:q * SUBLANES + CONV_CHUNK, :] * w_ref[k:k + 1, :]
                part = term if part is None else part + term
            acc = acc + part
        y = acc + b_ref[...]
        mu = jnp.mean(y, axis=-1, keepdims=True)
        yc = y - mu
        var = jnp.mean(yc * yc, axis=-1, keepdims=True)
        z = yc * lax.rsqrt(var + EPS) * g_ref[...] + beta_ref[...]
        o_ref[0, c * CONV_CHUNK:(c + 1) * CONV_CHUNK, :] = (z * jax.nn.sigmoid(z)).astype(BF16)


def _conv_call(u3, w_pad, b, g, beta):
    bsz, t, ch = u3.shape

    def const(a):
        return pl.BlockSpec(a.shape, lambda i: (0, 0))

    return pl.pallas_call(
        functools.partial(_conv_kernel, t=t),
        grid=(bsz,),
        in_specs=[pl.BlockSpec((1, t, ch), lambda i: (i, 0, 0)), const(w_pad), const(b), const(g), const(beta)],
        out_specs=pl.BlockSpec((1, t, ch), lambda i: (i, 0, 0)),
        out_shape=jax.ShapeDtypeStruct((bsz, t, ch), BF16),
        scratch_shapes=[pltpu.VMEM((t + 2 * CONV_PAD, ch), F32)],
        compiler_params=_cparams(("parallel",)),
        name="conv_module",
    )(u3, w_pad, b, g, beta)


def _mla_kernel(*refs, has_local):
    if has_local:
        q_ref, kx_ref, vxt_ref, kc_ref, vct_ref, o_ref = refs
    else:
        q_ref, kc_ref, vct_ref, o_ref = refs
    def scores(h):
        sl = slice(h * C_SLOT, (h + 1) * C_SLOT)
        q = q_ref[:, sl]
        s_c = _dot_nt(kc_ref[0, :, sl], q)
        s_x = _dot_nt(kx_ref[0, :, sl], q) if has_local else None
        return s_c, s_x

    outs = []
    nxt = scores(0)
    for h in range(C_HEADS):
        vs = slice(h * C_V, (h + 1) * C_V)
        s_c, s_x = nxt
        if h + 1 < C_HEADS:
            nxt = scores(h + 1)
        m = jnp.max(s_c, axis=0, keepdims=True)
        if has_local:
            m = jnp.maximum(m, jnp.max(s_x, axis=0, keepdims=True))
        e_c = jnp.exp2(s_c - m).astype(BF16)
        vt = jnp.concatenate([vct_ref[vs, :], jnp.ones((ONES_ROWS, e_c.shape[0]), BF16)], axis=0)
        acc = _dot(vt, e_c)
        if has_local:
            e_x = jnp.exp2(s_x - m).astype(BF16)
            vt = jnp.concatenate([vxt_ref[vs, :], jnp.ones((ONES_ROWS, e_x.shape[0]), BF16)], axis=0)
            acc = acc + _dot(vt, e_x)
        outs.append(acc[:C_V] * (1.0 / acc[C_V:C_V + 1]))
    o_ref[...] = jnp.concatenate(outs, axis=0).T.astype(BF16)


def _mla_call(qc, kx, vxt, kcc, vcct, bsz, t, tq):
    has_local = kx is not None
    nq = t // tq
    ctx_len = kcc.shape[1]
    wq = C_HEADS * C_SLOT
    wv = C_HEADS * C_V
    in_specs = [pl.BlockSpec((tq, wq), lambda b, n: (b * nq + n, 0))]
    args = [qc]
    if has_local:
        in_specs += [pl.BlockSpec((1, t, wq), lambda b, n: (b, 0, 0)),
                     pl.BlockSpec((wv, t), lambda b, n: (0, b))]
        args += [kx, vxt]
    in_specs += [pl.BlockSpec((1, ctx_len, wq), lambda b, n: (b, 0, 0)),
                 pl.BlockSpec((wv, ctx_len), lambda b, n: (0, b))]
    args += [kcc, vcct]
    return pl.pallas_call(
        functools.partial(_mla_kernel, has_local=has_local),
        grid=(bsz, nq),
        in_specs=in_specs,
        out_specs=pl.BlockSpec((tq, wv), lambda b, n: (b * nq + n, 0)),
        out_shape=jax.ShapeDtypeStruct((bsz * t, wv), BF16),
        compiler_params=_cparams(("parallel", "parallel")),
        name="mla_local" if has_local else "mla_ctx",
    )(*args)


def _out_kernel(x_ref, mod_ref, oa_ref, ob_ref, oc_ref, w_ref, o_ref):
    y = _dot(oa_ref[...], w_ref[0:A_Q, :])
    y = y + _dot(ob_ref[...], w_ref[A_Q:A_Q + B_CH, :])
    y = y + _dot(oc_ref[...], w_ref[A_Q + B_CH:, :])
    o_ref[...] = x_ref[...] + mod_ref[0, 2:3, :] * y


def _out_call(x2d, mod, tiles_per_mod, oa, ob, oc, w_out, tm):
    n, d = x2d.shape
    return pl.pallas_call(
        _out_kernel,
        grid=(n // tm,),
        in_specs=[
            pl.BlockSpec((tm, d), lambda i: (i, 0)),
            pl.BlockSpec((1, MOD_ROWS, d), lambda i: (i // tiles_per_mod, 0, 0)),
            pl.BlockSpec((tm, oa.shape[1]), lambda i: (i, 0)),
            pl.BlockSpec((tm, ob.shape[1]), lambda i: (i, 0)),
            pl.BlockSpec((tm, oc.shape[1]), lambda i: (i, 0)),
            pl.BlockSpec(w_out.shape, lambda i: (0, 0)),
        ],
        out_specs=pl.BlockSpec((tm, d), lambda i: (i, 0)),
        out_shape=jax.ShapeDtypeStruct((n, d), F32),
        compiler_params=_cparams(("parallel",)),
        name="out_proj",
    )(x2d, mod, oa, ob, oc, w_out)


def _split_bf16(a):
    hi = a.astype(BF16)
    lo = (a - hi.astype(F32)).astype(BF16)
    return hi, lo


def _ffn_input(x_ref, mod_ref, g_ref):
    x = x_ref[...]
    ms = jnp.mean(x * x, axis=-1, keepdims=True)
    y = x * lax.rsqrt(ms + EPS)
    return (y * g_ref[...]) * (1.0 + mod_ref[0, 4:5, :]) + mod_ref[0, 3:4, :]


def _swiglu_partial(h, w1, w3, w2):
    a = _dot(h, w1)
    b = _dot(h, w3)
    g = (a * jax.nn.sigmoid(a) * b).astype(BF16)
    return _dot(g, w2)


def _ffn_kernel(x_ref, mod_ref, g_ref, w1_ref, w3_ref, w2_ref, o_ref, h_ref, acc_ref):
    f = pl.program_id(1)

    @pl.when(f == 0)
    def _():
        h_ref[...] = _ffn_input(x_ref, mod_ref, g_ref).astype(BF16)
        acc_ref[...] = jnp.zeros_like(acc_ref)

    acc_ref[...] += _swiglu_partial(h_ref[...], w1_ref[...], w3_ref[...], w2_ref[...])

    @pl.when(f == pl.num_programs(1) - 1)
    def _():
        o_ref[...] = x_ref[...] + mod_ref[0, 5:6, :] * acc_ref[...]


def _ffn_call(x2d, mod, tiles_per_mod, g, w1, w3, w2, tm, nf):
    n, d = x2d.shape
    ff = w1.shape[1]
    tf = ff // nf
    return pl.pallas_call(
        _ffn_kernel,
        grid=(n // tm, nf),
        in_specs=[
            pl.BlockSpec((tm, d), lambda i, f: (i, 0)),
            pl.BlockSpec((1, MOD_ROWS, d), lambda i, f: (i // tiles_per_mod, 0, 0)),
            pl.BlockSpec(g.shape, lambda i, f: (0, 0)),
            pl.BlockSpec((d, tf), lambda i, f: (0, f)),
            pl.BlockSpec((d, tf), lambda i, f: (0, f)),
            pl.BlockSpec((tf, d), lambda i, f: (f, 0)),
        ],
        out_specs=pl.BlockSpec((tm, d), lambda i, f: (i, 0)),
        out_shape=jax.ShapeDtypeStruct((n, d), F32),
        scratch_shapes=[pltpu.VMEM((tm, d), BF16), pltpu.VMEM((tm, d), F32)],
        compiler_params=_cparams(("parallel", "arbitrary")),
        name="ffn_dense",
    )(x2d, mod, g, w1, w3, w2)


MOE_TM = 512
SEG_ALIGN = 16
SEG_PIECES = (512, 256, 128, 64, 32, 16)
CBUF_ROWS = 2 * MOE_TM + N_EXPERTS * SEG_ALIGN
META_I1, META_I2, META_G1, META_G2, META_R1, META_R2 = range(6)


def _router_kernel(x_ref, mod_ref, g_ref, wr_ref, br_ref, ltri_ref, h_ref, meta_ref, cnt_ref):
    h = _ffn_input(x_ref, mod_ref, g_ref)
    h_ref[...] = h.astype(BF16)
    h_hi, h_lo = _split_bf16(h)
    w_hi, w_lo = _split_bf16(wr_ref[...])
    logits = _dot(h_hi, w_hi) + (_dot(h_lo, w_hi) + _dot(h_hi, w_lo)) + br_ref[...]
    lane = lax.broadcasted_iota(jnp.int32, logits.shape, 1).astype(F32)
    logits = jnp.where(lane < N_EXPERTS, logits, NEG)
    m1 = jnp.max(logits, axis=-1, keepdims=True)
    i1 = jnp.min(jnp.where(logits == m1, lane, float(LANES)), axis=-1, keepdims=True)
    rest = jnp.where(lane == i1, NEG, logits)
    m2 = jnp.max(rest, axis=-1, keepdims=True)
    i2 = jnp.min(jnp.where(rest == m2, lane, float(LANES)), axis=-1, keepdims=True)
    e2 = jnp.exp(m2 - m1)
    den = 1.0 + e2
    sel1 = jnp.where(lane == i1, 1.0, 0.0)
    sel2 = jnp.where(lane == i2, 1.0, 0.0)
    sel = sel1 + sel2
    before = _dot(ltri_ref[...], sel.astype(BF16))
    r1 = jnp.sum(before * sel1, axis=-1, keepdims=True)
    r2 = jnp.sum(before * sel2, axis=-1, keepdims=True)
    cnt_ref[0] = jnp.sum(sel, axis=0, keepdims=True)
    cols = (i1, i2, 1.0 / den, e2 / den, r1, r2)
    meta = jnp.zeros_like(logits)
    for k, col in enumerate(cols):
        meta = jnp.where(lane == k, col, meta)
    meta_ref[...] = meta


def _router_call(x2d, mod, tiles_per_mod, g, wr, br, ltri):
    n, d = x2d.shape
    tm = MOE_TM
    nt = n // tm
    return pl.pallas_call(
        _router_kernel,
        grid=(nt,),
        in_specs=[
            pl.BlockSpec((tm, d), lambda i: (i, 0)),
            pl.BlockSpec((1, MOD_ROWS, d), lambda i: (i // tiles_per_mod, 0, 0)),
            pl.BlockSpec(g.shape, lambda i: (0, 0)),
            pl.BlockSpec(wr.shape, lambda i: (0, 0)),
            pl.BlockSpec(br.shape, lambda i: (0, 0)),
            pl.BlockSpec(ltri.shape, lambda i: (0, 0)),
        ],
        out_specs=[
            pl.BlockSpec((tm, d), lambda i: (i, 0)),
            pl.BlockSpec((tm, LANES), lambda i: (i, 0)),
            pl.BlockSpec((1, 1, LANES), lambda i: (i, 0, 0)),
        ],
        out_shape=[
            jax.ShapeDtypeStruct((n, d), BF16),
            jax.ShapeDtypeStruct((n, LANES), F32),
            jax.ShapeDtypeStruct((nt, 1, LANES), F32),
        ],
        compiler_params=_cparams(("parallel",)),
        name="moe_router",
    )(x2d, mod, g, wr, br, ltri)


def _pair_slots(meta, segoff_row):
    lane = lax.broadcasted_iota(jnp.int32, meta.shape, 1).astype(F32)
    i1 = meta[:, META_I1:META_I1 + 1]
    i2 = meta[:, META_I2:META_I2 + 1]
    s1 = jnp.sum(jnp.where(lane == i1, segoff_row, 0.0), axis=-1, keepdims=True) + meta[:, META_R1:META_R1 + 1]
    s2 = jnp.sum(jnp.where(lane == i2, segoff_row, 0.0), axis=-1, keepdims=True) + meta[:, META_R2:META_R2 + 1]
    return s1, s2


def _segment_copies(src, dst, src_off, dst_off, length, sem):
    out = []
    for size in SEG_PIECES:
        done = (length // (2 * size)) * (2 * size)
        s = pl.multiple_of(src_off + done, SEG_ALIGN)
        t = pl.multiple_of(dst_off + done, SEG_ALIGN)
        cp = pltpu.make_async_copy(src.at[pl.ds(s, size)], dst.at[pl.ds(t, size)], sem)
        out.append(((length & size) != 0, cp))
    return out


def _run_copies(copies):
    for pred, cp in copies:
        @pl.when(pred)
        def _(cp=cp):
            cp.start()
    for pred, cp in copies:
        @pl.when(pred)
        def _(cp=cp):
            cp.wait()


def _compact_kernel(segoff_s, base_s, len_s, h_ref, meta_ref, segoff_ref, xs_in, gs_in, xs_out, gs_out,
                    cbuf, gbuf, sems):
    del xs_in, gs_in
    i = pl.program_id(0)
    meta = meta_ref[...]
    s1, s2 = _pair_slots(meta, segoff_ref[0])
    slot = lax.broadcasted_iota(jnp.int32, (MOE_TM, CBUF_ROWS), 1).astype(F32)
    p1 = jnp.where(slot == s1, 1.0, 0.0)
    p2 = jnp.where(slot == s2, 1.0, 0.0)
    tn = (((0,), (0,)), ((), ()))
    perm = (p1 + p2).astype(BF16)
    cbuf[...] = lax.dot_general(perm, h_ref[...], tn, preferred_element_type=F32).astype(BF16)
    lane = lax.broadcasted_iota(jnp.int32, meta.shape, 1)
    g1 = meta[:, META_G1:META_G1 + 1]
    g2 = meta[:, META_G2:META_G2 + 1]
    g1_hi = g1.astype(BF16).astype(F32)
    g2_hi = g2.astype(BF16).astype(F32)
    a1 = jnp.where(lane == 0, g1_hi, jnp.where(lane == 1, g1 - g1_hi, 0.0)).astype(BF16)
    a2 = jnp.where(lane == 0, g2_hi, jnp.where(lane == 1, g2 - g2_hi, 0.0)).astype(BF16)
    gbuf[...] = (lax.dot_general(p1.astype(BF16), a1, tn, preferred_element_type=F32)
                 + lax.dot_general(p2.astype(BF16), a2, tn, preferred_element_type=F32))
    copies = []
    for e in range(N_EXPERTS):
        k = i * N_EXPERTS + e
        copies += _segment_copies(cbuf, xs_out, segoff_s[k], base_s[k], len_s[k], sems.at[0])
        copies += _segment_copies(gbuf, gs_out, segoff_s[k], base_s[k], len_s[k], sems.at[1])
    _run_copies(copies)


def _compact_call(sched, h, meta, segoff_v, xs0, gs0):
    n, d = h.shape
    nt = n // MOE_TM
    grid_spec = pltpu.PrefetchScalarGridSpec(
        num_scalar_prefetch=3,
        grid=(nt,),
        in_specs=[
            pl.BlockSpec((MOE_TM, d), lambda i, *_: (i, 0)),
            pl.BlockSpec((MOE_TM, LANES), lambda i, *_: (i, 0)),
            pl.BlockSpec((1, 1, LANES), lambda i, *_: (i, 0, 0)),
            pl.BlockSpec(memory_space=pl.ANY),
            pl.BlockSpec(memory_space=pl.ANY),
        ],
        out_specs=[pl.BlockSpec(memory_space=pl.ANY), pl.BlockSpec(memory_space=pl.ANY)],
        scratch_shapes=[pltpu.VMEM((CBUF_ROWS, d), BF16), pltpu.VMEM((CBUF_ROWS, LANES), F32),
                        pltpu.SemaphoreType.DMA((2,))],
    )
    return pl.pallas_call(
        _compact_kernel,
        grid_spec=grid_spec,
        out_shape=[jax.ShapeDtypeStruct(xs0.shape, xs0.dtype), jax.ShapeDtypeStruct(gs0.shape, gs0.dtype)],
        input_output_aliases={6: 0, 7: 1},
        compiler_params=_cparams(("arbitrary",)),
        name="moe_compact",
    )(*sched, h, meta, segoff_v, xs0, gs0)


def _expert_kernel(exp_s, blk_s, valid_s, xs_ref, gs_ref, w1_ref, w3_ref, w2_ref, y_ref, acc_ref):
    del exp_s, blk_s
    j = pl.program_id(0)
    f = pl.program_id(1)

    @pl.when(valid_s[j] != 0)
    def _():
        @pl.when(f == 0)
        def _():
            acc_ref[...] = jnp.zeros_like(acc_ref)

        acc_ref[...] += _swiglu_partial(xs_ref[...], w1_ref[0], w3_ref[0], w2_ref[0])

        @pl.when(f == pl.num_programs(1) - 1)
        def _():
            gate = gs_ref[:, 0:1] + gs_ref[:, 1:2]
            y_ref[...] = (acc_ref[...] * gate).astype(BF16)

    @pl.when(valid_s[j] == 0)
    def _():
        y_ref[...] = jnp.zeros_like(y_ref)


def _expert_call(tile_sched, xs, gs, w1, w3, w2, nf):
    rows, d = xs.shape
    ff = w1.shape[2]
    tf = ff // nf

    def f_eff(j, f, valid_s):
        return jnp.where(valid_s[j] != 0, f, nf - 1)

    grid_spec = pltpu.PrefetchScalarGridSpec(
        num_scalar_prefetch=3,
        grid=(rows // MOE_TM, nf),
        in_specs=[
            pl.BlockSpec((MOE_TM, d), lambda j, f, e_s, b_s, v_s: (b_s[j], 0)),
            pl.BlockSpec((MOE_TM, LANES), lambda j, f, e_s, b_s, v_s: (b_s[j], 0)),
            pl.BlockSpec((1, d, tf), lambda j, f, e_s, b_s, v_s: (e_s[j], 0, f_eff(j, f, v_s))),
            pl.BlockSpec((1, d, tf), lambda j, f, e_s, b_s, v_s: (e_s[j], 0, f_eff(j, f, v_s))),
            pl.BlockSpec((1, tf, d), lambda j, f, e_s, b_s, v_s: (e_s[j], f_eff(j, f, v_s), 0)),
        ],
        out_specs=pl.BlockSpec((MOE_TM, d), lambda j, f, e_s, b_s, v_s: (j, 0)),
        scratch_shapes=[pltpu.VMEM((MOE_TM, d), F32)],
    )
    return pl.pallas_call(
        _expert_kernel,
        grid_spec=grid_spec,
        out_shape=jax.ShapeDtypeStruct((rows, d), BF16),
        compiler_params=_cparams(("arbitrary", "arbitrary")),
        name="moe_experts",
    )(*tile_sched, xs, gs, w1, w3, w2)


def _combine_kernel(segoff_s, base_s, len_s, x_ref, mod_ref, meta_ref, segoff_ref, y_hbm, o_ref, ybuf, sems):
    i = pl.program_id(0)

    @pl.when(i == 0)
    def _():
        ybuf[...] = jnp.zeros_like(ybuf)

    copies = []
    for e in range(N_EXPERTS):
        k = i * N_EXPERTS + e
        copies += _segment_copies(y_hbm, ybuf, base_s[k], segoff_s[k], len_s[k], sems.at[0])
    _run_copies(copies)
    s1, s2 = _pair_slots(meta_ref[...], segoff_ref[0])
    slot = lax.broadcasted_iota(jnp.int32, (MOE_TM, CBUF_ROWS), 1).astype(F32)
    pick = (jnp.where(slot == s1, 1.0, 0.0) + jnp.where(slot == s2, 1.0, 0.0)).astype(BF16)
    o_ref[...] = x_ref[...] + mod_ref[0, 5:6, :] * _dot(pick, ybuf[...])


def _combine_call(sched, x2d, mod, tiles_per_mod, meta, segoff_v, y):
    n, d = x2d.shape
    grid_spec = pltpu.PrefetchScalarGridSpec(
        num_scalar_prefetch=3,
        grid=(n // MOE_TM,),
        in_specs=[
            pl.BlockSpec((MOE_TM, d), lambda i, *_: (i, 0)),
            pl.BlockSpec((1, MOD_ROWS, d), lambda i, *_: (i // tiles_per_mod, 0, 0)),
            pl.BlockSpec((MOE_TM, LANES), lambda i, *_: (i, 0)),
            pl.BlockSpec((1, 1, LANES), lambda i, *_: (i, 0, 0)),
            pl.BlockSpec(memory_space=pl.ANY),
        ],
        out_specs=pl.BlockSpec((MOE_TM, d), lambda i, *_: (i, 0)),
        scratch_shapes=[pltpu.VMEM((CBUF_ROWS, d), BF16), pltpu.SemaphoreType.DMA((1,))],
    )
    return pl.pallas_call(
        _combine_kernel,
        grid_spec=grid_spec,
        out_shape=jax.ShapeDtypeStruct((n, d), F32),
        compiler_params=_cparams(("arbitrary",)),
        name="moe_combine",
    )(*sched, x2d, mod, meta, segoff_v, y)


def _moe_call(x2d, mod, tiles_per_mod, g, wr, br, w1, w3, w2, nf):
    n, d = x2d.shape
    nt = n // MOE_TM
    ltri = jnp.asarray(np.tril(np.ones((MOE_TM, MOE_TM), np.float32), -1), BF16)
    h, meta, counts = _router_call(x2d, mod, tiles_per_mod, g, wr, br, ltri)

    cnt = counts[:, 0, :N_EXPERTS].astype(jnp.int32)
    seg_len = (cnt + SEG_ALIGN - 1) // SEG_ALIGN * SEG_ALIGN
    segoff = jnp.cumsum(seg_len, axis=1) - seg_len
    region = (jnp.sum(seg_len, axis=0) + MOE_TM - 1) // MOE_TM * MOE_TM
    region_start = jnp.cumsum(region) - region
    base = region_start[None, :] + jnp.cumsum(seg_len, axis=0) - seg_len
    rows_max = 2 * n + nt * N_EXPERTS * (SEG_ALIGN - 1) + N_EXPERTS * (MOE_TM - 1)
    n_sorted_tiles = (rows_max + MOE_TM - 1) // MOE_TM
    tile_end = jnp.cumsum(region // MOE_TM)
    total_tiles = tile_end[-1]
    jt = jnp.arange(n_sorted_tiles, dtype=jnp.int32)
    blk = jnp.minimum(jt, total_tiles - 1)
    tile_expert = jnp.sum((blk[:, None] >= tile_end[None, :]).astype(jnp.int32), axis=1)
    tile_sched = (tile_expert.astype(jnp.int32), blk.astype(jnp.int32), (jt < total_tiles).astype(jnp.int32))
    sched = tuple(a.reshape(-1).astype(jnp.int32) for a in (segoff, base, seg_len))
    segoff_v = jnp.pad(segoff.astype(F32), ((0, 0), (0, LANES - N_EXPERTS)))[:, None, :]

    rows = n_sorted_tiles * MOE_TM
    xs, gs = _compact_call(sched, h, meta, segoff_v, jnp.zeros((rows, d), BF16), jnp.zeros((rows, LANES), F32))
    y = _expert_call(tile_sched, xs, gs, w1, w3, w2, nf)
    return _combine_call(sched, x2d, mod, tiles_per_mod, meta, segoff_v, y)


def _rope_tables(t):
    rows = jnp.arange(t, dtype=F32) // GRID_W
    cols = jnp.arange(t, dtype=F32) % GRID_W

    def tables(rot_dim):
        a = rot_dim // 2
        inv = 1.0 / (ROPE_BASE ** (jnp.arange(0, a, 2, dtype=F32) / a))
        ar = rows[:, None] * inv
        ac = cols[:, None] * inv
        cos = jnp.concatenate([jnp.cos(ar), jnp.cos(ar), jnp.cos(ac), jnp.cos(ac)], axis=-1)
        sin = jnp.concatenate([-jnp.sin(ar), jnp.sin(ar), -jnp.sin(ac), jnp.sin(ac)], axis=-1)
        return cos, sin

    ca, sa = tables(HEAD_DIM)
    cos_a = jnp.tile(ca, (1, LANES // HEAD_DIM))
    sin_a = jnp.tile(sa, (1, LANES // HEAD_DIM))
    cc, sc = tables(C_ROPE)
    ones = jnp.ones((t, C_NOPE), F32)
    tail = C_SLOT - C_NOPE - C_ROPE
    cos_c = jnp.concatenate([ones, cc, jnp.ones((t, tail), F32)], axis=-1)
    sin_c = jnp.concatenate([0 * ones, sc, jnp.zeros((t, tail), F32)], axis=-1)
    return cos_a, sin_a, cos_c, sin_c


def _head_perm():
    order = []
    for j in range(A_HEADS // 2):
        order += [j, A_HEADS // 2 + j]
    return np.concatenate([np.arange(h * HEAD_DIM, (h + 1) * HEAD_DIM) for h in order])


def _segment_mean_matrix(widths, total):
    m = np.zeros((total, total), np.float32)
    o = 0
    while o < total:
        for w, used in widths:
            if used:
                m[o:o + w, o:o + w] = 1.0 / w
            o += w
    return jnp.asarray(m, BF16)


def _slot_vec(nope, rope):
    z = jnp.zeros((C_SLOT - C_NOPE - C_ROPE,), F32)
    n = jnp.zeros((C_NOPE,), F32) if nope is None else nope
    r = jnp.zeros((C_ROPE,), F32) if rope is None else rope
    return jnp.tile(jnp.concatenate([n, r, z]), C_HEADS)[None, :]


def _layer_consts(i, p, perm, tabs):
    w_in = p["w_in"][i]
    o_kr = A_Q + 2 * A_KV + 2 * B_CH + C_Q_RANK + C_KV_RANK
    d = w_in.shape[0]
    kr_cols = jnp.concatenate([jnp.zeros((d, C_NOPE), F32), w_in[:, o_kr:o_kr + C_ROPE],
                               jnp.zeros((d, C_SLOT - C_NOPE - C_ROPE), F32)], axis=1)
    win = jnp.concatenate([w_in[:, :A_Q][:, perm], w_in[:, A_Q:o_kr], kr_cols], axis=1).astype(BF16)

    w_uq = p["c_w_uq"][i].reshape(C_Q_RANK, C_HEADS, C_NOPE + C_ROPE)
    wuq = jnp.pad(w_uq, ((0, 0), (0, 0), (0, C_SLOT - C_NOPE - C_ROPE))).reshape(C_Q_RANK, C_HEADS * C_SLOT)
    w_ukv = p["c_w_ukv"][i].reshape(C_KV_RANK, C_HEADS, C_NOPE + C_V)
    wukvk = jnp.pad(w_ukv[:, :, :C_NOPE], ((0, 0), (0, 0), (0, C_SLOT - C_NOPE))).reshape(C_KV_RANK, -1)
    wukvv = w_ukv[:, :, C_NOPE:].reshape(C_KV_RANK, C_HEADS * C_V)

    sa = _segment_mean_matrix([(HEAD_DIM, True)], A_Q)
    sc = _segment_mean_matrix([(C_NOPE, True), (C_ROPE, True), (C_SLOT - C_NOPE - C_ROPE, False)], C_HEADS * C_SLOT)
    gq = jnp.tile(p["a_q_norm_g"][i] * (A_SCALE * LOG2E), A_HEADS)[None, :]
    gk = jnp.tile(p["a_k_norm_g"][i], A_KV_HEADS)[None, :]
    gqc = _slot_vec(p["c_q_nope_norm_g"][i], p["c_q_rope_norm_g"][i]) * (MLA_SCALE * LOG2E)
    gkn = _slot_vec(p["c_k_nope_norm_g"][i], None)
    gkr = _slot_vec(None, p["c_k_rope_norm_g"][i])[:, :C_SLOT]
    return (p["mix_norm_g"][i][None, :], win) + tabs + (
        sa, sc, gq, gk, p["c_q_rank_norm_g"][i][None, :], p["c_kv_rank_norm_g"][i][None, :],
        wuq.astype(BF16), wukvk.astype(BF16), wukvv.astype(BF16), gqc, gkn, gkr)


def kernel(x, c, ctx, c_ctx, ada_w, ada_b, mix_norm_g, ffn_norm_g, w_in, w_out, a_q_norm_g, a_k_norm_g, a_sink, b_conv_w, b_conv_b, b_ln_g, b_ln_b, c_q_rank_norm_g, c_kv_rank_norm_g, c_w_uq, c_w_ukv, c_q_nope_norm_g, c_k_nope_norm_g, c_q_rope_norm_g, c_k_rope_norm_g, dense_w1, dense_w3, dense_w2, moe_router_w, moe_router_b, moe_w1, moe_w3, moe_w2):
    p = dict(w_in=w_in, c_w_uq=c_w_uq, c_w_ukv=c_w_ukv, a_q_norm_g=a_q_norm_g, a_k_norm_g=a_k_norm_g,
             c_q_nope_norm_g=c_q_nope_norm_g, c_k_nope_norm_g=c_k_nope_norm_g,
             c_q_rope_norm_g=c_q_rope_norm_g, c_k_rope_norm_g=c_k_rope_norm_g,
             c_q_rank_norm_g=c_q_rank_norm_g, c_kv_rank_norm_g=c_kv_rank_norm_g, mix_norm_g=mix_norm_g)
    bsz, t, d = x.shape
    ctx_len = ctx.shape[1]
    depth = ada_w.shape[0]
    n_x, n_c = bsz * t, bsz * ctx_len
    tm_pre = 256
    tm_tok = 512
    tq = min(1024, t)

    ada_rows = ((bsz + 1 + 7) // 8) * 8
    c_pad = jnp.concatenate([c, c_ctx[None, :], jnp.zeros((ada_rows - bsz - 1, d), F32)], axis=0)
    mods = _ada_call(c_pad, ada_w, ada_b).reshape(depth, ada_rows, 6, d)
    mods = jnp.pad(mods, ((0, 0), (0, 0), (0, MOD_ROWS - 6), (0, 0)))

    tabs_x = _rope_tables(t)
    ones = jnp.ones((tm_pre, LANES), F32)
    tabs_c = (ones, 0 * ones, ones, 0 * ones)
    perm = _head_perm()

    x2 = x.reshape(n_x, d)
    c2 = ctx.reshape(n_c, d)
    for i in range(depth):
        last = i == depth - 1
        mod_x = mods[i, :bsz]
        mod_c = mods[i, bsz:bsz + 1]
        consts_x = _layer_consts(i, p, perm, tabs_x)
        consts_c = _layer_consts(i, p, perm, tabs_c)

        qa_x, ka_x, va_x, u_x, qc_x, kc_x, vc_x = _pre_call(x2, mod_x, t // tm_pre, t // tm_pre, consts_x, tm_pre)
        qa_c, ka_c, va_c, u_c, qc_c, kc_c, vc_c = _pre_call(c2, mod_c, n_c // tm_pre, 1, consts_c, tm_pre)

        def r3(a, length):
            return a.reshape(bsz, length, a.shape[-1])

        sink_row = jnp.repeat(a_sink[i] * LOG2E, BLOCK)[None, :]
        conv_w = jnp.pad(b_conv_w[i], ((0, 32 - B_WIDTH), (0, 0)))
        conv_p = (conv_w, b_conv_b[i][None, :], b_ln_g[i][None, :], b_ln_b[i][None, :])
        w_o = jnp.concatenate([w_out[i][:A_Q][perm], w_out[i][A_Q:]], axis=0).astype(BF16)

        o_a = _attn_a_call(qa_x, r3(ka_x, t), va_x, r3(ka_c, ctx_len), va_c, sink_row, bsz, t)
        o_b = _conv_call(r3(u_x, t), *conv_p).reshape(n_x, B_CH)
        o_c = _mla_call(qc_x, r3(kc_x, t), vc_x, r3(kc_c, ctx_len), vc_c, bsz, t, tq)
        x2 = _out_call(x2, mod_x, t // tm_tok, o_a, o_b, o_c, w_o, tm_tok)
        if not last:
            oc_a = _attn_a_call(qa_c, None, None, r3(ka_c, ctx_len), va_c, sink_row, bsz, ctx_len)
            oc_b = _conv_call(r3(u_c, ctx_len), *conv_p).reshape(n_c, B_CH)
            oc_c = _mla_call(qc_c, None, None, r3(kc_c, ctx_len), vc_c, bsz, ctx_len, ctx_len)
            c2 = _out_call(c2, mod_c, n_c // tm_tok, oc_a, oc_b, oc_c, w_o, tm_tok)

        j = i // 2
        g_ffn = ffn_norm_g[i][None, :]
        if i % 2 == 0:
            w = (dense_w1[j].astype(BF16), dense_w3[j].astype(BF16), dense_w2[j].astype(BF16))

            def ffn(a2, mod, tiles_per_mod, w=w, g_ffn=g_ffn):
                return _ffn_call(a2, mod, tiles_per_mod, g_ffn, *w, tm_tok, 2)
        else:
            w = (moe_w1[j].astype(BF16), moe_w3[j].astype(BF16), moe_w2[j].astype(BF16))
            wr = jnp.pad(moe_router_w[j], ((0, 0), (0, LANES - N_EXPERTS)))
            br = jnp.pad(moe_router_b[j], (0, LANES - N_EXPERTS))[None, :]

            def ffn(a2, mod, tiles_per_mod, w=w, g_ffn=g_ffn, wr=wr, br=br):
                return _moe_call(a2, mod, tiles_per_mod * (tm_tok // MOE_TM), g_ffn, wr, br, *w, 2)
        x2 = ffn(x2, mod_x, t // tm_tok)
        if not last:
            c2 = ffn(c2, mod_c, n_c // tm_tok)
    return x2.reshape(bsz, t, d)
```

```python
import functools

import jax
import jax.numpy as jnp
import numpy as np
from jax import lax
from jax.experimental import pallas as pl
from jax.experimental.pallas import tpu as pltpu

F32 = jnp.float32
BF16 = jnp.bfloat16

D_MODEL = 1024
GRID_W = 64
HEAD_DIM = 64
A_HEADS = 8
A_KV_HEADS = 2
A_WINDOW = 128
BLOCK = 128
B_CH = 256
B_WIDTH = 31
C_HEADS = 4
C_Q_RANK = 384
C_KV_RANK = 256
C_NOPE = 64
C_ROPE = 32
C_V = 64
A_Q = A_HEADS * HEAD_DIM
A_KV = A_KV_HEADS * HEAD_DIM
IN_COLS_PAD = 2048
D_FF = 2816
N_EXPERTS = 8
ROPE_BASE = 10000.0
EPS = 1e-6
NEG = -1e30
A_SCALE = HEAD_DIM ** -0.5
MLA_SCALE = (C_NOPE + C_ROPE) ** -0.5
LOG2E = 1.4426950408889634

LANES = 128
MOD_ROWS = 8
C_SLOT = 128
VMEM_LIMIT = 56 * 1024 * 1024


def _cparams(sem):
    return pltpu.CompilerParams(dimension_semantics=sem, vmem_limit_bytes=VMEM_LIMIT)


def _dot(a, b):
    return jnp.dot(a, b, preferred_element_type=F32)


def _dot_nt(a, b):
    return lax.dot_general(a, b, (((1,), (1,)), ((), ())), preferred_element_type=F32)


def _ada_kernel(c_ref, w_ref, b_ref, o_ref):
    c = c_ref[...]
    a = c * jax.nn.sigmoid(c)
    o_ref[0] = _dot(a.astype(BF16), w_ref[0].astype(BF16)) + b_ref[0]


def _ada_call(c_pad, ada_w, ada_b):
    depth, d, n6 = ada_w.shape
    rows = c_pad.shape[0]
    tn = 1536
    return pl.pallas_call(
        _ada_kernel,
        grid=(depth, n6 // tn),
        in_specs=[
            pl.BlockSpec((rows, d), lambda i, j: (0, 0)),
            pl.BlockSpec((1, d, tn), lambda i, j: (i, 0, j)),
            pl.BlockSpec((1, 1, tn), lambda i, j: (i, 0, j)),
        ],
        out_specs=pl.BlockSpec((1, rows, tn), lambda i, j: (i, 0, j)),
        out_shape=jax.ShapeDtypeStruct((depth, rows, n6), F32),
        compiler_params=_cparams(("parallel", "parallel")),
        name="ada_proj",
    )(c_pad, ada_w, ada_b.reshape(depth, 1, n6))


def _rope_chunk(c, cos, sin, half):
    lane = lax.broadcasted_iota(jnp.int32, c.shape, 1)
    lo = (lane & (2 * half - 1)) < half
    partner = jnp.where(lo, pltpu.roll(c, LANES - half, 1), pltpu.roll(c, half, 1))
    return c * cos + partner * sin


def _pre_kernel(x_ref, mod_ref, gmix_ref, win_ref, cosa_ref, sina_ref, cosc_ref, sinc_ref,
                sa_ref, sc_ref, gq_ref, gk_ref, gcq_ref, gckv_ref, wuq_ref, wukvk_ref, wukvv_ref,
                gqc_ref, gkn_ref, gkr_ref,
                qa_ref, ka_ref, va_ref, u_ref, qc_ref, kc_ref, vc_ref):
    x = x_ref[...]
    ms = jnp.mean(x * x, axis=-1, keepdims=True)
    y = x * lax.rsqrt(ms + EPS)
    shift = mod_ref[0, 0:1, :]
    scale = mod_ref[0, 1:2, :]
    h = (y * gmix_ref[...]) * (1.0 + scale) + shift
    p = _dot(h.astype(BF16), win_ref[...])

    cosa, sina = cosa_ref[...], sina_ref[...]
    cosc, sinc = cosc_ref[...], sinc_ref[...]

    qa = p[:, 0:A_Q]
    ssq = _dot((qa * qa).astype(BF16), sa_ref[...])
    qa = qa * lax.rsqrt(ssq + EPS) * gq_ref[...]
    for j in range(A_Q // LANES):
        sl = slice(j * LANES, (j + 1) * LANES)
        qa_ref[:, sl] = _rope_chunk(qa[:, sl], cosa, sina, HEAD_DIM // 4).astype(BF16)

    ka = p[:, A_Q:A_Q + A_KV]
    ssk = _dot((ka * ka).astype(BF16), sa_ref[0:A_KV, 0:A_KV])
    ka = ka * lax.rsqrt(ssk + EPS) * gk_ref[...]
    ka_ref[...] = _rope_chunk(ka, cosa, sina, HEAD_DIM // 4).astype(BF16)
    va_ref[...] = p[:, A_Q + A_KV:A_Q + 2 * A_KV].T.astype(BF16)

    o = A_Q + 2 * A_KV
    u_ref[...] = p[:, o:o + B_CH] * jax.nn.sigmoid(p[:, o + B_CH:o + 2 * B_CH])

    o = o + 2 * B_CH
    cq = p[:, o:o + C_Q_RANK]
    cq = cq * lax.rsqrt(jnp.mean(cq * cq, axis=-1, keepdims=True) + EPS) * gcq_ref[...]
    qc = _dot(cq.astype(BF16), wuq_ref[...])
    ssq = _dot((qc * qc).astype(BF16), sc_ref[...])
    qc = qc * lax.rsqrt(ssq + EPS) * gqc_ref[...]
    for j in range(C_HEADS):
        sl = slice(j * C_SLOT, (j + 1) * C_SLOT)
        qc_ref[:, sl] = _rope_chunk(qc[:, sl], cosc, sinc, C_ROPE // 4).astype(BF16)

    o = o + C_Q_RANK
    ckv = p[:, o:o + C_KV_RANK]
    ckv = (ckv * lax.rsqrt(jnp.mean(ckv * ckv, axis=-1, keepdims=True) + EPS) * gckv_ref[...]).astype(BF16)
    kn = _dot(ckv, wukvk_ref[...])
    vc_ref[...] = _dot(ckv, wukvv_ref[...]).T.astype(BF16)
    ssk = _dot((kn * kn).astype(BF16), sc_ref[...])
    kn = kn * lax.rsqrt(ssk + EPS) * gkn_ref[...]
    o = o + C_KV_RANK
    kr = p[:, o:o + C_SLOT]
    kr = kr * lax.rsqrt(jnp.sum(kr * kr, axis=-1, keepdims=True) * (1.0 / C_ROPE) + EPS) * gkr_ref[...]
    kr = _rope_chunk(kr, cosc, sinc, C_ROPE // 4)
    for j in range(C_HEADS):
        sl = slice(j * C_SLOT, (j + 1) * C_SLOT)
        kc_ref[:, sl] = (kn[:, sl] + kr).astype(BF16)


def _pre_call(x2d, mod, tiles_per_mod, tab_tiles, consts, tm):
    n, d = x2d.shape
    (gmix, win, cosa, sina, cosc, sinc, sa, sc, gq, gk, gcq, gckv, wuq, wukvk, wukvv, gqc, gkn, gkr) = consts

    def const(a):
        return pl.BlockSpec(a.shape, lambda i: (0,) * a.ndim)

    def tab(a):
        return pl.BlockSpec((tm, LANES), lambda i: (i % tab_tiles, 0))

    in_specs = [
        pl.BlockSpec((tm, d), lambda i: (i, 0)),
        pl.BlockSpec((1, MOD_ROWS, d), lambda i: (i // tiles_per_mod, 0, 0)),
        const(gmix), const(win), tab(cosa), tab(sina), tab(cosc), tab(sinc),
        const(sa), const(sc), const(gq), const(gk), const(gcq), const(gckv),
        const(wuq), const(wukvk), const(wukvv), const(gqc), const(gkn), const(gkr),
    ]
    widths = (A_Q, A_KV, A_KV, B_CH, C_HEADS * C_SLOT, C_HEADS * C_SLOT, C_HEADS * C_V)
    dtypes = (BF16, BF16, BF16, F32, BF16, BF16, BF16)
    out_specs = [pl.BlockSpec((tm, w), lambda i: (i, 0)) for w in widths]
    out_shape = [jax.ShapeDtypeStruct((n, w), dt) for w, dt in zip(widths, dtypes)]
    out_specs[2] = pl.BlockSpec((A_KV, tm), lambda i: (0, i))
    out_shape[2] = jax.ShapeDtypeStruct((A_KV, n), BF16)
    out_specs[6] = pl.BlockSpec((C_HEADS * C_V, tm), lambda i: (0, i))
    out_shape[6] = jax.ShapeDtypeStruct((C_HEADS * C_V, n), BF16)
    return pl.pallas_call(
        _pre_kernel,
        grid=(n // tm,),
        in_specs=in_specs,
        out_specs=out_specs,
        out_shape=out_shape,
        compiler_params=_cparams(("parallel",)),
        name="pre_attn",
    )(x2d, mod, gmix, win, cosa, sina, cosc, sinc, sa, sc, gq, gk, gcq, gckv, wuq, wukvk, wukvv,
      gqc, gkn, gkr)


ATTN_A_QBLOCKS = 4


ONES_ROWS = 16


def _attn_a_kernel(*refs, t, has_local, qblocks):
    if has_local:
        q_ref, k_ref, vt_ref, kc_ref, vct_ref, sink_ref, bias_ref, o_ref = refs
    else:
        q_ref, kc_ref, vct_ref, sink_ref, o_ref = refs
    nchunk = A_Q // LANES
    span = 3 * BLOCK
    lane = lax.broadcasted_iota(jnp.int32, (BLOCK, LANES), 1)
    row = lax.broadcasted_iota(jnp.int32, (A_KV, BLOCK), 0)
    zero = jnp.zeros((BLOCK, LANES), BF16)
    sink = sink_ref[...]
    kc = kc_ref[0]
    vct = vct_ref[...]
    for blk in range(qblocks):
        rows = slice(blk * BLOCK, (blk + 1) * BLOCK)
        chunks = [q_ref[rows, j * LANES:(j + 1) * LANES] for j in range(nchunk)]
        qs = jnp.concatenate([jnp.where(lane < HEAD_DIM, c, zero) for c in chunks]
                             + [jnp.where(lane >= HEAD_DIM, c, zero) for c in chunks], axis=0)
        if has_local:
            n = pl.program_id(1) * qblocks + blk
            start = pl.multiple_of(jnp.clip((n - 1) * BLOCK, 0, t - span), BLOCK)
            keys = jnp.concatenate([k_ref[0, pl.ds(start, span), :], kc], axis=0)
            vt = jnp.concatenate([vt_ref[:, pl.ds(start, span)], vct], axis=1)
        else:
            keys, vt = kc, vct
        s = _dot_nt(keys, qs)
        if has_local:
            s = jnp.concatenate([s[:span] + bias_ref[n - start // BLOCK], s[span:]], axis=0)
        m = jnp.maximum(jnp.max(s, axis=0, keepdims=True), sink)
        e = jnp.exp2(s - m).astype(BF16)
        vte = jnp.concatenate([vt, jnp.ones((ONES_ROWS, vt.shape[1]), BF16)], axis=0)
        acc = _dot(vte, e)
        den = acc[A_KV:A_KV + 1] + jnp.exp2(sink - m)
        out = acc[:A_KV] * (1.0 / den)
        for j in range(nchunk):
            x = jnp.where(row < HEAD_DIM, out[:, j * BLOCK:(j + 1) * BLOCK],
                          out[:, (nchunk + j) * BLOCK:(nchunk + j + 1) * BLOCK])
            o_ref[rows, j * LANES:(j + 1) * LANES] = x.T.astype(BF16)


def _window_bias():
    r = np.arange(BLOCK)[None, :]
    c = np.arange(3 * BLOCK)[:, None]
    pats = [np.where(np.abs(c - p * BLOCK - r) <= A_WINDOW, 0.0, NEG) for p in range(3)]
    return jnp.asarray(np.stack([np.tile(p, (1, A_HEADS)) for p in pats]), F32)


def _attn_a_call(qa, ka, vat, kac, vact, sink_row, bsz, t):
    has_local = ka is not None
    qblocks = min(ATTN_A_QBLOCKS, t // BLOCK)
    tq = qblocks * BLOCK
    nq = t // tq
    ctx_len = kac.shape[1]
    in_specs = [pl.BlockSpec((tq, A_Q), lambda b, n: (b * nq + n, 0))]
    args = [qa]
    if has_local:
        in_specs += [pl.BlockSpec((1, t, A_KV), lambda b, n: (b, 0, 0)),
                     pl.BlockSpec((A_KV, t), lambda b, n: (0, b))]
        args += [ka, vat]
    in_specs += [pl.BlockSpec((1, ctx_len, A_KV), lambda b, n: (b, 0, 0)),
                 pl.BlockSpec((A_KV, ctx_len), lambda b, n: (0, b))]
    in_specs += [pl.BlockSpec(sink_row.shape, lambda b, n: (0, 0))]
    args += [kac, vact, sink_row]
    if has_local:
        bias = _window_bias()
        in_specs += [pl.BlockSpec(bias.shape, lambda b, n: (0, 0, 0))]
        args += [bias]
    return pl.pallas_call(
        functools.partial(_attn_a_kernel, t=t, has_local=has_local, qblocks=qblocks),
        grid=(bsz, nq),
        in_specs=in_specs,
        out_specs=pl.BlockSpec((tq, A_Q), lambda b, n: (b * nq + n, 0)),
        out_shape=jax.ShapeDtypeStruct((bsz * t, A_Q), BF16),
        compiler_params=_cparams(("parallel", "parallel")),
        name="attn_a_local" if has_local else "attn_a_ctx",
    )(*args)


CONV_PAD = 16
CONV_CHUNK = 128
SUBLANES = 8


def _conv_kernel(u_ref, w_ref, b_ref, g_ref, beta_ref, o_ref, pad_ref, *, t):
    zeros = jnp.zeros((CONV_PAD, B_CH), F32)
    pad_ref[0:CONV_PAD, :] = zeros
    pad_ref[CONV_PAD + t:CONV_PAD + t + CONV_PAD, :] = zeros
    pad_ref[CONV_PAD:CONV_PAD + t, :] = u_ref[0]
    off = CONV_PAD - B_WIDTH // 2
    nq = (off + B_WIDTH - 1) // SUBLANES + 1
    win = CONV_CHUNK + (nq - 1) * SUBLANES
    for c in range(t // CONV_CHUNK):
        base = c * CONV_CHUNK
        acc = jnp.zeros((CONV_CHUNK, B_CH), F32)
        for r in range(SUBLANES):
            taps = [k for k in range(B_WIDTH) if (off + k) % SUBLANES == r]
            if not taps:
                continue
            w_r = pad_ref[base + r:base + r + win, :]
            part = None
            for k in taps:
                q = (off + k) // SUBLANES
                term = w_r[q * SUBLANES:q * SUBLANES + CONV_CHUNK, :] * w_ref[k:k + 1, :]
                part = term if part is None else part + term
            acc = acc + part
        y = acc + b_ref[...]
        mu = jnp.mean(y, axis=-1, keepdims=True)
        yc = y - mu
        var = jnp.mean(yc * yc, axis=-1, keepdims=True)
        z = yc * lax.rsqrt(var + EPS) * g_ref[...] + beta_ref[...]
        o_ref[0, c * CONV_CHUNK:(c + 1) * CONV_CHUNK, :] = (z * jax.nn.sigmoid(z)).astype(BF16)


def _conv_call(u3, w_pad, b, g, beta):
    bsz, t, ch = u3.shape

    def const(a):
        return pl.BlockSpec(a.shape, lambda i: (0, 0))

    return pl.pallas_call(
        functools.partial(_conv_kernel, t=t),
        grid=(bsz,),
        in_specs=[pl.BlockSpec((1, t, ch), lambda i: (i, 0, 0)), const(w_pad), const(b), const(g), const(beta)],
        out_specs=pl.BlockSpec((1, t, ch), lambda i: (i, 0, 0)),
        out_shape=jax.ShapeDtypeStruct((bsz, t, ch), BF16),
        scratch_shapes=[pltpu.VMEM((t + 2 * CONV_PAD, ch), F32)],
        compiler_params=_cparams(("parallel",)),
        name="conv_module",
    )(u3, w_pad, b, g, beta)


def _mla_kernel(*refs, has_local):
    if has_local:
        q_ref, kx_ref, vxt_ref, kc_ref, vct_ref, o_ref = refs
    else:
        q_ref, kc_ref, vct_ref, o_ref = refs
    def scores(h):
        sl = slice(h * C_SLOT, (h + 1) * C_SLOT)
        q = q_ref[:, sl]
        s_c = _dot_nt(kc_ref[0, :, sl], q)
        s_x = _dot_nt(kx_ref[0, :, sl], q) if has_local else None
        return s_c, s_x

    outs = []
    nxt = scores(0)
    for h in range(C_HEADS):
        vs = slice(h * C_V, (h + 1) * C_V)
        s_c, s_x = nxt
        if h + 1 < C_HEADS:
            nxt = scores(h + 1)
        m = jnp.max(s_c, axis=0, keepdims=True)
        if has_local:
            m = jnp.maximum(m, jnp.max(s_x, axis=0, keepdims=True))
        e_c = jnp.exp2(s_c - m).astype(BF16)
        vt = jnp.concatenate([vct_ref[vs, :], jnp.ones((ONES_ROWS, e_c.shape[0]), BF16)], axis=0)
        acc = _dot(vt, e_c)
        if has_local:
            e_x = jnp.exp2(s_x - m).astype(BF16)
            vt = jnp.concatenate([vxt_ref[vs, :], jnp.ones((ONES_ROWS, e_x.shape[0]), BF16)], axis=0)
            acc = acc + _dot(vt, e_x)
        outs.append(acc[:C_V] * (1.0 / acc[C_V:C_V + 1]))
    o_ref[...] = jnp.concatenate(outs, axis=0).T.astype(BF16)


def _mla_call(qc, kx, vxt, kcc, vcct, bsz, t, tq):
    has_local = kx is not None
    nq = t // tq
    ctx_len = kcc.shape[1]
    wq = C_HEADS * C_SLOT
    wv = C_HEADS * C_V
    in_specs = [pl.BlockSpec((tq, wq), lambda b, n: (b * nq + n, 0))]
    args = [qc]
    if has_local:
        in_specs += [pl.BlockSpec((1, t, wq), lambda b, n: (b, 0, 0)),
                     pl.BlockSpec((wv, t), lambda b, n: (0, b))]
        args += [kx, vxt]
    in_specs += [pl.BlockSpec((1, ctx_len, wq), lambda b, n: (b, 0, 0)),
                 pl.BlockSpec((wv, ctx_len), lambda b, n: (0, b))]
    args += [kcc, vcct]
    return pl.pallas_call(
        functools.partial(_mla_kernel, has_local=has_local),
        grid=(bsz, nq),
        in_specs=in_specs,
        out_specs=pl.BlockSpec((tq, wv), lambda b, n: (b * nq + n, 0)),
        out_shape=jax.ShapeDtypeStruct((bsz * t, wv), BF16),
        compiler_params=_cparams(("parallel", "parallel")),
        name="mla_local" if has_local else "mla_ctx",
    )(*args)


def _out_kernel(x_ref, mod_ref, oa_ref, ob_ref, oc_ref, w_ref, o_ref):
    y = _dot(oa_ref[...], w_ref[0:A_Q, :])
    y = y + _dot(ob_ref[...], w_ref[A_Q:A_Q + B_CH, :])
    y = y + _dot(oc_ref[...], w_ref[A_Q + B_CH:, :])
    o_ref[...] = x_ref[...] + mod_ref[0, 2:3, :] * y


def _out_call(x2d, mod, tiles_per_mod, oa, ob, oc, w_out, tm):
    n, d = x2d.shape
    return pl.pallas_call(
        _out_kernel,
        grid=(n // tm,),
        in_specs=[
            pl.BlockSpec((tm, d), lambda i: (i, 0)),
            pl.BlockSpec((1, MOD_ROWS, d), lambda i: (i // tiles_per_mod, 0, 0)),
            pl.BlockSpec((tm, oa.shape[1]), lambda i: (i, 0)),
            pl.BlockSpec((tm, ob.shape[1]), lambda i: (i, 0)),
            pl.BlockSpec((tm, oc.shape[1]), lambda i: (i, 0)),
            pl.BlockSpec(w_out.shape, lambda i: (0, 0)),
        ],
        out_specs=pl.BlockSpec((tm, d), lambda i: (i, 0)),
        out_shape=jax.ShapeDtypeStruct((n, d), F32),
        compiler_params=_cparams(("parallel",)),
        name="out_proj",
    )(x2d, mod, oa, ob, oc, w_out)


def _split_bf16(a):
    hi = a.astype(BF16)
    lo = (a - hi.astype(F32)).astype(BF16)
    return hi, lo


def _ffn_input(x_ref, mod_ref, g_ref):
    x = x_ref[...]
    ms = jnp.mean(x * x, axis=-1, keepdims=True)
    y = x * lax.rsqrt(ms + EPS)
    return (y * g_ref[...]) * (1.0 + mod_ref[0, 4:5, :]) + mod_ref[0, 3:4, :]


MXU_TILE = 256


def _ff_chunks(ff):
    tiles = ff // MXU_TILE
    if ff % MXU_TILE or tiles < 2:
        return (ff,)
    first = (tiles // 2) * MXU_TILE
    return (first, ff - first)


def _swiglu(h, w1_ref, w3_ref, w2_ref):
    y = None
    o = 0
    for tf in _ff_chunks(w1_ref.shape[1]):
        a = _dot(h, w1_ref[:, o:o + tf])
        b = _dot(h, w3_ref[:, o:o + tf])
        g = (a * jax.nn.sigmoid(a) * b).astype(BF16)
        yc = _dot(g, w2_ref[o:o + tf, :])
        y = yc if y is None else y + yc
        o += tf
    return y


def _ffn_kernel(x_ref, mod_ref, g_ref, w1_ref, w3_ref, w2_ref, o_ref):
    h = _ffn_input(x_ref, mod_ref, g_ref).astype(BF16)
    o_ref[...] = x_ref[...] + mod_ref[0, 5:6, :] * _swiglu(h, w1_ref, w3_ref, w2_ref)


def _ffn_call(x2d, mod, tiles_per_mod, g, w1, w3, w2, tm):
    n, d = x2d.shape

    def resident(a):
        return pl.BlockSpec(a.shape, lambda i: (0, 0), pipeline_mode=pl.Buffered(1))

    return pl.pallas_call(
        _ffn_kernel,
        grid=(n // tm,),
        in_specs=[
            pl.BlockSpec((tm, d), lambda i: (i, 0)),
            pl.BlockSpec((1, MOD_ROWS, d), lambda i: (i // tiles_per_mod, 0, 0)),
            pl.BlockSpec(g.shape, lambda i: (0, 0)),
            resident(w1), resident(w3), resident(w2),
        ],
        out_specs=pl.BlockSpec((tm, d), lambda i: (i, 0)),
        out_shape=jax.ShapeDtypeStruct((n, d), F32),
        compiler_params=_cparams(("parallel",)),
        name="ffn_dense",
    )(x2d, mod, g, w1, w3, w2)


MOE_TM = 512
SEG_ALIGN = 16
SEG_PIECES = (512, 256, 128, 64, 32, 16)
CBUF_ROWS = 2 * MOE_TM + N_EXPERTS * SEG_ALIGN
META_I1, META_I2, META_G1, META_G2, META_R1, META_R2 = range(6)


def _router_kernel(x_ref, mod_ref, g_ref, wr_ref, br_ref, ltri_ref, h_ref, meta_ref, cnt_ref):
    h = _ffn_input(x_ref, mod_ref, g_ref)
    h_ref[...] = h.astype(BF16)
    h_hi, h_lo = _split_bf16(h)
    w_hi, w_lo = _split_bf16(wr_ref[...])
    logits = _dot(h_hi, w_hi) + (_dot(h_lo, w_hi) + _dot(h_hi, w_lo)) + br_ref[...]
    lane = lax.broadcasted_iota(jnp.int32, logits.shape, 1).astype(F32)
    logits = jnp.where(lane < N_EXPERTS, logits, NEG)
    m1 = jnp.max(logits, axis=-1, keepdims=True)
    i1 = jnp.min(jnp.where(logits == m1, lane, float(LANES)), axis=-1, keepdims=True)
    rest = jnp.where(lane == i1, NEG, logits)
    m2 = jnp.max(rest, axis=-1, keepdims=True)
    i2 = jnp.min(jnp.where(rest == m2, lane, float(LANES)), axis=-1, keepdims=True)
    e2 = jnp.exp(m2 - m1)
    den = 1.0 + e2
    sel1 = jnp.where(lane == i1, 1.0, 0.0)
    sel2 = jnp.where(lane == i2, 1.0, 0.0)
    sel = sel1 + sel2
    before = _dot(ltri_ref[...], sel.astype(BF16))
    r1 = jnp.sum(before * sel1, axis=-1, keepdims=True)
    r2 = jnp.sum(before * sel2, axis=-1, keepdims=True)
    cnt_ref[0] = jnp.sum(sel, axis=0, keepdims=True)
    cols = (i1, i2, 1.0 / den, e2 / den, r1, r2)
    meta = jnp.zeros_like(logits)
    for k, col in enumerate(cols):
        meta = jnp.where(lane == k, col, meta)
    meta_ref[...] = meta


def _router_call(x2d, mod, tiles_per_mod, g, wr, br, ltri):
    n, d = x2d.shape
    tm = MOE_TM
    nt = n // tm
    return pl.pallas_call(
        _router_kernel,
        grid=(nt,),
        in_specs=[
            pl.BlockSpec((tm, d), lambda i: (i, 0)),
            pl.BlockSpec((1, MOD_ROWS, d), lambda i: (i // tiles_per_mod, 0, 0)),
            pl.BlockSpec(g.shape, lambda i: (0, 0)),
            pl.BlockSpec(wr.shape, lambda i: (0, 0)),
            pl.BlockSpec(br.shape, lambda i: (0, 0)),
            pl.BlockSpec(ltri.shape, lambda i: (0, 0)),
        ],
        out_specs=[
            pl.BlockSpec((tm, d), lambda i: (i, 0)),
            pl.BlockSpec((tm, LANES), lambda i: (i, 0)),
            pl.BlockSpec((1, 1, LANES), lambda i: (i, 0, 0)),
        ],
        out_shape=[
            jax.ShapeDtypeStruct((n, d), BF16),
            jax.ShapeDtypeStruct((n, LANES), F32),
            jax.ShapeDtypeStruct((nt, 1, LANES), F32),
        ],
        compiler_params=_cparams(("parallel",)),
        name="moe_router",
    )(x2d, mod, g, wr, br, ltri)


def _pair_slots(meta, segoff_row):
    lane = lax.broadcasted_iota(jnp.int32, meta.shape, 1).astype(F32)
    i1 = meta[:, META_I1:META_I1 + 1]
    i2 = meta[:, META_I2:META_I2 + 1]
    s1 = jnp.sum(jnp.where(lane == i1, segoff_row, 0.0), axis=-1, keepdims=True) + meta[:, META_R1:META_R1 + 1]
    s2 = jnp.sum(jnp.where(lane == i2, segoff_row, 0.0), axis=-1, keepdims=True) + meta[:, META_R2:META_R2 + 1]
    return s1, s2


def _segment_copies(src, dst, src_off, dst_off, length, sem):
    out = []
    for size in SEG_PIECES:
        done = (length // (2 * size)) * (2 * size)
        s = pl.multiple_of(src_off + done, SEG_ALIGN)
        t = pl.multiple_of(dst_off + done, SEG_ALIGN)
        cp = pltpu.make_async_copy(src.at[pl.ds(s, size)], dst.at[pl.ds(t, size)], sem)
        out.append(((length & size) != 0, cp))
    return out


def _start_copies(copies):
    for pred, cp in copies:
        @pl.when(pred)
        def _(cp=cp):
            cp.start()


def _wait_copies(copies):
    for pred, cp in copies:
        @pl.when(pred)
        def _(cp=cp):
            cp.wait()


def _compact_kernel(segoff_s, base_s, len_s, h_ref, meta_ref, segoff_ref, xs_in, gs_in, xs_out, gs_out,
                    cbuf2, gbuf2, sems):
    del xs_in, gs_in
    i = pl.program_id(0)
    last = pl.num_programs(0) - 1
    slot_i = i % 2
    cbuf = cbuf2.at[slot_i]
    gbuf = gbuf2.at[slot_i]

    def copies_of(tile, slot):
        out = []
        for e in range(N_EXPERTS):
            k = tile * N_EXPERTS + e
            out += _segment_copies(cbuf2.at[slot], xs_out, segoff_s[k], base_s[k], len_s[k], sems.at[0, slot])
            out += _segment_copies(gbuf2.at[slot], gs_out, segoff_s[k], base_s[k], len_s[k], sems.at[1, slot])
        return out

    @pl.when(i >= 2)
    def _():
        _wait_copies(copies_of(i - 2, slot_i))

    meta = meta_ref[...]
    s1, s2 = _pair_slots(meta, segoff_ref[0])
    slot = lax.broadcasted_iota(jnp.int32, (MOE_TM, CBUF_ROWS), 1).astype(F32)
    p1 = jnp.where(slot == s1, 1.0, 0.0)
    p2 = jnp.where(slot == s2, 1.0, 0.0)
    tn = (((0,), (0,)), ((), ()))
    perm = (p1 + p2).astype(BF16)
    cbuf[...] = lax.dot_general(perm, h_ref[...], tn, preferred_element_type=F32).astype(BF16)
    lane = lax.broadcasted_iota(jnp.int32, meta.shape, 1)
    g1 = meta[:, META_G1:META_G1 + 1]
    g2 = meta[:, META_G2:META_G2 + 1]
    g1_hi = g1.astype(BF16).astype(F32)
    g2_hi = g2.astype(BF16).astype(F32)
    a1 = jnp.where(lane == 0, g1_hi, jnp.where(lane == 1, g1 - g1_hi, 0.0)).astype(BF16)
    a2 = jnp.where(lane == 0, g2_hi, jnp.where(lane == 1, g2 - g2_hi, 0.0)).astype(BF16)
    gbuf[...] = (lax.dot_general(p1.astype(BF16), a1, tn, preferred_element_type=F32)
                 + lax.dot_general(p2.astype(BF16), a2, tn, preferred_element_type=F32))
    _start_copies(copies_of(i, slot_i))

    @pl.when(i == last)
    def _():
        @pl.when(i >= 1)
        def _():
            _wait_copies(copies_of(i - 1, 1 - slot_i))
        _wait_copies(copies_of(i, slot_i))


def _compact_call(sched, h, meta, segoff_v, xs0, gs0):
    n, d = h.shape
    nt = n // MOE_TM
    grid_spec = pltpu.PrefetchScalarGridSpec(
        num_scalar_prefetch=3,
        grid=(nt,),
        in_specs=[
            pl.BlockSpec((MOE_TM, d), lambda i, *_: (i, 0)),
            pl.BlockSpec((MOE_TM, LANES), lambda i, *_: (i, 0)),
            pl.BlockSpec((1, 1, LANES), lambda i, *_: (i, 0, 0)),
            pl.BlockSpec(memory_space=pl.ANY),
            pl.BlockSpec(memory_space=pl.ANY),
        ],
        out_specs=[pl.BlockSpec(memory_space=pl.ANY), pl.BlockSpec(memory_space=pl.ANY)],
        scratch_shapes=[pltpu.VMEM((2, CBUF_ROWS, d), BF16), pltpu.VMEM((2, CBUF_ROWS, LANES), F32),
                        pltpu.SemaphoreType.DMA((2, 2))],
    )
    return pl.pallas_call(
        _compact_kernel,
        grid_spec=grid_spec,
        out_shape=[jax.ShapeDtypeStruct(xs0.shape, xs0.dtype), jax.ShapeDtypeStruct(gs0.shape, gs0.dtype)],
        input_output_aliases={6: 0, 7: 1},
        compiler_params=_cparams(("arbitrary",)),
        name="moe_compact",
    )(*sched, h, meta, segoff_v, xs0, gs0)


def _expert_kernel(exp_s, blk_s, valid_s, xs_ref, gs_ref, w1_ref, w3_ref, w2_ref, y_ref):
    del exp_s, blk_s
    j = pl.program_id(0)

    @pl.when(valid_s[j] != 0)
    def _():
        gate = gs_ref[:, 0:1] + gs_ref[:, 1:2]
        y_ref[...] = (_swiglu(xs_ref[...], w1_ref, w3_ref, w2_ref) * gate).astype(BF16)

    @pl.when(valid_s[j] == 0)
    def _():
        y_ref[...] = jnp.zeros_like(y_ref)


def _expert_call(tile_sched, xs, gs, w1, w3, w2):
    rows, d = xs.shape
    ff = w1.shape[1]
    grid_spec = pltpu.PrefetchScalarGridSpec(
        num_scalar_prefetch=3,
        grid=(rows // MOE_TM,),
        in_specs=[
            pl.BlockSpec((MOE_TM, d), lambda j, e_s, b_s, v_s: (b_s[j], 0)),
            pl.BlockSpec((MOE_TM, LANES), lambda j, e_s, b_s, v_s: (b_s[j], 0)),
            pl.BlockSpec((d, ff), lambda j, e_s, b_s, v_s: (e_s[j], 0)),
            pl.BlockSpec((d, ff), lambda j, e_s, b_s, v_s: (e_s[j], 0)),
            pl.BlockSpec((ff, d), lambda j, e_s, b_s, v_s: (e_s[j], 0)),
        ],
        out_specs=pl.BlockSpec((MOE_TM, d), lambda j, e_s, b_s, v_s: (j, 0)),
    )
    return pl.pallas_call(
        _expert_kernel,
        grid_spec=grid_spec,
        out_shape=jax.ShapeDtypeStruct((rows, d), BF16),
        compiler_params=_cparams(("arbitrary",)),
        name="moe_experts",
    )(*tile_sched, xs, gs, w1, w3, w2)


def _combine_kernel(segoff_s, base_s, len_s, x_ref, mod_ref, meta_ref, segoff_ref, y_hbm, o_ref, ybuf2, sems):
    i = pl.program_id(0)
    last = pl.num_programs(0) - 1
    slot_i = i % 2

    def copies_of(tile, slot):
        out = []
        for e in range(N_EXPERTS):
            k = tile * N_EXPERTS + e
            out += _segment_copies(y_hbm, ybuf2.at[slot], base_s[k], segoff_s[k], len_s[k], sems.at[slot])
        return out

    @pl.when(i == 0)
    def _():
        ybuf2[...] = jnp.zeros_like(ybuf2)
        _start_copies(copies_of(i, slot_i))

    @pl.when(i < last)
    def _():
        _start_copies(copies_of(i + 1, 1 - slot_i))

    _wait_copies(copies_of(i, slot_i))
    s1, s2 = _pair_slots(meta_ref[...], segoff_ref[0])
    slot = lax.broadcasted_iota(jnp.int32, (MOE_TM, CBUF_ROWS), 1).astype(F32)
    pick = (jnp.where(slot == s1, 1.0, 0.0) + jnp.where(slot == s2, 1.0, 0.0)).astype(BF16)
    o_ref[...] = x_ref[...] + mod_ref[0, 5:6, :] * _dot(pick, ybuf2[slot_i])


def _combine_call(sched, x2d, mod, tiles_per_mod, meta, segoff_v, y):
    n, d = x2d.shape
    grid_spec = pltpu.PrefetchScalarGridSpec(
        num_scalar_prefetch=3,
        grid=(n // MOE_TM,),
        in_specs=[
            pl.BlockSpec((MOE_TM, d), lambda i, *_: (i, 0)),
            pl.BlockSpec((1, MOD_ROWS, d), lambda i, *_: (i // tiles_per_mod, 0, 0)),
            pl.BlockSpec((MOE_TM, LANES), lambda i, *_: (i, 0)),
            pl.BlockSpec((1, 1, LANES), lambda i, *_: (i, 0, 0)),
            pl.BlockSpec(memory_space=pl.ANY),
        ],
        out_specs=pl.BlockSpec((MOE_TM, d), lambda i, *_: (i, 0)),
        scratch_shapes=[pltpu.VMEM((2, CBUF_ROWS, d), BF16), pltpu.SemaphoreType.DMA((2,))],
    )
    return pl.pallas_call(
        _combine_kernel,
        grid_spec=grid_spec,
        out_shape=jax.ShapeDtypeStruct((n, d), F32),
        compiler_params=_cparams(("arbitrary",)),
        name="moe_combine",
    )(*sched, x2d, mod, meta, segoff_v, y)


def _moe_call(x2d, mod, tiles_per_mod, g, wr, br, w1, w3, w2):
    n, d = x2d.shape
    nt = n // MOE_TM
    ltri = jnp.asarray(np.tril(np.ones((MOE_TM, MOE_TM), np.float32), -1), BF16)
    h, meta, counts = _router_call(x2d, mod, tiles_per_mod, g, wr, br, ltri)

    cnt = counts[:, 0, :N_EXPERTS].astype(jnp.int32)
    seg_len = (cnt + SEG_ALIGN - 1) // SEG_ALIGN * SEG_ALIGN
    segoff = jnp.cumsum(seg_len, axis=1) - seg_len
    region = (jnp.sum(seg_len, axis=0) + MOE_TM - 1) // MOE_TM * MOE_TM
    region_start = jnp.cumsum(region) - region
    base = region_start[None, :] + jnp.cumsum(seg_len, axis=0) - seg_len
    rows_max = 2 * n + nt * N_EXPERTS * (SEG_ALIGN - 1) + N_EXPERTS * (MOE_TM - 1)
    n_sorted_tiles = (rows_max + MOE_TM - 1) // MOE_TM
    tile_end = jnp.cumsum(region // MOE_TM)
    total_tiles = tile_end[-1]
    jt = jnp.arange(n_sorted_tiles, dtype=jnp.int32)
    blk = jnp.minimum(jt, total_tiles - 1)
    tile_expert = jnp.sum((blk[:, None] >= tile_end[None, :]).astype(jnp.int32), axis=1)
    tile_sched = (tile_expert.astype(jnp.int32), blk.astype(jnp.int32), (jt < total_tiles).astype(jnp.int32))
    sched = tuple(a.reshape(-1).astype(jnp.int32) for a in (segoff, base, seg_len))
    segoff_v = jnp.pad(segoff.astype(F32), ((0, 0), (0, LANES - N_EXPERTS)))[:, None, :]

    rows = n_sorted_tiles * MOE_TM
    xs, gs = _compact_call(sched, h, meta, segoff_v, jnp.zeros((rows, d), BF16), jnp.zeros((rows, LANES), F32))
    y = _expert_call(tile_sched, xs, gs, w1, w3, w2)
    return _combine_call(sched, x2d, mod, tiles_per_mod, meta, segoff_v, y)


def _rope_tables(t):
    rows = jnp.arange(t, dtype=F32) // GRID_W
    cols = jnp.arange(t, dtype=F32) % GRID_W

    def tables(rot_dim):
        a = rot_dim // 2
        inv = 1.0 / (ROPE_BASE ** (jnp.arange(0, a, 2, dtype=F32) / a))
        ar = rows[:, None] * inv
        ac = cols[:, None] * inv
        cos = jnp.concatenate([jnp.cos(ar), jnp.cos(ar), jnp.cos(ac), jnp.cos(ac)], axis=-1)
        sin = jnp.concatenate([-jnp.sin(ar), jnp.sin(ar), -jnp.sin(ac), jnp.sin(ac)], axis=-1)
        return cos, sin

    ca, sa = tables(HEAD_DIM)
    cos_a = jnp.tile(ca, (1, LANES // HEAD_DIM))
    sin_a = jnp.tile(sa, (1, LANES // HEAD_DIM))
    cc, sc = tables(C_ROPE)
    ones = jnp.ones((t, C_NOPE), F32)
    tail = C_SLOT - C_NOPE - C_ROPE
    cos_c = jnp.concatenate([ones, cc, jnp.ones((t, tail), F32)], axis=-1)
    sin_c = jnp.concatenate([0 * ones, sc, jnp.zeros((t, tail), F32)], axis=-1)
    return cos_a, sin_a, cos_c, sin_c


def _head_perm():
    order = []
    for j in range(A_HEADS // 2):
        order += [j, A_HEADS // 2 + j]
    return np.concatenate([np.arange(h * HEAD_DIM, (h + 1) * HEAD_DIM) for h in order])


def _segment_mean_matrix(widths, total):
    m = np.zeros((total, total), np.float32)
    o = 0
    while o < total:
        for w, used in widths:
            if used:
                m[o:o + w, o:o + w] = 1.0 / w
            o += w
    return jnp.asarray(m, BF16)


def _slot_vec(nope, rope):
    z = jnp.zeros((C_SLOT - C_NOPE - C_ROPE,), F32)
    n = jnp.zeros((C_NOPE,), F32) if nope is None else nope
    r = jnp.zeros((C_ROPE,), F32) if rope is None else rope
    return jnp.tile(jnp.concatenate([n, r, z]), C_HEADS)[None, :]


def _layer_consts(i, p, perm, tabs):
    w_in = p["w_in"][i]
    o_kr = A_Q + 2 * A_KV + 2 * B_CH + C_Q_RANK + C_KV_RANK
    d = w_in.shape[0]
    kr_cols = jnp.concatenate([jnp.zeros((d, C_NOPE), F32), w_in[:, o_kr:o_kr + C_ROPE],
                               jnp.zeros((d, C_SLOT - C_NOPE - C_ROPE), F32)], axis=1)
    win = jnp.concatenate([w_in[:, :A_Q][:, perm], w_in[:, A_Q:o_kr], kr_cols], axis=1).astype(BF16)

    w_uq = p["c_w_uq"][i].reshape(C_Q_RANK, C_HEADS, C_NOPE + C_ROPE)
    wuq = jnp.pad(w_uq, ((0, 0), (0, 0), (0, C_SLOT - C_NOPE - C_ROPE))).reshape(C_Q_RANK, C_HEADS * C_SLOT)
    w_ukv = p["c_w_ukv"][i].reshape(C_KV_RANK, C_HEADS, C_NOPE + C_V)
    wukvk = jnp.pad(w_ukv[:, :, :C_NOPE], ((0, 0), (0, 0), (0, C_SLOT - C_NOPE))).reshape(C_KV_RANK, -1)
    wukvv = w_ukv[:, :, C_NOPE:].reshape(C_KV_RANK, C_HEADS * C_V)

    sa = _segment_mean_matrix([(HEAD_DIM, True)], A_Q)
    sc = _segment_mean_matrix([(C_NOPE, True), (C_ROPE, True), (C_SLOT - C_NOPE - C_ROPE, False)], C_HEADS * C_SLOT)
    gq = jnp.tile(p["a_q_norm_g"][i] * (A_SCALE * LOG2E), A_HEADS)[None, :]
    gk = jnp.tile(p["a_k_norm_g"][i], A_KV_HEADS)[None, :]
    gqc = _slot_vec(p["c_q_nope_norm_g"][i], p["c_q_rope_norm_g"][i]) * (MLA_SCALE * LOG2E)
    gkn = _slot_vec(p["c_k_nope_norm_g"][i], None)
    gkr = _slot_vec(None, p["c_k_rope_norm_g"][i])[:, :C_SLOT]
    return (p["mix_norm_g"][i][None, :], win) + tabs + (
        sa, sc, gq, gk, p["c_q_rank_norm_g"][i][None, :], p["c_kv_rank_norm_g"][i][None, :],
        wuq.astype(BF16), wukvk.astype(BF16), wukvv.astype(BF16), gqc, gkn, gkr)


def kernel(x, c, ctx, c_ctx, ada_w, ada_b, mix_norm_g, ffn_norm_g, w_in, w_out, a_q_norm_g, a_k_norm_g, a_sink, b_conv_w, b_conv_b, b_ln_g, b_ln_b, c_q_rank_norm_g, c_kv_rank_norm_g, c_w_uq, c_w_ukv, c_q_nope_norm_g, c_k_nope_norm_g, c_q_rope_norm_g, c_k_rope_norm_g, dense_w1, dense_w3, dense_w2, moe_router_w, moe_router_b, moe_w1, moe_w3, moe_w2):
    p = dict(w_in=w_in, c_w_uq=c_w_uq, c_w_ukv=c_w_ukv, a_q_norm_g=a_q_norm_g, a_k_norm_g=a_k_norm_g,
             c_q_nope_norm_g=c_q_nope_norm_g, c_k_nope_norm_g=c_k_nope_norm_g,
             c_q_rope_norm_g=c_q_rope_norm_g, c_k_rope_norm_g=c_k_rope_norm_g,
             c_q_rank_norm_g=c_q_rank_norm_g, c_kv_rank_norm_g=c_kv_rank_norm_g, mix_norm_g=mix_norm_g)
    bsz, t, d = x.shape
    ctx_len = ctx.shape[1]
    depth = ada_w.shape[0]
    n_x, n_c = bsz * t, bsz * ctx_len
    tm_pre = 256
    tm_tok = 512
    tq = min(1024, t)

    ada_rows = ((bsz + 1 + 7) // 8) * 8
    c_pad = jnp.concatenate([c, c_ctx[None, :], jnp.zeros((ada_rows - bsz - 1, d), F32)], axis=0)
    mods = _ada_call(c_pad, ada_w, ada_b).reshape(depth, ada_rows, 6, d)
    mods = jnp.pad(mods, ((0, 0), (0, 0), (0, MOD_ROWS - 6), (0, 0)))

    tabs_x = _rope_tables(t)
    ones = jnp.ones((tm_pre, LANES), F32)
    tabs_c = (ones, 0 * ones, ones, 0 * ones)
    perm = _head_perm()

    x2 = x.reshape(n_x, d)
    c2 = ctx.reshape(n_c, d)
    for i in range(depth):
        last = i == depth - 1
        mod_x = mods[i, :bsz]
        mod_c = mods[i, bsz:bsz + 1]
        consts_x = _layer_consts(i, p, perm, tabs_x)
        consts_c = _layer_consts(i, p, perm, tabs_c)

        qa_x, ka_x, va_x, u_x, qc_x, kc_x, vc_x = _pre_call(x2, mod_x, t // tm_pre, t // tm_pre, consts_x, tm_pre)
        qa_c, ka_c, va_c, u_c, qc_c, kc_c, vc_c = _pre_call(c2, mod_c, n_c // tm_pre, 1, consts_c, tm_pre)

        def r3(a, length):
            return a.reshape(bsz, length, a.shape[-1])

        sink_row = jnp.repeat(a_sink[i] * LOG2E, BLOCK)[None, :]
        conv_w = jnp.pad(b_conv_w[i], ((0, 32 - B_WIDTH), (0, 0)))
        conv_p = (conv_w, b_conv_b[i][None, :], b_ln_g[i][None, :], b_ln_b[i][None, :])
        w_o = jnp.concatenate([w_out[i][:A_Q][perm], w_out[i][A_Q:]], axis=0).astype(BF16)

        o_a = _attn_a_call(qa_x, r3(ka_x, t), va_x, r3(ka_c, ctx_len), va_c, sink_row, bsz, t)
        o_b = _conv_call(r3(u_x, t), *conv_p).reshape(n_x, B_CH)
        o_c = _mla_call(qc_x, r3(kc_x, t), vc_x, r3(kc_c, ctx_len), vc_c, bsz, t, tq)
        x2 = _out_call(x2, mod_x, t // tm_tok, o_a, o_b, o_c, w_o, tm_tok)
        if not last:
            oc_a = _attn_a_call(qa_c, None, None, r3(ka_c, ctx_len), va_c, sink_row, bsz, ctx_len)
            oc_b = _conv_call(r3(u_c, ctx_len), *conv_p).reshape(n_c, B_CH)
            oc_c = _mla_call(qc_c, None, None, r3(kc_c, ctx_len), vc_c, bsz, ctx_len, ctx_len)
            c2 = _out_call(c2, mod_c, n_c // tm_tok, oc_a, oc_b, oc_c, w_o, tm_tok)

        j = i // 2
        g_ffn = ffn_norm_g[i][None, :]
        if i % 2 == 0:
            w = (dense_w1[j].astype(BF16), dense_w3[j].astype(BF16), dense_w2[j].astype(BF16))

            def ffn(a2, mod, tiles_per_mod, w=w, g_ffn=g_ffn):
                return _ffn_call(a2, mod, tiles_per_mod, g_ffn, *w, tm_tok)
        else:
            w = tuple(a.astype(BF16).reshape(-1, a.shape[-1]) for a in (moe_w1[j], moe_w3[j], moe_w2[j]))
            wr = jnp.pad(moe_router_w[j], ((0, 0), (0, LANES - N_EXPERTS)))
            br = jnp.pad(moe_router_b[j], (0, LANES - N_EXPERTS))[None, :]

            def ffn(a2, mod, tiles_per_mod, w=w, g_ffn=g_ffn, wr=wr, br=br):
                return _moe_call(a2, mod, tiles_per_mod * (tm_tok // MOE_TM), g_ffn, wr, br, *w)
        x2 = ffn(x2, mod_x, t // tm_tok)
        if not last:
            c2 = ffn(c2, mod_c, n_c // tm_tok)
    return x2.reshape(bsz, t, d)
```

```python
import functools

import jax
import jax.numpy as jnp
import numpy as np
from jax import lax
from jax.experimental import pallas as pl
from jax.experimental.pallas import tpu as pltpu

F32 = jnp.float32
BF16 = jnp.bfloat16

D_MODEL = 1024
GRID_W = 64
HEAD_DIM = 64
A_HEADS = 8
A_KV_HEADS = 2
A_WINDOW = 128
BLOCK = 128
B_CH = 256
B_WIDTH = 31
C_HEADS = 4
C_Q_RANK = 384
C_KV_RANK = 256
C_NOPE = 64
C_ROPE = 32
C_V = 64
A_Q = A_HEADS * HEAD_DIM
A_KV = A_KV_HEADS * HEAD_DIM
IN_COLS_PAD = 2048
D_FF = 2816
N_EXPERTS = 8
ROPE_BASE = 10000.0
EPS = 1e-6
NEG = -1e30
A_SCALE = HEAD_DIM ** -0.5
MLA_SCALE = (C_NOPE + C_ROPE) ** -0.5
LOG2E = 1.4426950408889634

LANES = 128
MXU_TILE = 256
MOD_ROWS = 8
C_SLOT = 128
VMEM_LIMIT = 56 * 1024 * 1024


def _cparams(sem):
    return pltpu.CompilerParams(dimension_semantics=sem, vmem_limit_bytes=VMEM_LIMIT)


def _dot(a, b):
    return jnp.dot(a, b, preferred_element_type=F32)


def _dot_nt(a, b):
    return lax.dot_general(a, b, (((1,), (1,)), ((), ())), preferred_element_type=F32)


def _ada_kernel(c_ref, w_ref, b_ref, o_ref):
    c = c_ref[...]
    a = c * jax.nn.sigmoid(c)
    o_ref[0] = _dot(a.astype(BF16), w_ref[0].astype(BF16)) + b_ref[0]


def _ada_call(c_pad, ada_w, ada_b):
    depth, d, n6 = ada_w.shape
    rows = c_pad.shape[0]
    tn = 1536
    return pl.pallas_call(
        _ada_kernel,
        grid=(depth, n6 // tn),
        in_specs=[
            pl.BlockSpec((rows, d), lambda i, j: (0, 0)),
            pl.BlockSpec((1, d, tn), lambda i, j: (i, 0, j)),
            pl.BlockSpec((1, 1, tn), lambda i, j: (i, 0, j)),
        ],
        out_specs=pl.BlockSpec((1, rows, tn), lambda i, j: (i, 0, j)),
        out_shape=jax.ShapeDtypeStruct((depth, rows, n6), F32),
        compiler_params=_cparams(("parallel", "parallel")),
        name="ada_proj",
    )(c_pad, ada_w, ada_b.reshape(depth, 1, n6))


def _rope_chunk(c, cos, sin, half):
    lane = lax.broadcasted_iota(jnp.int32, c.shape, 1)
    lo = (lane & (2 * half - 1)) < half
    partner = jnp.where(lo, pltpu.roll(c, LANES - half, 1), pltpu.roll(c, half, 1))
    return c * cos + partner * sin


def _segment_mean(sq, s_ref):
    w = s_ref.shape[0]
    return jnp.concatenate([_dot(sq[:, c:c + w].astype(BF16), s_ref[...]) for c in range(0, sq.shape[1], w)], axis=1)


def _pre_kernel(x_ref, mod_ref, gmix_ref, win_ref, cosa_ref, sina_ref, cosc_ref, sinc_ref,
                sa_ref, sc_ref, gq_ref, gk_ref, gcq_ref, gckv_ref, wuq_ref, wukvk_ref, wukvv_ref,
                gqc_ref, gkn_ref, gkr_ref,
                qa_ref, ka_ref, va_ref, u_ref, qc_ref, kc_ref, vc_ref):
    x = x_ref[...]
    ms = jnp.mean(x * x, axis=-1, keepdims=True)
    y = x * lax.rsqrt(ms + EPS)
    shift = mod_ref[0, 0:1, :]
    scale = mod_ref[0, 1:2, :]
    h = (y * gmix_ref[...]) * (1.0 + scale) + shift
    p = _dot(h.astype(BF16), win_ref[...])

    cosa, sina = cosa_ref[...], sina_ref[...]
    cosc, sinc = cosc_ref[...], sinc_ref[...]

    qa = p[:, 0:A_Q]
    ssq = _segment_mean(qa * qa, sa_ref)
    qa = qa * lax.rsqrt(ssq + EPS) * gq_ref[...]
    for j in range(A_Q // LANES):
        sl = slice(j * LANES, (j + 1) * LANES)
        qa_ref[:, sl] = _rope_chunk(qa[:, sl], cosa, sina, HEAD_DIM // 4).astype(BF16)

    ka = p[:, A_Q:A_Q + A_KV]
    ssk = _dot((ka * ka).astype(BF16), sa_ref[0:A_KV, 0:A_KV])
    ka = ka * lax.rsqrt(ssk + EPS) * gk_ref[...]
    ka_ref[...] = _rope_chunk(ka, cosa, sina, HEAD_DIM // 4).astype(BF16)
    va_ref[...] = p[:, A_Q + A_KV:A_Q + 2 * A_KV].T.astype(BF16)

    o = A_Q + 2 * A_KV
    u_ref[...] = p[:, o:o + B_CH] * jax.nn.sigmoid(p[:, o + B_CH:o + 2 * B_CH])

    o = o + 2 * B_CH
    cq = p[:, o:o + C_Q_RANK]
    cq = cq * lax.rsqrt(jnp.mean(cq * cq, axis=-1, keepdims=True) + EPS) * gcq_ref[...]
    qc = _dot(cq.astype(BF16), wuq_ref[...])
    ssq = _segment_mean(qc * qc, sc_ref)
    qc = qc * lax.rsqrt(ssq + EPS) * gqc_ref[...]
    for j in range(C_HEADS):
        sl = slice(j * C_SLOT, (j + 1) * C_SLOT)
        qc_ref[:, sl] = _rope_chunk(qc[:, sl], cosc, sinc, C_ROPE // 4).astype(BF16)

    o = o + C_Q_RANK
    ckv = p[:, o:o + C_KV_RANK]
    ckv = (ckv * lax.rsqrt(jnp.mean(ckv * ckv, axis=-1, keepdims=True) + EPS) * gckv_ref[...]).astype(BF16)
    kn = _dot(ckv, wukvk_ref[...])
    vc_ref[...] = _dot(ckv, wukvv_ref[...]).T.astype(BF16)
    ssk = _segment_mean(kn * kn, sc_ref)
    kn = kn * lax.rsqrt(ssk + EPS) * gkn_ref[...]
    o = o + C_KV_RANK
    kr = p[:, o:o + C_SLOT]
    kr = kr * lax.rsqrt(jnp.sum(kr * kr, axis=-1, keepdims=True) * (1.0 / C_ROPE) + EPS) * gkr_ref[...]
    kr = _rope_chunk(kr, cosc, sinc, C_ROPE // 4)
    for j in range(C_HEADS):
        sl = slice(j * C_SLOT, (j + 1) * C_SLOT)
        kc_ref[:, sl] = (kn[:, sl] + kr).astype(BF16)


def _pre_call(x2d, mod, tiles_per_mod, tab_tiles, consts, tm):
    n, d = x2d.shape
    (gmix, win, cosa, sina, cosc, sinc, sa, sc, gq, gk, gcq, gckv, wuq, wukvk, wukvv, gqc, gkn, gkr) = consts

    def const(a):
        return pl.BlockSpec(a.shape, lambda i: (0,) * a.ndim)

    def tab(a):
        return pl.BlockSpec((tm, LANES), lambda i: (i % tab_tiles, 0))

    in_specs = [
        pl.BlockSpec((tm, d), lambda i: (i, 0)),
        pl.BlockSpec((1, MOD_ROWS, d), lambda i: (i // tiles_per_mod, 0, 0)),
        const(gmix), const(win), tab(cosa), tab(sina), tab(cosc), tab(sinc),
        const(sa), const(sc), const(gq), const(gk), const(gcq), const(gckv),
        const(wuq), const(wukvk), const(wukvv), const(gqc), const(gkn), const(gkr),
    ]
    widths = (A_Q, A_KV, A_KV, B_CH, C_HEADS * C_SLOT, C_HEADS * C_SLOT, C_HEADS * C_V)
    dtypes = (BF16, BF16, BF16, F32, BF16, BF16, BF16)
    out_specs = [pl.BlockSpec((tm, w), lambda i: (i, 0)) for w in widths]
    out_shape = [jax.ShapeDtypeStruct((n, w), dt) for w, dt in zip(widths, dtypes)]
    out_specs[2] = pl.BlockSpec((A_KV, tm), lambda i: (0, i))
    out_shape[2] = jax.ShapeDtypeStruct((A_KV, n), BF16)
    out_specs[6] = pl.BlockSpec((C_HEADS * C_V, tm), lambda i: (0, i))
    out_shape[6] = jax.ShapeDtypeStruct((C_HEADS * C_V, n), BF16)
    return pl.pallas_call(
        _pre_kernel,
        grid=(n // tm,),
        in_specs=in_specs,
        out_specs=out_specs,
        out_shape=out_shape,
        compiler_params=_cparams(("parallel",)),
        name="pre_attn",
    )(x2d, mod, gmix, win, cosa, sina, cosc, sinc, sa, sc, gq, gk, gcq, gckv, wuq, wukvk, wukvv,
      gqc, gkn, gkr)


ATTN_A_QBLOCKS = 4


ONES_ROWS = 16


def _attn_a_kernel(*refs, t, has_local, qblocks):
    if has_local:
        q_ref, k_ref, vt_ref, kc_ref, vct_ref, sink_ref, bias_ref, o_ref = refs
    else:
        q_ref, kc_ref, vct_ref, sink_ref, o_ref = refs
    nchunk = A_Q // LANES
    span = 3 * BLOCK
    lane = lax.broadcasted_iota(jnp.int32, (BLOCK, LANES), 1)
    row = lax.broadcasted_iota(jnp.int32, (A_KV, BLOCK), 0)
    zero = jnp.zeros((BLOCK, LANES), BF16)
    sink = sink_ref[...]
    kc = kc_ref[0]
    vct = vct_ref[...]
    for blk in range(qblocks):
        rows = slice(blk * BLOCK, (blk + 1) * BLOCK)
        chunks = [q_ref[rows, j * LANES:(j + 1) * LANES] for j in range(nchunk)]
        qs = jnp.concatenate([jnp.where(lane < HEAD_DIM, c, zero) for c in chunks]
                             + [jnp.where(lane >= HEAD_DIM, c, zero) for c in chunks], axis=0)
        if has_local:
            n = pl.program_id(1) * qblocks + blk
            start = pl.multiple_of(jnp.clip((n - 1) * BLOCK, 0, t - span), BLOCK)
            keys = jnp.concatenate([k_ref[0, pl.ds(start, span), :], kc], axis=0)
            vt = jnp.concatenate([vt_ref[:, pl.ds(start, span)], vct], axis=1)
        else:
            keys, vt = kc, vct
        s = _dot_nt(keys, qs)
        if has_local:
            s = jnp.concatenate([s[:span] + bias_ref[n - start // BLOCK], s[span:]], axis=0)
        m = jnp.maximum(jnp.max(s, axis=0, keepdims=True), sink)
        e = jnp.exp2(s - m).astype(BF16)
        vte = jnp.concatenate([vt, jnp.ones((ONES_ROWS, vt.shape[1]), BF16)], axis=0)
        acc = _dot(vte, e)
        den = acc[A_KV:A_KV + 1] + jnp.exp2(sink - m)
        out = acc[:A_KV] * (1.0 / den)
        for j in range(nchunk):
            x = jnp.where(row < HEAD_DIM, out[:, j * BLOCK:(j + 1) * BLOCK],
                          out[:, (nchunk + j) * BLOCK:(nchunk + j + 1) * BLOCK])
            o_ref[rows, j * LANES:(j + 1) * LANES] = x.T.astype(BF16)


def _window_bias():
    r = np.arange(BLOCK)[None, :]
    c = np.arange(3 * BLOCK)[:, None]
    pats = [np.where(np.abs(c - p * BLOCK - r) <= A_WINDOW, 0.0, NEG) for p in range(3)]
    return jnp.asarray(np.stack([np.tile(p, (1, A_HEADS)) for p in pats]), F32)


def _attn_a_call(qa, ka, vat, kac, vact, sink_row, bsz, t):
    has_local = ka is not None
    qblocks = min(ATTN_A_QBLOCKS, t // BLOCK)
    tq = qblocks * BLOCK
    nq = t // tq
    ctx_len = kac.shape[1]
    in_specs = [pl.BlockSpec((tq, A_Q), lambda b, n: (b * nq + n, 0))]
    args = [qa]
    if has_local:
        in_specs += [pl.BlockSpec((1, t, A_KV), lambda b, n: (b, 0, 0)),
                     pl.BlockSpec((A_KV, t), lambda b, n: (0, b))]
        args += [ka, vat]
    in_specs += [pl.BlockSpec((1, ctx_len, A_KV), lambda b, n: (b, 0, 0)),
                 pl.BlockSpec((A_KV, ctx_len), lambda b, n: (0, b))]
    in_specs += [pl.BlockSpec(sink_row.shape, lambda b, n: (0, 0))]
    args += [kac, vact, sink_row]
    if has_local:
        bias = _window_bias()
        in_specs += [pl.BlockSpec(bias.shape, lambda b, n: (0, 0, 0))]
        args += [bias]
    return pl.pallas_call(
        functools.partial(_attn_a_kernel, t=t, has_local=has_local, qblocks=qblocks),
        grid=(bsz, nq),
        in_specs=in_specs,
        out_specs=pl.BlockSpec((tq, A_Q), lambda b, n: (b * nq + n, 0)),
        out_shape=jax.ShapeDtypeStruct((bsz * t, A_Q), BF16),
        compiler_params=_cparams(("parallel", "parallel")),
        name="attn_a_local" if has_local else "attn_a_ctx",
    )(*args)


CONV_PAD = 16
CONV_CHUNK = 128
SUBLANES = 8


def _conv_kernel(u_ref, w_ref, b_ref, g_ref, beta_ref, o_ref, pad_ref, *, t):
    zeros = jnp.zeros((CONV_PAD, B_CH), F32)
    pad_ref[0:CONV_PAD, :] = zeros
    pad_ref[CONV_PAD + t:CONV_PAD + t + CONV_PAD, :] = zeros
    pad_ref[CONV_PAD:CONV_PAD + t, :] = u_ref[0]
    off = CONV_PAD - B_WIDTH // 2
    nq = (off + B_WIDTH - 1) // SUBLANES + 1
    win = CONV_CHUNK + (nq - 1) * SUBLANES
    for c in range(t // CONV_CHUNK):
        base = c * CONV_CHUNK
        acc = jnp.zeros((CONV_CHUNK, B_CH), F32)
        for r in range(SUBLANES):
            taps = [k for k in range(B_WIDTH) if (off + k) % SUBLANES == r]
            if not taps:
                continue
            w_r = pad_ref[base + r:base + r + win, :]
            part = None
            for k in taps:
                q = (off + k) // SUBLANES
                term = w_r[q * SUBLANES:q * SUBLANES + CONV_CHUNK, :] * w_ref[k:k + 1, :]
                part = term if part is None else part + term
            acc = acc + part
        y = acc + b_ref[...]
        mu = jnp.mean(y, axis=-1, keepdims=True)
        yc = y - mu
        var = jnp.mean(yc * yc, axis=-1, keepdims=True)
        z = yc * lax.rsqrt(var + EPS) * g_ref[...] + beta_ref[...]
        o_ref[0, c * CONV_CHUNK:(c + 1) * CONV_CHUNK, :] = (z * jax.nn.sigmoid(z)).astype(BF16)


def _conv_call(u3, w_pad, b, g, beta):
    bsz, t, ch = u3.shape

    def const(a):
        return pl.BlockSpec(a.shape, lambda i: (0, 0))

    return pl.pallas_call(
        functools.partial(_conv_kernel, t=t),
        grid=(bsz,),
        in_specs=[pl.BlockSpec((1, t, ch), lambda i: (i, 0, 0)), const(w_pad), const(b), const(g), const(beta)],
        out_specs=pl.BlockSpec((1, t, ch), lambda i: (i, 0, 0)),
        out_shape=jax.ShapeDtypeStruct((bsz, t, ch), BF16),
        scratch_shapes=[pltpu.VMEM((t + 2 * CONV_PAD, ch), F32)],
        compiler_params=_cparams(("parallel",)),
        name="conv_module",
    )(u3, w_pad, b, g, beta)


def _mla_kernel(*refs, has_local):
    if has_local:
        q_ref, kx_ref, vxt_ref, kc_ref, vct_ref, o_ref = refs
    else:
        q_ref, kc_ref, vct_ref, o_ref = refs
    def scores(h):
        sl = slice(h * C_SLOT, (h + 1) * C_SLOT)
        q = q_ref[:, sl]
        s_c = _dot_nt(kc_ref[0, :, sl], q)
        s_x = _dot_nt(kx_ref[0, :, sl], q) if has_local else None
        return s_c, s_x

    outs = []
    nxt = scores(0)
    for h in range(C_HEADS):
        vs = slice(h * C_V, (h + 1) * C_V)
        s_c, s_x = nxt
        if h + 1 < C_HEADS:
            nxt = scores(h + 1)
        m = jnp.max(s_c, axis=0, keepdims=True)
        if has_local:
            m = jnp.maximum(m, jnp.max(s_x, axis=0, keepdims=True))
        e_c = jnp.exp2(s_c - m).astype(BF16)
        vt = jnp.concatenate([vct_ref[vs, :], jnp.ones((ONES_ROWS, e_c.shape[0]), BF16)], axis=0)
        acc = _dot(vt, e_c)
        if has_local:
            e_x = jnp.exp2(s_x - m).astype(BF16)
            vt = jnp.concatenate([vxt_ref[vs, :], jnp.ones((ONES_ROWS, e_x.shape[0]), BF16)], axis=0)
            acc = acc + _dot(vt, e_x)
        outs.append(acc[:C_V] * (1.0 / acc[C_V:C_V + 1]))
    o_ref[...] = jnp.concatenate(outs, axis=0).T.astype(BF16)


def _mla_call(qc, kx, vxt, kcc, vcct, bsz, t, tq):
    has_local = kx is not None
    nq = t // tq
    ctx_len = kcc.shape[1]
    wq = C_HEADS * C_SLOT
    wv = C_HEADS * C_V
    in_specs = [pl.BlockSpec((tq, wq), lambda b, n: (b * nq + n, 0))]
    args = [qc]
    if has_local:
        in_specs += [pl.BlockSpec((1, t, wq), lambda b, n: (b, 0, 0)),
                     pl.BlockSpec((wv, t), lambda b, n: (0, b))]
        args += [kx, vxt]
    in_specs += [pl.BlockSpec((1, ctx_len, wq), lambda b, n: (b, 0, 0)),
                 pl.BlockSpec((wv, ctx_len), lambda b, n: (0, b))]
    args += [kcc, vcct]
    return pl.pallas_call(
        functools.partial(_mla_kernel, has_local=has_local),
        grid=(bsz, nq),
        in_specs=in_specs,
        out_specs=pl.BlockSpec((tq, wv), lambda b, n: (b * nq + n, 0)),
        out_shape=jax.ShapeDtypeStruct((bsz * t, wv), BF16),
        compiler_params=_cparams(("parallel", "parallel")),
        name="mla_local" if has_local else "mla_ctx",
    )(*args)


def _mixer_residual(x_ref, mod_ref, oa_ref, ob_ref, oc_ref, wo_ref):
    y = _dot(oa_ref[...], wo_ref[0:A_Q, :])
    y = y + _dot(ob_ref[...], wo_ref[A_Q:A_Q + B_CH, :])
    y = y + _dot(oc_ref[...], wo_ref[A_Q + B_CH:, :])
    return x_ref[...] + mod_ref[0, 2:3, :] * y


def _mixer_specs(tm, d, tiles_per_mod, oa, ob, oc, w_out):
    return [
        pl.BlockSpec((tm, d), lambda i: (i, 0)),
        pl.BlockSpec((1, MOD_ROWS, d), lambda i: (i // tiles_per_mod, 0, 0)),
        pl.BlockSpec((tm, oa.shape[1]), lambda i: (i, 0)),
        pl.BlockSpec((tm, ob.shape[1]), lambda i: (i, 0)),
        pl.BlockSpec((tm, oc.shape[1]), lambda i: (i, 0)),
        pl.BlockSpec(w_out.shape, lambda i: (0, 0), pipeline_mode=pl.Buffered(1)),
    ]


def _split_bf16(a):
    hi = a.astype(BF16)
    lo = (a - hi.astype(F32)).astype(BF16)
    return hi, lo


def _ffn_input(x, mod_ref, g_ref):
    ms = jnp.mean(x * x, axis=-1, keepdims=True)
    y = x * lax.rsqrt(ms + EPS)
    return (y * g_ref[...]) * (1.0 + mod_ref[0, 4:5, :]) + mod_ref[0, 3:4, :]


def _ff_chunks(ff):
    tiles = ff // MXU_TILE
    if ff % MXU_TILE or tiles < 2:
        return (ff,)
    first = (tiles // 2) * MXU_TILE
    return (first, ff - first)


def _swiglu(h, w1_ref, w3_ref, w2_ref):
    y = None
    o = 0
    for tf in _ff_chunks(w1_ref.shape[1]):
        a = _dot(h, w1_ref[:, o:o + tf])
        b = _dot(h, w3_ref[:, o:o + tf])
        g = (a * jax.nn.sigmoid(a) * b).astype(BF16)
        yc = _dot(g, w2_ref[o:o + tf, :])
        y = yc if y is None else y + yc
        o += tf
    return y


def _ffn_kernel(x_ref, mod_ref, oa_ref, ob_ref, oc_ref, wo_ref, g_ref, w1_ref, w3_ref, w2_ref, o_ref):
    x = _mixer_residual(x_ref, mod_ref, oa_ref, ob_ref, oc_ref, wo_ref)
    h = _ffn_input(x, mod_ref, g_ref).astype(BF16)
    o_ref[...] = x + mod_ref[0, 5:6, :] * _swiglu(h, w1_ref, w3_ref, w2_ref)


def _ffn_call(x2d, mod, tiles_per_mod, oa, ob, oc, w_out, g, w1, w3, w2, tm):
    n, d = x2d.shape

    def resident(a):
        return pl.BlockSpec(a.shape, lambda i: (0, 0), pipeline_mode=pl.Buffered(1))

    return pl.pallas_call(
        _ffn_kernel,
        grid=(n // tm,),
        in_specs=_mixer_specs(tm, d, tiles_per_mod, oa, ob, oc, w_out) + [
            pl.BlockSpec(g.shape, lambda i: (0, 0)), resident(w1), resident(w3), resident(w2)],
        out_specs=pl.BlockSpec((tm, d), lambda i: (i, 0)),
        out_shape=jax.ShapeDtypeStruct((n, d), F32),
        compiler_params=_cparams(("parallel",)),
        name="ffn_dense",
    )(x2d, mod, oa, ob, oc, w_out, g, w1, w3, w2)


MOE_TM = 512
SEG_ALIGN = 16
SEG_PIECES = (512, 256, 128, 64, 32, 16)
CBUF_ROWS = 2 * MOE_TM + N_EXPERTS * SEG_ALIGN
META_I1, META_I2, META_G1, META_G2, META_R1, META_R2 = range(6)
META_ROWS = 8


def _router_kernel(x_ref, mod_ref, oa_ref, ob_ref, oc_ref, wo_ref, g_ref, wr_ref, br_ref, ltri_ref,
                   x1_ref, h_ref, meta_ref, metat_ref, cnt_ref):
    x = _mixer_residual(x_ref, mod_ref, oa_ref, ob_ref, oc_ref, wo_ref)
    x1_ref[...] = x
    h = _ffn_input(x, mod_ref, g_ref)
    h_ref[...] = h.astype(BF16)
    h_hi, h_lo = _split_bf16(h)
    w_hi, w_lo = _split_bf16(wr_ref[...])
    logits = _dot(h_hi, w_hi) + (_dot(h_lo, w_hi) + _dot(h_hi, w_lo)) + br_ref[...]
    lane = lax.broadcasted_iota(jnp.int32, logits.shape, 1).astype(F32)
    logits = jnp.where(lane < N_EXPERTS, logits, NEG)
    m1 = jnp.max(logits, axis=-1, keepdims=True)
    i1 = jnp.min(jnp.where(logits == m1, lane, float(LANES)), axis=-1, keepdims=True)
    rest = jnp.where(lane == i1, NEG, logits)
    m2 = jnp.max(rest, axis=-1, keepdims=True)
    i2 = jnp.min(jnp.where(rest == m2, lane, float(LANES)), axis=-1, keepdims=True)
    e2 = jnp.exp(m2 - m1)
    den = 1.0 + e2
    sel1 = jnp.where(lane == i1, 1.0, 0.0)
    sel2 = jnp.where(lane == i2, 1.0, 0.0)
    sel = sel1 + sel2
    before = _dot(ltri_ref[...], sel.astype(BF16))
    r1 = jnp.sum(before * sel1, axis=-1, keepdims=True)
    r2 = jnp.sum(before * sel2, axis=-1, keepdims=True)
    cnt_ref[0] = jnp.sum(sel, axis=0, keepdims=True)
    cols = (i1, i2, 1.0 / den, e2 / den, r1, r2)
    meta = jnp.zeros_like(logits)
    for k, col in enumerate(cols):
        meta = jnp.where(lane == k, col, meta)
    meta_ref[...] = meta
    metat_ref[...] = meta.T[0:META_ROWS, :]


def _router_call(x2d, mod, tiles_per_mod, oa, ob, oc, w_out, g, wr, br, ltri):
    n, d = x2d.shape
    tm = MOE_TM
    nt = n // tm
    return pl.pallas_call(
        _router_kernel,
        grid=(nt,),
        in_specs=_mixer_specs(tm, d, tiles_per_mod, oa, ob, oc, w_out) + [
            pl.BlockSpec(g.shape, lambda i: (0, 0)),
            pl.BlockSpec(wr.shape, lambda i: (0, 0)),
            pl.BlockSpec(br.shape, lambda i: (0, 0)),
            pl.BlockSpec(ltri.shape, lambda i: (0, 0)),
        ],
        out_specs=[
            pl.BlockSpec((tm, d), lambda i: (i, 0)),
            pl.BlockSpec((tm, d), lambda i: (i, 0)),
            pl.BlockSpec((tm, LANES), lambda i: (i, 0)),
            pl.BlockSpec((META_ROWS, tm), lambda i: (0, i)),
            pl.BlockSpec((1, 1, LANES), lambda i: (i, 0, 0)),
        ],
        out_shape=[
            jax.ShapeDtypeStruct((n, d), F32),
            jax.ShapeDtypeStruct((n, d), BF16),
            jax.ShapeDtypeStruct((n, LANES), F32),
            jax.ShapeDtypeStruct((META_ROWS, n), F32),
            jax.ShapeDtypeStruct((nt, 1, LANES), F32),
        ],
        compiler_params=_cparams(("parallel",)),
        name="moe_router",
    )(x2d, mod, oa, ob, oc, w_out, g, wr, br, ltri)


def _pair_slots(meta, segoff_row):
    lane = lax.broadcasted_iota(jnp.int32, meta.shape, 1).astype(F32)
    i1 = meta[:, META_I1:META_I1 + 1]
    i2 = meta[:, META_I2:META_I2 + 1]
    s1 = jnp.sum(jnp.where(lane == i1, segoff_row, 0.0), axis=-1, keepdims=True) + meta[:, META_R1:META_R1 + 1]
    s2 = jnp.sum(jnp.where(lane == i2, segoff_row, 0.0), axis=-1, keepdims=True) + meta[:, META_R2:META_R2 + 1]
    return s1, s2


def _segment_copies(src, dst, src_off, dst_off, length, sem):
    out = []
    for size in SEG_PIECES:
        done = (length // (2 * size)) * (2 * size)
        s = pl.multiple_of(src_off + done, SEG_ALIGN)
        t = pl.multiple_of(dst_off + done, SEG_ALIGN)
        cp = pltpu.make_async_copy(src.at[pl.ds(s, size)], dst.at[pl.ds(t, size)], sem)
        out.append(((length & size) != 0, cp))
    return out


def _start_copies(copies):
    for pred, cp in copies:
        @pl.when(pred)
        def _(cp=cp):
            cp.start()


def _wait_copies(copies):
    for pred, cp in copies:
        @pl.when(pred)
        def _(cp=cp):
            cp.wait()


def _compact_kernel(segoff_s, base_s, len_s, h_ref, meta_ref, metat_ref, xs_in, gs_in, xs_out, gs_out,
                    cbuf2, gbuf2, sems):
    del xs_in, gs_in
    i = pl.program_id(0)
    last = pl.num_programs(0) - 1
    slot_i = i % 2
    cbuf = cbuf2.at[slot_i]
    gbuf = gbuf2.at[slot_i]

    def copies_of(tile, slot):
        out = []
        for e in range(N_EXPERTS):
            k = tile * N_EXPERTS + e
            out += _segment_copies(cbuf2.at[slot], xs_out, segoff_s[k], base_s[k], len_s[k], sems.at[0, slot])
            out += _segment_copies(gbuf2.at[slot], gs_out, segoff_s[k], base_s[k], len_s[k], sems.at[1, slot])
        return out

    @pl.when(i >= 2)
    def _():
        _wait_copies(copies_of(i - 2, slot_i))

    mt = metat_ref[...]
    s1 = mt[META_R1:META_R1 + 1]
    s2 = mt[META_R2:META_R2 + 1]
    for e in range(N_EXPERTS):
        off = segoff_s[i * N_EXPERTS + e].astype(F32)
        s1 = s1 + jnp.where(mt[META_I1:META_I1 + 1] == e, off, 0.0)
        s2 = s2 + jnp.where(mt[META_I2:META_I2 + 1] == e, off, 0.0)
    row = lax.broadcasted_iota(jnp.int32, (CBUF_ROWS, MOE_TM), 0).astype(F32)
    p1 = jnp.where(row == s1, 1.0, 0.0)
    p2 = jnp.where(row == s2, 1.0, 0.0)
    cbuf[...] = _dot((p1 + p2).astype(BF16), h_ref[...]).astype(BF16)
    meta = meta_ref[...]
    lane = lax.broadcasted_iota(jnp.int32, meta.shape, 1)
    g1 = meta[:, META_G1:META_G1 + 1]
    g2 = meta[:, META_G2:META_G2 + 1]
    g1_hi = g1.astype(BF16).astype(F32)
    g2_hi = g2.astype(BF16).astype(F32)
    a1 = jnp.where(lane == 0, g1_hi, jnp.where(lane == 1, g1 - g1_hi, 0.0)).astype(BF16)
    a2 = jnp.where(lane == 0, g2_hi, jnp.where(lane == 1, g2 - g2_hi, 0.0)).astype(BF16)
    gbuf[...] = _dot(p1.astype(BF16), a1) + _dot(p2.astype(BF16), a2)
    _start_copies(copies_of(i, slot_i))

    @pl.when(i == last)
    def _():
        @pl.when(i >= 1)
        def _():
            _wait_copies(copies_of(i - 1, 1 - slot_i))
        _wait_copies(copies_of(i, slot_i))


def _compact_call(sched, h, meta, metat, xs0, gs0):
    n, d = h.shape
    nt = n // MOE_TM
    grid_spec = pltpu.PrefetchScalarGridSpec(
        num_scalar_prefetch=3,
        grid=(nt,),
        in_specs=[
            pl.BlockSpec((MOE_TM, d), lambda i, *_: (i, 0)),
            pl.BlockSpec((MOE_TM, LANES), lambda i, *_: (i, 0)),
            pl.BlockSpec((META_ROWS, MOE_TM), lambda i, *_: (0, i)),
            pl.BlockSpec(memory_space=pl.ANY),
            pl.BlockSpec(memory_space=pl.ANY),
        ],
        out_specs=[pl.BlockSpec(memory_space=pl.ANY), pl.BlockSpec(memory_space=pl.ANY)],
        scratch_shapes=[pltpu.VMEM((2, CBUF_ROWS, d), BF16), pltpu.VMEM((2, CBUF_ROWS, LANES), F32),
                        pltpu.SemaphoreType.DMA((2, 2))],
    )
    return pl.pallas_call(
        _compact_kernel,
        grid_spec=grid_spec,
        out_shape=[jax.ShapeDtypeStruct(xs0.shape, xs0.dtype), jax.ShapeDtypeStruct(gs0.shape, gs0.dtype)],
        input_output_aliases={6: 0, 7: 1},
        compiler_params=_cparams(("arbitrary",)),
        name="moe_compact",
    )(*sched, h, meta, metat, xs0, gs0)


def _expert_kernel(exp_s, blk_s, valid_s, xs_ref, gs_ref, w1_ref, w3_ref, w2_ref, y_ref):
    del exp_s, blk_s
    j = pl.program_id(0)

    @pl.when(valid_s[j] != 0)
    def _():
        gate = gs_ref[:, 0:1] + gs_ref[:, 1:2]
        y_ref[...] = (_swiglu(xs_ref[...], w1_ref, w3_ref, w2_ref) * gate).astype(BF16)

    @pl.when(valid_s[j] == 0)
    def _():
        y_ref[...] = jnp.zeros_like(y_ref)


def _expert_call(tile_sched, xs, gs, w1, w3, w2):
    rows, d = xs.shape
    ff = w1.shape[1]
    grid_spec = pltpu.PrefetchScalarGridSpec(
        num_scalar_prefetch=3,
        grid=(rows // MOE_TM,),
        in_specs=[
            pl.BlockSpec((MOE_TM, d), lambda j, e_s, b_s, v_s: (b_s[j], 0)),
            pl.BlockSpec((MOE_TM, LANES), lambda j, e_s, b_s, v_s: (b_s[j], 0)),
            pl.BlockSpec((d, ff), lambda j, e_s, b_s, v_s: (e_s[j], 0)),
            pl.BlockSpec((d, ff), lambda j, e_s, b_s, v_s: (e_s[j], 0)),
            pl.BlockSpec((ff, d), lambda j, e_s, b_s, v_s: (e_s[j], 0)),
        ],
        out_specs=pl.BlockSpec((MOE_TM, d), lambda j, e_s, b_s, v_s: (j, 0)),
    )
    return pl.pallas_call(
        _expert_kernel,
        grid_spec=grid_spec,
        out_shape=jax.ShapeDtypeStruct((rows, d), BF16),
        compiler_params=_cparams(("arbitrary",)),
        name="moe_experts",
    )(*tile_sched, xs, gs, w1, w3, w2)


def _combine_kernel(segoff_s, base_s, len_s, x_ref, mod_ref, meta_ref, segoff_ref, y_hbm, o_ref, ybuf2, sems):
    i = pl.program_id(0)
    last = pl.num_programs(0) - 1
    slot_i = i % 2

    def copies_of(tile, slot):
        out = []
        for e in range(N_EXPERTS):
            k = tile * N_EXPERTS + e
            out += _segment_copies(y_hbm, ybuf2.at[slot], base_s[k], segoff_s[k], len_s[k], sems.at[slot])
        return out

    @pl.when(i == 0)
    def _():
        ybuf2[...] = jnp.zeros_like(ybuf2)
        _start_copies(copies_of(i, slot_i))

    @pl.when(i < last)
    def _():
        _start_copies(copies_of(i + 1, 1 - slot_i))

    _wait_copies(copies_of(i, slot_i))
    s1, s2 = _pair_slots(meta_ref[...], segoff_ref[0])
    slot = lax.broadcasted_iota(jnp.int32, (MOE_TM, CBUF_ROWS), 1).astype(F32)
    pick = (jnp.where(slot == s1, 1.0, 0.0) + jnp.where(slot == s2, 1.0, 0.0)).astype(BF16)
    o_ref[...] = x_ref[...] + mod_ref[0, 5:6, :] * _dot(pick, ybuf2[slot_i])


def _combine_call(sched, x2d, mod, tiles_per_mod, meta, segoff_v, y):
    n, d = x2d.shape
    grid_spec = pltpu.PrefetchScalarGridSpec(
        num_scalar_prefetch=3,
        grid=(n // MOE_TM,),
        in_specs=[
            pl.BlockSpec((MOE_TM, d), lambda i, *_: (i, 0)),
            pl.BlockSpec((1, MOD_ROWS, d), lambda i, *_: (i // tiles_per_mod, 0, 0)),
            pl.BlockSpec((MOE_TM, LANES), lambda i, *_: (i, 0)),
            pl.BlockSpec((1, 1, LANES), lambda i, *_: (i, 0, 0)),
            pl.BlockSpec(memory_space=pl.ANY),
        ],
        out_specs=pl.BlockSpec((MOE_TM, d), lambda i, *_: (i, 0)),
        scratch_shapes=[pltpu.VMEM((2, CBUF_ROWS, d), BF16), pltpu.SemaphoreType.DMA((2,))],
    )
    return pl.pallas_call(
        _combine_kernel,
        grid_spec=grid_spec,
        out_shape=jax.ShapeDtypeStruct((n, d), F32),
        compiler_params=_cparams(("arbitrary",)),
        name="moe_combine",
    )(*sched, x2d, mod, meta, segoff_v, y)


def _moe_call(x2d, mod, tiles_per_mod, outs, w_out, g, wr, br, w1, w3, w2):
    n, d = x2d.shape
    nt = n // MOE_TM
    ltri = jnp.asarray(np.tril(np.ones((MOE_TM, MOE_TM), np.float32), -1), BF16)
    x2d, h, meta, metat, counts = _router_call(x2d, mod, tiles_per_mod, *outs, w_out, g, wr, br, ltri)

    cnt = counts[:, 0, :N_EXPERTS].astype(jnp.int32)
    seg_len = (cnt + SEG_ALIGN - 1) // SEG_ALIGN * SEG_ALIGN
    segoff = jnp.cumsum(seg_len, axis=1) - seg_len
    region = (jnp.sum(seg_len, axis=0) + MOE_TM - 1) // MOE_TM * MOE_TM
    region_start = jnp.cumsum(region) - region
    base = region_start[None, :] + jnp.cumsum(seg_len, axis=0) - seg_len
    rows_max = 2 * n + nt * N_EXPERTS * (SEG_ALIGN - 1) + N_EXPERTS * (MOE_TM - 1)
    n_sorted_tiles = (rows_max + MOE_TM - 1) // MOE_TM
    tile_end = jnp.cumsum(region // MOE_TM)
    total_tiles = tile_end[-1]
    jt = jnp.arange(n_sorted_tiles, dtype=jnp.int32)
    blk = jnp.minimum(jt, total_tiles - 1)
    tile_expert = jnp.sum((blk[:, None] >= tile_end[None, :]).astype(jnp.int32), axis=1)
    tile_sched = (tile_expert.astype(jnp.int32), blk.astype(jnp.int32), (jt < total_tiles).astype(jnp.int32))
    sched = tuple(a.reshape(-1).astype(jnp.int32) for a in (segoff, base, seg_len))
    segoff_v = jnp.pad(segoff.astype(F32), ((0, 0), (0, LANES - N_EXPERTS)))[:, None, :]

    rows = n_sorted_tiles * MOE_TM
    xs, gs = _compact_call(sched, h, meta, metat, jnp.zeros((rows, d), BF16), jnp.zeros((rows, LANES), F32))
    y = _expert_call(tile_sched, xs, gs, w1, w3, w2)
    return _combine_call(sched, x2d, mod, tiles_per_mod, meta, segoff_v, y)


def _rope_tables(t):
    rows = jnp.arange(t, dtype=F32) // GRID_W
    cols = jnp.arange(t, dtype=F32) % GRID_W

    def tables(rot_dim):
        a = rot_dim // 2
        inv = 1.0 / (ROPE_BASE ** (jnp.arange(0, a, 2, dtype=F32) / a))
        ar = rows[:, None] * inv
        ac = cols[:, None] * inv
        cos = jnp.concatenate([jnp.cos(ar), jnp.cos(ar), jnp.cos(ac), jnp.cos(ac)], axis=-1)
        sin = jnp.concatenate([-jnp.sin(ar), jnp.sin(ar), -jnp.sin(ac), jnp.sin(ac)], axis=-1)
        return cos, sin

    ca, sa = tables(HEAD_DIM)
    cos_a = jnp.tile(ca, (1, LANES // HEAD_DIM))
    sin_a = jnp.tile(sa, (1, LANES // HEAD_DIM))
    cc, sc = tables(C_ROPE)
    ones = jnp.ones((t, C_NOPE), F32)
    tail = C_SLOT - C_NOPE - C_ROPE
    cos_c = jnp.concatenate([ones, cc, jnp.ones((t, tail), F32)], axis=-1)
    sin_c = jnp.concatenate([0 * ones, sc, jnp.zeros((t, tail), F32)], axis=-1)
    return cos_a, sin_a, cos_c, sin_c


def _head_perm():
    order = []
    for j in range(A_HEADS // 2):
        order += [j, A_HEADS // 2 + j]
    return np.concatenate([np.arange(h * HEAD_DIM, (h + 1) * HEAD_DIM) for h in order])


def _segment_mean_matrix(widths, total):
    m = np.zeros((total, total), np.float32)
    o = 0
    while o < total:
        for w, used in widths:
            if used:
                m[o:o + w, o:o + w] = 1.0 / w
            o += w
    return jnp.asarray(m, BF16)


def _slot_vec(nope, rope):
    z = jnp.zeros((C_SLOT - C_NOPE - C_ROPE,), F32)
    n = jnp.zeros((C_NOPE,), F32) if nope is None else nope
    r = jnp.zeros((C_ROPE,), F32) if rope is None else rope
    return jnp.tile(jnp.concatenate([n, r, z]), C_HEADS)[None, :]


def _layer_consts(i, p, perm, tabs):
    w_in = p["w_in"][i]
    o_kr = A_Q + 2 * A_KV + 2 * B_CH + C_Q_RANK + C_KV_RANK
    d = w_in.shape[0]
    kr_cols = jnp.concatenate([jnp.zeros((d, C_NOPE), F32), w_in[:, o_kr:o_kr + C_ROPE],
                               jnp.zeros((d, C_SLOT - C_NOPE - C_ROPE), F32)], axis=1)
    win = jnp.concatenate([w_in[:, :A_Q][:, perm], w_in[:, A_Q:o_kr], kr_cols], axis=1).astype(BF16)

    w_uq = p["c_w_uq"][i].reshape(C_Q_RANK, C_HEADS, C_NOPE + C_ROPE)
    wuq = jnp.pad(w_uq, ((0, 0), (0, 0), (0, C_SLOT - C_NOPE - C_ROPE))).reshape(C_Q_RANK, C_HEADS * C_SLOT)
    w_ukv = p["c_w_ukv"][i].reshape(C_KV_RANK, C_HEADS, C_NOPE + C_V)
    wukvk = jnp.pad(w_ukv[:, :, :C_NOPE], ((0, 0), (0, 0), (0, C_SLOT - C_NOPE))).reshape(C_KV_RANK, -1)
    wukvv = w_ukv[:, :, C_NOPE:].reshape(C_KV_RANK, C_HEADS * C_V)

    sa = _segment_mean_matrix([(HEAD_DIM, True)], MXU_TILE)
    sc = _segment_mean_matrix([(C_NOPE, True), (C_ROPE, True), (C_SLOT - C_NOPE - C_ROPE, False)], MXU_TILE)
    gq = jnp.tile(p["a_q_norm_g"][i] * (A_SCALE * LOG2E), A_HEADS)[None, :]
    gk = jnp.tile(p["a_k_norm_g"][i], A_KV_HEADS)[None, :]
    gqc = _slot_vec(p["c_q_nope_norm_g"][i], p["c_q_rope_norm_g"][i]) * (MLA_SCALE * LOG2E)
    gkn = _slot_vec(p["c_k_nope_norm_g"][i], None)
    gkr = _slot_vec(None, p["c_k_rope_norm_g"][i])[:, :C_SLOT]
    return (p["mix_norm_g"][i][None, :], win) + tabs + (
        sa, sc, gq, gk, p["c_q_rank_norm_g"][i][None, :], p["c_kv_rank_norm_g"][i][None, :],
        wuq.astype(BF16), wukvk.astype(BF16), wukvv.astype(BF16), gqc, gkn, gkr)


def kernel(x, c, ctx, c_ctx, ada_w, ada_b, mix_norm_g, ffn_norm_g, w_in, w_out, a_q_norm_g, a_k_norm_g, a_sink, b_conv_w, b_conv_b, b_ln_g, b_ln_b, c_q_rank_norm_g, c_kv_rank_norm_g, c_w_uq, c_w_ukv, c_q_nope_norm_g, c_k_nope_norm_g, c_q_rope_norm_g, c_k_rope_norm_g, dense_w1, dense_w3, dense_w2, moe_router_w, moe_router_b, moe_w1, moe_w3, moe_w2):
    p = dict(w_in=w_in, c_w_uq=c_w_uq, c_w_ukv=c_w_ukv, a_q_norm_g=a_q_norm_g, a_k_norm_g=a_k_norm_g,
             c_q_nope_norm_g=c_q_nope_norm_g, c_k_nope_norm_g=c_k_nope_norm_g,
             c_q_rope_norm_g=c_q_rope_norm_g, c_k_rope_norm_g=c_k_rope_norm_g,
             c_q_rank_norm_g=c_q_rank_norm_g, c_kv_rank_norm_g=c_kv_rank_norm_g, mix_norm_g=mix_norm_g)
    bsz, t, d = x.shape
    ctx_len = ctx.shape[1]
    depth = ada_w.shape[0]
    n_x, n_c = bsz * t, bsz * ctx_len
    tm_pre = 256
    tm_tok = 512
    tq = min(1024, t)

    ada_rows = ((bsz + 1 + 7) // 8) * 8
    c_pad = jnp.concatenate([c, c_ctx[None, :], jnp.zeros((ada_rows - bsz - 1, d), F32)], axis=0)
    mods = _ada_call(c_pad, ada_w, ada_b).reshape(depth, ada_rows, 6, d)
    mods = jnp.pad(mods, ((0, 0), (0, 0), (0, MOD_ROWS - 6), (0, 0)))

    tabs_x = _rope_tables(t)
    ones = jnp.ones((tm_pre, LANES), F32)
    tabs_c = (ones, 0 * ones, ones, 0 * ones)
    perm = _head_perm()

    x2 = x.reshape(n_x, d)
    c2 = ctx.reshape(n_c, d)
    for i in range(depth):
        last = i == depth - 1
        mod_x = mods[i, :bsz]
        mod_c = mods[i, bsz:bsz + 1]
        consts_x = _layer_consts(i, p, perm, tabs_x)
        consts_c = _layer_consts(i, p, perm, tabs_c)

        qa_x, ka_x, va_x, u_x, qc_x, kc_x, vc_x = _pre_call(x2, mod_x, t // tm_pre, t // tm_pre, consts_x, tm_pre)
        qa_c, ka_c, va_c, u_c, qc_c, kc_c, vc_c = _pre_call(c2, mod_c, n_c // tm_pre, 1, consts_c, tm_pre)

        def r3(a, length):
            return a.reshape(bsz, length, a.shape[-1])

        sink_row = jnp.repeat(a_sink[i] * LOG2E, BLOCK)[None, :]
        conv_w = jnp.pad(b_conv_w[i], ((0, 32 - B_WIDTH), (0, 0)))
        conv_p = (conv_w, b_conv_b[i][None, :], b_ln_g[i][None, :], b_ln_b[i][None, :])
        w_o = jnp.concatenate([w_out[i][:A_Q][perm], w_out[i][A_Q:]], axis=0).astype(BF16)

        o_a = _attn_a_call(qa_x, r3(ka_x, t), va_x, r3(ka_c, ctx_len), va_c, sink_row, bsz, t)
        o_b = _conv_call(r3(u_x, t), *conv_p).reshape(n_x, B_CH)
        o_c = _mla_call(qc_x, r3(kc_x, t), vc_x, r3(kc_c, ctx_len), vc_c, bsz, t, tq)
        if not last:
            oc_a = _attn_a_call(qa_c, None, None, r3(ka_c, ctx_len), va_c, sink_row, bsz, ctx_len)
            oc_b = _conv_call(r3(u_c, ctx_len), *conv_p).reshape(n_c, B_CH)
            oc_c = _mla_call(qc_c, None, None, r3(kc_c, ctx_len), vc_c, bsz, ctx_len, ctx_len)

        j = i // 2
        g_ffn = ffn_norm_g[i][None, :]
        if i % 2 == 0:
            w = (dense_w1[j].astype(BF16), dense_w3[j].astype(BF16), dense_w2[j].astype(BF16))

            def mix(a2, mod, tiles_per_mod, outs, w=w, g_ffn=g_ffn, w_o=w_o):
                return _ffn_call(a2, mod, tiles_per_mod, *outs, w_o, g_ffn, *w, tm_tok)
        else:
            w = tuple(a.astype(BF16).reshape(-1, a.shape[-1]) for a in (moe_w1[j], moe_w3[j], moe_w2[j]))
            wr = jnp.pad(moe_router_w[j], ((0, 0), (0, LANES - N_EXPERTS)))
            br = jnp.pad(moe_router_b[j], (0, LANES - N_EXPERTS))[None, :]

            def mix(a2, mod, tiles_per_mod, outs, w=w, g_ffn=g_ffn, wr=wr, br=br, w_o=w_o):
                return _moe_call(a2, mod, tiles_per_mod * (tm_tok // MOE_TM), outs, w_o, g_ffn, wr, br, *w)
        x2 = mix(x2, mod_x, t // tm_tok, (o_a, o_b, o_c))
        if not last:
            c2 = mix(c2, mod_c, n_c // tm_tok, (oc_a, oc_b, oc_c))
    return x2.reshape(bsz, t, d)
```

```python
import functools

import jax
import jax.numpy as jnp
import numpy as np
from jax import lax
from jax.experimental import pallas as pl
from jax.experimental.pallas import tpu as pltpu

F32 = jnp.float32
BF16 = jnp.bfloat16

D_MODEL = 1024
GRID_W = 64
HEAD_DIM = 64
A_HEADS = 8
A_KV_HEADS = 2
A_WINDOW = 128
BLOCK = 128
B_CH = 256
B_WIDTH = 31
C_HEADS = 4
C_Q_RANK = 384
C_KV_RANK = 256
C_NOPE = 64
C_ROPE = 32
C_V = 64
A_Q = A_HEADS * HEAD_DIM
A_KV = A_KV_HEADS * HEAD_DIM
IN_COLS_PAD = 2048
D_FF = 2816
N_EXPERTS = 8
ROPE_BASE = 10000.0
EPS = 1e-6
NEG = -1e30
A_SCALE = HEAD_DIM ** -0.5
MLA_SCALE = (C_NOPE + C_ROPE) ** -0.5
LOG2E = 1.4426950408889634

LANES = 128
MXU_TILE = 256
MOD_ROWS = 8
C_SLOT = 128
VMEM_LIMIT = 56 * 1024 * 1024


def _cparams(sem):
    return pltpu.CompilerParams(dimension_semantics=sem, vmem_limit_bytes=VMEM_LIMIT)


def _dot(a, b):
    return jnp.dot(a, b, preferred_element_type=F32)


def _dot_nt(a, b):
    return lax.dot_general(a, b, (((1,), (1,)), ((), ())), preferred_element_type=F32)


def _ada_kernel(c_ref, w_ref, b_ref, o_ref):
    c = c_ref[...]
    a = c * jax.nn.sigmoid(c)
    o_ref[0] = _dot(a.astype(BF16), w_ref[0].astype(BF16)) + b_ref[0]


def _ada_call(c_pad, ada_w, ada_b):
    depth, d, n6 = ada_w.shape
    rows = c_pad.shape[0]
    tn = 1536
    return pl.pallas_call(
        _ada_kernel,
        grid=(depth, n6 // tn),
        in_specs=[
            pl.BlockSpec((rows, d), lambda i, j: (0, 0)),
            pl.BlockSpec((1, d, tn), lambda i, j: (i, 0, j)),
            pl.BlockSpec((1, 1, tn), lambda i, j: (i, 0, j)),
        ],
        out_specs=pl.BlockSpec((1, rows, tn), lambda i, j: (i, 0, j)),
        out_shape=jax.ShapeDtypeStruct((depth, rows, n6), F32),
        compiler_params=_cparams(("parallel", "parallel")),
        name="ada_proj",
    )(c_pad, ada_w, ada_b.reshape(depth, 1, n6))


def _rope_chunk(c, cos, sin, half):
    lane = lax.broadcasted_iota(jnp.int32, c.shape, 1)
    lo = (lane & (2 * half - 1)) < half
    partner = jnp.where(lo, pltpu.roll(c, LANES - half, 1), pltpu.roll(c, half, 1))
    return c * cos + partner * sin


PRE_SUB = 256


def _segment_mean(sq, s_ref):
    w = s_ref.shape[0]
    return jnp.concatenate([_dot(sq[:, c:c + w].astype(BF16), s_ref[...]) for c in range(0, sq.shape[1], w)], axis=1)


def _pre_kernel(x_ref, mod_ref, gmix_ref, win_ref, cosa_ref, sina_ref, cosc_ref, sinc_ref,
                sa_ref, sc_ref, gq_ref, gk_ref, gcq_ref, gckv_ref, wuq_ref, wukvk_ref, wukvv_ref,
                gqc_ref, gkn_ref, gkr_ref,
                qa_ref, ka_ref, va_ref, u_ref, qc_ref, kc_ref, vc_ref):
    for sub in range(x_ref.shape[0] // PRE_SUB):
        rows = slice(sub * PRE_SUB, (sub + 1) * PRE_SUB)
        x = x_ref[rows, :]
        ms = jnp.mean(x * x, axis=-1, keepdims=True)
        y = x * lax.rsqrt(ms + EPS)
        shift = mod_ref[0, 0:1, :]
        scale = mod_ref[0, 1:2, :]
        h = (y * gmix_ref[...]) * (1.0 + scale) + shift
        p = _dot(h.astype(BF16), win_ref[...])

        cosa, sina = cosa_ref[rows, :], sina_ref[rows, :]
        cosc, sinc = cosc_ref[rows, :], sinc_ref[rows, :]

        qa = p[:, 0:A_Q]
        ssq = _segment_mean(qa * qa, sa_ref)
        qa = qa * lax.rsqrt(ssq + EPS) * gq_ref[...]
        for j in range(A_Q // LANES):
            sl = slice(j * LANES, (j + 1) * LANES)
            qa_ref[rows, sl] = _rope_chunk(qa[:, sl], cosa, sina, HEAD_DIM // 4).astype(BF16)

        ka = p[:, A_Q:A_Q + A_KV]
        ssk = _dot((ka * ka).astype(BF16), sa_ref[0:A_KV, 0:A_KV])
        ka = ka * lax.rsqrt(ssk + EPS) * gk_ref[...]
        ka_ref[rows, :] = _rope_chunk(ka, cosa, sina, HEAD_DIM // 4).astype(BF16)
        va_ref[:, rows] = p[:, A_Q + A_KV:A_Q + 2 * A_KV].T.astype(BF16)

        o = A_Q + 2 * A_KV
        u_ref[rows, :] = p[:, o:o + B_CH] * jax.nn.sigmoid(p[:, o + B_CH:o + 2 * B_CH])

        o = o + 2 * B_CH
        cq = p[:, o:o + C_Q_RANK]
        cq = cq * lax.rsqrt(jnp.mean(cq * cq, axis=-1, keepdims=True) + EPS) * gcq_ref[...]
        qc = _dot(cq.astype(BF16), wuq_ref[...])
        ssq = _segment_mean(qc * qc, sc_ref)
        qc = qc * lax.rsqrt(ssq + EPS) * gqc_ref[...]
        for j in range(C_HEADS):
            sl = slice(j * C_SLOT, (j + 1) * C_SLOT)
            qc_ref[rows, sl] = _rope_chunk(qc[:, sl], cosc, sinc, C_ROPE // 4).astype(BF16)

        o = o + C_Q_RANK
        ckv = p[:, o:o + C_KV_RANK]
        ckv = (ckv * lax.rsqrt(jnp.mean(ckv * ckv, axis=-1, keepdims=True) + EPS) * gckv_ref[...]).astype(BF16)
        kn = _dot(ckv, wukvk_ref[...])
        vc_ref[:, rows] = _dot(ckv, wukvv_ref[...]).T.astype(BF16)
        ssk = _segment_mean(kn * kn, sc_ref)
        kn = kn * lax.rsqrt(ssk + EPS) * gkn_ref[...]
        o = o + C_KV_RANK
        kr = p[:, o:o + C_SLOT]
        kr = kr * lax.rsqrt(jnp.sum(kr * kr, axis=-1, keepdims=True) * (1.0 / C_ROPE) + EPS) * gkr_ref[...]
        kr = _rope_chunk(kr, cosc, sinc, C_ROPE // 4)
        for j in range(C_HEADS):
            sl = slice(j * C_SLOT, (j + 1) * C_SLOT)
            kc_ref[rows, sl] = (kn[:, sl] + kr).astype(BF16)


def _pre_call(x2d, mod, tiles_per_mod, tab_tiles, consts, tm):
    n, d = x2d.shape
    (gmix, win, cosa, sina, cosc, sinc, sa, sc, gq, gk, gcq, gckv, wuq, wukvk, wukvv, gqc, gkn, gkr) = consts

    def const(a):
        return pl.BlockSpec(a.shape, lambda i: (0,) * a.ndim)

    def tab(a):
        return pl.BlockSpec((tm, LANES), lambda i: (i % tab_tiles, 0))

    in_specs = [
        pl.BlockSpec((tm, d), lambda i: (i, 0)),
        pl.BlockSpec((1, MOD_ROWS, d), lambda i: (i // tiles_per_mod, 0, 0)),
        const(gmix), const(win), tab(cosa), tab(sina), tab(cosc), tab(sinc),
        const(sa), const(sc), const(gq), const(gk), const(gcq), const(gckv),
        const(wuq), const(wukvk), const(wukvv), const(gqc), const(gkn), const(gkr),
    ]
    widths = (A_Q, A_KV, A_KV, B_CH, C_HEADS * C_SLOT, C_HEADS * C_SLOT, C_HEADS * C_V)
    dtypes = (BF16, BF16, BF16, F32, BF16, BF16, BF16)
    out_specs = [pl.BlockSpec((tm, w), lambda i: (i, 0)) for w in widths]
    out_shape = [jax.ShapeDtypeStruct((n, w), dt) for w, dt in zip(widths, dtypes)]
    out_specs[2] = pl.BlockSpec((A_KV, tm), lambda i: (0, i))
    out_shape[2] = jax.ShapeDtypeStruct((A_KV, n), BF16)
    out_specs[6] = pl.BlockSpec((C_HEADS * C_V, tm), lambda i: (0, i))
    out_shape[6] = jax.ShapeDtypeStruct((C_HEADS * C_V, n), BF16)
    return pl.pallas_call(
        _pre_kernel,
        grid=(n // tm,),
        in_specs=in_specs,
        out_specs=out_specs,
        out_shape=out_shape,
        compiler_params=_cparams(("parallel",)),
        name="pre_attn",
    )(x2d, mod, gmix, win, cosa, sina, cosc, sinc, sa, sc, gq, gk, gcq, gckv, wuq, wukvk, wukvv,
      gqc, gkn, gkr)


ATTN_A_QBLOCKS = 8


ONES_ROWS = 16


def _attn_a_kernel(*refs, t, has_local, qblocks):
    if has_local:
        q_ref, k_ref, vt_ref, kc_ref, vct_ref, sink_ref, bias_ref, o_ref = refs
    else:
        q_ref, kc_ref, vct_ref, sink_ref, o_ref = refs
    nchunk = A_Q // LANES
    span = 3 * BLOCK
    lane = lax.broadcasted_iota(jnp.int32, (BLOCK, LANES), 1)
    row = lax.broadcasted_iota(jnp.int32, (A_KV, BLOCK), 0)
    zero = jnp.zeros((BLOCK, LANES), BF16)
    sink = sink_ref[...]
    kc = kc_ref[0]
    vct = vct_ref[...]
    for blk in range(qblocks):
        rows = slice(blk * BLOCK, (blk + 1) * BLOCK)
        chunks = [q_ref[rows, j * LANES:(j + 1) * LANES] for j in range(nchunk)]
        qs = jnp.concatenate([jnp.where(lane < HEAD_DIM, c, zero) for c in chunks]
                             + [jnp.where(lane >= HEAD_DIM, c, zero) for c in chunks], axis=0)
        if has_local:
            n = pl.program_id(1) * qblocks + blk
            start = pl.multiple_of(jnp.clip((n - 1) * BLOCK, 0, t - span), BLOCK)
            keys = jnp.concatenate([k_ref[0, pl.ds(start, span), :], kc], axis=0)
            vt = jnp.concatenate([vt_ref[:, pl.ds(start, span)], vct], axis=1)
        else:
            keys, vt = kc, vct
        s = _dot_nt(keys, qs)
        if has_local:
            s = jnp.concatenate([s[:span] + bias_ref[n - start // BLOCK], s[span:]], axis=0)
        m = jnp.maximum(jnp.max(s, axis=0, keepdims=True), sink)
        e = jnp.exp2(s - m).astype(BF16)
        vte = jnp.concatenate([vt, jnp.ones((ONES_ROWS, vt.shape[1]), BF16)], axis=0)
        acc = _dot(vte, e)
        den = acc[A_KV:A_KV + 1] + jnp.exp2(sink - m)
        out = acc[:A_KV] * (1.0 / den)
        for j in range(nchunk):
            x = jnp.where(row < HEAD_DIM, out[:, j * BLOCK:(j + 1) * BLOCK],
                          out[:, (nchunk + j) * BLOCK:(nchunk + j + 1) * BLOCK])
            o_ref[rows, j * LANES:(j + 1) * LANES] = x.T.astype(BF16)


def _window_bias():
    r = np.arange(BLOCK)[None, :]
    c = np.arange(3 * BLOCK)[:, None]
    pats = [np.where(np.abs(c - p * BLOCK - r) <= A_WINDOW, 0.0, NEG) for p in range(3)]
    return jnp.asarray(np.stack([np.tile(p, (1, A_HEADS)) for p in pats]), F32)


def _attn_a_call(qa, ka, vat, kac, vact, sink_row, bsz, t):
    has_local = ka is not None
    qblocks = min(ATTN_A_QBLOCKS, t // BLOCK)
    tq = qblocks * BLOCK
    nq = t // tq
    ctx_len = kac.shape[1]
    in_specs = [pl.BlockSpec((tq, A_Q), lambda b, n: (b * nq + n, 0))]
    args = [qa]
    if has_local:
        in_specs += [pl.BlockSpec((1, t, A_KV), lambda b, n: (b, 0, 0)),
                     pl.BlockSpec((A_KV, t), lambda b, n: (0, b))]
        args += [ka, vat]
    in_specs += [pl.BlockSpec((1, ctx_len, A_KV), lambda b, n: (b, 0, 0)),
                 pl.BlockSpec((A_KV, ctx_len), lambda b, n: (0, b))]
    in_specs += [pl.BlockSpec(sink_row.shape, lambda b, n: (0, 0))]
    args += [kac, vact, sink_row]
    if has_local:
        bias = _window_bias()
        in_specs += [pl.BlockSpec(bias.shape, lambda b, n: (0, 0, 0))]
        args += [bias]
    return pl.pallas_call(
        functools.partial(_attn_a_kernel, t=t, has_local=has_local, qblocks=qblocks),
        grid=(bsz, nq),
        in_specs=in_specs,
        out_specs=pl.BlockSpec((tq, A_Q), lambda b, n: (b * nq + n, 0)),
        out_shape=jax.ShapeDtypeStruct((bsz * t, A_Q), BF16),
        compiler_params=_cparams(("parallel", "parallel")),
        name="attn_a_local" if has_local else "attn_a_ctx",
    )(*args)


CONV_PAD = 16
CONV_CHUNK = 128
SUBLANES = 8


def _conv_kernel(u_ref, w_ref, b_ref, g_ref, beta_ref, o_ref, pad_ref, *, t):
    zeros = jnp.zeros((CONV_PAD, B_CH), F32)
    pad_ref[0:CONV_PAD, :] = zeros
    pad_ref[CONV_PAD + t:CONV_PAD + t + CONV_PAD, :] = zeros
    pad_ref[CONV_PAD:CONV_PAD + t, :] = u_ref[0]
    off = CONV_PAD - B_WIDTH // 2
    nq = (off + B_WIDTH - 1) // SUBLANES + 1
    win = CONV_CHUNK + (nq - 1) * SUBLANES
    for c in range(t // CONV_CHUNK):
        base = c * CONV_CHUNK
        acc = jnp.zeros((CONV_CHUNK, B_CH), F32)
        for r in range(SUBLANES):
            taps = [k for k in range(B_WIDTH) if (off + k) % SUBLANES == r]
            if not taps:
                continue
            w_r = pad_ref[base + r:base + r + win, :]
            part = None
            for k in taps:
                q = (off + k) // SUBLANES
                term = w_r[q * SUBLANES:q * SUBLANES + CONV_CHUNK, :] * w_ref[k:k + 1, :]
                part = term if part is None else part + term
            acc = acc + part
        y = acc + b_ref[...]
        mu = jnp.mean(y, axis=-1, keepdims=True)
        yc = y - mu
        var = jnp.mean(yc * yc, axis=-1, keepdims=True)
        z = yc * lax.rsqrt(var + EPS) * g_ref[...] + beta_ref[...]
        o_ref[0, c * CONV_CHUNK:(c + 1) * CONV_CHUNK, :] = (z * jax.nn.sigmoid(z)).astype(BF16)


def _conv_call(u3, w_pad, b, g, beta):
    bsz, t, ch = u3.shape

    def const(a):
        return pl.BlockSpec(a.shape, lambda i: (0, 0))

    return pl.pallas_call(
        functools.partial(_conv_kernel, t=t),
        grid=(bsz,),
        in_specs=[pl.BlockSpec((1, t, ch), lambda i: (i, 0, 0)), const(w_pad), const(b), const(g), const(beta)],
        out_specs=pl.BlockSpec((1, t, ch), lambda i: (i, 0, 0)),
        out_shape=jax.ShapeDtypeStruct((bsz, t, ch), BF16),
        scratch_shapes=[pltpu.VMEM((t + 2 * CONV_PAD, ch), F32)],
        compiler_params=_cparams(("parallel",)),
        name="conv_module",
    )(u3, w_pad, b, g, beta)


def _mla_kernel(*refs, has_local):
    if has_local:
        q_ref, kx_ref, vxt_ref, kc_ref, vct_ref, o_ref = refs
    else:
        q_ref, kc_ref, vct_ref, o_ref = refs
    def scores(h):
        sl = slice(h * C_SLOT, (h + 1) * C_SLOT)
        q = q_ref[:, sl]
        s_c = _dot_nt(kc_ref[0, :, sl], q)
        s_x = _dot_nt(kx_ref[0, :, sl], q) if has_local else None
        return s_c, s_x

    outs = []
    nxt = scores(0)
    for h in range(C_HEADS):
        vs = slice(h * C_V, (h + 1) * C_V)
        s_c, s_x = nxt
        if h + 1 < C_HEADS:
            nxt = scores(h + 1)
        m = jnp.max(s_c, axis=0, keepdims=True)
        if has_local:
            m = jnp.maximum(m, jnp.max(s_x, axis=0, keepdims=True))
        e_c = jnp.exp2(s_c - m).astype(BF16)
        vt = jnp.concatenate([vct_ref[vs, :], jnp.ones((ONES_ROWS, e_c.shape[0]), BF16)], axis=0)
        acc = _dot(vt, e_c)
        if has_local:
            e_x = jnp.exp2(s_x - m).astype(BF16)
            vt = jnp.concatenate([vxt_ref[vs, :], jnp.ones((ONES_ROWS, e_x.shape[0]), BF16)], axis=0)
            acc = acc + _dot(vt, e_x)
        outs.append(acc[:C_V] * (1.0 / acc[C_V:C_V + 1]))
    o_ref[...] = jnp.concatenate(outs, axis=0).T.astype(BF16)


def _mla_call(qc, kx, vxt, kcc, vcct, bsz, t, tq):
    has_local = kx is not None
    nq = t // tq
    ctx_len = kcc.shape[1]
    wq = C_HEADS * C_SLOT
    wv = C_HEADS * C_V
    in_specs = [pl.BlockSpec((tq, wq), lambda b, n: (b * nq + n, 0))]
    args = [qc]
    if has_local:
        in_specs += [pl.BlockSpec((1, t, wq), lambda b, n: (b, 0, 0)),
                     pl.BlockSpec((wv, t), lambda b, n: (0, b))]
        args += [kx, vxt]
    in_specs += [pl.BlockSpec((1, ctx_len, wq), lambda b, n: (b, 0, 0)),
                 pl.BlockSpec((wv, ctx_len), lambda b, n: (0, b))]
    args += [kcc, vcct]
    return pl.pallas_call(
        functools.partial(_mla_kernel, has_local=has_local),
        grid=(bsz, nq),
        in_specs=in_specs,
        out_specs=pl.BlockSpec((tq, wv), lambda b, n: (b * nq + n, 0)),
        out_shape=jax.ShapeDtypeStruct((bsz * t, wv), BF16),
        compiler_params=_cparams(("parallel", "parallel")),
        name="mla_local" if has_local else "mla_ctx",
    )(*args)


def _mixer_residual(x_ref, mod_ref, oa_ref, ob_ref, oc_ref, wo_ref):
    y = _dot(oa_ref[...], wo_ref[0:A_Q, :])
    y = y + _dot(ob_ref[...], wo_ref[A_Q:A_Q + B_CH, :])
    y = y + _dot(oc_ref[...], wo_ref[A_Q + B_CH:, :])
    return x_ref[...] + mod_ref[0, 2:3, :] * y


def _mixer_specs(tm, d, tiles_per_mod, oa, ob, oc, w_out):
    return [
        pl.BlockSpec((tm, d), lambda i: (i, 0)),
        pl.BlockSpec((1, MOD_ROWS, d), lambda i: (i // tiles_per_mod, 0, 0)),
        pl.BlockSpec((tm, oa.shape[1]), lambda i: (i, 0)),
        pl.BlockSpec((tm, ob.shape[1]), lambda i: (i, 0)),
        pl.BlockSpec((tm, oc.shape[1]), lambda i: (i, 0)),
        pl.BlockSpec(w_out.shape, lambda i: (0, 0), pipeline_mode=pl.Buffered(1)),
    ]


def _split_bf16(a):
    hi = a.astype(BF16)
    lo = (a - hi.astype(F32)).astype(BF16)
    return hi, lo


def _ffn_input(x, mod_ref, g_ref):
    ms = jnp.mean(x * x, axis=-1, keepdims=True)
    y = x * lax.rsqrt(ms + EPS)
    return (y * g_ref[...]) * (1.0 + mod_ref[0, 4:5, :]) + mod_ref[0, 3:4, :]


def _ff_chunks(ff):
    tiles = ff // MXU_TILE
    if ff % MXU_TILE or tiles < 2:
        return (ff,)
    first = (tiles // 2) * MXU_TILE
    return (first, ff - first)


def _swiglu(h, w1_ref, w3_ref, w2_ref):
    y = None
    o = 0
    for tf in _ff_chunks(w1_ref.shape[1]):
        a = _dot(h, w1_ref[:, o:o + tf])
        b = _dot(h, w3_ref[:, o:o + tf])
        g = (a * jax.nn.sigmoid(a) * b).astype(BF16)
        yc = _dot(g, w2_ref[o:o + tf, :])
        y = yc if y is None else y + yc
        o += tf
    return y


def _ffn_kernel(x_ref, mod_ref, oa_ref, ob_ref, oc_ref, wo_ref, g_ref, w1_ref, w3_ref, w2_ref, o_ref):
    x = _mixer_residual(x_ref, mod_ref, oa_ref, ob_ref, oc_ref, wo_ref)
    h = _ffn_input(x, mod_ref, g_ref).astype(BF16)
    o_ref[...] = x + mod_ref[0, 5:6, :] * _swiglu(h, w1_ref, w3_ref, w2_ref)


def _ffn_call(x2d, mod, tiles_per_mod, oa, ob, oc, w_out, g, w1, w3, w2, tm):
    n, d = x2d.shape

    def resident(a):
        return pl.BlockSpec(a.shape, lambda i: (0, 0), pipeline_mode=pl.Buffered(1))

    return pl.pallas_call(
        _ffn_kernel,
        grid=(n // tm,),
        in_specs=_mixer_specs(tm, d, tiles_per_mod, oa, ob, oc, w_out) + [
            pl.BlockSpec(g.shape, lambda i: (0, 0)), resident(w1), resident(w3), resident(w2)],
        out_specs=pl.BlockSpec((tm, d), lambda i: (i, 0)),
        out_shape=jax.ShapeDtypeStruct((n, d), F32),
        compiler_params=_cparams(("parallel",)),
        name="ffn_dense",
    )(x2d, mod, oa, ob, oc, w_out, g, w1, w3, w2)


MOE_TM = 512
SEG_ALIGN = 16
SEG_PIECES = (512, 256, 128, 64, 32, 16)
CBUF_ROWS = 2 * MOE_TM + N_EXPERTS * SEG_ALIGN
META_I1, META_I2, META_G1, META_G2, META_R1, META_R2 = range(6)
META_ROWS = 8


def _router_kernel(x_ref, mod_ref, oa_ref, ob_ref, oc_ref, wo_ref, g_ref, wr_ref, br_ref, ltri_ref,
                   x1_ref, h_ref, meta_ref, metat_ref, cnt_ref):
    x = _mixer_residual(x_ref, mod_ref, oa_ref, ob_ref, oc_ref, wo_ref)
    x1_ref[...] = x
    h = _ffn_input(x, mod_ref, g_ref)
    h_ref[...] = h.astype(BF16)
    h_hi, h_lo = _split_bf16(h)
    w_hi, w_lo = _split_bf16(wr_ref[...])
    logits = _dot(h_hi, w_hi) + (_dot(h_lo, w_hi) + _dot(h_hi, w_lo)) + br_ref[...]
    lane = lax.broadcasted_iota(jnp.int32, logits.shape, 1).astype(F32)
    logits = jnp.where(lane < N_EXPERTS, logits, NEG)
    m1 = jnp.max(logits, axis=-1, keepdims=True)
    i1 = jnp.min(jnp.where(logits == m1, lane, float(LANES)), axis=-1, keepdims=True)
    rest = jnp.where(lane == i1, NEG, logits)
    m2 = jnp.max(rest, axis=-1, keepdims=True)
    i2 = jnp.min(jnp.where(rest == m2, lane, float(LANES)), axis=-1, keepdims=True)
    e2 = jnp.exp(m2 - m1)
    den = 1.0 + e2
    sel1 = jnp.where(lane == i1, 1.0, 0.0)
    sel2 = jnp.where(lane == i2, 1.0, 0.0)
    sel = sel1 + sel2
    before = _dot(ltri_ref[...], sel.astype(BF16))
    r1 = jnp.sum(before * sel1, axis=-1, keepdims=True)
    r2 = jnp.sum(before * sel2, axis=-1, keepdims=True)
    cnt_ref[0] = jnp.sum(sel, axis=0, keepdims=True)
    cols = (i1, i2, 1.0 / den, e2 / den, r1, r2)
    meta = jnp.zeros_like(logits)
    for k, col in enumerate(cols):
        meta = jnp.where(lane == k, col, meta)
    meta_ref[...] = meta
    metat_ref[...] = meta.T[0:META_ROWS, :]


def _router_call(x2d, mod, tiles_per_mod, oa, ob, oc, w_out, g, wr, br, ltri):
    n, d = x2d.shape
    tm = MOE_TM
    nt = n // tm
    return pl.pallas_call(
        _router_kernel,
        grid=(nt,),
        in_specs=_mixer_specs(tm, d, tiles_per_mod, oa, ob, oc, w_out) + [
            pl.BlockSpec(g.shape, lambda i: (0, 0)),
            pl.BlockSpec(wr.shape, lambda i: (0, 0)),
            pl.BlockSpec(br.shape, lambda i: (0, 0)),
            pl.BlockSpec(ltri.shape, lambda i: (0, 0)),
        ],
        out_specs=[
            pl.BlockSpec((tm, d), lambda i: (i, 0)),
            pl.BlockSpec((tm, d), lambda i: (i, 0)),
            pl.BlockSpec((tm, LANES), lambda i: (i, 0)),
            pl.BlockSpec((META_ROWS, tm), lambda i: (0, i)),
            pl.BlockSpec((1, 1, LANES), lambda i: (i, 0, 0)),
        ],
        out_shape=[
            jax.ShapeDtypeStruct((n, d), F32),
            jax.ShapeDtypeStruct((n, d), BF16),
            jax.ShapeDtypeStruct((n, LANES), F32),
            jax.ShapeDtypeStruct((META_ROWS, n), F32),
            jax.ShapeDtypeStruct((nt, 1, LANES), F32),
        ],
        compiler_params=_cparams(("parallel",)),
        name="moe_router",
    )(x2d, mod, oa, ob, oc, w_out, g, wr, br, ltri)


def _pair_slots(meta, segoff_row):
    lane = lax.broadcasted_iota(jnp.int32, meta.shape, 1).astype(F32)
    i1 = meta[:, META_I1:META_I1 + 1]
    i2 = meta[:, META_I2:META_I2 + 1]
    s1 = jnp.sum(jnp.where(lane == i1, segoff_row, 0.0), axis=-1, keepdims=True) + meta[:, META_R1:META_R1 + 1]
    s2 = jnp.sum(jnp.where(lane == i2, segoff_row, 0.0), axis=-1, keepdims=True) + meta[:, META_R2:META_R2 + 1]
    return s1, s2


def _segment_copies(src, dst, src_off, dst_off, length, sem):
    out = []
    for size in SEG_PIECES:
        done = (length // (2 * size)) * (2 * size)
        s = pl.multiple_of(src_off + done, SEG_ALIGN)
        t = pl.multiple_of(dst_off + done, SEG_ALIGN)
        cp = pltpu.make_async_copy(src.at[pl.ds(s, size)], dst.at[pl.ds(t, size)], sem)
        out.append(((length & size) != 0, cp))
    return out


def _start_copies(copies):
    for pred, cp in copies:
        @pl.when(pred)
        def _(cp=cp):
            cp.start()


def _wait_copies(copies):
    for pred, cp in copies:
        @pl.when(pred)
        def _(cp=cp):
            cp.wait()


def _compact_kernel(segoff_s, base_s, len_s, h_ref, meta_ref, metat_ref, xs_in, gs_in, xs_out, gs_out,
                    cbuf2, gbuf2, sems):
    del xs_in, gs_in
    i = pl.program_id(0)
    last = pl.num_programs(0) - 1
    slot_i = i % 2
    cbuf = cbuf2.at[slot_i]
    gbuf = gbuf2.at[slot_i]

    def copies_of(tile, slot):
        out = []
        for e in range(N_EXPERTS):
            k = tile * N_EXPERTS + e
            out += _segment_copies(cbuf2.at[slot], xs_out, segoff_s[k], base_s[k], len_s[k], sems.at[0, slot])
            out += _segment_copies(gbuf2.at[slot], gs_out, segoff_s[k], base_s[k], len_s[k], sems.at[1, slot])
        return out

    @pl.when(i >= 2)
    def _():
        _wait_copies(copies_of(i - 2, slot_i))

    mt = metat_ref[...]
    s1 = mt[META_R1:META_R1 + 1]
    s2 = mt[META_R2:META_R2 + 1]
    for e in range(N_EXPERTS):
        off = segoff_s[i * N_EXPERTS + e].astype(F32)
        s1 = s1 + jnp.where(mt[META_I1:META_I1 + 1] == e, off, 0.0)
        s2 = s2 + jnp.where(mt[META_I2:META_I2 + 1] == e, off, 0.0)
    row = lax.broadcasted_iota(jnp.int32, (CBUF_ROWS, MOE_TM), 0).astype(F32)
    p1 = jnp.where(row == s1, 1.0, 0.0)
    p2 = jnp.where(row == s2, 1.0, 0.0)
    cbuf[...] = _dot((p1 + p2).astype(BF16), h_ref[...]).astype(BF16)
    meta = meta_ref[...]
    lane = lax.broadcasted_iota(jnp.int32, meta.shape, 1)
    g1 = meta[:, META_G1:META_G1 + 1]
    g2 = meta[:, META_G2:META_G2 + 1]
    g1_hi = g1.astype(BF16).astype(F32)
    g2_hi = g2.astype(BF16).astype(F32)
    a1 = jnp.where(lane == 0, g1_hi, jnp.where(lane == 1, g1 - g1_hi, 0.0)).astype(BF16)
    a2 = jnp.where(lane == 0, g2_hi, jnp.where(lane == 1, g2 - g2_hi, 0.0)).astype(BF16)
    gbuf[...] = _dot(p1.astype(BF16), a1) + _dot(p2.astype(BF16), a2)
    _start_copies(copies_of(i, slot_i))

    @pl.when(i == last)
    def _():
        @pl.when(i >= 1)
        def _():
            _wait_copies(copies_of(i - 1, 1 - slot_i))
        _wait_copies(copies_of(i, slot_i))


def _compact_call(sched, h, meta, metat, xs0, gs0):
    n, d = h.shape
    nt = n // MOE_TM
    grid_spec = pltpu.PrefetchScalarGridSpec(
        num_scalar_prefetch=3,
        grid=(nt,),
        in_specs=[
            pl.BlockSpec((MOE_TM, d), lambda i, *_: (i, 0)),
            pl.BlockSpec((MOE_TM, LANES), lambda i, *_: (i, 0)),
            pl.BlockSpec((META_ROWS, MOE_TM), lambda i, *_: (0, i)),
            pl.BlockSpec(memory_space=pl.ANY),
            pl.BlockSpec(memory_space=pl.ANY),
        ],
        out_specs=[pl.BlockSpec(memory_space=pl.ANY), pl.BlockSpec(memory_space=pl.ANY)],
        scratch_shapes=[pltpu.VMEM((2, CBUF_ROWS, d), BF16), pltpu.VMEM((2, CBUF_ROWS, LANES), F32),
                        pltpu.SemaphoreType.DMA((2, 2))],
    )
    return pl.pallas_call(
        _compact_kernel,
        grid_spec=grid_spec,
        out_shape=[jax.ShapeDtypeStruct(xs0.shape, xs0.dtype), jax.ShapeDtypeStruct(gs0.shape, gs0.dtype)],
        input_output_aliases={6: 0, 7: 1},
        compiler_params=_cparams(("arbitrary",)),
        name="moe_compact",
    )(*sched, h, meta, metat, xs0, gs0)


def _expert_kernel(exp_s, blk_s, valid_s, xs_ref, gs_ref, w1_ref, w3_ref, w2_ref, y_ref):
    del exp_s, blk_s
    j = pl.program_id(0)

    @pl.when(valid_s[j] != 0)
    def _():
        gate = gs_ref[:, 0:1] + gs_ref[:, 1:2]
        y_ref[...] = (_swiglu(xs_ref[...], w1_ref, w3_ref, w2_ref) * gate).astype(BF16)

    @pl.when(valid_s[j] == 0)
    def _():
        y_ref[...] = jnp.zeros_like(y_ref)


def _expert_call(tile_sched, xs, gs, w1, w3, w2):
    rows, d = xs.shape
    ff = w1.shape[1]
    grid_spec = pltpu.PrefetchScalarGridSpec(
        num_scalar_prefetch=3,
        grid=(rows // MOE_TM,),
        in_specs=[
            pl.BlockSpec((MOE_TM, d), lambda j, e_s, b_s, v_s: (b_s[j], 0)),
            pl.BlockSpec((MOE_TM, LANES), lambda j, e_s, b_s, v_s: (b_s[j], 0)),
            pl.BlockSpec((d, ff), lambda j, e_s, b_s, v_s: (e_s[j], 0)),
            pl.BlockSpec((d, ff), lambda j, e_s, b_s, v_s: (e_s[j], 0)),
            pl.BlockSpec((ff, d), lambda j, e_s, b_s, v_s: (e_s[j], 0)),
        ],
        out_specs=pl.BlockSpec((MOE_TM, d), lambda j, e_s, b_s, v_s: (j, 0)),
    )
    return pl.pallas_call(
        _expert_kernel,
        grid_spec=grid_spec,
        out_shape=jax.ShapeDtypeStruct((rows, d), BF16),
        compiler_params=_cparams(("arbitrary",)),
        name="moe_experts",
    )(*tile_sched, xs, gs, w1, w3, w2)


def _combine_kernel(segoff_s, base_s, len_s, x_ref, mod_ref, meta_ref, segoff_ref, y_hbm, o_ref, ybuf2, sems):
    i = pl.program_id(0)
    last = pl.num_programs(0) - 1
    slot_i = i % 2

    def copies_of(tile, slot):
        out = []
        for e in range(N_EXPERTS):
            k = tile * N_EXPERTS + e
            out += _segment_copies(y_hbm, ybuf2.at[slot], base_s[k], segoff_s[k], len_s[k], sems.at[slot])
        return out

    @pl.when(i == 0)
    def _():
        ybuf2[...] = jnp.zeros_like(ybuf2)
        _start_copies(copies_of(i, slot_i))

    @pl.when(i < last)
    def _():
        _start_copies(copies_of(i + 1, 1 - slot_i))

    _wait_copies(copies_of(i, slot_i))
    s1, s2 = _pair_slots(meta_ref[...], segoff_ref[0])
    slot = lax.broadcasted_iota(jnp.int32, (MOE_TM, CBUF_ROWS), 1).astype(F32)
    pick = (jnp.where(slot == s1, 1.0, 0.0) + jnp.where(slot == s2, 1.0, 0.0)).astype(BF16)
    o_ref[...] = x_ref[...] + mod_ref[0, 5:6, :] * _dot(pick, ybuf2[slot_i])


def _combine_call(sched, x2d, mod, tiles_per_mod, meta, segoff_v, y):
    n, d = x2d.shape
    grid_spec = pltpu.PrefetchScalarGridSpec(
        num_scalar_prefetch=3,
        grid=(n // MOE_TM,),
        in_specs=[
            pl.BlockSpec((MOE_TM, d), lambda i, *_: (i, 0)),
            pl.BlockSpec((1, MOD_ROWS, d), lambda i, *_: (i // tiles_per_mod, 0, 0)),
            pl.BlockSpec((MOE_TM, LANES), lambda i, *_: (i, 0)),
            pl.BlockSpec((1, 1, LANES), lambda i, *_: (i, 0, 0)),
            pl.BlockSpec(memory_space=pl.ANY),
        ],
        out_specs=pl.BlockSpec((MOE_TM, d), lambda i, *_: (i, 0)),
        scratch_shapes=[pltpu.VMEM((2, CBUF_ROWS, d), BF16), pltpu.SemaphoreType.DMA((2,))],
    )
    return pl.pallas_call(
        _combine_kernel,
        grid_spec=grid_spec,
        out_shape=jax.ShapeDtypeStruct((n, d), F32),
        compiler_params=_cparams(("arbitrary",)),
        name="moe_combine",
    )(*sched, x2d, mod, meta, segoff_v, y)


def _moe_call(x2d, mod, tiles_per_mod, outs, w_out, g, wr, br, w1, w3, w2):
    n, d = x2d.shape
    nt = n // MOE_TM
    ltri = jnp.asarray(np.tril(np.ones((MOE_TM, MOE_TM), np.float32), -1), BF16)
    x2d, h, meta, metat, counts = _router_call(x2d, mod, tiles_per_mod, *outs, w_out, g, wr, br, ltri)

    cnt = counts[:, 0, :N_EXPERTS].astype(jnp.int32)
    seg_len = (cnt + SEG_ALIGN - 1) // SEG_ALIGN * SEG_ALIGN
    segoff = jnp.cumsum(seg_len, axis=1) - seg_len
    region = (jnp.sum(seg_len, axis=0) + MOE_TM - 1) // MOE_TM * MOE_TM
    region_start = jnp.cumsum(region) - region
    base = region_start[None, :] + jnp.cumsum(seg_len, axis=0) - seg_len
    rows_max = 2 * n + nt * N_EXPERTS * (SEG_ALIGN - 1) + N_EXPERTS * (MOE_TM - 1)
    n_sorted_tiles = (rows_max + MOE_TM - 1) // MOE_TM
    tile_end = jnp.cumsum(region // MOE_TM)
    total_tiles = tile_end[-1]
    jt = jnp.arange(n_sorted_tiles, dtype=jnp.int32)
    blk = jnp.minimum(jt, total_tiles - 1)
    tile_expert = jnp.sum((blk[:, None] >= tile_end[None, :]).astype(jnp.int32), axis=1)
    tile_sched = (tile_expert.astype(jnp.int32), blk.astype(jnp.int32), (jt < total_tiles).astype(jnp.int32))
    sched = tuple(a.reshape(-1).astype(jnp.int32) for a in (segoff, base, seg_len))
    segoff_v = jnp.pad(segoff.astype(F32), ((0, 0), (0, LANES - N_EXPERTS)))[:, None, :]

    rows = n_sorted_tiles * MOE_TM
    xs, gs = _compact_call(sched, h, meta, metat, jnp.zeros((rows, d), BF16), jnp.zeros((rows, LANES), F32))
    y = _expert_call(tile_sched, xs, gs, w1, w3, w2)
    return _combine_call(sched, x2d, mod, tiles_per_mod, meta, segoff_v, y)


def _rope_tables(t):
    rows = jnp.arange(t, dtype=F32) // GRID_W
    cols = jnp.arange(t, dtype=F32) % GRID_W

    def tables(rot_dim):
        a = rot_dim // 2
        inv = 1.0 / (ROPE_BASE ** (jnp.arange(0, a, 2, dtype=F32) / a))
        ar = rows[:, None] * inv
        ac = cols[:, None] * inv
        cos = jnp.concatenate([jnp.cos(ar), jnp.cos(ar), jnp.cos(ac), jnp.cos(ac)], axis=-1)
        sin = jnp.concatenate([-jnp.sin(ar), jnp.sin(ar), -jnp.sin(ac), jnp.sin(ac)], axis=-1)
        return cos, sin

    ca, sa = tables(HEAD_DIM)
    cos_a = jnp.tile(ca, (1, LANES // HEAD_DIM))
    sin_a = jnp.tile(sa, (1, LANES // HEAD_DIM))
    cc, sc = tables(C_ROPE)
    ones = jnp.ones((t, C_NOPE), F32)
    tail = C_SLOT - C_NOPE - C_ROPE
    cos_c = jnp.concatenate([ones, cc, jnp.ones((t, tail), F32)], axis=-1)
    sin_c = jnp.concatenate([0 * ones, sc, jnp.zeros((t, tail), F32)], axis=-1)
    return cos_a, sin_a, cos_c, sin_c


def _head_perm():
    order = []
    for j in range(A_HEADS // 2):
        order += [j, A_HEADS // 2 + j]
    return np.concatenate([np.arange(h * HEAD_DIM, (h + 1) * HEAD_DIM) for h in order])


def _segment_mean_matrix(widths, total):
    m = np.zeros((total, total), np.float32)
    o = 0
    while o < total:
        for w, used in widths:
            if used:
                m[o:o + w, o:o + w] = 1.0 / w
            o += w
    return jnp.asarray(m, BF16)


def _slot_vec(nope, rope):
    z = jnp.zeros((C_SLOT - C_NOPE - C_ROPE,), F32)
    n = jnp.zeros((C_NOPE,), F32) if nope is None else nope
    r = jnp.zeros((C_ROPE,), F32) if rope is None else rope
    return jnp.tile(jnp.concatenate([n, r, z]), C_HEADS)[None, :]


def _layer_consts(i, p, perm, tabs):
    w_in = p["w_in"][i]
    o_kr = A_Q + 2 * A_KV + 2 * B_CH + C_Q_RANK + C_KV_RANK
    d = w_in.shape[0]
    kr_cols = jnp.concatenate([jnp.zeros((d, C_NOPE), F32), w_in[:, o_kr:o_kr + C_ROPE],
                               jnp.zeros((d, C_SLOT - C_NOPE - C_ROPE), F32)], axis=1)
    win = jnp.concatenate([w_in[:, :A_Q][:, perm], w_in[:, A_Q:o_kr], kr_cols], axis=1).astype(BF16)

    w_uq = p["c_w_uq"][i].reshape(C_Q_RANK, C_HEADS, C_NOPE + C_ROPE)
    wuq = jnp.pad(w_uq, ((0, 0), (0, 0), (0, C_SLOT - C_NOPE - C_ROPE))).reshape(C_Q_RANK, C_HEADS * C_SLOT)
    w_ukv = p["c_w_ukv"][i].reshape(C_KV_RANK, C_HEADS, C_NOPE + C_V)
    wukvk = jnp.pad(w_ukv[:, :, :C_NOPE], ((0, 0), (0, 0), (0, C_SLOT - C_NOPE))).reshape(C_KV_RANK, -1)
    wukvv = w_ukv[:, :, C_NOPE:].reshape(C_KV_RANK, C_HEADS * C_V)

    sa = _segment_mean_matrix([(HEAD_DIM, True)], MXU_TILE)
    sc = _segment_mean_matrix([(C_NOPE, True), (C_ROPE, True), (C_SLOT - C_NOPE - C_ROPE, False)], MXU_TILE)
    gq = jnp.tile(p["a_q_norm_g"][i] * (A_SCALE * LOG2E), A_HEADS)[None, :]
    gk = jnp.tile(p["a_k_norm_g"][i], A_KV_HEADS)[None, :]
    gqc = _slot_vec(p["c_q_nope_norm_g"][i], p["c_q_rope_norm_g"][i]) * (MLA_SCALE * LOG2E)
    gkn = _slot_vec(p["c_k_nope_norm_g"][i], None)
    gkr = _slot_vec(None, p["c_k_rope_norm_g"][i])[:, :C_SLOT]
    return (p["mix_norm_g"][i][None, :], win) + tabs + (
        sa, sc, gq, gk, p["c_q_rank_norm_g"][i][None, :], p["c_kv_rank_norm_g"][i][None, :],
        wuq.astype(BF16), wukvk.astype(BF16), wukvv.astype(BF16), gqc, gkn, gkr)


def kernel(x, c, ctx, c_ctx, ada_w, ada_b, mix_norm_g, ffn_norm_g, w_in, w_out, a_q_norm_g, a_k_norm_g, a_sink, b_conv_w, b_conv_b, b_ln_g, b_ln_b, c_q_rank_norm_g, c_kv_rank_norm_g, c_w_uq, c_w_ukv, c_q_nope_norm_g, c_k_nope_norm_g, c_q_rope_norm_g, c_k_rope_norm_g, dense_w1, dense_w3, dense_w2, moe_router_w, moe_router_b, moe_w1, moe_w3, moe_w2):
    p = dict(w_in=w_in, c_w_uq=c_w_uq, c_w_ukv=c_w_ukv, a_q_norm_g=a_q_norm_g, a_k_norm_g=a_k_norm_g,
             c_q_nope_norm_g=c_q_nope_norm_g, c_k_nope_norm_g=c_k_nope_norm_g,
             c_q_rope_norm_g=c_q_rope_norm_g, c_k_rope_norm_g=c_k_rope_norm_g,
             c_q_rank_norm_g=c_q_rank_norm_g, c_kv_rank_norm_g=c_kv_rank_norm_g, mix_norm_g=mix_norm_g)
    bsz, t, d = x.shape
    ctx_len = ctx.shape[1]
    depth = ada_w.shape[0]
    n_x, n_c = bsz * t, bsz * ctx_len
    tm_pre = min(4 * PRE_SUB, n_c)
    tm_tok = 512
    tq = min(1024, t)

    ada_rows = ((bsz + 1 + 7) // 8) * 8
    c_pad = jnp.concatenate([c, c_ctx[None, :], jnp.zeros((ada_rows - bsz - 1, d), F32)], axis=0)
    mods = _ada_call(c_pad, ada_w, ada_b).reshape(depth, ada_rows, 6, d)
    mods = jnp.pad(mods, ((0, 0), (0, 0), (0, MOD_ROWS - 6), (0, 0)))

    tabs_x = _rope_tables(t)
    ones = jnp.ones((tm_pre, LANES), F32)
    tabs_c = (ones, 0 * ones, ones, 0 * ones)
    perm = _head_perm()

    x2 = x.reshape(n_x, d)
    c2 = ctx.reshape(n_c, d)
    for i in range(depth):
        last = i == depth - 1
        mod_x = mods[i, :bsz]
        mod_c = mods[i, bsz:bsz + 1]
        consts_x = _layer_consts(i, p, perm, tabs_x)
        consts_c = _layer_consts(i, p, perm, tabs_c)

        qa_x, ka_x, va_x, u_x, qc_x, kc_x, vc_x = _pre_call(x2, mod_x, t // tm_pre, t // tm_pre, consts_x, tm_pre)
        qa_c, ka_c, va_c, u_c, qc_c, kc_c, vc_c = _pre_call(c2, mod_c, n_c // tm_pre, 1, consts_c, tm_pre)

        def r3(a, length):
            return a.reshape(bsz, length, a.shape[-1])

        sink_row = jnp.repeat(a_sink[i] * LOG2E, BLOCK)[None, :]
        conv_w = jnp.pad(b_conv_w[i], ((0, 32 - B_WIDTH), (0, 0)))
        conv_p = (conv_w, b_conv_b[i][None, :], b_ln_g[i][None, :], b_ln_b[i][None, :])
        w_o = jnp.concatenate([w_out[i][:A_Q][perm], w_out[i][A_Q:]], axis=0).astype(BF16)

        o_a = _attn_a_call(qa_x, r3(ka_x, t), va_x, r3(ka_c, ctx_len), va_c, sink_row, bsz, t)
        o_b = _conv_call(r3(u_x, t), *conv_p).reshape(n_x, B_CH)
        o_c = _mla_call(qc_x, r3(kc_x, t), vc_x, r3(kc_c, ctx_len), vc_c, bsz, t, tq)
        if not last:
            oc_a = _attn_a_call(qa_c, None, None, r3(ka_c, ctx_len), va_c, sink_row, bsz, ctx_len)
            oc_b = _conv_call(r3(u_c, ctx_len), *conv_p).reshape(n_c, B_CH)
            oc_c = _mla_call(qc_c, None, None, r3(kc_c, ctx_len), vc_c, bsz, ctx_len, ctx_len)

        j = i // 2
        g_ffn = ffn_norm_g[i][None, :]
        if i % 2 == 0:
            w = (dense_w1[j].astype(BF16), dense_w3[j].astype(BF16), dense_w2[j].astype(BF16))

            def mix(a2, mod, tiles_per_mod, outs, w=w, g_ffn=g_ffn, w_o=w_o):
                return _ffn_call(a2, mod, tiles_per_mod, *outs, w_o, g_ffn, *w, tm_tok)
        else:
            w = tuple(a.astype(BF16).reshape(-1, a.shape[-1]) for a in (moe_w1[j], moe_w3[j], moe_w2[j]))
            wr = jnp.pad(moe_router_w[j], ((0, 0), (0, LANES - N_EXPERTS)))
            br = jnp.pad(moe_router_b[j], (0, LANES - N_EXPERTS))[None, :]

            def mix(a2, mod, tiles_per_mod, outs, w=w, g_ffn=g_ffn, wr=wr, br=br, w_o=w_o):
                return _moe_call(a2, mod, tiles_per_mod * (tm_tok // MOE_TM), outs, w_o, g_ffn, wr, br, *w)
        x2 = mix(x2, mod_x, t // tm_tok, (o_a, o_b, o_c))
        if not last:
            c2 = mix(c2, mod_c, n_c // tm_tok, (oc_a, oc_b, oc_c))
    return x2.reshape(bsz, t, d)
```

```python
import functools

import jax
import jax.numpy as jnp
import numpy as np
from jax import lax
from jax.experimental import pallas as pl
from jax.experimental.pallas import tpu as pltpu

F32 = jnp.float32
BF16 = jnp.bfloat16

D_MODEL = 1024
GRID_W = 64
HEAD_DIM = 64
A_HEADS = 8
A_KV_HEADS = 2
A_WINDOW = 128
BLOCK = 128
B_CH = 256
B_WIDTH = 31
C_HEADS = 4
C_Q_RANK = 384
C_KV_RANK = 256
C_NOPE = 64
C_ROPE = 32
C_V = 64
A_Q = A_HEADS * HEAD_DIM
A_KV = A_KV_HEADS * HEAD_DIM
IN_COLS_PAD = 2048
D_FF = 2816
N_EXPERTS = 8
ROPE_BASE = 10000.0
EPS = 1e-6
NEG = -1e30
A_SCALE = HEAD_DIM ** -0.5
MLA_SCALE = (C_NOPE + C_ROPE) ** -0.5
LOG2E = 1.4426950408889634

LANES = 128
MXU_TILE = 256
MOD_ROWS = 8
C_SLOT = 128
VMEM_LIMIT = 56 * 1024 * 1024


def _cparams(sem):
    return pltpu.CompilerParams(dimension_semantics=sem, vmem_limit_bytes=VMEM_LIMIT)


def _dot(a, b):
    return jnp.dot(a, b, preferred_element_type=F32)


def _dot_nt(a, b):
    return lax.dot_general(a, b, (((1,), (1,)), ((), ())), preferred_element_type=F32)


def _ada_kernel(c_ref, w_ref, b_ref, o_ref):
    c = c_ref[...]
    a = c * jax.nn.sigmoid(c)
    o_ref[0] = _dot(a.astype(BF16), w_ref[0].astype(BF16)) + b_ref[0]


def _ada_call(c_pad, ada_w, ada_b):
    depth, d, n6 = ada_w.shape
    rows = c_pad.shape[0]
    tn = 1536
    return pl.pallas_call(
        _ada_kernel,
        grid=(depth, n6 // tn),
        in_specs=[
            pl.BlockSpec((rows, d), lambda i, j: (0, 0)),
            pl.BlockSpec((1, d, tn), lambda i, j: (i, 0, j)),
            pl.BlockSpec((1, 1, tn), lambda i, j: (i, 0, j)),
        ],
        out_specs=pl.BlockSpec((1, rows, tn), lambda i, j: (i, 0, j)),
        out_shape=jax.ShapeDtypeStruct((depth, rows, n6), F32),
        compiler_params=_cparams(("parallel", "parallel")),
        name="ada_proj",
    )(c_pad, ada_w, ada_b.reshape(depth, 1, n6))


def _rope_chunk(c, cos, sin, half):
    lane = lax.broadcasted_iota(jnp.int32, c.shape, 1)
    lo = (lane & (2 * half - 1)) < half
    partner = jnp.where(lo, pltpu.roll(c, LANES - half, 1), pltpu.roll(c, half, 1))
    return c * cos + partner * sin


PRE_SUB = 256


def _segment_mean(sq, s_ref):
    w = s_ref.shape[0]
    return jnp.concatenate([_dot(sq[:, c:c + w].astype(BF16), s_ref[...]) for c in range(0, sq.shape[1], w)], axis=1)


def _pre_kernel(x_ref, mod_ref, gmix_ref, win_ref, cosa_ref, sina_ref, cosc_ref, sinc_ref,
                sa_ref, sc_ref, gq_ref, gk_ref, gcq_ref, gckv_ref, wuq_ref, wukvk_ref, wukvv_ref,
                gqc_ref, gkn_ref, gkr_ref, *out_refs, kv_only):
    if kv_only:
        ka_ref, va_ref, kc_ref, vc_ref = out_refs
        o_ka, o_ckv = 0, 2 * A_KV
    else:
        qa_ref, ka_ref, va_ref, u_ref, qc_ref, kc_ref, vc_ref = out_refs
        o_ka, o_ckv = A_Q, A_Q + 2 * A_KV + 2 * B_CH + C_Q_RANK
    for sub in range(x_ref.shape[0] // PRE_SUB):
        rows = slice(sub * PRE_SUB, (sub + 1) * PRE_SUB)
        x = x_ref[rows, :]
        ms = jnp.mean(x * x, axis=-1, keepdims=True)
        y = x * lax.rsqrt(ms + EPS)
        shift = mod_ref[0, 0:1, :]
        scale = mod_ref[0, 1:2, :]
        h = (y * gmix_ref[...]) * (1.0 + scale) + shift
        p = _dot(h.astype(BF16), win_ref[...])

        cosa, sina = cosa_ref[rows, :], sina_ref[rows, :]
        cosc, sinc = cosc_ref[rows, :], sinc_ref[rows, :]

        if not kv_only:
            qa = p[:, 0:A_Q]
            ssq = _segment_mean(qa * qa, sa_ref)
            qa = qa * lax.rsqrt(ssq + EPS) * gq_ref[...]
            for j in range(A_Q // LANES):
                sl = slice(j * LANES, (j + 1) * LANES)
                qa_ref[rows, sl] = _rope_chunk(qa[:, sl], cosa, sina, HEAD_DIM // 4).astype(BF16)

            o = A_Q + 2 * A_KV
            u_ref[rows, :] = p[:, o:o + B_CH] * jax.nn.sigmoid(p[:, o + B_CH:o + 2 * B_CH])

            o = o + 2 * B_CH
            cq = p[:, o:o + C_Q_RANK]
            cq = cq * lax.rsqrt(jnp.mean(cq * cq, axis=-1, keepdims=True) + EPS) * gcq_ref[...]
            qc = _dot(cq.astype(BF16), wuq_ref[...])
            ssq = _segment_mean(qc * qc, sc_ref)
            qc = qc * lax.rsqrt(ssq + EPS) * gqc_ref[...]
            for j in range(C_HEADS):
                sl = slice(j * C_SLOT, (j + 1) * C_SLOT)
                qc_ref[rows, sl] = _rope_chunk(qc[:, sl], cosc, sinc, C_ROPE // 4).astype(BF16)

        ka = p[:, o_ka:o_ka + A_KV]
        ssk = _dot((ka * ka).astype(BF16), sa_ref[0:A_KV, 0:A_KV])
        ka = ka * lax.rsqrt(ssk + EPS) * gk_ref[...]
        ka_ref[rows, :] = _rope_chunk(ka, cosa, sina, HEAD_DIM // 4).astype(BF16)
        va_ref[:, rows] = p[:, o_ka + A_KV:o_ka + 2 * A_KV].T.astype(BF16)

        o = o_ckv
        ckv = p[:, o:o + C_KV_RANK]
        ckv = (ckv * lax.rsqrt(jnp.mean(ckv * ckv, axis=-1, keepdims=True) + EPS) * gckv_ref[...]).astype(BF16)
        kn = _dot(ckv, wukvk_ref[...])
        vc_ref[:, rows] = _dot(ckv, wukvv_ref[...]).T.astype(BF16)
        ssk = _segment_mean(kn * kn, sc_ref)
        kn = kn * lax.rsqrt(ssk + EPS) * gkn_ref[...]
        o = o + C_KV_RANK
        kr = p[:, o:o + C_SLOT]
        kr = kr * lax.rsqrt(jnp.sum(kr * kr, axis=-1, keepdims=True) * (1.0 / C_ROPE) + EPS) * gkr_ref[...]
        kr = _rope_chunk(kr, cosc, sinc, C_ROPE // 4)
        for j in range(C_HEADS):
            sl = slice(j * C_SLOT, (j + 1) * C_SLOT)
            kc_ref[rows, sl] = (kn[:, sl] + kr).astype(BF16)


def _pre_call(x2d, mod, tiles_per_mod, tab_tiles, consts, tm, kv_only=False):
    n, d = x2d.shape
    (gmix, win, cosa, sina, cosc, sinc, sa, sc, gq, gk, gcq, gckv, wuq, wukvk, wukvv, gqc, gkn, gkr) = consts
    if kv_only:
        o_ckv = A_Q + 2 * A_KV + 2 * B_CH + C_Q_RANK
        win = jnp.concatenate([win[:, A_Q:A_Q + 2 * A_KV], win[:, o_ckv:]], axis=1)

    def const(a):
        return pl.BlockSpec(a.shape, lambda i: (0,) * a.ndim)

    def tab(a):
        return pl.BlockSpec((tm, LANES), lambda i: (i % tab_tiles, 0))

    in_specs = [
        pl.BlockSpec((tm, d), lambda i: (i, 0)),
        pl.BlockSpec((1, MOD_ROWS, d), lambda i: (i // tiles_per_mod, 0, 0)),
        const(gmix), const(win), tab(cosa), tab(sina), tab(cosc), tab(sinc),
        const(sa), const(sc), const(gq), const(gk), const(gcq), const(gckv),
        const(wuq), const(wukvk), const(wukvv), const(gqc), const(gkn), const(gkr),
    ]
    widths = (A_Q, A_KV, A_KV, B_CH, C_HEADS * C_SLOT, C_HEADS * C_SLOT, C_HEADS * C_V)
    dtypes = (BF16, BF16, BF16, F32, BF16, BF16, BF16)
    out_specs = [pl.BlockSpec((tm, w), lambda i: (i, 0)) for w in widths]
    out_shape = [jax.ShapeDtypeStruct((n, w), dt) for w, dt in zip(widths, dtypes)]
    out_specs[2] = pl.BlockSpec((A_KV, tm), lambda i: (0, i))
    out_shape[2] = jax.ShapeDtypeStruct((A_KV, n), BF16)
    out_specs[6] = pl.BlockSpec((C_HEADS * C_V, tm), lambda i: (0, i))
    out_shape[6] = jax.ShapeDtypeStruct((C_HEADS * C_V, n), BF16)
    if kv_only:
        keep = (1, 2, 5, 6)
        out_specs = [out_specs[k] for k in keep]
        out_shape = [out_shape[k] for k in keep]
    return pl.pallas_call(
        functools.partial(_pre_kernel, kv_only=kv_only),
        grid=(n // tm,),
        in_specs=in_specs,
        out_specs=out_specs,
        out_shape=out_shape,
        compiler_params=_cparams(("parallel",)),
        name="pre_attn",
    )(x2d, mod, gmix, win, cosa, sina, cosc, sinc, sa, sc, gq, gk, gcq, gckv, wuq, wukvk, wukvv,
      gqc, gkn, gkr)


CONV_PAD = 16
CONV_CHUNK = 128
SUBLANES = 8


def _conv_chunk(pad_ref, c, w_ref, b_ref, g_ref, beta_ref):
    off = CONV_PAD - B_WIDTH // 2
    nq = (off + B_WIDTH - 1) // SUBLANES + 1
    win = CONV_CHUNK + (nq - 1) * SUBLANES
    base = c * CONV_CHUNK
    acc = jnp.zeros((CONV_CHUNK, B_CH), F32)
    for r in range(SUBLANES):
        taps = [k for k in range(B_WIDTH) if (off + k) % SUBLANES == r]
        if not taps:
            continue
        w_r = pad_ref[base + r:base + r + win, :]
        part = None
        for k in taps:
            q = (off + k) // SUBLANES
            term = w_r[q * SUBLANES:q * SUBLANES + CONV_CHUNK, :] * w_ref[k:k + 1, :]
            part = term if part is None else part + term
        acc = acc + part
    y = acc + b_ref[...]
    mu = jnp.mean(y, axis=-1, keepdims=True)
    yc = y - mu
    var = jnp.mean(yc * yc, axis=-1, keepdims=True)
    z = yc * lax.rsqrt(var + EPS) * g_ref[...] + beta_ref[...]
    return (z * jax.nn.sigmoid(z)).astype(BF16)


ATTN_A_QBLOCKS = 8


ONES_ROWS = 16


def _attn_a_kernel(*refs, t, has_local, qblocks):
    u_ref, uprev_ref, unext_ref, cw_ref, cb_ref, cg_ref, cbeta_ref, o_ref, ob_ref, pad_ref = refs[-10:]
    if has_local:
        q_ref, k_ref, vt_ref, kc_ref, vct_ref, sink_ref, bias_ref = refs[:-10]
    else:
        q_ref, kc_ref, vct_ref, sink_ref = refs[:-10]
    step = pl.program_id(1)
    tq = q_ref.shape[0]
    pad_ref[0:CONV_PAD, :] = jnp.where(step > 0, uprev_ref[0], 0.0)
    pad_ref[CONV_PAD:CONV_PAD + tq, :] = u_ref[0]
    pad_ref[CONV_PAD + tq:CONV_PAD + tq + CONV_PAD, :] = jnp.where(step < pl.num_programs(1) - 1, unext_ref[0], 0.0)
    nchunk = A_Q // LANES
    span = 3 * BLOCK
    lane = lax.broadcasted_iota(jnp.int32, (BLOCK, LANES), 1)
    row = lax.broadcasted_iota(jnp.int32, (A_KV, BLOCK), 0)
    zero = jnp.zeros((BLOCK, LANES), BF16)
    sink = sink_ref[...]
    kc = kc_ref[0]
    vct = vct_ref[...]
    for blk in range(qblocks):
        rows = slice(blk * BLOCK, (blk + 1) * BLOCK)
        chunks = [q_ref[rows, j * LANES:(j + 1) * LANES] for j in range(nchunk)]
        qs = jnp.concatenate([jnp.where(lane < HEAD_DIM, c, zero) for c in chunks]
                             + [jnp.where(lane >= HEAD_DIM, c, zero) for c in chunks], axis=0)
        if has_local:
            n = pl.program_id(1) * qblocks + blk
            start = pl.multiple_of(jnp.clip((n - 1) * BLOCK, 0, t - span), BLOCK)
            keys = jnp.concatenate([k_ref[0, pl.ds(start, span), :], kc], axis=0)
            vt = jnp.concatenate([vt_ref[:, pl.ds(start, span)], vct], axis=1)
        else:
            keys, vt = kc, vct
        s = _dot_nt(keys, qs)
        if has_local:
            s = jnp.concatenate([s[:span] + bias_ref[n - start // BLOCK], s[span:]], axis=0)
        m = jnp.maximum(jnp.max(s, axis=0, keepdims=True), sink)
        e = jnp.exp2(s - m).astype(BF16)
        vte = jnp.concatenate([vt, jnp.ones((ONES_ROWS, vt.shape[1]), BF16)], axis=0)
        acc = _dot(vte, e)
        den = acc[A_KV:A_KV + 1] + jnp.exp2(sink - m)
        out = acc[:A_KV] * (1.0 / den)
        for j in range(nchunk):
            x = jnp.where(row < HEAD_DIM, out[:, j * BLOCK:(j + 1) * BLOCK],
                          out[:, (nchunk + j) * BLOCK:(nchunk + j + 1) * BLOCK])
            o_ref[rows, j * LANES:(j + 1) * LANES] = x.T.astype(BF16)
        ob_ref[rows, :] = _conv_chunk(pad_ref, blk, cw_ref, cb_ref, cg_ref, cbeta_ref)


def _window_bias():
    r = np.arange(BLOCK)[None, :]
    c = np.arange(3 * BLOCK)[:, None]
    pats = [np.where(np.abs(c - p * BLOCK - r) <= A_WINDOW, 0.0, NEG) for p in range(3)]
    return jnp.asarray(np.stack([np.tile(p, (1, A_HEADS)) for p in pats]), F32)


def _attn_a_call(qa, ka, vat, kac, vact, sink_row, u3, conv_p, bsz, t):
    has_local = ka is not None
    qblocks = min(ATTN_A_QBLOCKS, t // BLOCK)
    tq = qblocks * BLOCK
    nq = t // tq
    ctx_len = kac.shape[1]
    in_specs = [pl.BlockSpec((tq, A_Q), lambda b, n: (b * nq + n, 0))]
    args = [qa]
    if has_local:
        in_specs += [pl.BlockSpec((1, t, A_KV), lambda b, n: (b, 0, 0)),
                     pl.BlockSpec((A_KV, t), lambda b, n: (0, b))]
        args += [ka, vat]
    in_specs += [pl.BlockSpec((1, ctx_len, A_KV), lambda b, n: (b, 0, 0)),
                 pl.BlockSpec((A_KV, ctx_len), lambda b, n: (0, b))]
    in_specs += [pl.BlockSpec(sink_row.shape, lambda b, n: (0, 0))]
    args += [kac, vact, sink_row]
    if has_local:
        bias = _window_bias()
        in_specs += [pl.BlockSpec(bias.shape, lambda b, n: (0, 0, 0))]
        args += [bias]
    halo_per_step = tq // CONV_PAD
    n_halo = t // CONV_PAD
    in_specs += [
        pl.BlockSpec((1, tq, B_CH), lambda b, n: (b, n, 0)),
        pl.BlockSpec((1, CONV_PAD, B_CH), lambda b, n: (b, jnp.maximum(n * halo_per_step - 1, 0), 0)),
        pl.BlockSpec((1, CONV_PAD, B_CH), lambda b, n: (b, jnp.minimum((n + 1) * halo_per_step, n_halo - 1), 0)),
    ] + [pl.BlockSpec(a.shape, lambda b, n: (0, 0)) for a in conv_p]
    args += [u3, u3, u3, *conv_p]
    return pl.pallas_call(
        functools.partial(_attn_a_kernel, t=t, has_local=has_local, qblocks=qblocks),
        grid=(bsz, nq),
        in_specs=in_specs,
        out_specs=[pl.BlockSpec((tq, A_Q), lambda b, n: (b * nq + n, 0)),
                   pl.BlockSpec((tq, B_CH), lambda b, n: (b * nq + n, 0))],
        out_shape=[jax.ShapeDtypeStruct((bsz * t, A_Q), BF16), jax.ShapeDtypeStruct((bsz * t, B_CH), BF16)],
        scratch_shapes=[pltpu.VMEM((tq + 2 * CONV_PAD, B_CH), F32)],
        compiler_params=_cparams(("parallel", "parallel")),
        name="attn_ab_local" if has_local else "attn_ab_ctx",
    )(*args)


def _mla_kernel(*refs, has_local):
    if has_local:
        q_ref, kx_ref, vxt_ref, kc_ref, vct_ref, o_ref = refs
    else:
        q_ref, kc_ref, vct_ref, o_ref = refs
    def scores(h):
        sl = slice(h * C_SLOT, (h + 1) * C_SLOT)
        q = q_ref[:, sl]
        s_c = _dot_nt(kc_ref[0, :, sl], q)
        s_x = _dot_nt(kx_ref[0, :, sl], q) if has_local else None
        return s_c, s_x

    outs = []
    nxt = scores(0)
    for h in range(C_HEADS):
        vs = slice(h * C_V, (h + 1) * C_V)
        s_c, s_x = nxt
        if h + 1 < C_HEADS:
            nxt = scores(h + 1)
        m = jnp.max(s_c, axis=0, keepdims=True)
        if has_local:
            m = jnp.maximum(m, jnp.max(s_x, axis=0, keepdims=True))
        e_c = jnp.exp2(s_c - m).astype(BF16)
        vt = jnp.concatenate([vct_ref[vs, :], jnp.ones((ONES_ROWS, e_c.shape[0]), BF16)], axis=0)
        acc = _dot(vt, e_c)
        if has_local:
            e_x = jnp.exp2(s_x - m).astype(BF16)
            vt = jnp.concatenate([vxt_ref[vs, :], jnp.ones((ONES_ROWS, e_x.shape[0]), BF16)], axis=0)
            acc = acc + _dot(vt, e_x)
        outs.append(acc[:C_V] * (1.0 / acc[C_V:C_V + 1]))
    o_ref[...] = jnp.concatenate(outs, axis=0).T.astype(BF16)


def _mla_call(qc, kx, vxt, kcc, vcct, bsz, t, tq):
    has_local = kx is not None
    nq = t // tq
    ctx_len = kcc.shape[1]
    wq = C_HEADS * C_SLOT
    wv = C_HEADS * C_V
    in_specs = [pl.BlockSpec((tq, wq), lambda b, n: (b * nq + n, 0))]
    args = [qc]
    if has_local:
        in_specs += [pl.BlockSpec((1, t, wq), lambda b, n: (b, 0, 0)),
                     pl.BlockSpec((wv, t), lambda b, n: (0, b))]
        args += [kx, vxt]
    in_specs += [pl.BlockSpec((1, ctx_len, wq), lambda b, n: (b, 0, 0)),
                 pl.BlockSpec((wv, ctx_len), lambda b, n: (0, b))]
    args += [kcc, vcct]
    return pl.pallas_call(
        functools.partial(_mla_kernel, has_local=has_local),
        grid=(bsz, nq),
        in_specs=in_specs,
        out_specs=pl.BlockSpec((tq, wv), lambda b, n: (b * nq + n, 0)),
        out_shape=jax.ShapeDtypeStruct((bsz * t, wv), BF16),
        compiler_params=_cparams(("parallel", "parallel")),
        name="mla_local" if has_local else "mla_ctx",
    )(*args)


def _mixer_residual(x_ref, mod_ref, oa_ref, ob_ref, oc_ref, wo_ref):
    y = _dot(oa_ref[...], wo_ref[0:A_Q, :])
    y = y + _dot(ob_ref[...], wo_ref[A_Q:A_Q + B_CH, :])
    y = y + _dot(oc_ref[...], wo_ref[A_Q + B_CH:, :])
    return x_ref[...] + mod_ref[0, 2:3, :] * y


def _mixer_specs(tm, d, tiles_per_mod, oa, ob, oc, w_out):
    return [
        pl.BlockSpec((tm, d), lambda i: (i, 0)),
        pl.BlockSpec((1, MOD_ROWS, d), lambda i: (i // tiles_per_mod, 0, 0)),
        pl.BlockSpec((tm, oa.shape[1]), lambda i: (i, 0)),
        pl.BlockSpec((tm, ob.shape[1]), lambda i: (i, 0)),
        pl.BlockSpec((tm, oc.shape[1]), lambda i: (i, 0)),
        pl.BlockSpec(w_out.shape, lambda i: (0, 0), pipeline_mode=pl.Buffered(1)),
    ]


def _split_bf16(a):
    hi = a.astype(BF16)
    lo = (a - hi.astype(F32)).astype(BF16)
    return hi, lo


def _ffn_input(x, mod_ref, g_ref):
    ms = jnp.mean(x * x, axis=-1, keepdims=True)
    y = x * lax.rsqrt(ms + EPS)
    return (y * g_ref[...]) * (1.0 + mod_ref[0, 4:5, :]) + mod_ref[0, 3:4, :]


def _ff_chunks(ff):
    tiles = ff // MXU_TILE
    if ff % MXU_TILE or tiles < 2:
        return (ff,)
    first = (tiles // 2) * MXU_TILE
    return (first, ff - first)


def _swiglu(h, w1_ref, w3_ref, w2_ref):
    y = None
    o = 0
    for tf in _ff_chunks(w1_ref.shape[1]):
        a = _dot(h, w1_ref[:, o:o + tf])
        b = _dot(h, w3_ref[:, o:o + tf])
        g = (a * jax.nn.sigmoid(a) * b).astype(BF16)
        yc = _dot(g, w2_ref[o:o + tf, :])
        y = yc if y is None else y + yc
        o += tf
    return y


def _ffn_kernel(x_ref, mod_ref, oa_ref, ob_ref, oc_ref, wo_ref, g_ref, w1_ref, w3_ref, w2_ref, o_ref):
    x = _mixer_residual(x_ref, mod_ref, oa_ref, ob_ref, oc_ref, wo_ref)
    h = _ffn_input(x, mod_ref, g_ref).astype(BF16)
    o_ref[...] = x + mod_ref[0, 5:6, :] * _swiglu(h, w1_ref, w3_ref, w2_ref)


def _ffn_call(x2d, mod, tiles_per_mod, oa, ob, oc, w_out, g, w1, w3, w2, tm):
    n, d = x2d.shape

    def resident(a):
        return pl.BlockSpec(a.shape, lambda i: (0, 0), pipeline_mode=pl.Buffered(1))

    return pl.pallas_call(
        _ffn_kernel,
        grid=(n // tm,),
        in_specs=_mixer_specs(tm, d, tiles_per_mod, oa, ob, oc, w_out) + [
            pl.BlockSpec(g.shape, lambda i: (0, 0)), resident(w1), resident(w3), resident(w2)],
        out_specs=pl.BlockSpec((tm, d), lambda i: (i, 0)),
        out_shape=jax.ShapeDtypeStruct((n, d), F32),
        compiler_params=_cparams(("parallel",)),
        name="ffn_dense",
    )(x2d, mod, oa, ob, oc, w_out, g, w1, w3, w2)


MOE_TM = 512
SEG_ALIGN = 16
SEG_PIECES = (512, 256, 128, 64, 32, 16)
CBUF_ROWS = 2 * MOE_TM + N_EXPERTS * SEG_ALIGN
META_I1, META_I2, META_G1, META_G2, META_R1, META_R2 = range(6)
META_ROWS = 8


def _router_kernel(x_ref, mod_ref, oa_ref, ob_ref, oc_ref, wo_ref, g_ref, wr_ref, br_ref, ltri_ref,
                   x1_ref, h_ref, meta_ref, metat_ref, cnt_ref):
    x = _mixer_residual(x_ref, mod_ref, oa_ref, ob_ref, oc_ref, wo_ref)
    x1_ref[...] = x
    h = _ffn_input(x, mod_ref, g_ref)
    h_ref[...] = h.astype(BF16)
    h_hi, h_lo = _split_bf16(h)
    w_hi, w_lo = _split_bf16(wr_ref[...])
    logits = _dot(h_hi, w_hi) + (_dot(h_lo, w_hi) + _dot(h_hi, w_lo)) + br_ref[...]
    lane = lax.broadcasted_iota(jnp.int32, logits.shape, 1).astype(F32)
    logits = jnp.where(lane < N_EXPERTS, logits, NEG)
    m1 = jnp.max(logits, axis=-1, keepdims=True)
    i1 = jnp.min(jnp.where(logits == m1, lane, float(LANES)), axis=-1, keepdims=True)
    rest = jnp.where(lane == i1, NEG, logits)
    m2 = jnp.max(rest, axis=-1, keepdims=True)
    i2 = jnp.min(jnp.where(rest == m2, lane, float(LANES)), axis=-1, keepdims=True)
    e2 = jnp.exp(m2 - m1)
    den = 1.0 + e2
    sel1 = jnp.where(lane == i1, 1.0, 0.0)
    sel2 = jnp.where(lane == i2, 1.0, 0.0)
    sel = sel1 + sel2
    before = _dot(ltri_ref[...], sel.astype(BF16))
    r1 = jnp.sum(before * sel1, axis=-1, keepdims=True)
    r2 = jnp.sum(before * sel2, axis=-1, keepdims=True)
    cnt_ref[0] = jnp.sum(sel, axis=0, keepdims=True)
    cols = (i1, i2, 1.0 / den, e2 / den, r1, r2)
    meta = jnp.zeros_like(logits)
    for k, col in enumerate(cols):
        meta = jnp.where(lane == k, col, meta)
    meta_ref[...] = meta
    metat_ref[...] = meta.T[0:META_ROWS, :]


def _router_call(x2d, mod, tiles_per_mod, oa, ob, oc, w_out, g, wr, br, ltri):
    n, d = x2d.shape
    tm = MOE_TM
    nt = n // tm
    return pl.pallas_call(
        _router_kernel,
        grid=(nt,),
        in_specs=_mixer_specs(tm, d, tiles_per_mod, oa, ob, oc, w_out) + [
            pl.BlockSpec(g.shape, lambda i: (0, 0)),
            pl.BlockSpec(wr.shape, lambda i: (0, 0)),
            pl.BlockSpec(br.shape, lambda i: (0, 0)),
            pl.BlockSpec(ltri.shape, lambda i: (0, 0)),
        ],
        out_specs=[
            pl.BlockSpec((tm, d), lambda i: (i, 0)),
            pl.BlockSpec((tm, d), lambda i: (i, 0)),
            pl.BlockSpec((tm, LANES), lambda i: (i, 0)),
            pl.BlockSpec((META_ROWS, tm), lambda i: (0, i)),
            pl.BlockSpec((1, 1, LANES), lambda i: (i, 0, 0)),
        ],
        out_shape=[
            jax.ShapeDtypeStruct((n, d), F32),
            jax.ShapeDtypeStruct((n, d), BF16),
            jax.ShapeDtypeStruct((n, LANES), F32),
            jax.ShapeDtypeStruct((META_ROWS, n), F32),
            jax.ShapeDtypeStruct((nt, 1, LANES), F32),
        ],
        compiler_params=_cparams(("parallel",)),
        name="moe_router",
    )(x2d, mod, oa, ob, oc, w_out, g, wr, br, ltri)


def _pair_slots(meta, segoff_row):
    lane = lax.broadcasted_iota(jnp.int32, meta.shape, 1).astype(F32)
    i1 = meta[:, META_I1:META_I1 + 1]
    i2 = meta[:, META_I2:META_I2 + 1]
    s1 = jnp.sum(jnp.where(lane == i1, segoff_row, 0.0), axis=-1, keepdims=True) + meta[:, META_R1:META_R1 + 1]
    s2 = jnp.sum(jnp.where(lane == i2, segoff_row, 0.0), axis=-1, keepdims=True) + meta[:, META_R2:META_R2 + 1]
    return s1, s2


def _segment_copies(src, dst, src_off, dst_off, length, sem):
    out = []
    for size in SEG_PIECES:
        done = (length // (2 * size)) * (2 * size)
        s = pl.multiple_of(src_off + done, SEG_ALIGN)
        t = pl.multiple_of(dst_off + done, SEG_ALIGN)
        cp = pltpu.make_async_copy(src.at[pl.ds(s, size)], dst.at[pl.ds(t, size)], sem)
        out.append(((length & size) != 0, cp))
    return out


def _start_copies(copies):
    for pred, cp in copies:
        @pl.when(pred)
        def _(cp=cp):
            cp.start()


def _wait_copies(copies):
    for pred, cp in copies:
        @pl.when(pred)
        def _(cp=cp):
            cp.wait()


def _compact_kernel(segoff_s, base_s, len_s, h_ref, meta_ref, metat_ref, xs_in, gs_in, xs_out, gs_out,
                    cbuf2, gbuf2, sems):
    del xs_in, gs_in
    i = pl.program_id(0)
    last = pl.num_programs(0) - 1
    slot_i = i % 2
    cbuf = cbuf2.at[slot_i]
    gbuf = gbuf2.at[slot_i]

    def copies_of(tile, slot):
        out = []
        for e in range(N_EXPERTS):
            k = tile * N_EXPERTS + e
            out += _segment_copies(cbuf2.at[slot], xs_out, segoff_s[k], base_s[k], len_s[k], sems.at[0, slot])
            out += _segment_copies(gbuf2.at[slot], gs_out, segoff_s[k], base_s[k], len_s[k], sems.at[1, slot])
        return out

    @pl.when(i >= 2)
    def _():
        _wait_copies(copies_of(i - 2, slot_i))

    mt = metat_ref[...]
    s1 = mt[META_R1:META_R1 + 1]
    s2 = mt[META_R2:META_R2 + 1]
    for e in range(N_EXPERTS):
        off = segoff_s[i * N_EXPERTS + e].astype(F32)
        s1 = s1 + jnp.where(mt[META_I1:META_I1 + 1] == e, off, 0.0)
        s2 = s2 + jnp.where(mt[META_I2:META_I2 + 1] == e, off, 0.0)
    row = lax.broadcasted_iota(jnp.int32, (CBUF_ROWS, MOE_TM), 0).astype(F32)
    p1 = jnp.where(row == s1, 1.0, 0.0)
    p2 = jnp.where(row == s2, 1.0, 0.0)
    cbuf[...] = _dot((p1 + p2).astype(BF16), h_ref[...]).astype(BF16)
    meta = meta_ref[...]
    lane = lax.broadcasted_iota(jnp.int32, meta.shape, 1)
    g1 = meta[:, META_G1:META_G1 + 1]
    g2 = meta[:, META_G2:META_G2 + 1]
    g1_hi = g1.astype(BF16).astype(F32)
    g2_hi = g2.astype(BF16).astype(F32)
    a1 = jnp.where(lane == 0, g1_hi, jnp.where(lane == 1, g1 - g1_hi, 0.0)).astype(BF16)
    a2 = jnp.where(lane == 0, g2_hi, jnp.where(lane == 1, g2 - g2_hi, 0.0)).astype(BF16)
    gbuf[...] = _dot(p1.astype(BF16), a1) + _dot(p2.astype(BF16), a2)
    _start_copies(copies_of(i, slot_i))

    @pl.when(i == last)
    def _():
        @pl.when(i >= 1)
        def _():
            _wait_copies(copies_of(i - 1, 1 - slot_i))
        _wait_copies(copies_of(i, slot_i))


def _compact_call(sched, h, meta, metat, xs0, gs0):
    n, d = h.shape
    nt = n // MOE_TM
    grid_spec = pltpu.PrefetchScalarGridSpec(
        num_scalar_prefetch=3,
        grid=(nt,),
        in_specs=[
            pl.BlockSpec((MOE_TM, d), lambda i, *_: (i, 0)),
            pl.BlockSpec((MOE_TM, LANES), lambda i, *_: (i, 0)),
            pl.BlockSpec((META_ROWS, MOE_TM), lambda i, *_: (0, i)),
            pl.BlockSpec(memory_space=pl.ANY),
            pl.BlockSpec(memory_space=pl.ANY),
        ],
        out_specs=[pl.BlockSpec(memory_space=pl.ANY), pl.BlockSpec(memory_space=pl.ANY)],
        scratch_shapes=[pltpu.VMEM((2, CBUF_ROWS, d), BF16), pltpu.VMEM((2, CBUF_ROWS, LANES), F32),
                        pltpu.SemaphoreType.DMA((2, 2))],
    )
    return pl.pallas_call(
        _compact_kernel,
        grid_spec=grid_spec,
        out_shape=[jax.ShapeDtypeStruct(xs0.shape, xs0.dtype), jax.ShapeDtypeStruct(gs0.shape, gs0.dtype)],
        input_output_aliases={6: 0, 7: 1},
        compiler_params=_cparams(("arbitrary",)),
        name="moe_compact",
    )(*sched, h, meta, metat, xs0, gs0)


def _expert_kernel(exp_s, blk_s, valid_s, xs_ref, gs_ref, w1_ref, w3_ref, w2_ref, y_ref):
    del exp_s, blk_s
    j = pl.program_id(0)

    @pl.when(valid_s[j] != 0)
    def _():
        gate = gs_ref[:, 0:1] + gs_ref[:, 1:2]
        y_ref[...] = (_swiglu(xs_ref[...], w1_ref, w3_ref, w2_ref) * gate).astype(BF16)

    @pl.when(valid_s[j] == 0)
    def _():
        y_ref[...] = jnp.zeros_like(y_ref)


def _expert_call(tile_sched, xs, gs, w1, w3, w2):
    rows, d = xs.shape
    ff = w1.shape[1]
    grid_spec = pltpu.PrefetchScalarGridSpec(
        num_scalar_prefetch=3,
        grid=(rows // MOE_TM,),
        in_specs=[
            pl.BlockSpec((MOE_TM, d), lambda j, e_s, b_s, v_s: (b_s[j], 0)),
            pl.BlockSpec((MOE_TM, LANES), lambda j, e_s, b_s, v_s: (b_s[j], 0)),
            pl.BlockSpec((d, ff), lambda j, e_s, b_s, v_s: (e_s[j], 0)),
            pl.BlockSpec((d, ff), lambda j, e_s, b_s, v_s: (e_s[j], 0)),
            pl.BlockSpec((ff, d), lambda j, e_s, b_s, v_s: (e_s[j], 0)),
        ],
        out_specs=pl.BlockSpec((MOE_TM, d), lambda j, e_s, b_s, v_s: (j, 0)),
    )
    return pl.pallas_call(
        _expert_kernel,
        grid_spec=grid_spec,
        out_shape=jax.ShapeDtypeStruct((rows, d), BF16),
        compiler_params=_cparams(("arbitrary",)),
        name="moe_experts",
    )(*tile_sched, xs, gs, w1, w3, w2)


def _combine_kernel(segoff_s, base_s, len_s, x_ref, mod_ref, meta_ref, segoff_ref, y_hbm, o_ref, ybuf2, sems):
    i = pl.program_id(0)
    last = pl.num_programs(0) - 1
    slot_i = i % 2

    def copies_of(tile, slot):
        out = []
        for e in range(N_EXPERTS):
            k = tile * N_EXPERTS + e
            out += _segment_copies(y_hbm, ybuf2.at[slot], base_s[k], segoff_s[k], len_s[k], sems.at[slot])
        return out

    @pl.when(i == 0)
    def _():
        ybuf2[...] = jnp.zeros_like(ybuf2)
        _start_copies(copies_of(i, slot_i))

    @pl.when(i < last)
    def _():
        _start_copies(copies_of(i + 1, 1 - slot_i))

    _wait_copies(copies_of(i, slot_i))
    s1, s2 = _pair_slots(meta_ref[...], segoff_ref[0])
    slot = lax.broadcasted_iota(jnp.int32, (MOE_TM, CBUF_ROWS), 1).astype(F32)
    pick = (jnp.where(slot == s1, 1.0, 0.0) + jnp.where(slot == s2, 1.0, 0.0)).astype(BF16)
    o_ref[...] = x_ref[...] + mod_ref[0, 5:6, :] * _dot(pick, ybuf2[slot_i])


def _combine_call(sched, x2d, mod, tiles_per_mod, meta, segoff_v, y):
    n, d = x2d.shape
    grid_spec = pltpu.PrefetchScalarGridSpec(
        num_scalar_prefetch=3,
        grid=(n // MOE_TM,),
        in_specs=[
            pl.BlockSpec((MOE_TM, d), lambda i, *_: (i, 0)),
            pl.BlockSpec((1, MOD_ROWS, d), lambda i, *_: (i // tiles_per_mod, 0, 0)),
            pl.BlockSpec((MOE_TM, LANES), lambda i, *_: (i, 0)),
            pl.BlockSpec((1, 1, LANES), lambda i, *_: (i, 0, 0)),
            pl.BlockSpec(memory_space=pl.ANY),
        ],
        out_specs=pl.BlockSpec((MOE_TM, d), lambda i, *_: (i, 0)),
        scratch_shapes=[pltpu.VMEM((2, CBUF_ROWS, d), BF16), pltpu.SemaphoreType.DMA((2,))],
    )
    return pl.pallas_call(
        _combine_kernel,
        grid_spec=grid_spec,
        out_shape=jax.ShapeDtypeStruct((n, d), F32),
        compiler_params=_cparams(("arbitrary",)),
        name="moe_combine",
    )(*sched, x2d, mod, meta, segoff_v, y)


def _moe_call(x2d, mod, tiles_per_mod, outs, w_out, g, wr, br, w1, w3, w2):
    n, d = x2d.shape
    nt = n // MOE_TM
    ltri = jnp.asarray(np.tril(np.ones((MOE_TM, MOE_TM), np.float32), -1), BF16)
    x2d, h, meta, metat, counts = _router_call(x2d, mod, tiles_per_mod, *outs, w_out, g, wr, br, ltri)

    cnt = counts[:, 0, :N_EXPERTS].astype(jnp.int32)
    seg_len = (cnt + SEG_ALIGN - 1) // SEG_ALIGN * SEG_ALIGN
    segoff = jnp.cumsum(seg_len, axis=1) - seg_len
    region = (jnp.sum(seg_len, axis=0) + MOE_TM - 1) // MOE_TM * MOE_TM
    region_start = jnp.cumsum(region) - region
    base = region_start[None, :] + jnp.cumsum(seg_len, axis=0) - seg_len
    rows_max = 2 * n + nt * N_EXPERTS * (SEG_ALIGN - 1) + N_EXPERTS * (MOE_TM - 1)
    n_sorted_tiles = (rows_max + MOE_TM - 1) // MOE_TM
    tile_end = jnp.cumsum(region // MOE_TM)
    total_tiles = tile_end[-1]
    jt = jnp.arange(n_sorted_tiles, dtype=jnp.int32)
    blk = jnp.minimum(jt, total_tiles - 1)
    tile_expert = jnp.sum((blk[:, None] >= tile_end[None, :]).astype(jnp.int32), axis=1)
    tile_sched = (tile_expert.astype(jnp.int32), blk.astype(jnp.int32), (jt < total_tiles).astype(jnp.int32))
    sched = tuple(a.reshape(-1).astype(jnp.int32) for a in (segoff, base, seg_len))
    segoff_v = jnp.pad(segoff.astype(F32), ((0, 0), (0, LANES - N_EXPERTS)))[:, None, :]

    rows = n_sorted_tiles * MOE_TM
    xs, gs = _compact_call(sched, h, meta, metat, jnp.zeros((rows, d), BF16), jnp.zeros((rows, LANES), F32))
    y = _expert_call(tile_sched, xs, gs, w1, w3, w2)
    return _combine_call(sched, x2d, mod, tiles_per_mod, meta, segoff_v, y)


def _rope_tables(t):
    rows = jnp.arange(t, dtype=F32) // GRID_W
    cols = jnp.arange(t, dtype=F32) % GRID_W

    def tables(rot_dim):
        a = rot_dim // 2
        inv = 1.0 / (ROPE_BASE ** (jnp.arange(0, a, 2, dtype=F32) / a))
        ar = rows[:, None] * inv
        ac = cols[:, None] * inv
        cos = jnp.concatenate([jnp.cos(ar), jnp.cos(ar), jnp.cos(ac), jnp.cos(ac)], axis=-1)
        sin = jnp.concatenate([-jnp.sin(ar), jnp.sin(ar), -jnp.sin(ac), jnp.sin(ac)], axis=-1)
        return cos, sin

    ca, sa = tables(HEAD_DIM)
    cos_a = jnp.tile(ca, (1, LANES // HEAD_DIM))
    sin_a = jnp.tile(sa, (1, LANES // HEAD_DIM))
    cc, sc = tables(C_ROPE)
    ones = jnp.ones((t, C_NOPE), F32)
    tail = C_SLOT - C_NOPE - C_ROPE
    cos_c = jnp.concatenate([ones, cc, jnp.ones((t, tail), F32)], axis=-1)
    sin_c = jnp.concatenate([0 * ones, sc, jnp.zeros((t, tail), F32)], axis=-1)
    return cos_a, sin_a, cos_c, sin_c


def _head_perm():
    order = []
    for j in range(A_HEADS // 2):
        order += [j, A_HEADS // 2 + j]
    return np.concatenate([np.arange(h * HEAD_DIM, (h + 1) * HEAD_DIM) for h in order])


def _segment_mean_matrix(widths, total):
    m = np.zeros((total, total), np.float32)
    o = 0
    while o < total:
        for w, used in widths:
            if used:
                m[o:o + w, o:o + w] = 1.0 / w
            o += w
    return jnp.asarray(m, BF16)


def _slot_vec(nope, rope):
    z = jnp.zeros((C_SLOT - C_NOPE - C_ROPE,), F32)
    n = jnp.zeros((C_NOPE,), F32) if nope is None else nope
    r = jnp.zeros((C_ROPE,), F32) if rope is None else rope
    return jnp.tile(jnp.concatenate([n, r, z]), C_HEADS)[None, :]


def _layer_consts(i, p, perm, tabs):
    w_in = p["w_in"][i]
    o_kr = A_Q + 2 * A_KV + 2 * B_CH + C_Q_RANK + C_KV_RANK
    d = w_in.shape[0]
    kr_cols = jnp.concatenate([jnp.zeros((d, C_NOPE), F32), w_in[:, o_kr:o_kr + C_ROPE],
                               jnp.zeros((d, C_SLOT - C_NOPE - C_ROPE), F32)], axis=1)
    win = jnp.concatenate([w_in[:, :A_Q][:, perm], w_in[:, A_Q:o_kr], kr_cols], axis=1).astype(BF16)

    w_uq = p["c_w_uq"][i].reshape(C_Q_RANK, C_HEADS, C_NOPE + C_ROPE)
    wuq = jnp.pad(w_uq, ((0, 0), (0, 0), (0, C_SLOT - C_NOPE - C_ROPE))).reshape(C_Q_RANK, C_HEADS * C_SLOT)
    w_ukv = p["c_w_ukv"][i].reshape(C_KV_RANK, C_HEADS, C_NOPE + C_V)
    wukvk = jnp.pad(w_ukv[:, :, :C_NOPE], ((0, 0), (0, 0), (0, C_SLOT - C_NOPE))).reshape(C_KV_RANK, -1)
    wukvv = w_ukv[:, :, C_NOPE:].reshape(C_KV_RANK, C_HEADS * C_V)

    sa = _segment_mean_matrix([(HEAD_DIM, True)], MXU_TILE)
    sc = _segment_mean_matrix([(C_NOPE, True), (C_ROPE, True), (C_SLOT - C_NOPE - C_ROPE, False)], MXU_TILE)
    gq = jnp.tile(p["a_q_norm_g"][i] * (A_SCALE * LOG2E), A_HEADS)[None, :]
    gk = jnp.tile(p["a_k_norm_g"][i], A_KV_HEADS)[None, :]
    gqc = _slot_vec(p["c_q_nope_norm_g"][i], p["c_q_rope_norm_g"][i]) * (MLA_SCALE * LOG2E)
    gkn = _slot_vec(p["c_k_nope_norm_g"][i], None)
    gkr = _slot_vec(None, p["c_k_rope_norm_g"][i])[:, :C_SLOT]
    return (p["mix_norm_g"][i][None, :], win) + tabs + (
        sa, sc, gq, gk, p["c_q_rank_norm_g"][i][None, :], p["c_kv_rank_norm_g"][i][None, :],
        wuq.astype(BF16), wukvk.astype(BF16), wukvv.astype(BF16), gqc, gkn, gkr)


def kernel(x, c, ctx, c_ctx, ada_w, ada_b, mix_norm_g, ffn_norm_g, w_in, w_out, a_q_norm_g, a_k_norm_g, a_sink, b_conv_w, b_conv_b, b_ln_g, b_ln_b, c_q_rank_norm_g, c_kv_rank_norm_g, c_w_uq, c_w_ukv, c_q_nope_norm_g, c_k_nope_norm_g, c_q_rope_norm_g, c_k_rope_norm_g, dense_w1, dense_w3, dense_w2, moe_router_w, moe_router_b, moe_w1, moe_w3, moe_w2):
    p = dict(w_in=w_in, c_w_uq=c_w_uq, c_w_ukv=c_w_ukv, a_q_norm_g=a_q_norm_g, a_k_norm_g=a_k_norm_g,
             c_q_nope_norm_g=c_q_nope_norm_g, c_k_nope_norm_g=c_k_nope_norm_g,
             c_q_rope_norm_g=c_q_rope_norm_g, c_k_rope_norm_g=c_k_rope_norm_g,
             c_q_rank_norm_g=c_q_rank_norm_g, c_kv_rank_norm_g=c_kv_rank_norm_g, mix_norm_g=mix_norm_g)
    bsz, t, d = x.shape
    ctx_len = ctx.shape[1]
    depth = ada_w.shape[0]
    n_x, n_c = bsz * t, bsz * ctx_len
    tm_pre = min(4 * PRE_SUB, n_c)
    tm_tok = 512
    tq = min(1024, t)

    ada_rows = ((bsz + 1 + 7) // 8) * 8
    c_pad = jnp.concatenate([c, c_ctx[None, :], jnp.zeros((ada_rows - bsz - 1, d), F32)], axis=0)
    mods = _ada_call(c_pad, ada_w, ada_b).reshape(depth, ada_rows, 6, d)
    mods = jnp.pad(mods, ((0, 0), (0, 0), (0, MOD_ROWS - 6), (0, 0)))

    tabs_x = _rope_tables(t)
    ones = jnp.ones((tm_pre, LANES), F32)
    tabs_c = (ones, 0 * ones, ones, 0 * ones)
    perm = _head_perm()

    x2 = x.reshape(n_x, d)
    c2 = ctx.reshape(n_c, d)
    for i in range(depth):
        last = i == depth - 1
        mod_x = mods[i, :bsz]
        mod_c = mods[i, bsz:bsz + 1]
        consts_x = _layer_consts(i, p, perm, tabs_x)
        consts_c = _layer_consts(i, p, perm, tabs_c)

        qa_x, ka_x, va_x, u_x, qc_x, kc_x, vc_x = _pre_call(x2, mod_x, t // tm_pre, t // tm_pre, consts_x, tm_pre)
        if last:
            ka_c, va_c, kc_c, vc_c = _pre_call(c2, mod_c, n_c // tm_pre, 1, consts_c, tm_pre, kv_only=True)
        else:
            qa_c, ka_c, va_c, u_c, qc_c, kc_c, vc_c = _pre_call(c2, mod_c, n_c // tm_pre, 1, consts_c, tm_pre)

        def r3(a, length):
            return a.reshape(bsz, length, a.shape[-1])

        sink_row = jnp.repeat(a_sink[i] * LOG2E, BLOCK)[None, :]
        conv_w = jnp.pad(b_conv_w[i], ((0, 32 - B_WIDTH), (0, 0)))
        conv_p = (conv_w, b_conv_b[i][None, :], b_ln_g[i][None, :], b_ln_b[i][None, :])
        w_o = jnp.concatenate([w_out[i][:A_Q][perm], w_out[i][A_Q:]], axis=0).astype(BF16)

        o_a, o_b = _attn_a_call(qa_x, r3(ka_x, t), va_x, r3(ka_c, ctx_len), va_c, sink_row, r3(u_x, t), conv_p, bsz, t)
        o_c = _mla_call(qc_x, r3(kc_x, t), vc_x, r3(kc_c, ctx_len), vc_c, bsz, t, tq)
        if not last:
            oc_a, oc_b = _attn_a_call(qa_c, None, None, r3(ka_c, ctx_len), va_c, sink_row, r3(u_c, ctx_len), conv_p,
                                      bsz, ctx_len)
            oc_c = _mla_call(qc_c, None, None, r3(kc_c, ctx_len), vc_c, bsz, ctx_len, ctx_len)

        j = i // 2
        g_ffn = ffn_norm_g[i][None, :]
        if i % 2 == 0:
            w = (dense_w1[j].astype(BF16), dense_w3[j].astype(BF16), dense_w2[j].astype(BF16))

            def mix(a2, mod, tiles_per_mod, outs, w=w, g_ffn=g_ffn, w_o=w_o):
                return _ffn_call(a2, mod, tiles_per_mod, *outs, w_o, g_ffn, *w, tm_tok)
        else:
            w = tuple(a.astype(BF16).reshape(-1, a.shape[-1]) for a in (moe_w1[j], moe_w3[j], moe_w2[j]))
            wr = jnp.pad(moe_router_w[j], ((0, 0), (0, LANES - N_EXPERTS)))
            br = jnp.pad(moe_router_b[j], (0, LANES - N_EXPERTS))[None, :]

            def mix(a2, mod, tiles_per_mod, outs, w=w, g_ffn=g_ffn, wr=wr, br=br, w_o=w_o):
                return _moe_call(a2, mod, tiles_per_mod * (tm_tok // MOE_TM), outs, w_o, g_ffn, wr, br, *w)
        x2 = mix(x2, mod_x, t // tm_tok, (o_a, o_b, o_c))
        if not last:
            c2 = mix(c2, mod_c, n_c // tm_tok, (oc_a, oc_b, oc_c))
    return x2.reshape(bsz, t, d)
```

```python
import functools

import jax
import jax.numpy as jnp
import numpy as np
from jax import lax
from jax.experimental import pallas as pl
from jax.experimental.pallas import tpu as pltpu

F32 = jnp.float32
BF16 = jnp.bfloat16

D_MODEL = 1024
GRID_W = 64
HEAD_DIM = 64
A_HEADS = 8
A_KV_HEADS = 2
A_WINDOW = 128
BLOCK = 128
B_CH = 256
B_WIDTH = 31
C_HEADS = 4
C_Q_RANK = 384
C_KV_RANK = 256
C_NOPE = 64
C_ROPE = 32
C_V = 64
A_Q = A_HEADS * HEAD_DIM
A_KV = A_KV_HEADS * HEAD_DIM
IN_COLS_PAD = 2048
D_FF = 2816
N_EXPERTS = 8
ROPE_BASE = 10000.0
EPS = 1e-6
NEG = -1e30
A_SCALE = HEAD_DIM ** -0.5
MLA_SCALE = (C_NOPE + C_ROPE) ** -0.5
LOG2E = 1.4426950408889634

LANES = 128
MXU_TILE = 256
MOD_ROWS = 8
C_SLOT = 128
VMEM_LIMIT = 56 * 1024 * 1024


def _cparams(sem):
    return pltpu.CompilerParams(dimension_semantics=sem, vmem_limit_bytes=VMEM_LIMIT)


def _dot(a, b):
    return jnp.dot(a, b, preferred_element_type=F32)


def _dot_nt(a, b):
    return lax.dot_general(a, b, (((1,), (1,)), ((), ())), preferred_element_type=F32)


def _ada_kernel(c_ref, w_ref, b_ref, o_ref):
    c = c_ref[...]
    a = c * jax.nn.sigmoid(c)
    o_ref[0] = _dot(a.astype(BF16), w_ref[0].astype(BF16)) + b_ref[0]


def _ada_call(c_pad, ada_w, ada_b):
    depth, d, n6 = ada_w.shape
    rows = c_pad.shape[0]
    tn = 1536
    return pl.pallas_call(
        _ada_kernel,
        grid=(depth, n6 // tn),
        in_specs=[
            pl.BlockSpec((rows, d), lambda i, j: (0, 0)),
            pl.BlockSpec((1, d, tn), lambda i, j: (i, 0, j)),
            pl.BlockSpec((1, 1, tn), lambda i, j: (i, 0, j)),
        ],
        out_specs=pl.BlockSpec((1, rows, tn), lambda i, j: (i, 0, j)),
        out_shape=jax.ShapeDtypeStruct((depth, rows, n6), F32),
        compiler_params=_cparams(("parallel", "parallel")),
        name="ada_proj",
    )(c_pad, ada_w, ada_b.reshape(depth, 1, n6))


def _rope_chunk(c, cos, sin, half):
    lane = lax.broadcasted_iota(jnp.int32, c.shape, 1)
    lo = (lane & (2 * half - 1)) < half
    partner = jnp.where(lo, pltpu.roll(c, LANES - half, 1), pltpu.roll(c, half, 1))
    return c * cos + partner * sin


PRE_SUB = 256


def _segment_mean(sq, s_ref):
    w = s_ref.shape[0]
    return jnp.concatenate([_dot(sq[:, c:c + w].astype(BF16), s_ref[...]) for c in range(0, sq.shape[1], w)], axis=1)


def _pre_kernel(x_ref, mod_ref, gmix_ref, win_ref, cosa_ref, sina_ref, cosc_ref, sinc_ref,
                sa_ref, sc_ref, gq_ref, gk_ref, gcq_ref, gckv_ref, wuq_ref, wukvk_ref, wukvv_ref,
                gqc_ref, gkn_ref, gkr_ref, *out_refs, kv_only):
    if kv_only:
        ka_ref, va_ref, kc_ref, vc_ref = out_refs
        o_ka, o_ckv = 0, 2 * A_KV
    else:
        qa_ref, ka_ref, va_ref, u_ref, qc_ref, kc_ref, vc_ref = out_refs
        o_ka, o_ckv = A_Q, A_Q + 2 * A_KV + 2 * B_CH + C_Q_RANK
    for sub in range(x_ref.shape[0] // PRE_SUB):
        rows = slice(sub * PRE_SUB, (sub + 1) * PRE_SUB)
        x = x_ref[rows, :]
        ms = jnp.mean(x * x, axis=-1, keepdims=True)
        y = x * lax.rsqrt(ms + EPS)
        shift = mod_ref[0, 0:1, :]
        scale = mod_ref[0, 1:2, :]
        h = (y * gmix_ref[...]) * (1.0 + scale) + shift
        p = _dot(h.astype(BF16), win_ref[...])

        cosa, sina = cosa_ref[rows, :], sina_ref[rows, :]
        cosc, sinc = cosc_ref[rows, :], sinc_ref[rows, :]

        if not kv_only:
            qa = p[:, 0:A_Q]
            ssq = _segment_mean(qa * qa, sa_ref)
            qa = qa * lax.rsqrt(ssq + EPS) * gq_ref[...]
            for j in range(A_Q // LANES):
                sl = slice(j * LANES, (j + 1) * LANES)
                qa_ref[rows, sl] = _rope_chunk(qa[:, sl], cosa, sina, HEAD_DIM // 4).astype(BF16)

            o = A_Q + 2 * A_KV
            u_ref[rows, :] = p[:, o:o + B_CH] * jax.nn.sigmoid(p[:, o + B_CH:o + 2 * B_CH])

            o = o + 2 * B_CH
            cq = p[:, o:o + C_Q_RANK]
            cq = cq * lax.rsqrt(jnp.mean(cq * cq, axis=-1, keepdims=True) + EPS) * gcq_ref[...]
            qc = _dot(cq.astype(BF16), wuq_ref[...])
            ssq = _segment_mean(qc * qc, sc_ref)
            qc = qc * lax.rsqrt(ssq + EPS) * gqc_ref[...]
            for j in range(C_HEADS):
                sl = slice(j * C_SLOT, (j + 1) * C_SLOT)
                qc_ref[rows, sl] = _rope_chunk(qc[:, sl], cosc, sinc, C_ROPE // 4).astype(BF16)

        ka = p[:, o_ka:o_ka + A_KV]
        ssk = _dot((ka * ka).astype(BF16), sa_ref[0:A_KV, 0:A_KV])
        ka = ka * lax.rsqrt(ssk + EPS) * gk_ref[...]
        ka_ref[rows, :] = _rope_chunk(ka, cosa, sina, HEAD_DIM // 4).astype(BF16)
        va_ref[:, rows] = p[:, o_ka + A_KV:o_ka + 2 * A_KV].T.astype(BF16)

        o = o_ckv
        ckv = p[:, o:o + C_KV_RANK]
        ckv = (ckv * lax.rsqrt(jnp.mean(ckv * ckv, axis=-1, keepdims=True) + EPS) * gckv_ref[...]).astype(BF16)
        kn = _dot(ckv, wukvk_ref[...])
        vc_ref[:, rows] = _dot(ckv, wukvv_ref[...]).T.astype(BF16)
        ssk = _segment_mean(kn * kn, sc_ref)
        kn = kn * lax.rsqrt(ssk + EPS) * gkn_ref[...]
        o = o + C_KV_RANK
        kr = p[:, o:o + C_SLOT]
        kr = kr * lax.rsqrt(jnp.sum(kr * kr, axis=-1, keepdims=True) * (1.0 / C_ROPE) + EPS) * gkr_ref[...]
        kr = _rope_chunk(kr, cosc, sinc, C_ROPE // 4)
        for j in range(C_HEADS):
            sl = slice(j * C_SLOT, (j + 1) * C_SLOT)
            kc_ref[rows, sl] = (kn[:, sl] + kr).astype(BF16)


def _pre_call(x2d, mod, tiles_per_mod, tab_tiles, consts, tm, kv_only=False):
    n, d = x2d.shape
    (gmix, win, cosa, sina, cosc, sinc, sa, sc, gq, gk, gcq, gckv, wuq, wukvk, wukvv, gqc, gkn, gkr) = consts
    if kv_only:
        o_ckv = A_Q + 2 * A_KV + 2 * B_CH + C_Q_RANK
        win = jnp.concatenate([win[:, A_Q:A_Q + 2 * A_KV], win[:, o_ckv:]], axis=1)

    def const(a):
        return pl.BlockSpec(a.shape, lambda i: (0,) * a.ndim)

    def tab(a):
        return pl.BlockSpec((tm, LANES), lambda i: (i % tab_tiles, 0))

    in_specs = [
        pl.BlockSpec((tm, d), lambda i: (i, 0)),
        pl.BlockSpec((1, MOD_ROWS, d), lambda i: (i // tiles_per_mod, 0, 0)),
        const(gmix), const(win), tab(cosa), tab(sina), tab(cosc), tab(sinc),
        const(sa), const(sc), const(gq), const(gk), const(gcq), const(gckv),
        const(wuq), const(wukvk), const(wukvv), const(gqc), const(gkn), const(gkr),
    ]
    widths = (A_Q, A_KV, A_KV, B_CH, C_HEADS * C_SLOT, C_HEADS * C_SLOT, C_HEADS * C_V)
    dtypes = (BF16, BF16, BF16, F32, BF16, BF16, BF16)
    out_specs = [pl.BlockSpec((tm, w), lambda i: (i, 0)) for w in widths]
    out_shape = [jax.ShapeDtypeStruct((n, w), dt) for w, dt in zip(widths, dtypes)]
    out_specs[2] = pl.BlockSpec((A_KV, tm), lambda i: (0, i))
    out_shape[2] = jax.ShapeDtypeStruct((A_KV, n), BF16)
    out_specs[6] = pl.BlockSpec((C_HEADS * C_V, tm), lambda i: (0, i))
    out_shape[6] = jax.ShapeDtypeStruct((C_HEADS * C_V, n), BF16)
    if kv_only:
        keep = (1, 2, 5, 6)
        out_specs = [out_specs[k] for k in keep]
        out_shape = [out_shape[k] for k in keep]
    return pl.pallas_call(
        functools.partial(_pre_kernel, kv_only=kv_only),
        grid=(n // tm,),
        in_specs=in_specs,
        out_specs=out_specs,
        out_shape=out_shape,
        compiler_params=_cparams(("parallel",)),
        name="pre_attn",
    )(x2d, mod, gmix, win, cosa, sina, cosc, sinc, sa, sc, gq, gk, gcq, gckv, wuq, wukvk, wukvv,
      gqc, gkn, gkr)


CONV_PAD = 16
CONV_CHUNK = 128
SUBLANES = 8


def _conv_chunk(pad_ref, c, w_ref, b_ref, g_ref, beta_ref):
    off = CONV_PAD - B_WIDTH // 2
    nq = (off + B_WIDTH - 1) // SUBLANES + 1
    win = CONV_CHUNK + (nq - 1) * SUBLANES
    base = c * CONV_CHUNK
    acc = jnp.zeros((CONV_CHUNK, B_CH), F32)
    for r in range(SUBLANES):
        taps = [k for k in range(B_WIDTH) if (off + k) % SUBLANES == r]
        if not taps:
            continue
        w_r = pad_ref[base + r:base + r + win, :]
        part = None
        for k in taps:
            q = (off + k) // SUBLANES
            term = w_r[q * SUBLANES:q * SUBLANES + CONV_CHUNK, :] * w_ref[k:k + 1, :]
            part = term if part is None else part + term
        acc = acc + part
    y = acc + b_ref[...]
    mu = jnp.mean(y, axis=-1, keepdims=True)
    yc = y - mu
    var = jnp.mean(yc * yc, axis=-1, keepdims=True)
    z = yc * lax.rsqrt(var + EPS) * g_ref[...] + beta_ref[...]
    return (z * jax.nn.sigmoid(z)).astype(BF16)


ATTN_A_QBLOCKS = 8


ONES_ROWS = 16


def _attn_a_kernel(*refs, t, has_local, qblocks):
    u_ref, uprev_ref, unext_ref, cw_ref, cb_ref, cg_ref, cbeta_ref, o_ref, ob_ref, pad_ref = refs[-10:]
    if has_local:
        q_ref, k_ref, vt_ref, kc_ref, vct_ref, sink_ref, bias_ref = refs[:-10]
    else:
        q_ref, kc_ref, vct_ref, sink_ref = refs[:-10]
    step = pl.program_id(1)
    tq = q_ref.shape[0]
    pad_ref[0:CONV_PAD, :] = jnp.where(step > 0, uprev_ref[0], 0.0)
    pad_ref[CONV_PAD:CONV_PAD + tq, :] = u_ref[0]
    pad_ref[CONV_PAD + tq:CONV_PAD + tq + CONV_PAD, :] = jnp.where(step < pl.num_programs(1) - 1, unext_ref[0], 0.0)
    nchunk = A_Q // LANES
    span = 3 * BLOCK
    lane = lax.broadcasted_iota(jnp.int32, (BLOCK, LANES), 1)
    row = lax.broadcasted_iota(jnp.int32, (A_KV, BLOCK), 0)
    zero = jnp.zeros((BLOCK, LANES), BF16)
    sink = sink_ref[...]
    kc = kc_ref[0]
    vct = vct_ref[...]
    for blk in range(qblocks):
        rows = slice(blk * BLOCK, (blk + 1) * BLOCK)
        chunks = [q_ref[rows, j * LANES:(j + 1) * LANES] for j in range(nchunk)]
        qs = jnp.concatenate([jnp.where(lane < HEAD_DIM, c, zero) for c in chunks]
                             + [jnp.where(lane >= HEAD_DIM, c, zero) for c in chunks], axis=0)
        if has_local:
            n = pl.program_id(1) * qblocks + blk
            start = pl.multiple_of(jnp.clip((n - 1) * BLOCK, 0, t - span), BLOCK)
            keys = jnp.concatenate([k_ref[0, pl.ds(start, span), :], kc], axis=0)
            vt = jnp.concatenate([vt_ref[:, pl.ds(start, span)], vct], axis=1)
        else:
            keys, vt = kc, vct
        s = _dot_nt(keys, qs)
        if has_local:
            s = jnp.concatenate([s[:span] + bias_ref[n - start // BLOCK], s[span:]], axis=0)
        m = jnp.maximum(jnp.max(s, axis=0, keepdims=True), sink)
        e = jnp.exp2(s - m).astype(BF16)
        vte = jnp.concatenate([vt, jnp.ones((ONES_ROWS, vt.shape[1]), BF16)], axis=0)
        acc = _dot(vte, e)
        den = acc[A_KV:A_KV + 1] + jnp.exp2(sink - m)
        out = acc[:A_KV] * (1.0 / den)
        for j in range(nchunk):
            x = jnp.where(row < HEAD_DIM, out[:, j * BLOCK:(j + 1) * BLOCK],
                          out[:, (nchunk + j) * BLOCK:(nchunk + j + 1) * BLOCK])
            o_ref[rows, j * LANES:(j + 1) * LANES] = x.T.astype(BF16)
        ob_ref[rows, :] = _conv_chunk(pad_ref, blk, cw_ref, cb_ref, cg_ref, cbeta_ref)


def _window_bias():
    r = np.arange(BLOCK)[None, :]
    c = np.arange(3 * BLOCK)[:, None]
    pats = [np.where(np.abs(c - p * BLOCK - r) <= A_WINDOW, 0.0, NEG) for p in range(3)]
    return jnp.asarray(np.stack([np.tile(p, (1, A_HEADS)) for p in pats]), F32)


def _attn_a_call(qa, ka, vat, kac, vact, sink_row, u3, conv_p, bsz, t):
    has_local = ka is not None
    qblocks = min(ATTN_A_QBLOCKS, t // BLOCK)
    tq = qblocks * BLOCK
    nq = t // tq
    ctx_len = kac.shape[1]
    in_specs = [pl.BlockSpec((tq, A_Q), lambda b, n: (b * nq + n, 0))]
    args = [qa]
    if has_local:
        in_specs += [pl.BlockSpec((1, t, A_KV), lambda b, n: (b, 0, 0)),
                     pl.BlockSpec((A_KV, t), lambda b, n: (0, b))]
        args += [ka, vat]
    in_specs += [pl.BlockSpec((1, ctx_len, A_KV), lambda b, n: (b, 0, 0)),
                 pl.BlockSpec((A_KV, ctx_len), lambda b, n: (0, b))]
    in_specs += [pl.BlockSpec(sink_row.shape, lambda b, n: (0, 0))]
    args += [kac, vact, sink_row]
    if has_local:
        bias = _window_bias()
        in_specs += [pl.BlockSpec(bias.shape, lambda b, n: (0, 0, 0))]
        args += [bias]
    halo_per_step = tq // CONV_PAD
    n_halo = t // CONV_PAD
    in_specs += [
        pl.BlockSpec((1, tq, B_CH), lambda b, n: (b, n, 0)),
        pl.BlockSpec((1, CONV_PAD, B_CH), lambda b, n: (b, jnp.maximum(n * halo_per_step - 1, 0), 0)),
        pl.BlockSpec((1, CONV_PAD, B_CH), lambda b, n: (b, jnp.minimum((n + 1) * halo_per_step, n_halo - 1), 0)),
    ] + [pl.BlockSpec(a.shape, lambda b, n: (0, 0)) for a in conv_p]
    args += [u3, u3, u3, *conv_p]
    return pl.pallas_call(
        functools.partial(_attn_a_kernel, t=t, has_local=has_local, qblocks=qblocks),
        grid=(bsz, nq),
        in_specs=in_specs,
        out_specs=[pl.BlockSpec((tq, A_Q), lambda b, n: (b * nq + n, 0)),
                   pl.BlockSpec((tq, B_CH), lambda b, n: (b * nq + n, 0))],
        out_shape=[jax.ShapeDtypeStruct((bsz * t, A_Q), BF16), jax.ShapeDtypeStruct((bsz * t, B_CH), BF16)],
        scratch_shapes=[pltpu.VMEM((tq + 2 * CONV_PAD, B_CH), F32)],
        compiler_params=_cparams(("parallel", "parallel")),
        name="attn_ab_local" if has_local else "attn_ab_ctx",
    )(*args)


def _mla_kernel(*refs, has_local):
    if has_local:
        q_ref, kx_ref, vxt_ref, kc_ref, vct_ref, o_ref = refs
    else:
        q_ref, kc_ref, vct_ref, o_ref = refs
    def scores(h):
        sl = slice(h * C_SLOT, (h + 1) * C_SLOT)
        q = q_ref[:, sl]
        s_c = _dot_nt(kc_ref[0, :, sl], q)
        s_x = _dot_nt(kx_ref[0, :, sl], q) if has_local else None
        return s_c, s_x

    outs = []
    nxt = scores(0)
    for h in range(C_HEADS):
        vs = slice(h * C_V, (h + 1) * C_V)
        s_c, s_x = nxt
        if h + 1 < C_HEADS:
            nxt = scores(h + 1)
        m = jnp.max(s_c, axis=0, keepdims=True)
        if has_local:
            m = jnp.maximum(m, jnp.max(s_x, axis=0, keepdims=True))
        e_c = jnp.exp2(s_c - m).astype(BF16)
        vt = jnp.concatenate([vct_ref[vs, :], jnp.ones((ONES_ROWS, e_c.shape[0]), BF16)], axis=0)
        acc = _dot(vt, e_c)
        if has_local:
            e_x = jnp.exp2(s_x - m).astype(BF16)
            vt = jnp.concatenate([vxt_ref[vs, :], jnp.ones((ONES_ROWS, e_x.shape[0]), BF16)], axis=0)
            acc = acc + _dot(vt, e_x)
        outs.append(acc[:C_V] * (1.0 / acc[C_V:C_V + 1]))
    o_ref[...] = jnp.concatenate(outs, axis=0).T.astype(BF16)


def _mla_call(qc, kx, vxt, kcc, vcct, bsz, t, tq):
    has_local = kx is not None
    nq = t // tq
    ctx_len = kcc.shape[1]
    wq = C_HEADS * C_SLOT
    wv = C_HEADS * C_V
    in_specs = [pl.BlockSpec((tq, wq), lambda b, n: (b * nq + n, 0))]
    args = [qc]
    if has_local:
        in_specs += [pl.BlockSpec((1, t, wq), lambda b, n: (b, 0, 0)),
                     pl.BlockSpec((wv, t), lambda b, n: (0, b))]
        args += [kx, vxt]
    in_specs += [pl.BlockSpec((1, ctx_len, wq), lambda b, n: (b, 0, 0)),
                 pl.BlockSpec((wv, ctx_len), lambda b, n: (0, b))]
    args += [kcc, vcct]
    return pl.pallas_call(
        functools.partial(_mla_kernel, has_local=has_local),
        grid=(bsz, nq),
        in_specs=in_specs,
        out_specs=pl.BlockSpec((tq, wv), lambda b, n: (b * nq + n, 0)),
        out_shape=jax.ShapeDtypeStruct((bsz * t, wv), BF16),
        compiler_params=_cparams(("parallel", "parallel")),
        name="mla_local" if has_local else "mla_ctx",
    )(*args)


def _mixer_residual(x_ref, mod_ref, oa_ref, ob_ref, oc_ref, wo_ref):
    y = _dot(oa_ref[...], wo_ref[0:A_Q, :])
    y = y + _dot(ob_ref[...], wo_ref[A_Q:A_Q + B_CH, :])
    y = y + _dot(oc_ref[...], wo_ref[A_Q + B_CH:, :])
    return x_ref[...] + mod_ref[0, 2:3, :] * y


def _mixer_specs(tm, d, tiles_per_mod, oa, ob, oc, w_out):
    return [
        pl.BlockSpec((tm, d), lambda i: (i, 0)),
        pl.BlockSpec((1, MOD_ROWS, d), lambda i: (i // tiles_per_mod, 0, 0)),
        pl.BlockSpec((tm, oa.shape[1]), lambda i: (i, 0)),
        pl.BlockSpec((tm, ob.shape[1]), lambda i: (i, 0)),
        pl.BlockSpec((tm, oc.shape[1]), lambda i: (i, 0)),
        pl.BlockSpec(w_out.shape, lambda i: (0, 0), pipeline_mode=pl.Buffered(1)),
    ]


def _split_bf16(a):
    hi = a.astype(BF16)
    lo = (a - hi.astype(F32)).astype(BF16)
    return hi, lo


def _ffn_input(x, mod_ref, g_ref):
    ms = jnp.mean(x * x, axis=-1, keepdims=True)
    y = x * lax.rsqrt(ms + EPS)
    return (y * g_ref[...]) * (1.0 + mod_ref[0, 4:5, :]) + mod_ref[0, 3:4, :]


def _ff_chunks(ff):
    tiles = ff // MXU_TILE
    if ff % MXU_TILE or tiles < 2:
        return (ff,)
    first = (tiles // 2) * MXU_TILE
    return (first, ff - first)


def _swiglu(h, w1_ref, w3_ref, w2_ref):
    y = None
    o = 0
    for tf in _ff_chunks(w1_ref.shape[1]):
        a = _dot(h, w1_ref[:, o:o + tf])
        b = _dot(h, w3_ref[:, o:o + tf])
        g = (a * jax.nn.sigmoid(a) * b).astype(BF16)
        yc = _dot(g, w2_ref[o:o + tf, :])
        y = yc if y is None else y + yc
        o += tf
    return y


def _ffn_kernel(x_ref, mod_ref, oa_ref, ob_ref, oc_ref, wo_ref, g_ref, w1_ref, w3_ref, w2_ref, o_ref):
    x = _mixer_residual(x_ref, mod_ref, oa_ref, ob_ref, oc_ref, wo_ref)
    h = _ffn_input(x, mod_ref, g_ref).astype(BF16)
    o_ref[...] = x + mod_ref[0, 5:6, :] * _swiglu(h, w1_ref, w3_ref, w2_ref)


def _ffn_call(x2d, mod, tiles_per_mod, oa, ob, oc, w_out, g, w1, w3, w2, tm):
    n, d = x2d.shape

    def resident(a):
        return pl.BlockSpec(a.shape, lambda i: (0, 0), pipeline_mode=pl.Buffered(1))

    return pl.pallas_call(
        _ffn_kernel,
        grid=(n // tm,),
        in_specs=_mixer_specs(tm, d, tiles_per_mod, oa, ob, oc, w_out) + [
            pl.BlockSpec(g.shape, lambda i: (0, 0)), resident(w1), resident(w3), resident(w2)],
        out_specs=pl.BlockSpec((tm, d), lambda i: (i, 0)),
        out_shape=jax.ShapeDtypeStruct((n, d), F32),
        compiler_params=_cparams(("parallel",)),
        name="ffn_dense",
    )(x2d, mod, oa, ob, oc, w_out, g, w1, w3, w2)


MOE_TM = 512
SEG_ALIGN = 16
SEG_PIECES = (512, 256, 128, 64, 32, 16)
CBUF_ROWS = 2 * MOE_TM + N_EXPERTS * SEG_ALIGN
META_I1, META_I2, META_G1, META_G2, META_R1, META_R2 = range(6)
META_ROWS = 8


def _router_kernel(x_ref, mod_ref, oa_ref, ob_ref, oc_ref, wo_ref, g_ref, wr_ref, br_ref, ltri_ref,
                   x1_ref, h_ref, meta_ref, metat_ref, cnt_ref):
    x = _mixer_residual(x_ref, mod_ref, oa_ref, ob_ref, oc_ref, wo_ref)
    x1_ref[...] = x
    h = _ffn_input(x, mod_ref, g_ref)
    h_ref[...] = h.astype(BF16)
    h_hi, h_lo = _split_bf16(h)
    w_hi, w_lo = _split_bf16(wr_ref[...])
    logits = _dot(h_hi, w_hi) + (_dot(h_lo, w_hi) + _dot(h_hi, w_lo)) + br_ref[...]
    lane = lax.broadcasted_iota(jnp.int32, logits.shape, 1).astype(F32)
    logits = jnp.where(lane < N_EXPERTS, logits, NEG)
    m1 = jnp.max(logits, axis=-1, keepdims=True)
    i1 = jnp.min(jnp.where(logits == m1, lane, float(LANES)), axis=-1, keepdims=True)
    rest = jnp.where(lane == i1, NEG, logits)
    m2 = jnp.max(rest, axis=-1, keepdims=True)
    i2 = jnp.min(jnp.where(rest == m2, lane, float(LANES)), axis=-1, keepdims=True)
    e2 = jnp.exp(m2 - m1)
    den = 1.0 + e2
    sel1 = jnp.where(lane == i1, 1.0, 0.0)
    sel2 = jnp.where(lane == i2, 1.0, 0.0)
    sel = sel1 + sel2
    before = _dot(ltri_ref[...], sel.astype(BF16))
    r1 = jnp.sum(before * sel1, axis=-1, keepdims=True)
    r2 = jnp.sum(before * sel2, axis=-1, keepdims=True)
    cnt_ref[0] = jnp.sum(sel, axis=0, keepdims=True)
    cols = (i1, i2, 1.0 / den, e2 / den, r1, r2)
    meta = jnp.zeros_like(logits)
    for k, col in enumerate(cols):
        meta = jnp.where(lane == k, col, meta)
    meta_ref[...] = meta
    metat_ref[...] = meta.T[0:META_ROWS, :]


def _router_call(x2d, mod, tiles_per_mod, oa, ob, oc, w_out, g, wr, br, ltri):
    n, d = x2d.shape
    tm = MOE_TM
    nt = n // tm
    return pl.pallas_call(
        _router_kernel,
        grid=(nt,),
        in_specs=_mixer_specs(tm, d, tiles_per_mod, oa, ob, oc, w_out) + [
            pl.BlockSpec(g.shape, lambda i: (0, 0)),
            pl.BlockSpec(wr.shape, lambda i: (0, 0)),
            pl.BlockSpec(br.shape, lambda i: (0, 0)),
            pl.BlockSpec(ltri.shape, lambda i: (0, 0)),
        ],
        out_specs=[
            pl.BlockSpec((tm, d), lambda i: (i, 0)),
            pl.BlockSpec((tm, d), lambda i: (i, 0)),
            pl.BlockSpec((tm, LANES), lambda i: (i, 0)),
            pl.BlockSpec((META_ROWS, tm), lambda i: (0, i)),
            pl.BlockSpec((1, 1, LANES), lambda i: (i, 0, 0)),
        ],
        out_shape=[
            jax.ShapeDtypeStruct((n, d), F32),
            jax.ShapeDtypeStruct((n, d), BF16),
            jax.ShapeDtypeStruct((n, LANES), F32),
            jax.ShapeDtypeStruct((META_ROWS, n), F32),
            jax.ShapeDtypeStruct((nt, 1, LANES), F32),
        ],
        compiler_params=_cparams(("parallel",)),
        name="moe_router",
    )(x2d, mod, oa, ob, oc, w_out, g, wr, br, ltri)


def _pair_slots(meta, segoff_row):
    lane = lax.broadcasted_iota(jnp.int32, meta.shape, 1).astype(F32)
    i1 = meta[:, META_I1:META_I1 + 1]
    i2 = meta[:, META_I2:META_I2 + 1]
    s1 = jnp.sum(jnp.where(lane == i1, segoff_row, 0.0), axis=-1, keepdims=True) + meta[:, META_R1:META_R1 + 1]
    s2 = jnp.sum(jnp.where(lane == i2, segoff_row, 0.0), axis=-1, keepdims=True) + meta[:, META_R2:META_R2 + 1]
    return s1, s2


def _segment_copies(src, dst, src_off, dst_off, length, sem):
    out = []
    for size in SEG_PIECES:
        done = (length // (2 * size)) * (2 * size)
        s = pl.multiple_of(src_off + done, SEG_ALIGN)
        t = pl.multiple_of(dst_off + done, SEG_ALIGN)
        cp = pltpu.make_async_copy(src.at[pl.ds(s, size)], dst.at[pl.ds(t, size)], sem)
        out.append(((length & size) != 0, cp))
    return out


def _start_copies(copies):
    for pred, cp in copies:
        @pl.when(pred)
        def _(cp=cp):
            cp.start()


def _wait_copies(copies):
    for pred, cp in copies:
        @pl.when(pred)
        def _(cp=cp):
            cp.wait()


def _compact_kernel(segoff_s, base_s, len_s, fill_s, h_ref, meta_ref, metat_ref, xs_out, gs_out,
                    cbuf2, gbuf2, zx, zg, sems):
    i = pl.program_id(0)
    last = pl.num_programs(0) - 1
    slot_i = i % 2
    cbuf = cbuf2.at[slot_i]
    gbuf = gbuf2.at[slot_i]

    def copies_of(tile, slot):
        out = []
        for e in range(N_EXPERTS):
            k = tile * N_EXPERTS + e
            out += _segment_copies(cbuf2.at[slot], xs_out, segoff_s[k], base_s[k], len_s[k], sems.at[0, slot])
            out += _segment_copies(gbuf2.at[slot], gs_out, segoff_s[k], base_s[k], len_s[k], sems.at[1, slot])
        return out

    @pl.when(i >= 2)
    def _():
        _wait_copies(copies_of(i - 2, slot_i))

    mt = metat_ref[...]
    s1 = mt[META_R1:META_R1 + 1]
    s2 = mt[META_R2:META_R2 + 1]
    for e in range(N_EXPERTS):
        off = segoff_s[i * N_EXPERTS + e].astype(F32)
        s1 = s1 + jnp.where(mt[META_I1:META_I1 + 1] == e, off, 0.0)
        s2 = s2 + jnp.where(mt[META_I2:META_I2 + 1] == e, off, 0.0)
    row = lax.broadcasted_iota(jnp.int32, (CBUF_ROWS, MOE_TM), 0).astype(F32)
    p1 = jnp.where(row == s1, 1.0, 0.0)
    p2 = jnp.where(row == s2, 1.0, 0.0)
    cbuf[...] = _dot((p1 + p2).astype(BF16), h_ref[...]).astype(BF16)
    meta = meta_ref[...]
    lane = lax.broadcasted_iota(jnp.int32, meta.shape, 1)
    g1 = meta[:, META_G1:META_G1 + 1]
    g2 = meta[:, META_G2:META_G2 + 1]
    g1_hi = g1.astype(BF16).astype(F32)
    g2_hi = g2.astype(BF16).astype(F32)
    a1 = jnp.where(lane == 0, g1_hi, jnp.where(lane == 1, g1 - g1_hi, 0.0)).astype(BF16)
    a2 = jnp.where(lane == 0, g2_hi, jnp.where(lane == 1, g2 - g2_hi, 0.0)).astype(BF16)
    gbuf[...] = _dot(p1.astype(BF16), a1) + _dot(p2.astype(BF16), a2)
    _start_copies(copies_of(i, slot_i))

    @pl.when(i == last)
    def _():
        @pl.when(i >= 1)
        def _():
            _wait_copies(copies_of(i - 1, 1 - slot_i))
        _wait_copies(copies_of(i, slot_i))
        zx[...] = jnp.zeros_like(zx)
        zg[...] = jnp.zeros_like(zg)
        fills = []
        for e in range(N_EXPERTS):
            fills += _segment_copies(zx, xs_out, 0, fill_s[e], fill_s[N_EXPERTS + e], sems.at[0, 0])
            fills += _segment_copies(zg, gs_out, 0, fill_s[e], fill_s[N_EXPERTS + e], sems.at[1, 0])
        _start_copies(fills)
        _wait_copies(fills)

        def fill_tile(j, carry):
            r = pl.multiple_of(fill_s[2 * N_EXPERTS] + j * MOE_TM, MOE_TM)
            cx = pltpu.make_async_copy(zx, xs_out.at[pl.ds(r, MOE_TM)], sems.at[0, 0])
            cg = pltpu.make_async_copy(zg, gs_out.at[pl.ds(r, MOE_TM)], sems.at[1, 0])
            cx.start()
            cg.start()
            cx.wait()
            cg.wait()
            return carry

        lax.fori_loop(0, fill_s[2 * N_EXPERTS + 1], fill_tile, 0)


def _compact_call(sched, fill, h, meta, metat, rows):
    n, d = h.shape
    nt = n // MOE_TM
    grid_spec = pltpu.PrefetchScalarGridSpec(
        num_scalar_prefetch=4,
        grid=(nt,),
        in_specs=[
            pl.BlockSpec((MOE_TM, d), lambda i, *_: (i, 0)),
            pl.BlockSpec((MOE_TM, LANES), lambda i, *_: (i, 0)),
            pl.BlockSpec((META_ROWS, MOE_TM), lambda i, *_: (0, i)),
        ],
        out_specs=[pl.BlockSpec(memory_space=pl.ANY), pl.BlockSpec(memory_space=pl.ANY)],
        scratch_shapes=[pltpu.VMEM((2, CBUF_ROWS, d), BF16), pltpu.VMEM((2, CBUF_ROWS, LANES), F32),
                        pltpu.VMEM((MOE_TM, d), BF16), pltpu.VMEM((MOE_TM, LANES), F32),
                        pltpu.SemaphoreType.DMA((2, 2))],
    )
    return pl.pallas_call(
        _compact_kernel,
        grid_spec=grid_spec,
        out_shape=[jax.ShapeDtypeStruct((rows, d), BF16), jax.ShapeDtypeStruct((rows, LANES), F32)],
        compiler_params=_cparams(("arbitrary",)),
        name="moe_compact",
    )(*sched, fill, h, meta, metat)


def _expert_kernel(exp_s, blk_s, valid_s, xs_ref, gs_ref, w1_ref, w3_ref, w2_ref, y_ref):
    del exp_s, blk_s
    j = pl.program_id(0)

    @pl.when(valid_s[j] != 0)
    def _():
        gate = gs_ref[:, 0:1] + gs_ref[:, 1:2]
        y_ref[...] = (_swiglu(xs_ref[...], w1_ref, w3_ref, w2_ref) * gate).astype(BF16)

    @pl.when(valid_s[j] == 0)
    def _():
        y_ref[...] = jnp.zeros_like(y_ref)


def _expert_call(tile_sched, xs, gs, w1, w3, w2):
    rows, d = xs.shape
    ff = w1.shape[1]
    grid_spec = pltpu.PrefetchScalarGridSpec(
        num_scalar_prefetch=3,
        grid=(rows // MOE_TM,),
        in_specs=[
            pl.BlockSpec((MOE_TM, d), lambda j, e_s, b_s, v_s: (b_s[j], 0)),
            pl.BlockSpec((MOE_TM, LANES), lambda j, e_s, b_s, v_s: (b_s[j], 0)),
            pl.BlockSpec((d, ff), lambda j, e_s, b_s, v_s: (e_s[j], 0)),
            pl.BlockSpec((d, ff), lambda j, e_s, b_s, v_s: (e_s[j], 0)),
            pl.BlockSpec((ff, d), lambda j, e_s, b_s, v_s: (e_s[j], 0)),
        ],
        out_specs=pl.BlockSpec((MOE_TM, d), lambda j, e_s, b_s, v_s: (j, 0)),
    )
    return pl.pallas_call(
        _expert_kernel,
        grid_spec=grid_spec,
        out_shape=jax.ShapeDtypeStruct((rows, d), BF16),
        compiler_params=_cparams(("arbitrary",)),
        name="moe_experts",
    )(*tile_sched, xs, gs, w1, w3, w2)


def _combine_kernel(segoff_s, base_s, len_s, x_ref, mod_ref, meta_ref, segoff_ref, y_hbm, o_ref, ybuf2, sems):
    i = pl.program_id(0)
    last = pl.num_programs(0) - 1
    slot_i = i % 2

    def copies_of(tile, slot):
        out = []
        for e in range(N_EXPERTS):
            k = tile * N_EXPERTS + e
            out += _segment_copies(y_hbm, ybuf2.at[slot], base_s[k], segoff_s[k], len_s[k], sems.at[slot])
        return out

    @pl.when(i == 0)
    def _():
        ybuf2[...] = jnp.zeros_like(ybuf2)
        _start_copies(copies_of(i, slot_i))

    @pl.when(i < last)
    def _():
        _start_copies(copies_of(i + 1, 1 - slot_i))

    _wait_copies(copies_of(i, slot_i))
    s1, s2 = _pair_slots(meta_ref[...], segoff_ref[0])
    slot = lax.broadcasted_iota(jnp.int32, (MOE_TM, CBUF_ROWS), 1).astype(F32)
    pick = (jnp.where(slot == s1, 1.0, 0.0) + jnp.where(slot == s2, 1.0, 0.0)).astype(BF16)
    o_ref[...] = x_ref[...] + mod_ref[0, 5:6, :] * _dot(pick, ybuf2[slot_i])


def _combine_call(sched, x2d, mod, tiles_per_mod, meta, segoff_v, y):
    n, d = x2d.shape
    grid_spec = pltpu.PrefetchScalarGridSpec(
        num_scalar_prefetch=3,
        grid=(n // MOE_TM,),
        in_specs=[
            pl.BlockSpec((MOE_TM, d), lambda i, *_: (i, 0)),
            pl.BlockSpec((1, MOD_ROWS, d), lambda i, *_: (i // tiles_per_mod, 0, 0)),
            pl.BlockSpec((MOE_TM, LANES), lambda i, *_: (i, 0)),
            pl.BlockSpec((1, 1, LANES), lambda i, *_: (i, 0, 0)),
            pl.BlockSpec(memory_space=pl.ANY),
        ],
        out_specs=pl.BlockSpec((MOE_TM, d), lambda i, *_: (i, 0)),
        scratch_shapes=[pltpu.VMEM((2, CBUF_ROWS, d), BF16), pltpu.SemaphoreType.DMA((2,))],
    )
    return pl.pallas_call(
        _combine_kernel,
        grid_spec=grid_spec,
        out_shape=jax.ShapeDtypeStruct((n, d), F32),
        compiler_params=_cparams(("arbitrary",)),
        name="moe_combine",
    )(*sched, x2d, mod, meta, segoff_v, y)


def _moe_call(x2d, mod, tiles_per_mod, outs, w_out, g, wr, br, w1, w3, w2):
    n, d = x2d.shape
    nt = n // MOE_TM
    ltri = jnp.asarray(np.tril(np.ones((MOE_TM, MOE_TM), np.float32), -1), BF16)
    x2d, h, meta, metat, counts = _router_call(x2d, mod, tiles_per_mod, *outs, w_out, g, wr, br, ltri)

    cnt = counts[:, 0, :N_EXPERTS].astype(jnp.int32)
    seg_len = (cnt + SEG_ALIGN - 1) // SEG_ALIGN * SEG_ALIGN
    segoff = jnp.cumsum(seg_len, axis=1) - seg_len
    region = (jnp.sum(seg_len, axis=0) + MOE_TM - 1) // MOE_TM * MOE_TM
    region_start = jnp.cumsum(region) - region
    base = region_start[None, :] + jnp.cumsum(seg_len, axis=0) - seg_len
    rows_max = 2 * n + nt * N_EXPERTS * (SEG_ALIGN - 1) + N_EXPERTS * (MOE_TM - 1)
    n_sorted_tiles = (rows_max + MOE_TM - 1) // MOE_TM
    tile_end = jnp.cumsum(region // MOE_TM)
    total_tiles = tile_end[-1]
    jt = jnp.arange(n_sorted_tiles, dtype=jnp.int32)
    blk = jnp.minimum(jt, total_tiles - 1)
    tile_expert = jnp.sum((blk[:, None] >= tile_end[None, :]).astype(jnp.int32), axis=1)
    tile_sched = (tile_expert.astype(jnp.int32), blk.astype(jnp.int32), (jt < total_tiles).astype(jnp.int32))
    sched = tuple(a.reshape(-1).astype(jnp.int32) for a in (segoff, base, seg_len))
    segoff_v = jnp.pad(segoff.astype(F32), ((0, 0), (0, LANES - N_EXPERTS)))[:, None, :]

    total = jnp.sum(seg_len, axis=0)
    fill = jnp.concatenate([region_start + total, region - total,
                            jnp.stack([total_tiles * MOE_TM, n_sorted_tiles - total_tiles])]).astype(jnp.int32)
    xs, gs = _compact_call(sched, fill, h, meta, metat, n_sorted_tiles * MOE_TM)
    y = _expert_call(tile_sched, xs, gs, w1, w3, w2)
    return _combine_call(sched, x2d, mod, tiles_per_mod, meta, segoff_v, y)


def _rope_tables(t):
    rows = jnp.arange(t, dtype=F32) // GRID_W
    cols = jnp.arange(t, dtype=F32) % GRID_W

    def tables(rot_dim):
        a = rot_dim // 2
        inv = 1.0 / (ROPE_BASE ** (jnp.arange(0, a, 2, dtype=F32) / a))
        ar = rows[:, None] * inv
        ac = cols[:, None] * inv
        cos = jnp.concatenate([jnp.cos(ar), jnp.cos(ar), jnp.cos(ac), jnp.cos(ac)], axis=-1)
        sin = jnp.concatenate([-jnp.sin(ar), jnp.sin(ar), -jnp.sin(ac), jnp.sin(ac)], axis=-1)
        return cos, sin

    ca, sa = tables(HEAD_DIM)
    cos_a = jnp.tile(ca, (1, LANES // HEAD_DIM))
    sin_a = jnp.tile(sa, (1, LANES // HEAD_DIM))
    cc, sc = tables(C_ROPE)
    ones = jnp.ones((t, C_NOPE), F32)
    tail = C_SLOT - C_NOPE - C_ROPE
    cos_c = jnp.concatenate([ones, cc, jnp.ones((t, tail), F32)], axis=-1)
    sin_c = jnp.concatenate([0 * ones, sc, jnp.zeros((t, tail), F32)], axis=-1)
    return cos_a, sin_a, cos_c, sin_c


def _head_perm():
    order = []
    for j in range(A_HEADS // 2):
        order += [j, A_HEADS // 2 + j]
    return np.concatenate([np.arange(h * HEAD_DIM, (h + 1) * HEAD_DIM) for h in order])


def _segment_mean_matrix(widths, total):
    m = np.zeros((total, total), np.float32)
    o = 0
    while o < total:
        for w, used in widths:
            if used:
                m[o:o + w, o:o + w] = 1.0 / w
            o += w
    return jnp.asarray(m, BF16)


def _slot_vec(nope, rope):
    z = jnp.zeros((C_SLOT - C_NOPE - C_ROPE,), F32)
    n = jnp.zeros((C_NOPE,), F32) if nope is None else nope
    r = jnp.zeros((C_ROPE,), F32) if rope is None else rope
    return jnp.tile(jnp.concatenate([n, r, z]), C_HEADS)[None, :]


def _layer_consts(i, p, perm, tabs):
    w_in = p["w_in"][i]
    o_kr = A_Q + 2 * A_KV + 2 * B_CH + C_Q_RANK + C_KV_RANK
    d = w_in.shape[0]
    kr_cols = jnp.concatenate([jnp.zeros((d, C_NOPE), F32), w_in[:, o_kr:o_kr + C_ROPE],
                               jnp.zeros((d, C_SLOT - C_NOPE - C_ROPE), F32)], axis=1)
    win = jnp.concatenate([w_in[:, :A_Q][:, perm], w_in[:, A_Q:o_kr], kr_cols], axis=1).astype(BF16)

    w_uq = p["c_w_uq"][i].reshape(C_Q_RANK, C_HEADS, C_NOPE + C_ROPE)
    wuq = jnp.pad(w_uq, ((0, 0), (0, 0), (0, C_SLOT - C_NOPE - C_ROPE))).reshape(C_Q_RANK, C_HEADS * C_SLOT)
    w_ukv = p["c_w_ukv"][i].reshape(C_KV_RANK, C_HEADS, C_NOPE + C_V)
    wukvk = jnp.pad(w_ukv[:, :, :C_NOPE], ((0, 0), (0, 0), (0, C_SLOT - C_NOPE))).reshape(C_KV_RANK, -1)
    wukvv = w_ukv[:, :, C_NOPE:].reshape(C_KV_RANK, C_HEADS * C_V)

    sa = _segment_mean_matrix([(HEAD_DIM, True)], MXU_TILE)
    sc = _segment_mean_matrix([(C_NOPE, True), (C_ROPE, True), (C_SLOT - C_NOPE - C_ROPE, False)], MXU_TILE)
    gq = jnp.tile(p["a_q_norm_g"][i] * (A_SCALE * LOG2E), A_HEADS)[None, :]
    gk = jnp.tile(p["a_k_norm_g"][i], A_KV_HEADS)[None, :]
    gqc = _slot_vec(p["c_q_nope_norm_g"][i], p["c_q_rope_norm_g"][i]) * (MLA_SCALE * LOG2E)
    gkn = _slot_vec(p["c_k_nope_norm_g"][i], None)
    gkr = _slot_vec(None, p["c_k_rope_norm_g"][i])[:, :C_SLOT]
    return (p["mix_norm_g"][i][None, :], win) + tabs + (
        sa, sc, gq, gk, p["c_q_rank_norm_g"][i][None, :], p["c_kv_rank_norm_g"][i][None, :],
        wuq.astype(BF16), wukvk.astype(BF16), wukvv.astype(BF16), gqc, gkn, gkr)


def kernel(x, c, ctx, c_ctx, ada_w, ada_b, mix_norm_g, ffn_norm_g, w_in, w_out, a_q_norm_g, a_k_norm_g, a_sink, b_conv_w, b_conv_b, b_ln_g, b_ln_b, c_q_rank_norm_g, c_kv_rank_norm_g, c_w_uq, c_w_ukv, c_q_nope_norm_g, c_k_nope_norm_g, c_q_rope_norm_g, c_k_rope_norm_g, dense_w1, dense_w3, dense_w2, moe_router_w, moe_router_b, moe_w1, moe_w3, moe_w2):
    p = dict(w_in=w_in, c_w_uq=c_w_uq, c_w_ukv=c_w_ukv, a_q_norm_g=a_q_norm_g, a_k_norm_g=a_k_norm_g,
             c_q_nope_norm_g=c_q_nope_norm_g, c_k_nope_norm_g=c_k_nope_norm_g,
             c_q_rope_norm_g=c_q_rope_norm_g, c_k_rope_norm_g=c_k_rope_norm_g,
             c_q_rank_norm_g=c_q_rank_norm_g, c_kv_rank_norm_g=c_kv_rank_norm_g, mix_norm_g=mix_norm_g)
    bsz, t, d = x.shape
    ctx_len = ctx.shape[1]
    depth = ada_w.shape[0]
    n_x, n_c = bsz * t, bsz * ctx_len
    tm_pre = min(4 * PRE_SUB, t, n_c)
    tm_tok = 512
    tq = min(1024, t)
    assert t % tm_pre == 0 and n_c % tm_pre == 0 and t % tm_tok == 0 and n_c % tm_tok == 0 and t % tq == 0
    assert t % GRID_W == 0 and t >= 3 * BLOCK and ctx_len % BLOCK == 0

    ada_rows = ((bsz + 1 + 7) // 8) * 8
    c_pad = jnp.concatenate([c, c_ctx[None, :], jnp.zeros((ada_rows - bsz - 1, d), F32)], axis=0)
    mods = _ada_call(c_pad, ada_w, ada_b).reshape(depth, ada_rows, 6, d)
    mods = jnp.pad(mods, ((0, 0), (0, 0), (0, MOD_ROWS - 6), (0, 0)))

    tabs_x = _rope_tables(t)
    ones = jnp.ones((tm_pre, LANES), F32)
    tabs_c = (ones, 0 * ones, ones, 0 * ones)
    perm = _head_perm()

    x2 = x.reshape(n_x, d)
    c2 = ctx.reshape(n_c, d)
    for i in range(depth):
        last = i == depth - 1
        mod_x = mods[i, :bsz]
        mod_c = mods[i, bsz:bsz + 1]
        consts_x = _layer_consts(i, p, perm, tabs_x)
        consts_c = _layer_consts(i, p, perm, tabs_c)

        qa_x, ka_x, va_x, u_x, qc_x, kc_x, vc_x = _pre_call(x2, mod_x, t // tm_pre, t // tm_pre, consts_x, tm_pre)
        if last:
            ka_c, va_c, kc_c, vc_c = _pre_call(c2, mod_c, n_c // tm_pre, 1, consts_c, tm_pre, kv_only=True)
        else:
            qa_c, ka_c, va_c, u_c, qc_c, kc_c, vc_c = _pre_call(c2, mod_c, n_c // tm_pre, 1, consts_c, tm_pre)

        def r3(a, length):
            return a.reshape(bsz, length, a.shape[-1])

        sink_row = jnp.repeat(a_sink[i] * LOG2E, BLOCK)[None, :]
        conv_w = jnp.pad(b_conv_w[i], ((0, 32 - B_WIDTH), (0, 0)))
        conv_p = (conv_w, b_conv_b[i][None, :], b_ln_g[i][None, :], b_ln_b[i][None, :])
        w_o = jnp.concatenate([w_out[i][:A_Q][perm], w_out[i][A_Q:]], axis=0).astype(BF16)

        o_a, o_b = _attn_a_call(qa_x, r3(ka_x, t), va_x, r3(ka_c, ctx_len), va_c, sink_row, r3(u_x, t), conv_p, bsz, t)
        o_c = _mla_call(qc_x, r3(kc_x, t), vc_x, r3(kc_c, ctx_len), vc_c, bsz, t, tq)
        if not last:
            oc_a, oc_b = _attn_a_call(qa_c, None, None, r3(ka_c, ctx_len), va_c, sink_row, r3(u_c, ctx_len), conv_p,
                                      bsz, ctx_len)
            oc_c = _mla_call(qc_c, None, None, r3(kc_c, ctx_len), vc_c, bsz, ctx_len, ctx_len)

        j = i // 2
        g_ffn = ffn_norm_g[i][None, :]
        if i % 2 == 0:
            w = (dense_w1[j].astype(BF16), dense_w3[j].astype(BF16), dense_w2[j].astype(BF16))

            def mix(a2, mod, tiles_per_mod, outs, w=w, g_ffn=g_ffn, w_o=w_o):
                return _ffn_call(a2, mod, tiles_per_mod, *outs, w_o, g_ffn, *w, tm_tok)
        else:
            w = tuple(a.astype(BF16).reshape(-1, a.shape[-1]) for a in (moe_w1[j], moe_w3[j], moe_w2[j]))
            wr = jnp.pad(moe_router_w[j], ((0, 0), (0, LANES - N_EXPERTS)))
            br = jnp.pad(moe_router_b[j], (0, LANES - N_EXPERTS))[None, :]

            def mix(a2, mod, tiles_per_mod, outs, w=w, g_ffn=g_ffn, wr=wr, br=br, w_o=w_o):
                return _moe_call(a2, mod, tiles_per_mod * (tm_tok // MOE_TM), outs, w_o, g_ffn, wr, br, *w)
        x2 = mix(x2, mod_x, t // tm_tok, (o_a, o_b, o_c))
        if not last:
            c2 = mix(c2, mod_c, n_c // tm_tok, (oc_a, oc_b, oc_c))
    return x2.reshape(bsz, t, d)
```

```python
import functools

import jax
import jax.numpy as jnp
import numpy as np
from jax import lax
from jax.experimental import pallas as pl
from jax.experimental.pallas import tpu as pltpu

F32 = jnp.float32
BF16 = jnp.bfloat16

D_MODEL = 1024
GRID_W = 64
HEAD_DIM = 64
A_HEADS = 8
A_KV_HEADS = 2
A_WINDOW = 128
BLOCK = 128
B_CH = 256
B_WIDTH = 31
C_HEADS = 4
C_Q_RANK = 384
C_KV_RANK = 256
C_NOPE = 64
C_ROPE = 32
C_V = 64
A_Q = A_HEADS * HEAD_DIM
A_KV = A_KV_HEADS * HEAD_DIM
IN_COLS_PAD = 2048
D_FF = 2816
N_EXPERTS = 8
ROPE_BASE = 10000.0
EPS = 1e-6
NEG = -1e30
A_SCALE = HEAD_DIM ** -0.5
MLA_SCALE = (C_NOPE + C_ROPE) ** -0.5
LOG2E = 1.4426950408889634

LANES = 128
MXU_TILE = 256
MOD_ROWS = 8
C_SLOT = 128
VMEM_LIMIT = 56 * 1024 * 1024


def _cparams(sem):
    return pltpu.CompilerParams(dimension_semantics=sem, vmem_limit_bytes=VMEM_LIMIT)


def _dot(a, b):
    return jnp.dot(a, b, preferred_element_type=F32)


def _dot_nt(a, b):
    return lax.dot_general(a, b, (((1,), (1,)), ((), ())), preferred_element_type=F32)


def _ada_kernel(c_ref, w_ref, b_ref, o_ref):
    c = c_ref[...]
    a = c * jax.nn.sigmoid(c)
    o_ref[0] = _dot(a.astype(BF16), w_ref[0].astype(BF16)) + b_ref[0]


def _ada_call(c_pad, ada_w, ada_b):
    depth, d, n6 = ada_w.shape
    rows = c_pad.shape[0]
    tn = 1536
    return pl.pallas_call(
        _ada_kernel,
        grid=(depth, n6 // tn),
        in_specs=[
            pl.BlockSpec((rows, d), lambda i, j: (0, 0)),
            pl.BlockSpec((1, d, tn), lambda i, j: (i, 0, j)),
            pl.BlockSpec((1, 1, tn), lambda i, j: (i, 0, j)),
        ],
        out_specs=pl.BlockSpec((1, rows, tn), lambda i, j: (i, 0, j)),
        out_shape=jax.ShapeDtypeStruct((depth, rows, n6), F32),
        compiler_params=_cparams(("parallel", "parallel")),
        name="ada_proj",
    )(c_pad, ada_w, ada_b.reshape(depth, 1, n6))


def _rope_chunk(c, cos, sin, half):
    lane = lax.broadcasted_iota(jnp.int32, c.shape, 1)
    lo = (lane & (2 * half - 1)) < half
    partner = jnp.where(lo, pltpu.roll(c, LANES - half, 1), pltpu.roll(c, half, 1))
    return c * cos + partner * sin


PRE_SUB = 256


def _segment_mean(sq, s_ref):
    w = s_ref.shape[0]
    return jnp.concatenate([_dot(sq[:, c:c + w].astype(BF16), s_ref[...]) for c in range(0, sq.shape[1], w)], axis=1)


def _pre_kernel(x_ref, mod_ref, gmix_ref, win_ref, cosa_ref, sina_ref, cosc_ref, sinc_ref,
                sa_ref, sc_ref, gq_ref, gk_ref, gcq_ref, gckv_ref, wuq_ref, wukvk_ref, wukvv_ref,
                gqc_ref, gkn_ref, gkr_ref, *out_refs, kv_only):
    if kv_only:
        ka_ref, va_ref, kc_ref, vc_ref = out_refs
        o_ka, o_ckv = 0, 2 * A_KV
    else:
        qa_ref, ka_ref, va_ref, u_ref, qc_ref, kc_ref, vc_ref = out_refs
        o_ka, o_ckv = A_Q, A_Q + 2 * A_KV + 2 * B_CH + C_Q_RANK
    for sub in range(x_ref.shape[0] // PRE_SUB):
        rows = slice(sub * PRE_SUB, (sub + 1) * PRE_SUB)
        x = x_ref[rows, :]
        ms = jnp.mean(x * x, axis=-1, keepdims=True)
        y = x * lax.rsqrt(ms + EPS)
        shift = mod_ref[0, 0:1, :]
        scale = mod_ref[0, 1:2, :]
        h = (y * gmix_ref[...]) * (1.0 + scale) + shift
        p = _dot(h.astype(BF16), win_ref[...])

        cosa, sina = cosa_ref[rows, :], sina_ref[rows, :]
        cosc, sinc = cosc_ref[rows, :], sinc_ref[rows, :]

        if not kv_only:
            qa = p[:, 0:A_Q]
            ssq = _segment_mean(qa * qa, sa_ref)
            qa = qa * lax.rsqrt(ssq + EPS) * gq_ref[...]
            for j in range(A_Q // LANES):
                sl = slice(j * LANES, (j + 1) * LANES)
                qa_ref[rows, sl] = _rope_chunk(qa[:, sl], cosa, sina, HEAD_DIM // 4).astype(BF16)

            o = A_Q + 2 * A_KV
            u_ref[rows, :] = p[:, o:o + B_CH] * jax.nn.sigmoid(p[:, o + B_CH:o + 2 * B_CH])

            o = o + 2 * B_CH
            cq = p[:, o:o + C_Q_RANK]
            cq = cq * lax.rsqrt(jnp.mean(cq * cq, axis=-1, keepdims=True) + EPS) * gcq_ref[...]
            qc = _dot(cq.astype(BF16), wuq_ref[...])
            ssq = _segment_mean(qc * qc, sc_ref)
            qc = qc * lax.rsqrt(ssq + EPS) * gqc_ref[...]
            for j in range(C_HEADS):
                sl = slice(j * C_SLOT, (j + 1) * C_SLOT)
                qc_ref[rows, sl] = _rope_chunk(qc[:, sl], cosc, sinc, C_ROPE // 4).astype(BF16)

        ka = p[:, o_ka:o_ka + A_KV]
        ssk = _dot((ka * ka).astype(BF16), sa_ref[0:A_KV, 0:A_KV])
        ka = ka * lax.rsqrt(ssk + EPS) * gk_ref[...]
        ka_ref[rows, :] = _rope_chunk(ka, cosa, sina, HEAD_DIM // 4).astype(BF16)
        va_ref[:, rows] = p[:, o_ka + A_KV:o_ka + 2 * A_KV].T.astype(BF16)

        o = o_ckv
        ckv = p[:, o:o + C_KV_RANK]
        ckv = (ckv * lax.rsqrt(jnp.mean(ckv * ckv, axis=-1, keepdims=True) + EPS) * gckv_ref[...]).astype(BF16)
        kn = _dot(ckv, wukvk_ref[...])
        vc_ref[:, rows] = _dot(ckv, wukvv_ref[...]).T.astype(BF16)
        ssk = _segment_mean(kn * kn, sc_ref)
        kn = kn * lax.rsqrt(ssk + EPS) * gkn_ref[...]
        o = o + C_KV_RANK
        kr = p[:, o:o + C_SLOT]
        kr = kr * lax.rsqrt(jnp.sum(kr * kr, axis=-1, keepdims=True) * (1.0 / C_ROPE) + EPS) * gkr_ref[...]
        kr = _rope_chunk(kr, cosc, sinc, C_ROPE // 4)
        for j in range(C_HEADS):
            sl = slice(j * C_SLOT, (j + 1) * C_SLOT)
            kc_ref[rows, sl] = (kn[:, sl] + kr).astype(BF16)


def _pre_call(x2d, mod, tiles_per_mod, tab_tiles, consts, tm, kv_only=False):
    n, d = x2d.shape
    (gmix, win, cosa, sina, cosc, sinc, sa, sc, gq, gk, gcq, gckv, wuq, wukvk, wukvv, gqc, gkn, gkr) = consts
    if kv_only:
        o_ckv = A_Q + 2 * A_KV + 2 * B_CH + C_Q_RANK
        win = jnp.concatenate([win[:, A_Q:A_Q + 2 * A_KV], win[:, o_ckv:]], axis=1)

    def const(a):
        return pl.BlockSpec(a.shape, lambda i: (0,) * a.ndim)

    def tab(a):
        return pl.BlockSpec((tm, LANES), lambda i: (i % tab_tiles, 0))

    in_specs = [
        pl.BlockSpec((tm, d), lambda i: (i, 0)),
        pl.BlockSpec((1, MOD_ROWS, d), lambda i: (i // tiles_per_mod, 0, 0)),
        const(gmix), const(win), tab(cosa), tab(sina), tab(cosc), tab(sinc),
        const(sa), const(sc), const(gq), const(gk), const(gcq), const(gckv),
        const(wuq), const(wukvk), const(wukvv), const(gqc), const(gkn), const(gkr),
    ]
    widths = (A_Q, A_KV, A_KV, B_CH, C_HEADS * C_SLOT, C_HEADS * C_SLOT, C_HEADS * C_V)
    dtypes = (BF16, BF16, BF16, F32, BF16, BF16, BF16)
    out_specs = [pl.BlockSpec((tm, w), lambda i: (i, 0)) for w in widths]
    out_shape = [jax.ShapeDtypeStruct((n, w), dt) for w, dt in zip(widths, dtypes)]
    out_specs[2] = pl.BlockSpec((A_KV, tm), lambda i: (0, i))
    out_shape[2] = jax.ShapeDtypeStruct((A_KV, n), BF16)
    out_specs[6] = pl.BlockSpec((C_HEADS * C_V, tm), lambda i: (0, i))
    out_shape[6] = jax.ShapeDtypeStruct((C_HEADS * C_V, n), BF16)
    if kv_only:
        keep = (1, 2, 5, 6)
        out_specs = [out_specs[k] for k in keep]
        out_shape = [out_shape[k] for k in keep]
    return pl.pallas_call(
        functools.partial(_pre_kernel, kv_only=kv_only),
        grid=(n // tm,),
        in_specs=in_specs,
        out_specs=out_specs,
        out_shape=out_shape,
        compiler_params=_cparams(("parallel",)),
        name="pre_attn",
    )(x2d, mod, gmix, win, cosa, sina, cosc, sinc, sa, sc, gq, gk, gcq, gckv, wuq, wukvk, wukvv,
      gqc, gkn, gkr)


CONV_PAD = 16
CONV_CHUNK = 128
SUBLANES = 8


def _conv_chunk(pad_ref, c, w_ref, b_ref, g_ref, beta_ref):
    off = CONV_PAD - B_WIDTH // 2
    nq = (off + B_WIDTH - 1) // SUBLANES + 1
    win = CONV_CHUNK + (nq - 1) * SUBLANES
    base = c * CONV_CHUNK
    acc = jnp.zeros((CONV_CHUNK, B_CH), F32)
    for r in range(SUBLANES):
        taps = [k for k in range(B_WIDTH) if (off + k) % SUBLANES == r]
        if not taps:
            continue
        w_r = pad_ref[base + r:base + r + win, :]
        part = None
        for k in taps:
            q = (off + k) // SUBLANES
            term = w_r[q * SUBLANES:q * SUBLANES + CONV_CHUNK, :] * w_ref[k:k + 1, :]
            part = term if part is None else part + term
        acc = acc + part
    y = acc + b_ref[...]
    mu = jnp.mean(y, axis=-1, keepdims=True)
    yc = y - mu
    var = jnp.mean(yc * yc, axis=-1, keepdims=True)
    z = yc * lax.rsqrt(var + EPS) * g_ref[...] + beta_ref[...]
    return (z * jax.nn.sigmoid(z)).astype(BF16)


ATTN_A_QBLOCKS = 16


ONES_ROWS = 16


def _attn_a_kernel(*refs, t, has_local, qblocks):
    u_ref, uprev_ref, unext_ref, cw_ref, cb_ref, cg_ref, cbeta_ref, o_ref, ob_ref, pad_ref = refs[-10:]
    if has_local:
        q_ref, k_ref, vt_ref, kc_ref, vct_ref, sink_ref, bias_ref = refs[:-10]
    else:
        q_ref, kc_ref, vct_ref, sink_ref = refs[:-10]
    step = pl.program_id(1)
    tq = q_ref.shape[0]
    pad_ref[0:CONV_PAD, :] = jnp.where(step > 0, uprev_ref[0], 0.0)
    pad_ref[CONV_PAD:CONV_PAD + tq, :] = u_ref[0]
    pad_ref[CONV_PAD + tq:CONV_PAD + tq + CONV_PAD, :] = jnp.where(step < pl.num_programs(1) - 1, unext_ref[0], 0.0)
    nchunk = A_Q // LANES
    span = 3 * BLOCK
    lane = lax.broadcasted_iota(jnp.int32, (BLOCK, LANES), 1)
    row = lax.broadcasted_iota(jnp.int32, (A_KV, BLOCK), 0)
    zero = jnp.zeros((BLOCK, LANES), BF16)
    sink = sink_ref[...]
    kc = kc_ref[0]
    vct = vct_ref[...]
    for blk in range(qblocks):
        rows = slice(blk * BLOCK, (blk + 1) * BLOCK)
        chunks = [q_ref[rows, j * LANES:(j + 1) * LANES] for j in range(nchunk)]
        qs = jnp.concatenate([jnp.where(lane < HEAD_DIM, c, zero) for c in chunks]
                             + [jnp.where(lane >= HEAD_DIM, c, zero) for c in chunks], axis=0)
        if has_local:
            n = pl.program_id(1) * qblocks + blk
            start = pl.multiple_of(jnp.clip((n - 1) * BLOCK, 0, t - span), BLOCK)
            keys = jnp.concatenate([k_ref[0, pl.ds(start, span), :], kc], axis=0)
            vt = jnp.concatenate([vt_ref[:, pl.ds(start, span)], vct], axis=1)
        else:
            keys, vt = kc, vct
        s = _dot_nt(keys, qs)
        if has_local:
            s = jnp.concatenate([s[:span] + bias_ref[n - start // BLOCK], s[span:]], axis=0)
        m = jnp.maximum(jnp.max(s, axis=0, keepdims=True), sink)
        e = jnp.exp2(s - m).astype(BF16)
        vte = jnp.concatenate([vt, jnp.ones((ONES_ROWS, vt.shape[1]), BF16)], axis=0)
        acc = _dot(vte, e)
        den = acc[A_KV:A_KV + 1] + jnp.exp2(sink - m)
        out = acc[:A_KV] * (1.0 / den)
        for j in range(nchunk):
            x = jnp.where(row < HEAD_DIM, out[:, j * BLOCK:(j + 1) * BLOCK],
                          out[:, (nchunk + j) * BLOCK:(nchunk + j + 1) * BLOCK])
            o_ref[rows, j * LANES:(j + 1) * LANES] = x.T.astype(BF16)
        ob_ref[rows, :] = _conv_chunk(pad_ref, blk, cw_ref, cb_ref, cg_ref, cbeta_ref)


def _window_bias():
    r = np.arange(BLOCK)[None, :]
    c = np.arange(3 * BLOCK)[:, None]
    pats = [np.where(np.abs(c - p * BLOCK - r) <= A_WINDOW, 0.0, NEG) for p in range(3)]
    return jnp.asarray(np.stack([np.tile(p, (1, A_HEADS)) for p in pats]), F32)


def _attn_a_call(qa, ka, vat, kac, vact, sink_row, u3, conv_p, bsz, t):
    has_local = ka is not None
    qblocks = min(ATTN_A_QBLOCKS, t // BLOCK)
    tq = qblocks * BLOCK
    nq = t // tq
    ctx_len = kac.shape[1]
    in_specs = [pl.BlockSpec((tq, A_Q), lambda b, n: (b * nq + n, 0))]
    args = [qa]
    if has_local:
        in_specs += [pl.BlockSpec((1, t, A_KV), lambda b, n: (b, 0, 0)),
                     pl.BlockSpec((A_KV, t), lambda b, n: (0, b))]
        args += [ka, vat]
    in_specs += [pl.BlockSpec((1, ctx_len, A_KV), lambda b, n: (b, 0, 0)),
                 pl.BlockSpec((A_KV, ctx_len), lambda b, n: (0, b))]
    in_specs += [pl.BlockSpec(sink_row.shape, lambda b, n: (0, 0))]
    args += [kac, vact, sink_row]
    if has_local:
        bias = _window_bias()
        in_specs += [pl.BlockSpec(bias.shape, lambda b, n: (0, 0, 0))]
        args += [bias]
    halo_per_step = tq // CONV_PAD
    n_halo = t // CONV_PAD
    in_specs += [
        pl.BlockSpec((1, tq, B_CH), lambda b, n: (b, n, 0)),
        pl.BlockSpec((1, CONV_PAD, B_CH), lambda b, n: (b, jnp.maximum(n * halo_per_step - 1, 0), 0)),
        pl.BlockSpec((1, CONV_PAD, B_CH), lambda b, n: (b, jnp.minimum((n + 1) * halo_per_step, n_halo - 1), 0)),
    ] + [pl.BlockSpec(a.shape, lambda b, n: (0, 0)) for a in conv_p]
    args += [u3, u3, u3, *conv_p]
    return pl.pallas_call(
        functools.partial(_attn_a_kernel, t=t, has_local=has_local, qblocks=qblocks),
        grid=(bsz, nq),
        in_specs=in_specs,
        out_specs=[pl.BlockSpec((tq, A_Q), lambda b, n: (b * nq + n, 0)),
                   pl.BlockSpec((tq, B_CH), lambda b, n: (b * nq + n, 0))],
        out_shape=[jax.ShapeDtypeStruct((bsz * t, A_Q), BF16), jax.ShapeDtypeStruct((bsz * t, B_CH), BF16)],
        scratch_shapes=[pltpu.VMEM((tq + 2 * CONV_PAD, B_CH), F32)],
        compiler_params=_cparams(("parallel", "parallel")),
        name="attn_ab_local" if has_local else "attn_ab_ctx",
    )(*args)


def _mla_kernel(*refs, has_local):
    if has_local:
        q_ref, kx_ref, vxt_ref, kc_ref, vct_ref, o_ref = refs
    else:
        q_ref, kc_ref, vct_ref, o_ref = refs
    def scores(h):
        sl = slice(h * C_SLOT, (h + 1) * C_SLOT)
        q = q_ref[:, sl]
        s_c = _dot_nt(kc_ref[0, :, sl], q)
        s_x = _dot_nt(kx_ref[0, :, sl], q) if has_local else None
        return s_c, s_x

    outs = []
    nxt = scores(0)
    for h in range(C_HEADS):
        vs = slice(h * C_V, (h + 1) * C_V)
        s_c, s_x = nxt
        if h + 1 < C_HEADS:
            nxt = scores(h + 1)
        m = jnp.max(s_c, axis=0, keepdims=True)
        if has_local:
            m = jnp.maximum(m, jnp.max(s_x, axis=0, keepdims=True))
        e_c = jnp.exp2(s_c - m).astype(BF16)
        vt = jnp.concatenate([vct_ref[vs, :], jnp.ones((ONES_ROWS, e_c.shape[0]), BF16)], axis=0)
        acc = _dot(vt, e_c)
        if has_local:
            e_x = jnp.exp2(s_x - m).astype(BF16)
            vt = jnp.concatenate([vxt_ref[vs, :], jnp.ones((ONES_ROWS, e_x.shape[0]), BF16)], axis=0)
            acc = acc + _dot(vt, e_x)
        outs.append(acc[:C_V] * (1.0 / acc[C_V:C_V + 1]))
    o_ref[...] = jnp.concatenate(outs, axis=0).T.astype(BF16)


def _mla_call(qc, kx, vxt, kcc, vcct, bsz, t, tq):
    has_local = kx is not None
    nq = t // tq
    ctx_len = kcc.shape[1]
    wq = C_HEADS * C_SLOT
    wv = C_HEADS * C_V
    in_specs = [pl.BlockSpec((tq, wq), lambda b, n: (b * nq + n, 0))]
    args = [qc]
    if has_local:
        in_specs += [pl.BlockSpec((1, t, wq), lambda b, n: (b, 0, 0)),
                     pl.BlockSpec((wv, t), lambda b, n: (0, b))]
        args += [kx, vxt]
    in_specs += [pl.BlockSpec((1, ctx_len, wq), lambda b, n: (b, 0, 0)),
                 pl.BlockSpec((wv, ctx_len), lambda b, n: (0, b))]
    args += [kcc, vcct]
    return pl.pallas_call(
        functools.partial(_mla_kernel, has_local=has_local),
        grid=(bsz, nq),
        in_specs=in_specs,
        out_specs=pl.BlockSpec((tq, wv), lambda b, n: (b * nq + n, 0)),
        out_shape=jax.ShapeDtypeStruct((bsz * t, wv), BF16),
        compiler_params=_cparams(("parallel", "parallel")),
        name="mla_local" if has_local else "mla_ctx",
    )(*args)


def _mixer_residual(x_ref, mod_ref, oa_ref, ob_ref, oc_ref, wo_ref):
    y = _dot(oa_ref[...], wo_ref[0:A_Q, :])
    y = y + _dot(ob_ref[...], wo_ref[A_Q:A_Q + B_CH, :])
    y = y + _dot(oc_ref[...], wo_ref[A_Q + B_CH:, :])
    return x_ref[...] + mod_ref[0, 2:3, :] * y


def _mixer_specs(tm, d, tiles_per_mod, oa, ob, oc, w_out):
    return [
        pl.BlockSpec((tm, d), lambda i: (i, 0)),
        pl.BlockSpec((1, MOD_ROWS, d), lambda i: (i // tiles_per_mod, 0, 0)),
        pl.BlockSpec((tm, oa.shape[1]), lambda i: (i, 0)),
        pl.BlockSpec((tm, ob.shape[1]), lambda i: (i, 0)),
        pl.BlockSpec((tm, oc.shape[1]), lambda i: (i, 0)),
        pl.BlockSpec(w_out.shape, lambda i: (0, 0), pipeline_mode=pl.Buffered(1)),
    ]


def _split_bf16(a):
    hi = a.astype(BF16)
    lo = (a - hi.astype(F32)).astype(BF16)
    return hi, lo


def _ffn_input(x, mod_ref, g_ref):
    ms = jnp.mean(x * x, axis=-1, keepdims=True)
    y = x * lax.rsqrt(ms + EPS)
    return (y * g_ref[...]) * (1.0 + mod_ref[0, 4:5, :]) + mod_ref[0, 3:4, :]


def _ff_chunks(ff):
    tiles = ff // MXU_TILE
    if ff % MXU_TILE or tiles < 2:
        return (ff,)
    first = (tiles // 2) * MXU_TILE
    return (first, ff - first)


def _swiglu(h, w1_ref, w3_ref, w2_ref):
    y = None
    o = 0
    for tf in _ff_chunks(w1_ref.shape[1]):
        a = _dot(h, w1_ref[:, o:o + tf])
        b = _dot(h, w3_ref[:, o:o + tf])
        g = (a * jax.nn.sigmoid(a) * b).astype(BF16)
        yc = _dot(g, w2_ref[o:o + tf, :])
        y = yc if y is None else y + yc
        o += tf
    return y


def _ffn_kernel(x_ref, mod_ref, oa_ref, ob_ref, oc_ref, wo_ref, g_ref, w1_ref, w3_ref, w2_ref, o_ref):
    x = _mixer_residual(x_ref, mod_ref, oa_ref, ob_ref, oc_ref, wo_ref)
    h = _ffn_input(x, mod_ref, g_ref).astype(BF16)
    o_ref[...] = x + mod_ref[0, 5:6, :] * _swiglu(h, w1_ref, w3_ref, w2_ref)


def _ffn_call(x2d, mod, tiles_per_mod, oa, ob, oc, w_out, g, w1, w3, w2, tm):
    n, d = x2d.shape

    def resident(a):
        return pl.BlockSpec(a.shape, lambda i: (0, 0), pipeline_mode=pl.Buffered(1))

    return pl.pallas_call(
        _ffn_kernel,
        grid=(n // tm,),
        in_specs=_mixer_specs(tm, d, tiles_per_mod, oa, ob, oc, w_out) + [
            pl.BlockSpec(g.shape, lambda i: (0, 0)), resident(w1), resident(w3), resident(w2)],
        out_specs=pl.BlockSpec((tm, d), lambda i: (i, 0)),
        out_shape=jax.ShapeDtypeStruct((n, d), F32),
        compiler_params=_cparams(("parallel",)),
        name="ffn_dense",
    )(x2d, mod, oa, ob, oc, w_out, g, w1, w3, w2)


MOE_TM = 512
SEG_ALIGN = 16
SEG_PIECES = (512, 256, 128, 64, 32, 16)
CBUF_ROWS = 2 * MOE_TM + N_EXPERTS * SEG_ALIGN
META_I1, META_I2, META_G1, META_G2, META_R1, META_R2 = range(6)
META_ROWS = 8


def _router_kernel(x_ref, mod_ref, oa_ref, ob_ref, oc_ref, wo_ref, g_ref, wr_ref, br_ref, ltri_ref,
                   x1_ref, h_ref, meta_ref, metat_ref, cnt_ref):
    x = _mixer_residual(x_ref, mod_ref, oa_ref, ob_ref, oc_ref, wo_ref)
    x1_ref[...] = x
    h = _ffn_input(x, mod_ref, g_ref)
    h_ref[...] = h.astype(BF16)
    h_hi, h_lo = _split_bf16(h)
    w_hi, w_lo = _split_bf16(wr_ref[...])
    logits = _dot(h_hi, w_hi) + (_dot(h_lo, w_hi) + _dot(h_hi, w_lo)) + br_ref[...]
    lane = lax.broadcasted_iota(jnp.int32, logits.shape, 1).astype(F32)
    logits = jnp.where(lane < N_EXPERTS, logits, NEG)
    m1 = jnp.max(logits, axis=-1, keepdims=True)
    i1 = jnp.min(jnp.where(logits == m1, lane, float(LANES)), axis=-1, keepdims=True)
    rest = jnp.where(lane == i1, NEG, logits)
    m2 = jnp.max(rest, axis=-1, keepdims=True)
    i2 = jnp.min(jnp.where(rest == m2, lane, float(LANES)), axis=-1, keepdims=True)
    e2 = jnp.exp(m2 - m1)
    den = 1.0 + e2
    sel1 = jnp.where(lane == i1, 1.0, 0.0)
    sel2 = jnp.where(lane == i2, 1.0, 0.0)
    sel = sel1 + sel2
    before = _dot(ltri_ref[...], sel.astype(BF16))
    r1 = jnp.sum(before * sel1, axis=-1, keepdims=True)
    r2 = jnp.sum(before * sel2, axis=-1, keepdims=True)
    cnt_ref[0] = jnp.sum(sel, axis=0, keepdims=True)
    cols = (i1, i2, 1.0 / den, e2 / den, r1, r2)
    meta = jnp.zeros_like(logits)
    for k, col in enumerate(cols):
        meta = jnp.where(lane == k, col, meta)
    meta_ref[...] = meta
    metat_ref[...] = meta.T[0:META_ROWS, :]


def _router_call(x2d, mod, tiles_per_mod, oa, ob, oc, w_out, g, wr, br, ltri):
    n, d = x2d.shape
    tm = MOE_TM
    nt = n // tm
    return pl.pallas_call(
        _router_kernel,
        grid=(nt,),
        in_specs=_mixer_specs(tm, d, tiles_per_mod, oa, ob, oc, w_out) + [
            pl.BlockSpec(g.shape, lambda i: (0, 0)),
            pl.BlockSpec(wr.shape, lambda i: (0, 0)),
            pl.BlockSpec(br.shape, lambda i: (0, 0)),
            pl.BlockSpec(ltri.shape, lambda i: (0, 0)),
        ],
        out_specs=[
            pl.BlockSpec((tm, d), lambda i: (i, 0)),
            pl.BlockSpec((tm, d), lambda i: (i, 0)),
            pl.BlockSpec((tm, LANES), lambda i: (i, 0)),
            pl.BlockSpec((META_ROWS, tm), lambda i: (0, i)),
            pl.BlockSpec((1, 1, LANES), lambda i: (i, 0, 0)),
        ],
        out_shape=[
            jax.ShapeDtypeStruct((n, d), F32),
            jax.ShapeDtypeStruct((n, d), BF16),
            jax.ShapeDtypeStruct((n, LANES), F32),
            jax.ShapeDtypeStruct((META_ROWS, n), F32),
            jax.ShapeDtypeStruct((nt, 1, LANES), F32),
        ],
        compiler_params=_cparams(("parallel",)),
        name="moe_router",
    )(x2d, mod, oa, ob, oc, w_out, g, wr, br, ltri)


def _pair_slots(meta, segoff_row):
    lane = lax.broadcasted_iota(jnp.int32, meta.shape, 1).astype(F32)
    i1 = meta[:, META_I1:META_I1 + 1]
    i2 = meta[:, META_I2:META_I2 + 1]
    s1 = jnp.sum(jnp.where(lane == i1, segoff_row, 0.0), axis=-1, keepdims=True) + meta[:, META_R1:META_R1 + 1]
    s2 = jnp.sum(jnp.where(lane == i2, segoff_row, 0.0), axis=-1, keepdims=True) + meta[:, META_R2:META_R2 + 1]
    return s1, s2


def _segment_copies(src, dst, src_off, dst_off, length, sem):
    out = []
    for size in SEG_PIECES:
        done = (length // (2 * size)) * (2 * size)
        s = pl.multiple_of(src_off + done, SEG_ALIGN)
        t = pl.multiple_of(dst_off + done, SEG_ALIGN)
        cp = pltpu.make_async_copy(src.at[pl.ds(s, size)], dst.at[pl.ds(t, size)], sem)
        out.append(((length & size) != 0, cp))
    return out


def _start_copies(copies):
    for pred, cp in copies:
        @pl.when(pred)
        def _(cp=cp):
            cp.start()


def _wait_copies(copies):
    for pred, cp in copies:
        @pl.when(pred)
        def _(cp=cp):
            cp.wait()


def _compact_kernel(segoff_s, base_s, len_s, fill_s, h_ref, meta_ref, metat_ref, xs_out, gs_out,
                    cbuf2, gbuf2, zx, zg, sems):
    i = pl.program_id(0)
    last = pl.num_programs(0) - 1
    slot_i = i % 2
    cbuf = cbuf2.at[slot_i]
    gbuf = gbuf2.at[slot_i]

    def copies_of(tile, slot):
        out = []
        for e in range(N_EXPERTS):
            k = tile * N_EXPERTS + e
            out += _segment_copies(cbuf2.at[slot], xs_out, segoff_s[k], base_s[k], len_s[k], sems.at[0, slot])
            out += _segment_copies(gbuf2.at[slot], gs_out, segoff_s[k], base_s[k], len_s[k], sems.at[1, slot])
        return out

    @pl.when(i >= 2)
    def _():
        _wait_copies(copies_of(i - 2, slot_i))

    mt = metat_ref[...]
    s1 = mt[META_R1:META_R1 + 1]
    s2 = mt[META_R2:META_R2 + 1]
    for e in range(N_EXPERTS):
        off = segoff_s[i * N_EXPERTS + e].astype(F32)
        s1 = s1 + jnp.where(mt[META_I1:META_I1 + 1] == e, off, 0.0)
        s2 = s2 + jnp.where(mt[META_I2:META_I2 + 1] == e, off, 0.0)
    row = lax.broadcasted_iota(jnp.int32, (CBUF_ROWS, MOE_TM), 0).astype(F32)
    p1 = jnp.where(row == s1, 1.0, 0.0)
    p2 = jnp.where(row == s2, 1.0, 0.0)
    cbuf[...] = _dot((p1 + p2).astype(BF16), h_ref[...]).astype(BF16)
    meta = meta_ref[...]
    lane = lax.broadcasted_iota(jnp.int32, meta.shape, 1)
    g1 = meta[:, META_G1:META_G1 + 1]
    g2 = meta[:, META_G2:META_G2 + 1]
    g1_hi = g1.astype(BF16).astype(F32)
    g2_hi = g2.astype(BF16).astype(F32)
    a1 = jnp.where(lane == 0, g1_hi, jnp.where(lane == 1, g1 - g1_hi, 0.0)).astype(BF16)
    a2 = jnp.where(lane == 0, g2_hi, jnp.where(lane == 1, g2 - g2_hi, 0.0)).astype(BF16)
    gbuf[...] = _dot(p1.astype(BF16), a1) + _dot(p2.astype(BF16), a2)
    _start_copies(copies_of(i, slot_i))

    @pl.when(i == last)
    def _():
        @pl.when(i >= 1)
        def _():
            _wait_copies(copies_of(i - 1, 1 - slot_i))
        _wait_copies(copies_of(i, slot_i))
        zx[...] = jnp.zeros_like(zx)
        zg[...] = jnp.zeros_like(zg)
        fills = []
        for e in range(N_EXPERTS):
            fills += _segment_copies(zx, xs_out, 0, fill_s[e], fill_s[N_EXPERTS + e], sems.at[0, 0])
            fills += _segment_copies(zg, gs_out, 0, fill_s[e], fill_s[N_EXPERTS + e], sems.at[1, 0])
        _start_copies(fills)
        _wait_copies(fills)

        def fill_copies(j):
            r = pl.multiple_of(fill_s[2 * N_EXPERTS] + j * MOE_TM, MOE_TM)
            return (pltpu.make_async_copy(zx, xs_out.at[pl.ds(r, MOE_TM)], sems.at[0, 0]),
                    pltpu.make_async_copy(zg, gs_out.at[pl.ds(r, MOE_TM)], sems.at[1, 0]))

        def start_tile(j, carry):
            for cp in fill_copies(j):
                cp.start()
            return carry

        def wait_tile(j, carry):
            for cp in fill_copies(j):
                cp.wait()
            return carry

        lax.fori_loop(0, fill_s[2 * N_EXPERTS + 1], start_tile, 0)
        lax.fori_loop(0, fill_s[2 * N_EXPERTS + 1], wait_tile, 0)


def _compact_call(sched, fill, h, meta, metat, rows):
    n, d = h.shape
    nt = n // MOE_TM
    grid_spec = pltpu.PrefetchScalarGridSpec(
        num_scalar_prefetch=4,
        grid=(nt,),
        in_specs=[
            pl.BlockSpec((MOE_TM, d), lambda i, *_: (i, 0)),
            pl.BlockSpec((MOE_TM, LANES), lambda i, *_: (i, 0)),
            pl.BlockSpec((META_ROWS, MOE_TM), lambda i, *_: (0, i)),
        ],
        out_specs=[pl.BlockSpec(memory_space=pl.ANY), pl.BlockSpec(memory_space=pl.ANY)],
        scratch_shapes=[pltpu.VMEM((2, CBUF_ROWS, d), BF16), pltpu.VMEM((2, CBUF_ROWS, LANES), F32),
                        pltpu.VMEM((MOE_TM, d), BF16), pltpu.VMEM((MOE_TM, LANES), F32),
                        pltpu.SemaphoreType.DMA((2, 2))],
    )
    return pl.pallas_call(
        _compact_kernel,
        grid_spec=grid_spec,
        out_shape=[jax.ShapeDtypeStruct((rows, d), BF16), jax.ShapeDtypeStruct((rows, LANES), F32)],
        compiler_params=_cparams(("arbitrary",)),
        name="moe_compact",
    )(*sched, fill, h, meta, metat)


def _expert_kernel(exp_s, blk_s, valid_s, xs_ref, gs_ref, w1_ref, w3_ref, w2_ref, y_ref):
    del exp_s, blk_s
    j = pl.program_id(0)

    @pl.when(valid_s[j] != 0)
    def _():
        gate = gs_ref[:, 0:1] + gs_ref[:, 1:2]
        y_ref[...] = (_swiglu(xs_ref[...], w1_ref, w3_ref, w2_ref) * gate).astype(BF16)

    @pl.when(valid_s[j] == 0)
    def _():
        y_ref[...] = jnp.zeros_like(y_ref)


def _expert_call(tile_sched, xs, gs, w1, w3, w2):
    rows, d = xs.shape
    ff = w1.shape[1]
    grid_spec = pltpu.PrefetchScalarGridSpec(
        num_scalar_prefetch=3,
        grid=(rows // MOE_TM,),
        in_specs=[
            pl.BlockSpec((MOE_TM, d), lambda j, e_s, b_s, v_s: (b_s[j], 0)),
            pl.BlockSpec((MOE_TM, LANES), lambda j, e_s, b_s, v_s: (b_s[j], 0)),
            pl.BlockSpec((d, ff), lambda j, e_s, b_s, v_s: (e_s[j], 0)),
            pl.BlockSpec((d, ff), lambda j, e_s, b_s, v_s: (e_s[j], 0)),
            pl.BlockSpec((ff, d), lambda j, e_s, b_s, v_s: (e_s[j], 0)),
        ],
        out_specs=pl.BlockSpec((MOE_TM, d), lambda j, e_s, b_s, v_s: (j, 0)),
    )
    return pl.pallas_call(
        _expert_kernel,
        grid_spec=grid_spec,
        out_shape=jax.ShapeDtypeStruct((rows, d), BF16),
        compiler_params=_cparams(("arbitrary",)),
        name="moe_experts",
    )(*tile_sched, xs, gs, w1, w3, w2)


def _combine_kernel(segoff_s, base_s, len_s, x_ref, mod_ref, meta_ref, segoff_ref, y_hbm, o_ref, ybuf2, sems):
    i = pl.program_id(0)
    last = pl.num_programs(0) - 1
    slot_i = i % 2

    def copies_of(tile, slot):
        out = []
        for e in range(N_EXPERTS):
            k = tile * N_EXPERTS + e
            out += _segment_copies(y_hbm, ybuf2.at[slot], base_s[k], segoff_s[k], len_s[k], sems.at[slot])
        return out

    @pl.when(i == 0)
    def _():
        ybuf2[...] = jnp.zeros_like(ybuf2)
        _start_copies(copies_of(i, slot_i))

    @pl.when(i < last)
    def _():
        _start_copies(copies_of(i + 1, 1 - slot_i))

    _wait_copies(copies_of(i, slot_i))
    s1, s2 = _pair_slots(meta_ref[...], segoff_ref[0])
    slot = lax.broadcasted_iota(jnp.int32, (MOE_TM, CBUF_ROWS), 1).astype(F32)
    pick = (jnp.where(slot == s1, 1.0, 0.0) + jnp.where(slot == s2, 1.0, 0.0)).astype(BF16)
    o_ref[...] = x_ref[...] + mod_ref[0, 5:6, :] * _dot(pick, ybuf2[slot_i])


def _combine_call(sched, x2d, mod, tiles_per_mod, meta, segoff_v, y):
    n, d = x2d.shape
    grid_spec = pltpu.PrefetchScalarGridSpec(
        num_scalar_prefetch=3,
        grid=(n // MOE_TM,),
        in_specs=[
            pl.BlockSpec((MOE_TM, d), lambda i, *_: (i, 0)),
            pl.BlockSpec((1, MOD_ROWS, d), lambda i, *_: (i // tiles_per_mod, 0, 0)),
            pl.BlockSpec((MOE_TM, LANES), lambda i, *_: (i, 0)),
            pl.BlockSpec((1, 1, LANES), lambda i, *_: (i, 0, 0)),
            pl.BlockSpec(memory_space=pl.ANY),
        ],
        out_specs=pl.BlockSpec((MOE_TM, d), lambda i, *_: (i, 0)),
        scratch_shapes=[pltpu.VMEM((2, CBUF_ROWS, d), BF16), pltpu.SemaphoreType.DMA((2,))],
    )
    return pl.pallas_call(
        _combine_kernel,
        grid_spec=grid_spec,
        out_shape=jax.ShapeDtypeStruct((n, d), F32),
        compiler_params=_cparams(("arbitrary",)),
        name="moe_combine",
    )(*sched, x2d, mod, meta, segoff_v, y)


def _moe_call(x2d, mod, tiles_per_mod, outs, w_out, g, wr, br, w1, w3, w2):
    n, d = x2d.shape
    nt = n // MOE_TM
    ltri = jnp.asarray(np.tril(np.ones((MOE_TM, MOE_TM), np.float32), -1), BF16)
    x2d, h, meta, metat, counts = _router_call(x2d, mod, tiles_per_mod, *outs, w_out, g, wr, br, ltri)

    cnt = counts[:, 0, :N_EXPERTS].astype(jnp.int32)
    seg_len = (cnt + SEG_ALIGN - 1) // SEG_ALIGN * SEG_ALIGN
    segoff = jnp.cumsum(seg_len, axis=1) - seg_len
    region = (jnp.sum(seg_len, axis=0) + MOE_TM - 1) // MOE_TM * MOE_TM
    region_start = jnp.cumsum(region) - region
    base = region_start[None, :] + jnp.cumsum(seg_len, axis=0) - seg_len
    rows_max = 2 * n + nt * N_EXPERTS * (SEG_ALIGN - 1) + N_EXPERTS * (MOE_TM - 1)
    n_sorted_tiles = (rows_max + MOE_TM - 1) // MOE_TM
    tile_end = jnp.cumsum(region // MOE_TM)
    total_tiles = tile_end[-1]
    jt = jnp.arange(n_sorted_tiles, dtype=jnp.int32)
    blk = jnp.minimum(jt, total_tiles - 1)
    tile_expert = jnp.sum((blk[:, None] >= tile_end[None, :]).astype(jnp.int32), axis=1)
    tile_sched = (tile_expert.astype(jnp.int32), blk.astype(jnp.int32), (jt < total_tiles).astype(jnp.int32))
    sched = tuple(a.reshape(-1).astype(jnp.int32) for a in (segoff, base, seg_len))
    segoff_v = jnp.pad(segoff.astype(F32), ((0, 0), (0, LANES - N_EXPERTS)))[:, None, :]

    total = jnp.sum(seg_len, axis=0)
    fill = jnp.concatenate([region_start + total, region - total,
                            jnp.stack([total_tiles * MOE_TM, n_sorted_tiles - total_tiles])]).astype(jnp.int32)
    xs, gs = _compact_call(sched, fill, h, meta, metat, n_sorted_tiles * MOE_TM)
    y = _expert_call(tile_sched, xs, gs, w1, w3, w2)
    return _combine_call(sched, x2d, mod, tiles_per_mod, meta, segoff_v, y)


def _rope_tables(t):
    rows = jnp.arange(t, dtype=F32) // GRID_W
    cols = jnp.arange(t, dtype=F32) % GRID_W

    def tables(rot_dim):
        a = rot_dim // 2
        inv = 1.0 / (ROPE_BASE ** (jnp.arange(0, a, 2, dtype=F32) / a))
        ar = rows[:, None] * inv
        ac = cols[:, None] * inv
        cos = jnp.concatenate([jnp.cos(ar), jnp.cos(ar), jnp.cos(ac), jnp.cos(ac)], axis=-1)
        sin = jnp.concatenate([-jnp.sin(ar), jnp.sin(ar), -jnp.sin(ac), jnp.sin(ac)], axis=-1)
        return cos, sin

    ca, sa = tables(HEAD_DIM)
    cos_a = jnp.tile(ca, (1, LANES // HEAD_DIM))
    sin_a = jnp.tile(sa, (1, LANES // HEAD_DIM))
    cc, sc = tables(C_ROPE)
    ones = jnp.ones((t, C_NOPE), F32)
    tail = C_SLOT - C_NOPE - C_ROPE
    cos_c = jnp.concatenate([ones, cc, jnp.ones((t, tail), F32)], axis=-1)
    sin_c = jnp.concatenate([0 * ones, sc, jnp.zeros((t, tail), F32)], axis=-1)
    return cos_a, sin_a, cos_c, sin_c


def _head_perm():
    order = []
    for j in range(A_HEADS // 2):
        order += [j, A_HEADS // 2 + j]
    return np.concatenate([np.arange(h * HEAD_DIM, (h + 1) * HEAD_DIM) for h in order])


def _segment_mean_matrix(widths, total):
    m = np.zeros((total, total), np.float32)
    o = 0
    while o < total:
        for w, used in widths:
            if used:
                m[o:o + w, o:o + w] = 1.0 / w
            o += w
    return jnp.asarray(m, BF16)


def _slot_vec(nope, rope):
    z = jnp.zeros((C_SLOT - C_NOPE - C_ROPE,), F32)
    n = jnp.zeros((C_NOPE,), F32) if nope is None else nope
    r = jnp.zeros((C_ROPE,), F32) if rope is None else rope
    return jnp.tile(jnp.concatenate([n, r, z]), C_HEADS)[None, :]


def _layer_consts(i, p, perm, tabs):
    w_in = p["w_in"][i]
    o_kr = A_Q + 2 * A_KV + 2 * B_CH + C_Q_RANK + C_KV_RANK
    d = w_in.shape[0]
    kr_cols = jnp.concatenate([jnp.zeros((d, C_NOPE), F32), w_in[:, o_kr:o_kr + C_ROPE],
                               jnp.zeros((d, C_SLOT - C_NOPE - C_ROPE), F32)], axis=1)
    win = jnp.concatenate([w_in[:, :A_Q][:, perm], w_in[:, A_Q:o_kr], kr_cols], axis=1).astype(BF16)

    w_uq = p["c_w_uq"][i].reshape(C_Q_RANK, C_HEADS, C_NOPE + C_ROPE)
    wuq = jnp.pad(w_uq, ((0, 0), (0, 0), (0, C_SLOT - C_NOPE - C_ROPE))).reshape(C_Q_RANK, C_HEADS * C_SLOT)
    w_ukv = p["c_w_ukv"][i].reshape(C_KV_RANK, C_HEADS, C_NOPE + C_V)
    wukvk = jnp.pad(w_ukv[:, :, :C_NOPE], ((0, 0), (0, 0), (0, C_SLOT - C_NOPE))).reshape(C_KV_RANK, -1)
    wukvv = w_ukv[:, :, C_NOPE:].reshape(C_KV_RANK, C_HEADS * C_V)

    sa = _segment_mean_matrix([(HEAD_DIM, True)], MXU_TILE)
    sc = _segment_mean_matrix([(C_NOPE, True), (C_ROPE, True), (C_SLOT - C_NOPE - C_ROPE, False)], MXU_TILE)
    gq = jnp.tile(p["a_q_norm_g"][i] * (A_SCALE * LOG2E), A_HEADS)[None, :]
    gk = jnp.tile(p["a_k_norm_g"][i], A_KV_HEADS)[None, :]
    gqc = _slot_vec(p["c_q_nope_norm_g"][i], p["c_q_rope_norm_g"][i]) * (MLA_SCALE * LOG2E)
    gkn = _slot_vec(p["c_k_nope_norm_g"][i], None)
    gkr = _slot_vec(None, p["c_k_rope_norm_g"][i])[:, :C_SLOT]
    return (p["mix_norm_g"][i][None, :], win) + tabs + (
        sa, sc, gq, gk, p["c_q_rank_norm_g"][i][None, :], p["c_kv_rank_norm_g"][i][None, :],
        wuq.astype(BF16), wukvk.astype(BF16), wukvv.astype(BF16), gqc, gkn, gkr)


def kernel(x, c, ctx, c_ctx, ada_w, ada_b, mix_norm_g, ffn_norm_g, w_in, w_out, a_q_norm_g, a_k_norm_g, a_sink, b_conv_w, b_conv_b, b_ln_g, b_ln_b, c_q_rank_norm_g, c_kv_rank_norm_g, c_w_uq, c_w_ukv, c_q_nope_norm_g, c_k_nope_norm_g, c_q_rope_norm_g, c_k_rope_norm_g, dense_w1, dense_w3, dense_w2, moe_router_w, moe_router_b, moe_w1, moe_w3, moe_w2):
    p = dict(w_in=w_in, c_w_uq=c_w_uq, c_w_ukv=c_w_ukv, a_q_norm_g=a_q_norm_g, a_k_norm_g=a_k_norm_g,
             c_q_nope_norm_g=c_q_nope_norm_g, c_k_nope_norm_g=c_k_nope_norm_g,
             c_q_rope_norm_g=c_q_rope_norm_g, c_k_rope_norm_g=c_k_rope_norm_g,
             c_q_rank_norm_g=c_q_rank_norm_g, c_kv_rank_norm_g=c_kv_rank_norm_g, mix_norm_g=mix_norm_g)
    bsz, t, d = x.shape
    ctx_len = ctx.shape[1]
    depth = ada_w.shape[0]
    n_x, n_c = bsz * t, bsz * ctx_len
    tm_pre = min(4 * PRE_SUB, t, n_c)
    tm_tok = 512
    tq = min(2048, t)
    assert t % tm_pre == 0 and n_c % tm_pre == 0 and t % tm_tok == 0 and n_c % tm_tok == 0 and t % tq == 0
    assert t % GRID_W == 0 and t >= 3 * BLOCK and ctx_len % BLOCK == 0

    ada_rows = ((bsz + 1 + 7) // 8) * 8
    c_pad = jnp.concatenate([c, c_ctx[None, :], jnp.zeros((ada_rows - bsz - 1, d), F32)], axis=0)
    mods = _ada_call(c_pad, ada_w, ada_b).reshape(depth, ada_rows, 6, d)
    mods = jnp.pad(mods, ((0, 0), (0, 0), (0, MOD_ROWS - 6), (0, 0)))

    tabs_x = _rope_tables(t)
    ones = jnp.ones((tm_pre, LANES), F32)
    tabs_c = (ones, 0 * ones, ones, 0 * ones)
    perm = _head_perm()

    x2 = x.reshape(n_x, d)
    c2 = ctx.reshape(n_c, d)
    for i in range(depth):
        last = i == depth - 1
        mod_x = mods[i, :bsz]
        mod_c = mods[i, bsz:bsz + 1]
        consts_x = _layer_consts(i, p, perm, tabs_x)
        consts_c = _layer_consts(i, p, perm, tabs_c)

        qa_x, ka_x, va_x, u_x, qc_x, kc_x, vc_x = _pre_call(x2, mod_x, t // tm_pre, t // tm_pre, consts_x, tm_pre)
        if last:
            ka_c, va_c, kc_c, vc_c = _pre_call(c2, mod_c, n_c // tm_pre, 1, consts_c, tm_pre, kv_only=True)
        else:
            qa_c, ka_c, va_c, u_c, qc_c, kc_c, vc_c = _pre_call(c2, mod_c, n_c // tm_pre, 1, consts_c, tm_pre)

        def r3(a, length):
            return a.reshape(bsz, length, a.shape[-1])

        sink_row = jnp.repeat(a_sink[i] * LOG2E, BLOCK)[None, :]
        conv_w = jnp.pad(b_conv_w[i], ((0, 32 - B_WIDTH), (0, 0)))
        conv_p = (conv_w, b_conv_b[i][None, :], b_ln_g[i][None, :], b_ln_b[i][None, :])
        w_o = jnp.concatenate([w_out[i][:A_Q][perm], w_out[i][A_Q:]], axis=0).astype(BF16)

        o_a, o_b = _attn_a_call(qa_x, r3(ka_x, t), va_x, r3(ka_c, ctx_len), va_c, sink_row, r3(u_x, t), conv_p, bsz, t)
        o_c = _mla_call(qc_x, r3(kc_x, t), vc_x, r3(kc_c, ctx_len), vc_c, bsz, t, tq)
        if not last:
            oc_a, oc_b = _attn_a_call(qa_c, None, None, r3(ka_c, ctx_len), va_c, sink_row, r3(u_c, ctx_len), conv_p,
                                      bsz, ctx_len)
            oc_c = _mla_call(qc_c, None, None, r3(kc_c, ctx_len), vc_c, bsz, ctx_len, ctx_len)

        j = i // 2
        g_ffn = ffn_norm_g[i][None, :]
        if i % 2 == 0:
            w = (dense_w1[j].astype(BF16), dense_w3[j].astype(BF16), dense_w2[j].astype(BF16))

            def mix(a2, mod, tiles_per_mod, outs, w=w, g_ffn=g_ffn, w_o=w_o):
                return _ffn_call(a2, mod, tiles_per_mod, *outs, w_o, g_ffn, *w, tm_tok)
        else:
            w = tuple(a.astype(BF16).reshape(-1, a.shape[-1]) for a in (moe_w1[j], moe_w3[j], moe_w2[j]))
            wr = jnp.pad(moe_router_w[j], ((0, 0), (0, LANES - N_EXPERTS)))
            br = jnp.pad(moe_router_b[j], (0, LANES - N_EXPERTS))[None, :]

            def mix(a2, mod, tiles_per_mod, outs, w=w, g_ffn=g_ffn, wr=wr, br=br, w_o=w_o):
                return _moe_call(a2, mod, tiles_per_mod * (tm_tok // MOE_TM), outs, w_o, g_ffn, wr, br, *w)
        x2 = mix(x2, mod_x, t // tm_tok, (o_a, o_b, o_c))
        if not last:
            c2 = mix(c2, mod_c, n_c // tm_tok, (oc_a, oc_b, oc_c))
    return x2.reshape(bsz, t, d)
```

```python
import functools

import jax
import jax.numpy as jnp
import numpy as np
from jax import lax
from jax.experimental import pallas as pl
from jax.experimental.pallas import tpu as pltpu

F32 = jnp.float32
BF16 = jnp.bfloat16

D_MODEL = 1024
GRID_W = 64
HEAD_DIM = 64
A_HEADS = 8
A_KV_HEADS = 2
A_WINDOW = 128
BLOCK = 128
B_CH = 256
B_WIDTH = 31
C_HEADS = 4
C_Q_RANK = 384
C_KV_RANK = 256
C_NOPE = 64
C_ROPE = 32
C_V = 64
A_Q = A_HEADS * HEAD_DIM
A_KV = A_KV_HEADS * HEAD_DIM
IN_COLS_PAD = 2048
D_FF = 2816
N_EXPERTS = 8
ROPE_BASE = 10000.0
EPS = 1e-6
NEG = -1e30
A_SCALE = HEAD_DIM ** -0.5
MLA_SCALE = (C_NOPE + C_ROPE) ** -0.5
LOG2E = 1.4426950408889634

LANES = 128
MXU_TILE = 256
MOD_ROWS = 8
C_SLOT = 128
VMEM_LIMIT = 56 * 1024 * 1024


def _cparams(sem):
    return pltpu.CompilerParams(dimension_semantics=sem, vmem_limit_bytes=VMEM_LIMIT)


def _dot(a, b):
    return jnp.dot(a, b, preferred_element_type=F32)


def _dot_nt(a, b):
    return lax.dot_general(a, b, (((1,), (1,)), ((), ())), preferred_element_type=F32)


def _ada_kernel(c_ref, w_ref, b_ref, o_ref):
    c = c_ref[...]
    a = c * jax.nn.sigmoid(c)
    o_ref[0] = _dot(a.astype(BF16), w_ref[0].astype(BF16)) + b_ref[0]


def _ada_call(c_pad, ada_w, ada_b):
    depth, d, n6 = ada_w.shape
    rows = c_pad.shape[0]
    tn = 1536
    return pl.pallas_call(
        _ada_kernel,
        grid=(depth, n6 // tn),
        in_specs=[
            pl.BlockSpec((rows, d), lambda i, j: (0, 0)),
            pl.BlockSpec((1, d, tn), lambda i, j: (i, 0, j)),
            pl.BlockSpec((1, 1, tn), lambda i, j: (i, 0, j)),
        ],
        out_specs=pl.BlockSpec((1, rows, tn), lambda i, j: (i, 0, j)),
        out_shape=jax.ShapeDtypeStruct((depth, rows, n6), F32),
        compiler_params=_cparams(("parallel", "parallel")),
        name="ada_proj",
    )(c_pad, ada_w, ada_b.reshape(depth, 1, n6))


def _rope_chunk(c, cos, sin, half):
    lane = lax.broadcasted_iota(jnp.int32, c.shape, 1)
    lo = (lane & (2 * half - 1)) < half
    partner = jnp.where(lo, pltpu.roll(c, LANES - half, 1), pltpu.roll(c, half, 1))
    return c * cos + partner * sin


PRE_SUB = 256


def _segment_mean(sq, s_ref):
    w = s_ref.shape[0]
    return jnp.concatenate([_dot(sq[:, c:c + w].astype(BF16), s_ref[...]) for c in range(0, sq.shape[1], w)], axis=1)


def _pre_kernel(x_ref, mod_ref, gmix_ref, win_ref, cosa_ref, sina_ref, cosc_ref, sinc_ref,
                sa_ref, sc_ref, gq_ref, gk_ref, gcq_ref, gckv_ref, wuq_ref, wukvk_ref, wukvv_ref,
                gqc_ref, gkn_ref, gkr_ref, *out_refs, kv_only):
    if kv_only:
        ka_ref, va_ref, kc_ref, vc_ref = out_refs
        o_ka, o_ckv = 0, 2 * A_KV
    else:
        qa_ref, ka_ref, va_ref, u_ref, qc_ref, kc_ref, vc_ref = out_refs
        o_ka, o_ckv = A_Q, A_Q + 2 * A_KV + 2 * B_CH + C_Q_RANK
    for sub in range(x_ref.shape[0] // PRE_SUB):
        rows = slice(sub * PRE_SUB, (sub + 1) * PRE_SUB)
        x = x_ref[rows, :]
        ms = jnp.mean(x * x, axis=-1, keepdims=True)
        y = x * lax.rsqrt(ms + EPS)
        shift = mod_ref[0, 0:1, :]
        scale = mod_ref[0, 1:2, :]
        h = (y * gmix_ref[...]) * (1.0 + scale) + shift
        p = _dot(h.astype(BF16), win_ref[...])

        cosa, sina = cosa_ref[rows, :], sina_ref[rows, :]
        cosc, sinc = cosc_ref[rows, :], sinc_ref[rows, :]

        if not kv_only:
            qa = p[:, 0:A_Q]
            ssq = _segment_mean(qa * qa, sa_ref)
            qa = qa * lax.rsqrt(ssq + EPS) * gq_ref[...]
            for j in range(A_Q // LANES):
                sl = slice(j * LANES, (j + 1) * LANES)
                qa_ref[rows, sl] = _rope_chunk(qa[:, sl], cosa, sina, HEAD_DIM // 4).astype(BF16)

            o = A_Q + 2 * A_KV
            u_ref[rows, :] = p[:, o:o + B_CH] * jax.nn.sigmoid(p[:, o + B_CH:o + 2 * B_CH])

            o = o + 2 * B_CH
            cq = p[:, o:o + C_Q_RANK]
            cq = cq * lax.rsqrt(jnp.mean(cq * cq, axis=-1, keepdims=True) + EPS) * gcq_ref[...]
            qc = _dot(cq.astype(BF16), wuq_ref[...])
            ssq = _segment_mean(qc * qc, sc_ref)
            qc = qc * lax.rsqrt(ssq + EPS) * gqc_ref[...]
            for j in range(C_HEADS):
                sl = slice(j * C_SLOT, (j + 1) * C_SLOT)
                qc_ref[rows, sl] = _rope_chunk(qc[:, sl], cosc, sinc, C_ROPE // 4).astype(BF16)

        ka = p[:, o_ka:o_ka + A_KV]
        ssk = _dot((ka * ka).astype(BF16), sa_ref[0:A_KV, 0:A_KV])
        ka = ka * lax.rsqrt(ssk + EPS) * gk_ref[...]
        ka_ref[rows, :] = _rope_chunk(ka, cosa, sina, HEAD_DIM // 4).astype(BF16)
        va_ref[:, rows] = p[:, o_ka + A_KV:o_ka + 2 * A_KV].T.astype(BF16)

        o = o_ckv
        ckv = p[:, o:o + C_KV_RANK]
        ckv = (ckv * lax.rsqrt(jnp.mean(ckv * ckv, axis=-1, keepdims=True) + EPS) * gckv_ref[...]).astype(BF16)
        kn = _dot(ckv, wukvk_ref[...])
        vc_ref[:, rows] = _dot(ckv, wukvv_ref[...]).T.astype(BF16)
        ssk = _segment_mean(kn * kn, sc_ref)
        kn = kn * lax.rsqrt(ssk + EPS) * gkn_ref[...]
        o = o + C_KV_RANK
        kr = p[:, o:o + C_SLOT]
        kr = kr * lax.rsqrt(jnp.sum(kr * kr, axis=-1, keepdims=True) * (1.0 / C_ROPE) + EPS) * gkr_ref[...]
        kr = _rope_chunk(kr, cosc, sinc, C_ROPE // 4)
        for j in range(C_HEADS):
            sl = slice(j * C_SLOT, (j + 1) * C_SLOT)
            kc_ref[rows, sl] = (kn[:, sl] + kr).astype(BF16)


def _pre_call(x2d, mod, tiles_per_mod, tab_tiles, consts, tm, kv_only=False):
    n, d = x2d.shape
    (gmix, win, cosa, sina, cosc, sinc, sa, sc, gq, gk, gcq, gckv, wuq, wukvk, wukvv, gqc, gkn, gkr) = consts
    if kv_only:
        o_ckv = A_Q + 2 * A_KV + 2 * B_CH + C_Q_RANK
        win = jnp.concatenate([win[:, A_Q:A_Q + 2 * A_KV], win[:, o_ckv:]], axis=1)

    def const(a):
        return pl.BlockSpec(a.shape, lambda i: (0,) * a.ndim)

    def tab(a):
        return pl.BlockSpec((tm, LANES), lambda i: (i % tab_tiles, 0))

    in_specs = [
        pl.BlockSpec((tm, d), lambda i: (i, 0)),
        pl.BlockSpec((1, MOD_ROWS, d), lambda i: (i // tiles_per_mod, 0, 0)),
        const(gmix), const(win), tab(cosa), tab(sina), tab(cosc), tab(sinc),
        const(sa), const(sc), const(gq), const(gk), const(gcq), const(gckv),
        const(wuq), const(wukvk), const(wukvv), const(gqc), const(gkn), const(gkr),
    ]
    widths = (A_Q, A_KV, A_KV, B_CH, C_HEADS * C_SLOT, C_HEADS * C_SLOT, C_HEADS * C_V)
    dtypes = (BF16, BF16, BF16, F32, BF16, BF16, BF16)
    out_specs = [pl.BlockSpec((tm, w), lambda i: (i, 0)) for w in widths]
    out_shape = [jax.ShapeDtypeStruct((n, w), dt) for w, dt in zip(widths, dtypes)]
    out_specs[2] = pl.BlockSpec((A_KV, tm), lambda i: (0, i))
    out_shape[2] = jax.ShapeDtypeStruct((A_KV, n), BF16)
    out_specs[6] = pl.BlockSpec((C_HEADS * C_V, tm), lambda i: (0, i))
    out_shape[6] = jax.ShapeDtypeStruct((C_HEADS * C_V, n), BF16)
    if kv_only:
        keep = (1, 2, 5, 6)
        out_specs = [out_specs[k] for k in keep]
        out_shape = [out_shape[k] for k in keep]
    return pl.pallas_call(
        functools.partial(_pre_kernel, kv_only=kv_only),
        grid=(n // tm,),
        in_specs=in_specs,
        out_specs=out_specs,
        out_shape=out_shape,
        compiler_params=_cparams(("parallel",)),
        name="pre_attn",
    )(x2d, mod, gmix, win, cosa, sina, cosc, sinc, sa, sc, gq, gk, gcq, gckv, wuq, wukvk, wukvv,
      gqc, gkn, gkr)


CONV_PAD = 16
CONV_CHUNK = 128
SUBLANES = 8


def _conv_chunk(pad_ref, c, w_ref, b_ref, g_ref, beta_ref):
    off = CONV_PAD - B_WIDTH // 2
    nq = (off + B_WIDTH - 1) // SUBLANES + 1
    win = CONV_CHUNK + (nq - 1) * SUBLANES
    base = c * CONV_CHUNK
    acc = jnp.zeros((CONV_CHUNK, B_CH), F32)
    for r in range(SUBLANES):
        taps = [k for k in range(B_WIDTH) if (off + k) % SUBLANES == r]
        if not taps:
            continue
        w_r = pad_ref[base + r:base + r + win, :]
        part = None
        for k in taps:
            q = (off + k) // SUBLANES
            term = w_r[q * SUBLANES:q * SUBLANES + CONV_CHUNK, :] * w_ref[k:k + 1, :]
            part = term if part is None else part + term
        acc = acc + part
    y = acc + b_ref[...]
    mu = jnp.mean(y, axis=-1, keepdims=True)
    yc = y - mu
    var = jnp.mean(yc * yc, axis=-1, keepdims=True)
    z = yc * lax.rsqrt(var + EPS) * g_ref[...] + beta_ref[...]
    return (z * jax.nn.sigmoid(z)).astype(BF16)


ATTN_A_QBLOCKS = 16


ONES_ROWS = 16


def _attn_a_kernel(*refs, t, has_local, qblocks):
    u_ref, uprev_ref, unext_ref, cw_ref, cb_ref, cg_ref, cbeta_ref, o_ref, ob_ref, pad_ref = refs[-10:]
    if has_local:
        q_ref, k_ref, vt_ref, kc_ref, vct_ref, sink_ref, bias_ref = refs[:-10]
    else:
        q_ref, kc_ref, vct_ref, sink_ref = refs[:-10]
    step = pl.program_id(1)
    tq = q_ref.shape[0]
    pad_ref[0:CONV_PAD, :] = jnp.where(step > 0, uprev_ref[0], 0.0)
    pad_ref[CONV_PAD:CONV_PAD + tq, :] = u_ref[0]
    pad_ref[CONV_PAD + tq:CONV_PAD + tq + CONV_PAD, :] = jnp.where(step < pl.num_programs(1) - 1, unext_ref[0], 0.0)
    nchunk = A_Q // LANES
    span = 3 * BLOCK
    lane = lax.broadcasted_iota(jnp.int32, (BLOCK, LANES), 1)
    row = lax.broadcasted_iota(jnp.int32, (A_KV, BLOCK), 0)
    zero = jnp.zeros((BLOCK, LANES), BF16)
    sink = sink_ref[...]
    kc = kc_ref[0]
    vct = vct_ref[...]
    for blk in range(qblocks):
        rows = slice(blk * BLOCK, (blk + 1) * BLOCK)
        chunks = [q_ref[rows, j * LANES:(j + 1) * LANES] for j in range(nchunk)]
        qs = jnp.concatenate([jnp.where(lane < HEAD_DIM, c, zero) for c in chunks]
                             + [jnp.where(lane >= HEAD_DIM, c, zero) for c in chunks], axis=0)
        if has_local:
            n = pl.program_id(1) * qblocks + blk
            start = pl.multiple_of(jnp.clip((n - 1) * BLOCK, 0, t - span), BLOCK)
            keys = jnp.concatenate([k_ref[0, pl.ds(start, span), :], kc], axis=0)
            vt = jnp.concatenate([vt_ref[:, pl.ds(start, span)], vct], axis=1)
        else:
            keys, vt = kc, vct
        s = _dot_nt(keys, qs)
        if has_local:
            s = jnp.concatenate([s[:span] + bias_ref[n - start // BLOCK], s[span:]], axis=0)
        m = jnp.maximum(jnp.max(s, axis=0, keepdims=True), sink)
        e = jnp.exp2(s - m).astype(BF16)
        vte = jnp.concatenate([vt, jnp.ones((ONES_ROWS, vt.shape[1]), BF16)], axis=0)
        acc = _dot(vte, e)
        den = acc[A_KV:A_KV + 1] + jnp.exp2(sink - m)
        out = acc[:A_KV] * (1.0 / den)
        for j in range(nchunk):
            x = jnp.where(row < HEAD_DIM, out[:, j * BLOCK:(j + 1) * BLOCK],
                          out[:, (nchunk + j) * BLOCK:(nchunk + j + 1) * BLOCK])
            o_ref[rows, j * LANES:(j + 1) * LANES] = x.T.astype(BF16)
        ob_ref[rows, :] = _conv_chunk(pad_ref, blk, cw_ref, cb_ref, cg_ref, cbeta_ref)


def _window_bias():
    r = np.arange(BLOCK)[None, :]
    c = np.arange(3 * BLOCK)[:, None]
    pats = [np.where(np.abs(c - p * BLOCK - r) <= A_WINDOW, 0.0, NEG) for p in range(3)]
    return jnp.asarray(np.stack([np.tile(p, (1, A_HEADS)) for p in pats]), F32)


def _attn_a_call(qa, ka, vat, kac, vact, sink_row, u3, conv_p, bsz, t):
    has_local = ka is not None
    qblocks = min(ATTN_A_QBLOCKS, t // BLOCK)
    tq = qblocks * BLOCK
    nq = t // tq
    ctx_len = kac.shape[1]
    in_specs = [pl.BlockSpec((tq, A_Q), lambda b, n: (b * nq + n, 0))]
    args = [qa]
    if has_local:
        in_specs += [pl.BlockSpec((1, t, A_KV), lambda b, n: (b, 0, 0)),
                     pl.BlockSpec((A_KV, t), lambda b, n: (0, b))]
        args += [ka, vat]
    in_specs += [pl.BlockSpec((1, ctx_len, A_KV), lambda b, n: (b, 0, 0)),
                 pl.BlockSpec((A_KV, ctx_len), lambda b, n: (0, b))]
    in_specs += [pl.BlockSpec(sink_row.shape, lambda b, n: (0, 0))]
    args += [kac, vact, sink_row]
    if has_local:
        bias = _window_bias()
        in_specs += [pl.BlockSpec(bias.shape, lambda b, n: (0, 0, 0))]
        args += [bias]
    halo_per_step = tq // CONV_PAD
    n_halo = t // CONV_PAD
    in_specs += [
        pl.BlockSpec((1, tq, B_CH), lambda b, n: (b, n, 0)),
        pl.BlockSpec((1, CONV_PAD, B_CH), lambda b, n: (b, jnp.maximum(n * halo_per_step - 1, 0), 0)),
        pl.BlockSpec((1, CONV_PAD, B_CH), lambda b, n: (b, jnp.minimum((n + 1) * halo_per_step, n_halo - 1), 0)),
    ] + [pl.BlockSpec(a.shape, lambda b, n: (0, 0)) for a in conv_p]
    args += [u3, u3, u3, *conv_p]
    return pl.pallas_call(
        functools.partial(_attn_a_kernel, t=t, has_local=has_local, qblocks=qblocks),
        grid=(bsz, nq),
        in_specs=in_specs,
        out_specs=[pl.BlockSpec((tq, A_Q), lambda b, n: (b * nq + n, 0)),
                   pl.BlockSpec((tq, B_CH), lambda b, n: (b * nq + n, 0))],
        out_shape=[jax.ShapeDtypeStruct((bsz * t, A_Q), BF16), jax.ShapeDtypeStruct((bsz * t, B_CH), BF16)],
        scratch_shapes=[pltpu.VMEM((tq + 2 * CONV_PAD, B_CH), F32)],
        compiler_params=_cparams(("parallel", "parallel")),
        name="attn_ab_local" if has_local else "attn_ab_ctx",
    )(*args)


def _mla_kernel(*refs, has_local):
    if has_local:
        q_ref, kx_ref, vxt_ref, kc_ref, vct_ref, o_ref = refs
    else:
        q_ref, kc_ref, vct_ref, o_ref = refs
    def scores(h):
        sl = slice(h * C_SLOT, (h + 1) * C_SLOT)
        q = q_ref[:, sl]
        s_c = _dot_nt(kc_ref[0, :, sl], q)
        s_x = _dot_nt(kx_ref[0, :, sl], q) if has_local else None
        return s_c, s_x

    outs = []
    nxt = scores(0)
    for h in range(C_HEADS):
        vs = slice(h * C_V, (h + 1) * C_V)
        s_c, s_x = nxt
        if h + 1 < C_HEADS:
            nxt = scores(h + 1)
        m = jnp.max(s_c, axis=0, keepdims=True)
        if has_local:
            m = jnp.maximum(m, jnp.max(s_x, axis=0, keepdims=True))
        e_c = jnp.exp2(s_c - m).astype(BF16)
        vt = jnp.concatenate([vct_ref[vs, :], jnp.ones((ONES_ROWS, e_c.shape[0]), BF16)], axis=0)
        acc = _dot(vt, e_c)
        if has_local:
            e_x = jnp.exp2(s_x - m).astype(BF16)
            vt = jnp.concatenate([vxt_ref[vs, :], jnp.ones((ONES_ROWS, e_x.shape[0]), BF16)], axis=0)
            acc = acc + _dot(vt, e_x)
        outs.append(acc[:C_V] * (1.0 / acc[C_V:C_V + 1]))
    o_ref[...] = jnp.concatenate(outs, axis=0).T.astype(BF16)


def _mla_call(qc, kx, vxt, kcc, vcct, bsz, t, tq):
    has_local = kx is not None
    nq = t // tq
    ctx_len = kcc.shape[1]
    wq = C_HEADS * C_SLOT
    wv = C_HEADS * C_V
    in_specs = [pl.BlockSpec((tq, wq), lambda b, n: (b * nq + n, 0))]
    args = [qc]
    if has_local:
        in_specs += [pl.BlockSpec((1, t, wq), lambda b, n: (b, 0, 0)),
                     pl.BlockSpec((wv, t), lambda b, n: (0, b))]
        args += [kx, vxt]
    in_specs += [pl.BlockSpec((1, ctx_len, wq), lambda b, n: (b, 0, 0)),
                 pl.BlockSpec((wv, ctx_len), lambda b, n: (0, b))]
    args += [kcc, vcct]
    return pl.pallas_call(
        functools.partial(_mla_kernel, has_local=has_local),
        grid=(bsz, nq),
        in_specs=in_specs,
        out_specs=pl.BlockSpec((tq, wv), lambda b, n: (b * nq + n, 0)),
        out_shape=jax.ShapeDtypeStruct((bsz * t, wv), BF16),
        compiler_params=_cparams(("parallel", "parallel")),
        name="mla_local" if has_local else "mla_ctx",
    )(*args)


def _mixer_residual(x_ref, mod_ref, oa_ref, ob_ref, oc_ref, wo_ref):
    y = _dot(oa_ref[...], wo_ref[0:A_Q, :])
    y = y + _dot(ob_ref[...], wo_ref[A_Q:A_Q + B_CH, :])
    y = y + _dot(oc_ref[...], wo_ref[A_Q + B_CH:, :])
    return x_ref[...] + mod_ref[0, 2:3, :] * y


def _mixer_specs(tm, d, tiles_per_mod, oa, ob, oc, w_out):
    return [
        pl.BlockSpec((tm, d), lambda i: (i, 0)),
        pl.BlockSpec((1, MOD_ROWS, d), lambda i: (i // tiles_per_mod, 0, 0)),
        pl.BlockSpec((tm, oa.shape[1]), lambda i: (i, 0)),
        pl.BlockSpec((tm, ob.shape[1]), lambda i: (i, 0)),
        pl.BlockSpec((tm, oc.shape[1]), lambda i: (i, 0)),
        pl.BlockSpec(w_out.shape, lambda i: (0, 0), pipeline_mode=pl.Buffered(1)),
    ]


def _split_bf16(a):
    hi = a.astype(BF16)
    lo = (a - hi.astype(F32)).astype(BF16)
    return hi, lo


def _ffn_input(x, mod_ref, g_ref):
    ms = jnp.mean(x * x, axis=-1, keepdims=True)
    y = x * lax.rsqrt(ms + EPS)
    return (y * g_ref[...]) * (1.0 + mod_ref[0, 4:5, :]) + mod_ref[0, 3:4, :]


def _ff_chunks(ff):
    tiles = ff // MXU_TILE
    if ff % MXU_TILE or tiles < 2:
        return (ff,)
    first = (tiles // 2) * MXU_TILE
    return (first, ff - first)


def _swiglu(h, w1_ref, w3_ref, w2_ref):
    y = None
    o = 0
    for tf in _ff_chunks(w1_ref.shape[1]):
        a = _dot(h, w1_ref[:, o:o + tf])
        b = _dot(h, w3_ref[:, o:o + tf])
        g = (a * jax.nn.sigmoid(a) * b).astype(BF16)
        yc = _dot(g, w2_ref[o:o + tf, :])
        y = yc if y is None else y + yc
        o += tf
    return y


def _ffn_kernel(x_ref, mod_ref, oa_ref, ob_ref, oc_ref, wo_ref, g_ref, w1_ref, w3_ref, w2_ref, o_ref):
    x = _mixer_residual(x_ref, mod_ref, oa_ref, ob_ref, oc_ref, wo_ref)
    h = _ffn_input(x, mod_ref, g_ref).astype(BF16)
    o_ref[...] = x + mod_ref[0, 5:6, :] * _swiglu(h, w1_ref, w3_ref, w2_ref)


def _ffn_call(x2d, mod, tiles_per_mod, oa, ob, oc, w_out, g, w1, w3, w2, tm):
    n, d = x2d.shape

    def resident(a):
        return pl.BlockSpec(a.shape, lambda i: (0, 0), pipeline_mode=pl.Buffered(1))

    return pl.pallas_call(
        _ffn_kernel,
        grid=(n // tm,),
        in_specs=_mixer_specs(tm, d, tiles_per_mod, oa, ob, oc, w_out) + [
            pl.BlockSpec(g.shape, lambda i: (0, 0)), resident(w1), resident(w3), resident(w2)],
        out_specs=pl.BlockSpec((tm, d), lambda i: (i, 0)),
        out_shape=jax.ShapeDtypeStruct((n, d), F32),
        compiler_params=_cparams(("parallel",)),
        name="ffn_dense",
    )(x2d, mod, oa, ob, oc, w_out, g, w1, w3, w2)


MOE_TM = 512
SEG_ALIGN = 16
SEG_PIECES = (512, 256, 128, 64, 32, 16)
CBUF_ROWS = 2 * MOE_TM + N_EXPERTS * SEG_ALIGN
META_I1, META_I2, META_G1, META_G2, META_R1, META_R2 = range(6)
META_ROWS = 8


def _router_kernel(x_ref, mod_ref, oa_ref, ob_ref, oc_ref, wo_ref, g_ref, wr_ref, br_ref, utri_ref,
                   x1_ref, h_ref, meta_ref, metat_ref, cnt_ref):
    x = _mixer_residual(x_ref, mod_ref, oa_ref, ob_ref, oc_ref, wo_ref)
    x1_ref[...] = x
    h = _ffn_input(x, mod_ref, g_ref)
    h_ref[...] = h.astype(BF16)
    h_hi, h_lo = _split_bf16(h)
    w_hi, w_lo = _split_bf16(wr_ref[...])
    logits = _dot(h_hi, w_hi) + (_dot(h_lo, w_hi) + _dot(h_hi, w_lo)) + br_ref[...]
    lt = logits.T[0:N_EXPERTS, :]
    row = lax.broadcasted_iota(jnp.int32, lt.shape, 0).astype(F32)
    m1 = jnp.max(lt, axis=0, keepdims=True)
    i1 = jnp.min(jnp.where(lt == m1, row, float(N_EXPERTS)), axis=0, keepdims=True)
    rest = jnp.where(row == i1, NEG, lt)
    m2 = jnp.max(rest, axis=0, keepdims=True)
    i2 = jnp.min(jnp.where(rest == m2, row, float(N_EXPERTS)), axis=0, keepdims=True)
    e2 = jnp.exp(m2 - m1)
    den = 1.0 + e2
    sel1 = jnp.where(row == i1, 1.0, 0.0)
    sel2 = jnp.where(row == i2, 1.0, 0.0)
    sel = sel1 + sel2
    before = _dot(sel.astype(BF16), utri_ref[...])
    r1 = jnp.sum(before * sel1, axis=0, keepdims=True)
    r2 = jnp.sum(before * sel2, axis=0, keepdims=True)
    cnt_ref[0] = jnp.broadcast_to(jnp.sum(sel, axis=1, keepdims=True), (N_EXPERTS, LANES))
    zero = jnp.zeros_like(m1)
    metat = jnp.concatenate([i1, i2, 1.0 / den, e2 / den, r1, r2, zero, zero], axis=0)
    metat_ref[...] = metat
    pad = jnp.zeros((LANES - META_ROWS, metat.shape[1]), F32)
    meta_ref[...] = jnp.concatenate([metat, pad], axis=0).T


def _router_call(x2d, mod, tiles_per_mod, oa, ob, oc, w_out, g, wr, br, utri):
    n, d = x2d.shape
    tm = MOE_TM
    nt = n // tm
    return pl.pallas_call(
        _router_kernel,
        grid=(nt,),
        in_specs=_mixer_specs(tm, d, tiles_per_mod, oa, ob, oc, w_out) + [
            pl.BlockSpec(g.shape, lambda i: (0, 0)),
            pl.BlockSpec(wr.shape, lambda i: (0, 0)),
            pl.BlockSpec(br.shape, lambda i: (0, 0)),
            pl.BlockSpec(utri.shape, lambda i: (0, 0)),
        ],
        out_specs=[
            pl.BlockSpec((tm, d), lambda i: (i, 0)),
            pl.BlockSpec((tm, d), lambda i: (i, 0)),
            pl.BlockSpec((tm, LANES), lambda i: (i, 0)),
            pl.BlockSpec((META_ROWS, tm), lambda i: (0, i)),
            pl.BlockSpec((1, N_EXPERTS, LANES), lambda i: (i, 0, 0)),
        ],
        out_shape=[
            jax.ShapeDtypeStruct((n, d), F32),
            jax.ShapeDtypeStruct((n, d), BF16),
            jax.ShapeDtypeStruct((n, LANES), F32),
            jax.ShapeDtypeStruct((META_ROWS, n), F32),
            jax.ShapeDtypeStruct((nt, N_EXPERTS, LANES), F32),
        ],
        compiler_params=_cparams(("parallel",)),
        name="moe_router",
    )(x2d, mod, oa, ob, oc, w_out, g, wr, br, utri)


def _pair_slots(meta, segoff_row):
    lane = lax.broadcasted_iota(jnp.int32, meta.shape, 1).astype(F32)
    i1 = meta[:, META_I1:META_I1 + 1]
    i2 = meta[:, META_I2:META_I2 + 1]
    s1 = jnp.sum(jnp.where(lane == i1, segoff_row, 0.0), axis=-1, keepdims=True) + meta[:, META_R1:META_R1 + 1]
    s2 = jnp.sum(jnp.where(lane == i2, segoff_row, 0.0), axis=-1, keepdims=True) + meta[:, META_R2:META_R2 + 1]
    return s1, s2


def _segment_copies(src, dst, src_off, dst_off, length, sem):
    out = []
    for size in SEG_PIECES:
        done = (length // (2 * size)) * (2 * size)
        s = pl.multiple_of(src_off + done, SEG_ALIGN)
        t = pl.multiple_of(dst_off + done, SEG_ALIGN)
        cp = pltpu.make_async_copy(src.at[pl.ds(s, size)], dst.at[pl.ds(t, size)], sem)
        out.append(((length & size) != 0, cp))
    return out


def _start_copies(copies):
    for pred, cp in copies:
        @pl.when(pred)
        def _(cp=cp):
            cp.start()


def _wait_copies(copies):
    for pred, cp in copies:
        @pl.when(pred)
        def _(cp=cp):
            cp.wait()


def _compact_kernel(segoff_s, base_s, len_s, fill_s, h_ref, meta_ref, metat_ref, xs_out, gs_out,
                    cbuf2, gbuf2, zx, zg, sems):
    i = pl.program_id(0)
    last = pl.num_programs(0) - 1
    slot_i = i % 2
    cbuf = cbuf2.at[slot_i]
    gbuf = gbuf2.at[slot_i]

    def copies_of(tile, slot):
        out = []
        for e in range(N_EXPERTS):
            k = tile * N_EXPERTS + e
            out += _segment_copies(cbuf2.at[slot], xs_out, segoff_s[k], base_s[k], len_s[k], sems.at[0, slot])
            out += _segment_copies(gbuf2.at[slot], gs_out, segoff_s[k], base_s[k], len_s[k], sems.at[1, slot])
        return out

    @pl.when(i >= 2)
    def _():
        _wait_copies(copies_of(i - 2, slot_i))

    mt = metat_ref[...]
    s1 = mt[META_R1:META_R1 + 1]
    s2 = mt[META_R2:META_R2 + 1]
    for e in range(N_EXPERTS):
        off = segoff_s[i * N_EXPERTS + e].astype(F32)
        s1 = s1 + jnp.where(mt[META_I1:META_I1 + 1] == e, off, 0.0)
        s2 = s2 + jnp.where(mt[META_I2:META_I2 + 1] == e, off, 0.0)
    row = lax.broadcasted_iota(jnp.int32, (CBUF_ROWS, MOE_TM), 0).astype(F32)
    p1 = jnp.where(row == s1, 1.0, 0.0)
    p2 = jnp.where(row == s2, 1.0, 0.0)
    cbuf[...] = _dot((p1 + p2).astype(BF16), h_ref[...]).astype(BF16)
    meta = meta_ref[...]
    lane = lax.broadcasted_iota(jnp.int32, meta.shape, 1)
    g1 = meta[:, META_G1:META_G1 + 1]
    g2 = meta[:, META_G2:META_G2 + 1]
    g1_hi = g1.astype(BF16).astype(F32)
    g2_hi = g2.astype(BF16).astype(F32)
    a1 = jnp.where(lane == 0, g1_hi, jnp.where(lane == 1, g1 - g1_hi, 0.0)).astype(BF16)
    a2 = jnp.where(lane == 0, g2_hi, jnp.where(lane == 1, g2 - g2_hi, 0.0)).astype(BF16)
    gbuf[...] = _dot(p1.astype(BF16), a1) + _dot(p2.astype(BF16), a2)
    _start_copies(copies_of(i, slot_i))

    @pl.when(i == last)
    def _():
        @pl.when(i >= 1)
        def _():
            _wait_copies(copies_of(i - 1, 1 - slot_i))
        _wait_copies(copies_of(i, slot_i))
        zx[...] = jnp.zeros_like(zx)
        zg[...] = jnp.zeros_like(zg)
        fills = []
        for e in range(N_EXPERTS):
            fills += _segment_copies(zx, xs_out, 0, fill_s[e], fill_s[N_EXPERTS + e], sems.at[0, 0])
            fills += _segment_copies(zg, gs_out, 0, fill_s[e], fill_s[N_EXPERTS + e], sems.at[1, 0])
        _start_copies(fills)
        _wait_copies(fills)

        def fill_copies(j):
            r = pl.multiple_of(fill_s[2 * N_EXPERTS] + j * MOE_TM, MOE_TM)
            return (pltpu.make_async_copy(zx, xs_out.at[pl.ds(r, MOE_TM)], sems.at[0, 0]),
                    pltpu.make_async_copy(zg, gs_out.at[pl.ds(r, MOE_TM)], sems.at[1, 0]))

        def start_tile(j, carry):
            for cp in fill_copies(j):
                cp.start()
            return carry

        def wait_tile(j, carry):
            for cp in fill_copies(j):
                cp.wait()
            return carry

        lax.fori_loop(0, fill_s[2 * N_EXPERTS + 1], start_tile, 0)
        lax.fori_loop(0, fill_s[2 * N_EXPERTS + 1], wait_tile, 0)


def _compact_call(sched, fill, h, meta, metat, rows):
    n, d = h.shape
    nt = n // MOE_TM
    grid_spec = pltpu.PrefetchScalarGridSpec(
        num_scalar_prefetch=4,
        grid=(nt,),
        in_specs=[
            pl.BlockSpec((MOE_TM, d), lambda i, *_: (i, 0)),
            pl.BlockSpec((MOE_TM, LANES), lambda i, *_: (i, 0)),
            pl.BlockSpec((META_ROWS, MOE_TM), lambda i, *_: (0, i)),
        ],
        out_specs=[pl.BlockSpec(memory_space=pl.ANY), pl.BlockSpec(memory_space=pl.ANY)],
        scratch_shapes=[pltpu.VMEM((2, CBUF_ROWS, d), BF16), pltpu.VMEM((2, CBUF_ROWS, LANES), F32),
                        pltpu.VMEM((MOE_TM, d), BF16), pltpu.VMEM((MOE_TM, LANES), F32),
                        pltpu.SemaphoreType.DMA((2, 2))],
    )
    return pl.pallas_call(
        _compact_kernel,
        grid_spec=grid_spec,
        out_shape=[jax.ShapeDtypeStruct((rows, d), BF16), jax.ShapeDtypeStruct((rows, LANES), F32)],
        compiler_params=_cparams(("arbitrary",)),
        name="moe_compact",
    )(*sched, fill, h, meta, metat)


def _expert_kernel(exp_s, blk_s, valid_s, xs_ref, gs_ref, w1_ref, w3_ref, w2_ref, y_ref):
    del exp_s, blk_s
    j = pl.program_id(0)

    @pl.when(valid_s[j] != 0)
    def _():
        gate = gs_ref[:, 0:1] + gs_ref[:, 1:2]
        y_ref[...] = (_swiglu(xs_ref[...], w1_ref, w3_ref, w2_ref) * gate).astype(BF16)

    @pl.when(valid_s[j] == 0)
    def _():
        y_ref[...] = jnp.zeros_like(y_ref)


def _expert_call(tile_sched, xs, gs, w1, w3, w2):
    rows, d = xs.shape
    ff = w1.shape[1]
    grid_spec = pltpu.PrefetchScalarGridSpec(
        num_scalar_prefetch=3,
        grid=(rows // MOE_TM,),
        in_specs=[
            pl.BlockSpec((MOE_TM, d), lambda j, e_s, b_s, v_s: (b_s[j], 0)),
            pl.BlockSpec((MOE_TM, LANES), lambda j, e_s, b_s, v_s: (b_s[j], 0)),
            pl.BlockSpec((d, ff), lambda j, e_s, b_s, v_s: (e_s[j], 0)),
            pl.BlockSpec((d, ff), lambda j, e_s, b_s, v_s: (e_s[j], 0)),
            pl.BlockSpec((ff, d), lambda j, e_s, b_s, v_s: (e_s[j], 0)),
        ],
        out_specs=pl.BlockSpec((MOE_TM, d), lambda j, e_s, b_s, v_s: (j, 0)),
    )
    return pl.pallas_call(
        _expert_kernel,
        grid_spec=grid_spec,
        out_shape=jax.ShapeDtypeStruct((rows, d), BF16),
        compiler_params=_cparams(("arbitrary",)),
        name="moe_experts",
    )(*tile_sched, xs, gs, w1, w3, w2)


def _combine_kernel(segoff_s, base_s, len_s, x_ref, mod_ref, meta_ref, segoff_ref, y_hbm, o_ref, ybuf2, sems):
    i = pl.program_id(0)
    last = pl.num_programs(0) - 1
    slot_i = i % 2

    def copies_of(tile, slot):
        out = []
        for e in range(N_EXPERTS):
            k = tile * N_EXPERTS + e
            out += _segment_copies(y_hbm, ybuf2.at[slot], base_s[k], segoff_s[k], len_s[k], sems.at[slot])
        return out

    @pl.when(i == 0)
    def _():
        ybuf2[...] = jnp.zeros_like(ybuf2)
        _start_copies(copies_of(i, slot_i))

    @pl.when(i < last)
    def _():
        _start_copies(copies_of(i + 1, 1 - slot_i))

    _wait_copies(copies_of(i, slot_i))
    s1, s2 = _pair_slots(meta_ref[...], segoff_ref[0])
    slot = lax.broadcasted_iota(jnp.int32, (MOE_TM, CBUF_ROWS), 1).astype(F32)
    pick = (jnp.where(slot == s1, 1.0, 0.0) + jnp.where(slot == s2, 1.0, 0.0)).astype(BF16)
    o_ref[...] = x_ref[...] + mod_ref[0, 5:6, :] * _dot(pick, ybuf2[slot_i])


def _combine_call(sched, x2d, mod, tiles_per_mod, meta, segoff_v, y):
    n, d = x2d.shape
    grid_spec = pltpu.PrefetchScalarGridSpec(
        num_scalar_prefetch=3,
        grid=(n // MOE_TM,),
        in_specs=[
            pl.BlockSpec((MOE_TM, d), lambda i, *_: (i, 0)),
            pl.BlockSpec((1, MOD_ROWS, d), lambda i, *_: (i // tiles_per_mod, 0, 0)),
            pl.BlockSpec((MOE_TM, LANES), lambda i, *_: (i, 0)),
            pl.BlockSpec((1, 1, LANES), lambda i, *_: (i, 0, 0)),
            pl.BlockSpec(memory_space=pl.ANY),
        ],
        out_specs=pl.BlockSpec((MOE_TM, d), lambda i, *_: (i, 0)),
        scratch_shapes=[pltpu.VMEM((2, CBUF_ROWS, d), BF16), pltpu.SemaphoreType.DMA((2,))],
    )
    return pl.pallas_call(
        _combine_kernel,
        grid_spec=grid_spec,
        out_shape=jax.ShapeDtypeStruct((n, d), F32),
        compiler_params=_cparams(("arbitrary",)),
        name="moe_combine",
    )(*sched, x2d, mod, meta, segoff_v, y)


def _moe_call(x2d, mod, tiles_per_mod, outs, w_out, g, wr, br, w1, w3, w2):
    n, d = x2d.shape
    nt = n // MOE_TM
    utri = jnp.asarray(np.triu(np.ones((MOE_TM, MOE_TM), np.float32), 1), BF16)
    x2d, h, meta, metat, counts = _router_call(x2d, mod, tiles_per_mod, *outs, w_out, g, wr, br, utri)

    cnt = counts[:, :, 0].astype(jnp.int32)
    seg_len = (cnt + SEG_ALIGN - 1) // SEG_ALIGN * SEG_ALIGN
    segoff = jnp.cumsum(seg_len, axis=1) - seg_len
    region = (jnp.sum(seg_len, axis=0) + MOE_TM - 1) // MOE_TM * MOE_TM
    region_start = jnp.cumsum(region) - region
    base = region_start[None, :] + jnp.cumsum(seg_len, axis=0) - seg_len
    rows_max = 2 * n + nt * N_EXPERTS * (SEG_ALIGN - 1) + N_EXPERTS * (MOE_TM - 1)
    n_sorted_tiles = (rows_max + MOE_TM - 1) // MOE_TM
    tile_end = jnp.cumsum(region // MOE_TM)
    total_tiles = tile_end[-1]
    jt = jnp.arange(n_sorted_tiles, dtype=jnp.int32)
    blk = jnp.minimum(jt, total_tiles - 1)
    tile_expert = jnp.sum((blk[:, None] >= tile_end[None, :]).astype(jnp.int32), axis=1)
    tile_sched = (tile_expert.astype(jnp.int32), blk.astype(jnp.int32), (jt < total_tiles).astype(jnp.int32))
    sched = tuple(a.reshape(-1).astype(jnp.int32) for a in (segoff, base, seg_len))
    segoff_v = jnp.pad(segoff.astype(F32), ((0, 0), (0, LANES - N_EXPERTS)))[:, None, :]

    total = jnp.sum(seg_len, axis=0)
    fill = jnp.concatenate([region_start + total, region - total,
                            jnp.stack([total_tiles * MOE_TM, n_sorted_tiles - total_tiles])]).astype(jnp.int32)
    xs, gs = _compact_call(sched, fill, h, meta, metat, n_sorted_tiles * MOE_TM)
    y = _expert_call(tile_sched, xs, gs, w1, w3, w2)
    return _combine_call(sched, x2d, mod, tiles_per_mod, meta, segoff_v, y)


def _rope_tables(t):
    rows = jnp.arange(t, dtype=F32) // GRID_W
    cols = jnp.arange(t, dtype=F32) % GRID_W

    def tables(rot_dim):
        a = rot_dim // 2
        inv = 1.0 / (ROPE_BASE ** (jnp.arange(0, a, 2, dtype=F32) / a))
        ar = rows[:, None] * inv
        ac = cols[:, None] * inv
        cos = jnp.concatenate([jnp.cos(ar), jnp.cos(ar), jnp.cos(ac), jnp.cos(ac)], axis=-1)
        sin = jnp.concatenate([-jnp.sin(ar), jnp.sin(ar), -jnp.sin(ac), jnp.sin(ac)], axis=-1)
        return cos, sin

    ca, sa = tables(HEAD_DIM)
    cos_a = jnp.tile(ca, (1, LANES // HEAD_DIM))
    sin_a = jnp.tile(sa, (1, LANES // HEAD_DIM))
    cc, sc = tables(C_ROPE)
    ones = jnp.ones((t, C_NOPE), F32)
    tail = C_SLOT - C_NOPE - C_ROPE
    cos_c = jnp.concatenate([ones, cc, jnp.ones((t, tail), F32)], axis=-1)
    sin_c = jnp.concatenate([0 * ones, sc, jnp.zeros((t, tail), F32)], axis=-1)
    return cos_a, sin_a, cos_c, sin_c


def _head_perm():
    order = []
    for j in range(A_HEADS // 2):
        order += [j, A_HEADS // 2 + j]
    return np.concatenate([np.arange(h * HEAD_DIM, (h + 1) * HEAD_DIM) for h in order])


def _segment_mean_matrix(widths, total):
    m = np.zeros((total, total), np.float32)
    o = 0
    while o < total:
        for w, used in widths:
            if used:
                m[o:o + w, o:o + w] = 1.0 / w
            o += w
    return jnp.asarray(m, BF16)


def _slot_vec(nope, rope):
    z = jnp.zeros((C_SLOT - C_NOPE - C_ROPE,), F32)
    n = jnp.zeros((C_NOPE,), F32) if nope is None else nope
    r = jnp.zeros((C_ROPE,), F32) if rope is None else rope
    return jnp.tile(jnp.concatenate([n, r, z]), C_HEADS)[None, :]


def _layer_consts(i, p, perm, tabs):
    w_in = p["w_in"][i]
    o_kr = A_Q + 2 * A_KV + 2 * B_CH + C_Q_RANK + C_KV_RANK
    d = w_in.shape[0]
    kr_cols = jnp.concatenate([jnp.zeros((d, C_NOPE), F32), w_in[:, o_kr:o_kr + C_ROPE],
                               jnp.zeros((d, C_SLOT - C_NOPE - C_ROPE), F32)], axis=1)
    win = jnp.concatenate([w_in[:, :A_Q][:, perm], w_in[:, A_Q:o_kr], kr_cols], axis=1).astype(BF16)

    w_uq = p["c_w_uq"][i].reshape(C_Q_RANK, C_HEADS, C_NOPE + C_ROPE)
    wuq = jnp.pad(w_uq, ((0, 0), (0, 0), (0, C_SLOT - C_NOPE - C_ROPE))).reshape(C_Q_RANK, C_HEADS * C_SLOT)
    w_ukv = p["c_w_ukv"][i].reshape(C_KV_RANK, C_HEADS, C_NOPE + C_V)
    wukvk = jnp.pad(w_ukv[:, :, :C_NOPE], ((0, 0), (0, 0), (0, C_SLOT - C_NOPE))).reshape(C_KV_RANK, -1)
    wukvv = w_ukv[:, :, C_NOPE:].reshape(C_KV_RANK, C_HEADS * C_V)

    sa = _segment_mean_matrix([(HEAD_DIM, True)], MXU_TILE)
    sc = _segment_mean_matrix([(C_NOPE, True), (C_ROPE, True), (C_SLOT - C_NOPE - C_ROPE, False)], MXU_TILE)
    gq = jnp.tile(p["a_q_norm_g"][i] * (A_SCALE * LOG2E), A_HEADS)[None, :]
    gk = jnp.tile(p["a_k_norm_g"][i], A_KV_HEADS)[None, :]
    gqc = _slot_vec(p["c_q_nope_norm_g"][i], p["c_q_rope_norm_g"][i]) * (MLA_SCALE * LOG2E)
    gkn = _slot_vec(p["c_k_nope_norm_g"][i], None)
    gkr = _slot_vec(None, p["c_k_rope_norm_g"][i])[:, :C_SLOT]
    return (p["mix_norm_g"][i][None, :], win) + tabs + (
        sa, sc, gq, gk, p["c_q_rank_norm_g"][i][None, :], p["c_kv_rank_norm_g"][i][None, :],
        wuq.astype(BF16), wukvk.astype(BF16), wukvv.astype(BF16), gqc, gkn, gkr)


def kernel(x, c, ctx, c_ctx, ada_w, ada_b, mix_norm_g, ffn_norm_g, w_in, w_out, a_q_norm_g, a_k_norm_g, a_sink, b_conv_w, b_conv_b, b_ln_g, b_ln_b, c_q_rank_norm_g, c_kv_rank_norm_g, c_w_uq, c_w_ukv, c_q_nope_norm_g, c_k_nope_norm_g, c_q_rope_norm_g, c_k_rope_norm_g, dense_w1, dense_w3, dense_w2, moe_router_w, moe_router_b, moe_w1, moe_w3, moe_w2):
    p = dict(w_in=w_in, c_w_uq=c_w_uq, c_w_ukv=c_w_ukv, a_q_norm_g=a_q_norm_g, a_k_norm_g=a_k_norm_g,
             c_q_nope_norm_g=c_q_nope_norm_g, c_k_nope_norm_g=c_k_nope_norm_g,
             c_q_rope_norm_g=c_q_rope_norm_g, c_k_rope_norm_g=c_k_rope_norm_g,
             c_q_rank_norm_g=c_q_rank_norm_g, c_kv_rank_norm_g=c_kv_rank_norm_g, mix_norm_g=mix_norm_g)
    bsz, t, d = x.shape
    ctx_len = ctx.shape[1]
    depth = ada_w.shape[0]
    n_x, n_c = bsz * t, bsz * ctx_len
    tm_pre = min(4 * PRE_SUB, t, n_c)
    tm_tok = 512
    tq = min(2048, t)
    assert t % tm_pre == 0 and n_c % tm_pre == 0 and t % tm_tok == 0 and n_c % tm_tok == 0 and t % tq == 0
    assert t % GRID_W == 0 and t >= 3 * BLOCK and ctx_len % BLOCK == 0

    ada_rows = ((bsz + 1 + 7) // 8) * 8
    c_pad = jnp.concatenate([c, c_ctx[None, :], jnp.zeros((ada_rows - bsz - 1, d), F32)], axis=0)
    mods = _ada_call(c_pad, ada_w, ada_b).reshape(depth, ada_rows, 6, d)
    mods = jnp.pad(mods, ((0, 0), (0, 0), (0, MOD_ROWS - 6), (0, 0)))

    tabs_x = _rope_tables(t)
    ones = jnp.ones((tm_pre, LANES), F32)
    tabs_c = (ones, 0 * ones, ones, 0 * ones)
    perm = _head_perm()

    x2 = x.reshape(n_x, d)
    c2 = ctx.reshape(n_c, d)
    for i in range(depth):
        last = i == depth - 1
        mod_x = mods[i, :bsz]
        mod_c = mods[i, bsz:bsz + 1]
        consts_x = _layer_consts(i, p, perm, tabs_x)
        consts_c = _layer_consts(i, p, perm, tabs_c)

        qa_x, ka_x, va_x, u_x, qc_x, kc_x, vc_x = _pre_call(x2, mod_x, t // tm_pre, t // tm_pre, consts_x, tm_pre)
        if last:
            ka_c, va_c, kc_c, vc_c = _pre_call(c2, mod_c, n_c // tm_pre, 1, consts_c, tm_pre, kv_only=True)
        else:
            qa_c, ka_c, va_c, u_c, qc_c, kc_c, vc_c = _pre_call(c2, mod_c, n_c // tm_pre, 1, consts_c, tm_pre)

        def r3(a, length):
            return a.reshape(bsz, length, a.shape[-1])

        sink_row = jnp.repeat(a_sink[i] * LOG2E, BLOCK)[None, :]
        conv_w = jnp.pad(b_conv_w[i], ((0, 32 - B_WIDTH), (0, 0)))
        conv_p = (conv_w, b_conv_b[i][None, :], b_ln_g[i][None, :], b_ln_b[i][None, :])
        w_o = jnp.concatenate([w_out[i][:A_Q][perm], w_out[i][A_Q:]], axis=0).astype(BF16)

        o_a, o_b = _attn_a_call(qa_x, r3(ka_x, t), va_x, r3(ka_c, ctx_len), va_c, sink_row, r3(u_x, t), conv_p, bsz, t)
        o_c = _mla_call(qc_x, r3(kc_x, t), vc_x, r3(kc_c, ctx_len), vc_c, bsz, t, tq)
        if not last:
            oc_a, oc_b = _attn_a_call(qa_c, None, None, r3(ka_c, ctx_len), va_c, sink_row, r3(u_c, ctx_len), conv_p,
                                      bsz, ctx_len)
            oc_c = _mla_call(qc_c, None, None, r3(kc_c, ctx_len), vc_c, bsz, ctx_len, ctx_len)

        j = i // 2
        g_ffn = ffn_norm_g[i][None, :]
        if i % 2 == 0:
            w = (dense_w1[j].astype(BF16), dense_w3[j].astype(BF16), dense_w2[j].astype(BF16))

            def mix(a2, mod, tiles_per_mod, outs, w=w, g_ffn=g_ffn, w_o=w_o):
                return _ffn_call(a2, mod, tiles_per_mod, *outs, w_o, g_ffn, *w, tm_tok)
        else:
            w = tuple(a.astype(BF16).reshape(-1, a.shape[-1]) for a in (moe_w1[j], moe_w3[j], moe_w2[j]))
            wr = jnp.pad(moe_router_w[j], ((0, 0), (0, LANES - N_EXPERTS)))
            br = jnp.pad(moe_router_b[j], (0, LANES - N_EXPERTS))[None, :]

            def mix(a2, mod, tiles_per_mod, outs, w=w, g_ffn=g_ffn, wr=wr, br=br, w_o=w_o):
                return _moe_call(a2, mod, tiles_per_mod * (tm_tok // MOE_TM), outs, w_o, g_ffn, wr, br, *w)
        x2 = mix(x2, mod_x, t // tm_tok, (o_a, o_b, o_c))
        if not last:
            c2 = mix(c2, mod_c, n_c // tm_tok, (oc_a, oc_b, oc_c))
    return x2.reshape(bsz, t, d)
```

```python
import functools

import jax
import jax.numpy as jnp
import numpy as np
from jax import lax
from jax.experimental import pallas as pl
from jax.experimental.pallas import tpu as pltpu

F32 = jnp.float32
BF16 = jnp.bfloat16

GRID_W = 64
HEAD_DIM = 64
A_HEADS = 8
A_KV_HEADS = 2
A_WINDOW = 128
BLOCK = 128
B_CH = 256
B_WIDTH = 31
C_HEADS = 4
C_Q_RANK = 384
C_KV_RANK = 256
C_NOPE = 64
C_ROPE = 32
C_V = 64
A_Q = A_HEADS * HEAD_DIM
A_KV = A_KV_HEADS * HEAD_DIM
N_EXPERTS = 8
ROPE_BASE = 10000.0
EPS = 1e-6
NEG = -1e30
A_SCALE = HEAD_DIM ** -0.5
MLA_SCALE = (C_NOPE + C_ROPE) ** -0.5
LOG2E = 1.4426950408889634

LANES = 128
MXU_TILE = 256
MOD_ROWS = 8
C_SLOT = 128
VMEM_LIMIT = 56 * 1024 * 1024


def _cparams(sem):
    return pltpu.CompilerParams(dimension_semantics=sem, vmem_limit_bytes=VMEM_LIMIT)


def _dot(a, b):
    return jnp.dot(a, b, preferred_element_type=F32)


def _dot_nt(a, b):
    return lax.dot_general(a, b, (((1,), (1,)), ((), ())), preferred_element_type=F32)


def _ada_kernel(c_ref, w_ref, b_ref, o_ref):
    c = c_ref[...]
    a = c * jax.nn.sigmoid(c)
    o_ref[0] = _dot(a.astype(BF16), w_ref[0].astype(BF16)) + b_ref[0]


def _ada_call(c_pad, ada_w, ada_b):
    depth, d, n6 = ada_w.shape
    rows = c_pad.shape[0]
    tn = 1536
    return pl.pallas_call(
        _ada_kernel,
        grid=(depth, n6 // tn),
        in_specs=[
            pl.BlockSpec((rows, d), lambda i, j: (0, 0)),
            pl.BlockSpec((1, d, tn), lambda i, j: (i, 0, j)),
            pl.BlockSpec((1, 1, tn), lambda i, j: (i, 0, j)),
        ],
        out_specs=pl.BlockSpec((1, rows, tn), lambda i, j: (i, 0, j)),
        out_shape=jax.ShapeDtypeStruct((depth, rows, n6), F32),
        compiler_params=_cparams(("parallel", "parallel")),
        name="ada_proj",
    )(c_pad, ada_w, ada_b.reshape(depth, 1, n6))


def _rope_chunk(c, cos, sin, half):
    lane = lax.broadcasted_iota(jnp.int32, c.shape, 1)
    lo = (lane & (2 * half - 1)) < half
    partner = jnp.where(lo, pltpu.roll(c, LANES - half, 1), pltpu.roll(c, half, 1))
    return c * cos + partner * sin


PRE_SUB = 256


def _segment_mean(sq, s_ref):
    w = s_ref.shape[0]
    return jnp.concatenate([_dot(sq[:, c:c + w].astype(BF16), s_ref[...]) for c in range(0, sq.shape[1], w)], axis=1)


def _pre_kernel(x_ref, mod_ref, gmix_ref, win_ref, cosa_ref, sina_ref, cosc_ref, sinc_ref,
                sa_ref, sc_ref, gq_ref, gk_ref, gcq_ref, gckv_ref, wuq_ref, wukvk_ref, wukvv_ref,
                gqc_ref, gkn_ref, gkr_ref, *out_refs, kv_only):
    if kv_only:
        ka_ref, va_ref, kc_ref, vc_ref = out_refs
        o_ka, o_ckv = 0, 2 * A_KV
    else:
        qa_ref, ka_ref, va_ref, u_ref, qc_ref, kc_ref, vc_ref = out_refs
        o_ka, o_ckv = A_Q, A_Q + 2 * A_KV + 2 * B_CH + C_Q_RANK
    for sub in range(x_ref.shape[0] // PRE_SUB):
        rows = slice(sub * PRE_SUB, (sub + 1) * PRE_SUB)
        x = x_ref[rows, :]
        ms = jnp.mean(x * x, axis=-1, keepdims=True)
        y = x * lax.rsqrt(ms + EPS)
        shift = mod_ref[0, 0:1, :]
        scale = mod_ref[0, 1:2, :]
        h = (y * gmix_ref[...]) * (1.0 + scale) + shift
        p = _dot(h.astype(BF16), win_ref[...])

        cosa, sina = cosa_ref[rows, :], sina_ref[rows, :]
        cosc, sinc = cosc_ref[rows, :], sinc_ref[rows, :]

        if not kv_only:
            qa = p[:, 0:A_Q]
            ssq = _segment_mean(qa * qa, sa_ref)
            qa = qa * lax.rsqrt(ssq + EPS) * gq_ref[...]
            for j in range(A_Q // LANES):
                sl = slice(j * LANES, (j + 1) * LANES)
                qa_ref[rows, sl] = _rope_chunk(qa[:, sl], cosa, sina, HEAD_DIM // 4).astype(BF16)

            o = A_Q + 2 * A_KV
            u_ref[rows, :] = p[:, o:o + B_CH] * jax.nn.sigmoid(p[:, o + B_CH:o + 2 * B_CH])

            o = o + 2 * B_CH
            cq = p[:, o:o + C_Q_RANK]
            cq = cq * lax.rsqrt(jnp.mean(cq * cq, axis=-1, keepdims=True) + EPS) * gcq_ref[...]
            qc = _dot(cq.astype(BF16), wuq_ref[...])
            ssq = _segment_mean(qc * qc, sc_ref)
            qc = qc * lax.rsqrt(ssq + EPS) * gqc_ref[...]
            for j in range(C_HEADS):
                sl = slice(j * C_SLOT, (j + 1) * C_SLOT)
                qc_ref[rows, sl] = _rope_chunk(qc[:, sl], cosc, sinc, C_ROPE // 4).astype(BF16)

        ka = p[:, o_ka:o_ka + A_KV]
        ssk = _dot((ka * ka).astype(BF16), sa_ref[0:A_KV, 0:A_KV])
        ka = ka * lax.rsqrt(ssk + EPS) * gk_ref[...]
        ka_ref[rows, :] = _rope_chunk(ka, cosa, sina, HEAD_DIM // 4).astype(BF16)
        va_ref[:, rows] = p[:, o_ka + A_KV:o_ka + 2 * A_KV].T.astype(BF16)

        o = o_ckv
        ckv = p[:, o:o + C_KV_RANK]
        ckv = (ckv * lax.rsqrt(jnp.mean(ckv * ckv, axis=-1, keepdims=True) + EPS) * gckv_ref[...]).astype(BF16)
        kn = _dot(ckv, wukvk_ref[...])
        vc_ref[:, rows] = _dot(ckv, wukvv_ref[...]).T.astype(BF16)
        ssk = _segment_mean(kn * kn, sc_ref)
        kn = kn * lax.rsqrt(ssk + EPS) * gkn_ref[...]
        o = o + C_KV_RANK
        kr = p[:, o:o + C_SLOT]
        kr = kr * lax.rsqrt(jnp.sum(kr * kr, axis=-1, keepdims=True) * (1.0 / C_ROPE) + EPS) * gkr_ref[...]
        kr = _rope_chunk(kr, cosc, sinc, C_ROPE // 4)
        for j in range(C_HEADS):
            sl = slice(j * C_SLOT, (j + 1) * C_SLOT)
            kc_ref[rows, sl] = (kn[:, sl] + kr).astype(BF16)


def _pre_call(x2d, mod, tiles_per_mod, tab_tiles, consts, tm, kv_only=False):
    n, d = x2d.shape
    (gmix, win, cosa, sina, cosc, sinc, sa, sc, gq, gk, gcq, gckv, wuq, wukvk, wukvv, gqc, gkn, gkr) = consts
    if kv_only:
        o_ckv = A_Q + 2 * A_KV + 2 * B_CH + C_Q_RANK
        win = jnp.concatenate([win[:, A_Q:A_Q + 2 * A_KV], win[:, o_ckv:]], axis=1)

    def const(a):
        return pl.BlockSpec(a.shape, lambda i: (0,) * a.ndim)

    def tab(a):
        return pl.BlockSpec((tm, LANES), lambda i: (i % tab_tiles, 0))

    in_specs = [
        pl.BlockSpec((tm, d), lambda i: (i, 0)),
        pl.BlockSpec((1, MOD_ROWS, d), lambda i: (i // tiles_per_mod, 0, 0)),
        const(gmix), const(win), tab(cosa), tab(sina), tab(cosc), tab(sinc),
        const(sa), const(sc), const(gq), const(gk), const(gcq), const(gckv),
        const(wuq), const(wukvk), const(wukvv), const(gqc), const(gkn), const(gkr),
    ]
    widths = (A_Q, A_KV, A_KV, B_CH, C_HEADS * C_SLOT, C_HEADS * C_SLOT, C_HEADS * C_V)
    dtypes = (BF16, BF16, BF16, F32, BF16, BF16, BF16)
    out_specs = [pl.BlockSpec((tm, w), lambda i: (i, 0)) for w in widths]
    out_shape = [jax.ShapeDtypeStruct((n, w), dt) for w, dt in zip(widths, dtypes)]
    out_specs[2] = pl.BlockSpec((A_KV, tm), lambda i: (0, i))
    out_shape[2] = jax.ShapeDtypeStruct((A_KV, n), BF16)
    out_specs[6] = pl.BlockSpec((C_HEADS * C_V, tm), lambda i: (0, i))
    out_shape[6] = jax.ShapeDtypeStruct((C_HEADS * C_V, n), BF16)
    if kv_only:
        keep = (1, 2, 5, 6)
        out_specs = [out_specs[k] for k in keep]
        out_shape = [out_shape[k] for k in keep]
    return pl.pallas_call(
        functools.partial(_pre_kernel, kv_only=kv_only),
        grid=(n // tm,),
        in_specs=in_specs,
        out_specs=out_specs,
        out_shape=out_shape,
        compiler_params=_cparams(("parallel",)),
        name="pre_attn",
    )(x2d, mod, gmix, win, cosa, sina, cosc, sinc, sa, sc, gq, gk, gcq, gckv, wuq, wukvk, wukvv,
      gqc, gkn, gkr)


CONV_PAD = 16
CONV_CHUNK = 128
SUBLANES = 8


def _conv_chunk(pad_ref, c, w_ref, b_ref, g_ref, beta_ref):
    off = CONV_PAD - B_WIDTH // 2
    nq = (off + B_WIDTH - 1) // SUBLANES + 1
    win = CONV_CHUNK + (nq - 1) * SUBLANES
    base = c * CONV_CHUNK
    acc = jnp.zeros((CONV_CHUNK, B_CH), F32)
    for r in range(SUBLANES):
        taps = [k for k in range(B_WIDTH) if (off + k) % SUBLANES == r]
        if not taps:
            continue
        w_r = pad_ref[base + r:base + r + win, :]
        part = None
        for k in taps:
            q = (off + k) // SUBLANES
            term = w_r[q * SUBLANES:q * SUBLANES + CONV_CHUNK, :] * w_ref[k:k + 1, :]
            part = term if part is None else part + term
        acc = acc + part
    y = acc + b_ref[...]
    mu = jnp.mean(y, axis=-1, keepdims=True)
    yc = y - mu
    var = jnp.mean(yc * yc, axis=-1, keepdims=True)
    z = yc * lax.rsqrt(var + EPS) * g_ref[...] + beta_ref[...]
    return (z * jax.nn.sigmoid(z)).astype(BF16)


ATTN_A_QBLOCKS = 16


ONES_ROWS = 16


def _attn_a_kernel(*refs, t, has_local, qblocks):
    u_ref, uprev_ref, unext_ref, cw_ref, cb_ref, cg_ref, cbeta_ref, o_ref, ob_ref, pad_ref = refs[-10:]
    if has_local:
        q_ref, k_ref, vt_ref, kc_ref, vct_ref, sink_ref, bias_ref = refs[:-10]
    else:
        q_ref, kc_ref, vct_ref, sink_ref = refs[:-10]
    step = pl.program_id(1)
    tq = q_ref.shape[0]
    pad_ref[0:CONV_PAD, :] = jnp.where(step > 0, uprev_ref[0], 0.0)
    pad_ref[CONV_PAD:CONV_PAD + tq, :] = u_ref[0]
    pad_ref[CONV_PAD + tq:CONV_PAD + tq + CONV_PAD, :] = jnp.where(step < pl.num_programs(1) - 1, unext_ref[0], 0.0)
    nchunk = A_Q // LANES
    span = 3 * BLOCK
    lane = lax.broadcasted_iota(jnp.int32, (BLOCK, LANES), 1)
    row = lax.broadcasted_iota(jnp.int32, (A_KV, BLOCK), 0)
    zero = jnp.zeros((BLOCK, LANES), BF16)
    sink = sink_ref[...]
    kc = kc_ref[0]
    vct = vct_ref[...]
    for blk in range(qblocks):
        rows = slice(blk * BLOCK, (blk + 1) * BLOCK)
        chunks = [q_ref[rows, j * LANES:(j + 1) * LANES] for j in range(nchunk)]
        qs = jnp.concatenate([jnp.where(lane < HEAD_DIM, c, zero) for c in chunks]
                             + [jnp.where(lane >= HEAD_DIM, c, zero) for c in chunks], axis=0)
        if has_local:
            n = pl.program_id(1) * qblocks + blk
            start = pl.multiple_of(jnp.clip((n - 1) * BLOCK, 0, t - span), BLOCK)
            keys = jnp.concatenate([k_ref[0, pl.ds(start, span), :], kc], axis=0)
            vt = jnp.concatenate([vt_ref[:, pl.ds(start, span)], vct], axis=1)
        else:
            keys, vt = kc, vct
        s = _dot_nt(keys, qs)
        if has_local:
            s = jnp.concatenate([s[:span] + bias_ref[n - start // BLOCK], s[span:]], axis=0)
        m = jnp.maximum(jnp.max(s, axis=0, keepdims=True), sink)
        e = jnp.exp2(s - m).astype(BF16)
        vte = jnp.concatenate([vt, jnp.ones((ONES_ROWS, vt.shape[1]), BF16)], axis=0)
        acc = _dot(vte, e)
        den = acc[A_KV:A_KV + 1] + jnp.exp2(sink - m)
        out = acc[:A_KV] * (1.0 / den)
        for j in range(nchunk):
            x = jnp.where(row < HEAD_DIM, out[:, j * BLOCK:(j + 1) * BLOCK],
                          out[:, (nchunk + j) * BLOCK:(nchunk + j + 1) * BLOCK])
            o_ref[rows, j * LANES:(j + 1) * LANES] = x.T.astype(BF16)
        ob_ref[rows, :] = _conv_chunk(pad_ref, blk, cw_ref, cb_ref, cg_ref, cbeta_ref)


def _window_bias():
    r = np.arange(BLOCK)[None, :]
    c = np.arange(3 * BLOCK)[:, None]
    pats = [np.where(np.abs(c - p * BLOCK - r) <= A_WINDOW, 0.0, NEG) for p in range(3)]
    return jnp.asarray(np.stack([np.tile(p, (1, A_HEADS)) for p in pats]), F32)


def _attn_a_call(qa, ka, vat, kac, vact, sink_row, u3, conv_p, bsz, t):
    has_local = ka is not None
    qblocks = min(ATTN_A_QBLOCKS, t // BLOCK)
    tq = qblocks * BLOCK
    nq = t // tq
    ctx_len = kac.shape[1]
    in_specs = [pl.BlockSpec((tq, A_Q), lambda b, n: (b * nq + n, 0))]
    args = [qa]
    if has_local:
        in_specs += [pl.BlockSpec((1, t, A_KV), lambda b, n: (b, 0, 0)),
                     pl.BlockSpec((A_KV, t), lambda b, n: (0, b))]
        args += [ka, vat]
    in_specs += [pl.BlockSpec((1, ctx_len, A_KV), lambda b, n: (b, 0, 0)),
                 pl.BlockSpec((A_KV, ctx_len), lambda b, n: (0, b))]
    in_specs += [pl.BlockSpec(sink_row.shape, lambda b, n: (0, 0))]
    args += [kac, vact, sink_row]
    if has_local:
        bias = _window_bias()
        in_specs += [pl.BlockSpec(bias.shape, lambda b, n: (0, 0, 0))]
        args += [bias]
    halo_per_step = tq // CONV_PAD
    n_halo = t // CONV_PAD
    in_specs += [
        pl.BlockSpec((1, tq, B_CH), lambda b, n: (b, n, 0)),
        pl.BlockSpec((1, CONV_PAD, B_CH), lambda b, n: (b, jnp.maximum(n * halo_per_step - 1, 0), 0)),
        pl.BlockSpec((1, CONV_PAD, B_CH), lambda b, n: (b, jnp.minimum((n + 1) * halo_per_step, n_halo - 1), 0)),
    ] + [pl.BlockSpec(a.shape, lambda b, n: (0, 0)) for a in conv_p]
    args += [u3, u3, u3, *conv_p]
    return pl.pallas_call(
        functools.partial(_attn_a_kernel, t=t, has_local=has_local, qblocks=qblocks),
        grid=(bsz, nq),
        in_specs=in_specs,
        out_specs=[pl.BlockSpec((tq, A_Q), lambda b, n: (b * nq + n, 0)),
                   pl.BlockSpec((tq, B_CH), lambda b, n: (b * nq + n, 0))],
        out_shape=[jax.ShapeDtypeStruct((bsz * t, A_Q), BF16), jax.ShapeDtypeStruct((bsz * t, B_CH), BF16)],
        scratch_shapes=[pltpu.VMEM((tq + 2 * CONV_PAD, B_CH), F32)],
        compiler_params=_cparams(("parallel", "parallel")),
        name="attn_ab_local" if has_local else "attn_ab_ctx",
    )(*args)


def _mla_kernel(*refs, has_local):
    if has_local:
        q_ref, kx_ref, vxt_ref, kc_ref, vct_ref, o_ref = refs
    else:
        q_ref, kc_ref, vct_ref, o_ref = refs
    def scores(h):
        sl = slice(h * C_SLOT, (h + 1) * C_SLOT)
        q = q_ref[:, sl]
        s_c = _dot_nt(kc_ref[0, :, sl], q)
        s_x = _dot_nt(kx_ref[0, :, sl], q) if has_local else None
        return s_c, s_x

    outs = []
    nxt = scores(0)
    for h in range(C_HEADS):
        vs = slice(h * C_V, (h + 1) * C_V)
        s_c, s_x = nxt
        if h + 1 < C_HEADS:
            nxt = scores(h + 1)
        m = jnp.max(s_c, axis=0, keepdims=True)
        if has_local:
            m = jnp.maximum(m, jnp.max(s_x, axis=0, keepdims=True))
        e_c = jnp.exp2(s_c - m).astype(BF16)
        vt = jnp.concatenate([vct_ref[vs, :], jnp.ones((ONES_ROWS, e_c.shape[0]), BF16)], axis=0)
        acc = _dot(vt, e_c)
        if has_local:
            e_x = jnp.exp2(s_x - m).astype(BF16)
            vt = jnp.concatenate([vxt_ref[vs, :], jnp.ones((ONES_ROWS, e_x.shape[0]), BF16)], axis=0)
            acc = acc + _dot(vt, e_x)
        outs.append(acc[:C_V] * (1.0 / acc[C_V:C_V + 1]))
    o_ref[...] = jnp.concatenate(outs, axis=0).T.astype(BF16)


def _mla_call(qc, kx, vxt, kcc, vcct, bsz, t, tq):
    has_local = kx is not None
    nq = t // tq
    ctx_len = kcc.shape[1]
    wq = C_HEADS * C_SLOT
    wv = C_HEADS * C_V
    in_specs = [pl.BlockSpec((tq, wq), lambda b, n: (b * nq + n, 0))]
    args = [qc]
    if has_local:
        in_specs += [pl.BlockSpec((1, t, wq), lambda b, n: (b, 0, 0)),
                     pl.BlockSpec((wv, t), lambda b, n: (0, b))]
        args += [kx, vxt]
    in_specs += [pl.BlockSpec((1, ctx_len, wq), lambda b, n: (b, 0, 0)),
                 pl.BlockSpec((wv, ctx_len), lambda b, n: (0, b))]
    args += [kcc, vcct]
    return pl.pallas_call(
        functools.partial(_mla_kernel, has_local=has_local),
        grid=(bsz, nq),
        in_specs=in_specs,
        out_specs=pl.BlockSpec((tq, wv), lambda b, n: (b * nq + n, 0)),
        out_shape=jax.ShapeDtypeStruct((bsz * t, wv), BF16),
        compiler_params=_cparams(("parallel", "parallel")),
        name="mla_local" if has_local else "mla_ctx",
    )(*args)


def _mixer_residual(x_ref, mod_ref, oa_ref, ob_ref, oc_ref, wo_ref):
    y = _dot(oa_ref[...], wo_ref[0:A_Q, :])
    y = y + _dot(ob_ref[...], wo_ref[A_Q:A_Q + B_CH, :])
    y = y + _dot(oc_ref[...], wo_ref[A_Q + B_CH:, :])
    return x_ref[...] + mod_ref[0, 2:3, :] * y


def _mixer_specs(tm, d, tiles_per_mod, oa, ob, oc, w_out):
    return [
        pl.BlockSpec((tm, d), lambda i: (i, 0)),
        pl.BlockSpec((1, MOD_ROWS, d), lambda i: (i // tiles_per_mod, 0, 0)),
        pl.BlockSpec((tm, oa.shape[1]), lambda i: (i, 0)),
        pl.BlockSpec((tm, ob.shape[1]), lambda i: (i, 0)),
        pl.BlockSpec((tm, oc.shape[1]), lambda i: (i, 0)),
        pl.BlockSpec(w_out.shape, lambda i: (0, 0), pipeline_mode=pl.Buffered(1)),
    ]


def _split_bf16(a):
    hi = a.astype(BF16)
    lo = (a - hi.astype(F32)).astype(BF16)
    return hi, lo


def _ffn_input(x, mod_ref, g_ref):
    ms = jnp.mean(x * x, axis=-1, keepdims=True)
    y = x * lax.rsqrt(ms + EPS)
    return (y * g_ref[...]) * (1.0 + mod_ref[0, 4:5, :]) + mod_ref[0, 3:4, :]


def _ff_chunks(ff):
    tiles = ff // MXU_TILE
    if ff % MXU_TILE or tiles < 2:
        return (ff,)
    first = (tiles // 2) * MXU_TILE
    return (first, ff - first)


def _swiglu(h, w1_ref, w3_ref, w2_ref):
    y = None
    o = 0
    for tf in _ff_chunks(w1_ref.shape[1]):
        a = _dot(h, w1_ref[:, o:o + tf])
        b = _dot(h, w3_ref[:, o:o + tf])
        g = (a * jax.nn.sigmoid(a) * b).astype(BF16)
        yc = _dot(g, w2_ref[o:o + tf, :])
        y = yc if y is None else y + yc
        o += tf
    return y


def _ffn_kernel(x_ref, mod_ref, oa_ref, ob_ref, oc_ref, wo_ref, g_ref, w1_ref, w3_ref, w2_ref, o_ref):
    x = _mixer_residual(x_ref, mod_ref, oa_ref, ob_ref, oc_ref, wo_ref)
    h = _ffn_input(x, mod_ref, g_ref).astype(BF16)
    o_ref[...] = x + mod_ref[0, 5:6, :] * _swiglu(h, w1_ref, w3_ref, w2_ref)


def _ffn_call(x2d, mod, tiles_per_mod, oa, ob, oc, w_out, g, w1, w3, w2, tm):
    n, d = x2d.shape

    def resident(a):
        return pl.BlockSpec(a.shape, lambda i: (0, 0), pipeline_mode=pl.Buffered(1))

    return pl.pallas_call(
        _ffn_kernel,
        grid=(n // tm,),
        in_specs=_mixer_specs(tm, d, tiles_per_mod, oa, ob, oc, w_out) + [
            pl.BlockSpec(g.shape, lambda i: (0, 0)), resident(w1), resident(w3), resident(w2)],
        out_specs=pl.BlockSpec((tm, d), lambda i: (i, 0)),
        out_shape=jax.ShapeDtypeStruct((n, d), F32),
        compiler_params=_cparams(("parallel",)),
        name="ffn_dense",
    )(x2d, mod, oa, ob, oc, w_out, g, w1, w3, w2)


MOE_TM = 512
SEG_ALIGN = 16
SEG_PIECES = (512, 256, 128, 64, 32, 16)
CBUF_ROWS = 2 * MOE_TM + N_EXPERTS * SEG_ALIGN
META_I1, META_I2, META_G1, META_G2, META_R1, META_R2 = range(6)
META_ROWS = 8


def _router_kernel(x_ref, mod_ref, oa_ref, ob_ref, oc_ref, wo_ref, g_ref, wr_ref, br_ref, utri_ref,
                   x1_ref, h_ref, meta_ref, metat_ref, cnt_ref):
    x = _mixer_residual(x_ref, mod_ref, oa_ref, ob_ref, oc_ref, wo_ref)
    x1_ref[...] = x
    h = _ffn_input(x, mod_ref, g_ref)
    h_ref[...] = h.astype(BF16)
    h_hi, h_lo = _split_bf16(h)
    w_hi, w_lo = _split_bf16(wr_ref[...])
    logits = _dot(h_hi, w_hi) + (_dot(h_lo, w_hi) + _dot(h_hi, w_lo)) + br_ref[...]
    lt = logits.T[0:N_EXPERTS, :]
    row = lax.broadcasted_iota(jnp.int32, lt.shape, 0).astype(F32)
    m1 = jnp.max(lt, axis=0, keepdims=True)
    i1 = jnp.min(jnp.where(lt == m1, row, float(N_EXPERTS)), axis=0, keepdims=True)
    rest = jnp.where(row == i1, NEG, lt)
    m2 = jnp.max(rest, axis=0, keepdims=True)
    i2 = jnp.min(jnp.where(rest == m2, row, float(N_EXPERTS)), axis=0, keepdims=True)
    e2 = jnp.exp(m2 - m1)
    den = 1.0 + e2
    sel1 = jnp.where(row == i1, 1.0, 0.0)
    sel2 = jnp.where(row == i2, 1.0, 0.0)
    sel = sel1 + sel2
    before = _dot(sel.astype(BF16), utri_ref[...])
    r1 = jnp.sum(before * sel1, axis=0, keepdims=True)
    r2 = jnp.sum(before * sel2, axis=0, keepdims=True)
    cnt_ref[0] = jnp.broadcast_to(jnp.sum(sel, axis=1, keepdims=True), (N_EXPERTS, LANES))
    zero = jnp.zeros_like(m1)
    metat = jnp.concatenate([i1, i2, 1.0 / den, e2 / den, r1, r2, zero, zero], axis=0)
    metat_ref[...] = metat
    pad = jnp.zeros((LANES - META_ROWS, metat.shape[1]), F32)
    meta_ref[...] = jnp.concatenate([metat, pad], axis=0).T


def _router_call(x2d, mod, tiles_per_mod, oa, ob, oc, w_out, g, wr, br, utri):
    n, d = x2d.shape
    tm = MOE_TM
    nt = n // tm
    return pl.pallas_call(
        _router_kernel,
        grid=(nt,),
        in_specs=_mixer_specs(tm, d, tiles_per_mod, oa, ob, oc, w_out) + [
            pl.BlockSpec(g.shape, lambda i: (0, 0)),
            pl.BlockSpec(wr.shape, lambda i: (0, 0)),
            pl.BlockSpec(br.shape, lambda i: (0, 0)),
            pl.BlockSpec(utri.shape, lambda i: (0, 0)),
        ],
        out_specs=[
            pl.BlockSpec((tm, d), lambda i: (i, 0)),
            pl.BlockSpec((tm, d), lambda i: (i, 0)),
            pl.BlockSpec((tm, LANES), lambda i: (i, 0)),
            pl.BlockSpec((META_ROWS, tm), lambda i: (0, i)),
            pl.BlockSpec((1, N_EXPERTS, LANES), lambda i: (i, 0, 0)),
        ],
        out_shape=[
            jax.ShapeDtypeStruct((n, d), F32),
            jax.ShapeDtypeStruct((n, d), BF16),
            jax.ShapeDtypeStruct((n, LANES), F32),
            jax.ShapeDtypeStruct((META_ROWS, n), F32),
            jax.ShapeDtypeStruct((nt, N_EXPERTS, LANES), F32),
        ],
        compiler_params=_cparams(("parallel",)),
        name="moe_router",
    )(x2d, mod, oa, ob, oc, w_out, g, wr, br, utri)


def _pair_slots(meta, segoff_row):
    lane = lax.broadcasted_iota(jnp.int32, meta.shape, 1).astype(F32)
    i1 = meta[:, META_I1:META_I1 + 1]
    i2 = meta[:, META_I2:META_I2 + 1]
    s1 = jnp.sum(jnp.where(lane == i1, segoff_row, 0.0), axis=-1, keepdims=True) + meta[:, META_R1:META_R1 + 1]
    s2 = jnp.sum(jnp.where(lane == i2, segoff_row, 0.0), axis=-1, keepdims=True) + meta[:, META_R2:META_R2 + 1]
    return s1, s2


def _segment_copies(src, dst, src_off, dst_off, length, sem):
    out = []
    for size in SEG_PIECES:
        done = (length // (2 * size)) * (2 * size)
        s = pl.multiple_of(src_off + done, SEG_ALIGN)
        t = pl.multiple_of(dst_off + done, SEG_ALIGN)
        cp = pltpu.make_async_copy(src.at[pl.ds(s, size)], dst.at[pl.ds(t, size)], sem)
        out.append(((length & size) != 0, cp))
    return out


def _start_copies(copies):
    for pred, cp in copies:
        @pl.when(pred)
        def _(cp=cp):
            cp.start()


def _wait_copies(copies):
    for pred, cp in copies:
        @pl.when(pred)
        def _(cp=cp):
            cp.wait()


def _compact_kernel(segoff_s, base_s, len_s, fill_s, h_ref, meta_ref, metat_ref, xs_out, gs_out,
                    cbuf2, gbuf2, zx, zg, sems):
    i = pl.program_id(0)
    last = pl.num_programs(0) - 1
    slot_i = i % 2
    cbuf = cbuf2.at[slot_i]
    gbuf = gbuf2.at[slot_i]

    def copies_of(tile, slot):
        out = []
        for e in range(N_EXPERTS):
            k = tile * N_EXPERTS + e
            out += _segment_copies(cbuf2.at[slot], xs_out, segoff_s[k], base_s[k], len_s[k], sems.at[0, slot])
            out += _segment_copies(gbuf2.at[slot], gs_out, segoff_s[k], base_s[k], len_s[k], sems.at[1, slot])
        return out

    @pl.when(i >= 2)
    def _():
        _wait_copies(copies_of(i - 2, slot_i))

    mt = metat_ref[...]
    s1 = mt[META_R1:META_R1 + 1]
    s2 = mt[META_R2:META_R2 + 1]
    for e in range(N_EXPERTS):
        off = segoff_s[i * N_EXPERTS + e].astype(F32)
        s1 = s1 + jnp.where(mt[META_I1:META_I1 + 1] == e, off, 0.0)
        s2 = s2 + jnp.where(mt[META_I2:META_I2 + 1] == e, off, 0.0)
    row = lax.broadcasted_iota(jnp.int32, (CBUF_ROWS, MOE_TM), 0).astype(F32)
    p1 = jnp.where(row == s1, 1.0, 0.0)
    p2 = jnp.where(row == s2, 1.0, 0.0)
    cbuf[...] = _dot((p1 + p2).astype(BF16), h_ref[...]).astype(BF16)
    meta = meta_ref[...]
    lane = lax.broadcasted_iota(jnp.int32, meta.shape, 1)
    g1 = meta[:, META_G1:META_G1 + 1]
    g2 = meta[:, META_G2:META_G2 + 1]
    g1_hi = g1.astype(BF16).astype(F32)
    g2_hi = g2.astype(BF16).astype(F32)
    a1 = jnp.where(lane == 0, g1_hi, jnp.where(lane == 1, g1 - g1_hi, 0.0)).astype(BF16)
    a2 = jnp.where(lane == 0, g2_hi, jnp.where(lane == 1, g2 - g2_hi, 0.0)).astype(BF16)
    gbuf[...] = _dot(p1.astype(BF16), a1) + _dot(p2.astype(BF16), a2)
    _start_copies(copies_of(i, slot_i))

    @pl.when(i == last)
    def _():
        @pl.when(i >= 1)
        def _():
            _wait_copies(copies_of(i - 1, 1 - slot_i))
        _wait_copies(copies_of(i, slot_i))
        zx[...] = jnp.zeros_like(zx)
        zg[...] = jnp.zeros_like(zg)
        fills = []
        for e in range(N_EXPERTS):
            fills += _segment_copies(zx, xs_out, 0, fill_s[e], fill_s[N_EXPERTS + e], sems.at[0, 0])
            fills += _segment_copies(zg, gs_out, 0, fill_s[e], fill_s[N_EXPERTS + e], sems.at[1, 0])
        _start_copies(fills)
        _wait_copies(fills)

        def fill_copies(j):
            r = pl.multiple_of(fill_s[2 * N_EXPERTS] + j * MOE_TM, MOE_TM)
            return (pltpu.make_async_copy(zx, xs_out.at[pl.ds(r, MOE_TM)], sems.at[0, 0]),
                    pltpu.make_async_copy(zg, gs_out.at[pl.ds(r, MOE_TM)], sems.at[1, 0]))

        def start_tile(j, carry):
            for cp in fill_copies(j):
                cp.start()
            return carry

        def wait_tile(j, carry):
            for cp in fill_copies(j):
                cp.wait()
            return carry

        lax.fori_loop(0, fill_s[2 * N_EXPERTS + 1], start_tile, 0)
        lax.fori_loop(0, fill_s[2 * N_EXPERTS + 1], wait_tile, 0)


def _compact_call(sched, fill, h, meta, metat, rows):
    n, d = h.shape
    nt = n // MOE_TM
    grid_spec = pltpu.PrefetchScalarGridSpec(
        num_scalar_prefetch=4,
        grid=(nt,),
        in_specs=[
            pl.BlockSpec((MOE_TM, d), lambda i, *_: (i, 0)),
            pl.BlockSpec((MOE_TM, LANES), lambda i, *_: (i, 0)),
            pl.BlockSpec((META_ROWS, MOE_TM), lambda i, *_: (0, i)),
        ],
        out_specs=[pl.BlockSpec(memory_space=pl.ANY), pl.BlockSpec(memory_space=pl.ANY)],
        scratch_shapes=[pltpu.VMEM((2, CBUF_ROWS, d), BF16), pltpu.VMEM((2, CBUF_ROWS, LANES), F32),
                        pltpu.VMEM((MOE_TM, d), BF16), pltpu.VMEM((MOE_TM, LANES), F32),
                        pltpu.SemaphoreType.DMA((2, 2))],
    )
    return pl.pallas_call(
        _compact_kernel,
        grid_spec=grid_spec,
        out_shape=[jax.ShapeDtypeStruct((rows, d), BF16), jax.ShapeDtypeStruct((rows, LANES), F32)],
        compiler_params=_cparams(("arbitrary",)),
        name="moe_compact",
    )(*sched, fill, h, meta, metat)


def _expert_kernel(exp_s, blk_s, valid_s, xs_ref, gs_ref, w1_ref, w3_ref, w2_ref, y_ref):
    del exp_s, blk_s
    j = pl.program_id(0)

    @pl.when(valid_s[j] != 0)
    def _():
        gate = gs_ref[:, 0:1] + gs_ref[:, 1:2]
        y_ref[...] = (_swiglu(xs_ref[...], w1_ref, w3_ref, w2_ref) * gate).astype(BF16)

    @pl.when(valid_s[j] == 0)
    def _():
        y_ref[...] = jnp.zeros_like(y_ref)


def _expert_call(tile_sched, xs, gs, w1, w3, w2):
    rows, d = xs.shape
    ff = w1.shape[1]
    grid_spec = pltpu.PrefetchScalarGridSpec(
        num_scalar_prefetch=3,
        grid=(rows // MOE_TM,),
        in_specs=[
            pl.BlockSpec((MOE_TM, d), lambda j, e_s, b_s, v_s: (b_s[j], 0)),
            pl.BlockSpec((MOE_TM, LANES), lambda j, e_s, b_s, v_s: (b_s[j], 0)),
            pl.BlockSpec((d, ff), lambda j, e_s, b_s, v_s: (e_s[j], 0)),
            pl.BlockSpec((d, ff), lambda j, e_s, b_s, v_s: (e_s[j], 0)),
            pl.BlockSpec((ff, d), lambda j, e_s, b_s, v_s: (e_s[j], 0)),
        ],
        out_specs=pl.BlockSpec((MOE_TM, d), lambda j, e_s, b_s, v_s: (j, 0)),
    )
    return pl.pallas_call(
        _expert_kernel,
        grid_spec=grid_spec,
        out_shape=jax.ShapeDtypeStruct((rows, d), BF16),
        compiler_params=_cparams(("arbitrary",)),
        name="moe_experts",
    )(*tile_sched, xs, gs, w1, w3, w2)


def _combine_kernel(segoff_s, base_s, len_s, x_ref, mod_ref, meta_ref, segoff_ref, y_hbm, o_ref, ybuf2, sems):
    i = pl.program_id(0)
    last = pl.num_programs(0) - 1
    slot_i = i % 2

    def copies_of(tile, slot):
        out = []
        for e in range(N_EXPERTS):
            k = tile * N_EXPERTS + e
            out += _segment_copies(y_hbm, ybuf2.at[slot], base_s[k], segoff_s[k], len_s[k], sems.at[slot])
        return out

    @pl.when(i == 0)
    def _():
        ybuf2[...] = jnp.zeros_like(ybuf2)
        _start_copies(copies_of(i, slot_i))

    @pl.when(i < last)
    def _():
        _start_copies(copies_of(i + 1, 1 - slot_i))

    _wait_copies(copies_of(i, slot_i))
    s1, s2 = _pair_slots(meta_ref[...], segoff_ref[0])
    slot = lax.broadcasted_iota(jnp.int32, (MOE_TM, CBUF_ROWS), 1).astype(F32)
    pick = (jnp.where(slot == s1, 1.0, 0.0) + jnp.where(slot == s2, 1.0, 0.0)).astype(BF16)
    o_ref[...] = x_ref[...] + mod_ref[0, 5:6, :] * _dot(pick, ybuf2[slot_i])


def _combine_call(sched, x2d, mod, tiles_per_mod, meta, segoff_v, y):
    n, d = x2d.shape
    grid_spec = pltpu.PrefetchScalarGridSpec(
        num_scalar_prefetch=3,
        grid=(n // MOE_TM,),
        in_specs=[
            pl.BlockSpec((MOE_TM, d), lambda i, *_: (i, 0)),
            pl.BlockSpec((1, MOD_ROWS, d), lambda i, *_: (i // tiles_per_mod, 0, 0)),
            pl.BlockSpec((MOE_TM, LANES), lambda i, *_: (i, 0)),
            pl.BlockSpec((1, 1, LANES), lambda i, *_: (i, 0, 0)),
            pl.BlockSpec(memory_space=pl.ANY),
        ],
        out_specs=pl.BlockSpec((MOE_TM, d), lambda i, *_: (i, 0)),
        scratch_shapes=[pltpu.VMEM((2, CBUF_ROWS, d), BF16), pltpu.SemaphoreType.DMA((2,))],
    )
    return pl.pallas_call(
        _combine_kernel,
        grid_spec=grid_spec,
        out_shape=jax.ShapeDtypeStruct((n, d), F32),
        compiler_params=_cparams(("arbitrary",)),
        name="moe_combine",
    )(*sched, x2d, mod, meta, segoff_v, y)


def _moe_call(x2d, mod, tiles_per_mod, outs, w_out, g, wr, br, w1, w3, w2):
    n, d = x2d.shape
    nt = n // MOE_TM
    utri = jnp.asarray(np.triu(np.ones((MOE_TM, MOE_TM), np.float32), 1), BF16)
    x2d, h, meta, metat, counts = _router_call(x2d, mod, tiles_per_mod, *outs, w_out, g, wr, br, utri)

    cnt = counts[:, :, 0].astype(jnp.int32)
    seg_len = (cnt + SEG_ALIGN - 1) // SEG_ALIGN * SEG_ALIGN
    segoff = jnp.cumsum(seg_len, axis=1) - seg_len
    region = (jnp.sum(seg_len, axis=0) + MOE_TM - 1) // MOE_TM * MOE_TM
    region_start = jnp.cumsum(region) - region
    base = region_start[None, :] + jnp.cumsum(seg_len, axis=0) - seg_len
    rows_max = 2 * n + nt * N_EXPERTS * (SEG_ALIGN - 1) + N_EXPERTS * (MOE_TM - 1)
    n_sorted_tiles = (rows_max + MOE_TM - 1) // MOE_TM
    tile_end = jnp.cumsum(region // MOE_TM)
    total_tiles = tile_end[-1]
    jt = jnp.arange(n_sorted_tiles, dtype=jnp.int32)
    blk = jnp.minimum(jt, total_tiles - 1)
    tile_expert = jnp.sum((blk[:, None] >= tile_end[None, :]).astype(jnp.int32), axis=1)
    tile_sched = (tile_expert.astype(jnp.int32), blk.astype(jnp.int32), (jt < total_tiles).astype(jnp.int32))
    sched = tuple(a.reshape(-1).astype(jnp.int32) for a in (segoff, base, seg_len))
    segoff_v = jnp.pad(segoff.astype(F32), ((0, 0), (0, LANES - N_EXPERTS)))[:, None, :]

    total = jnp.sum(seg_len, axis=0)
    fill = jnp.concatenate([region_start + total, region - total,
                            jnp.stack([total_tiles * MOE_TM, n_sorted_tiles - total_tiles])]).astype(jnp.int32)
    xs, gs = _compact_call(sched, fill, h, meta, metat, n_sorted_tiles * MOE_TM)
    y = _expert_call(tile_sched, xs, gs, w1, w3, w2)
    return _combine_call(sched, x2d, mod, tiles_per_mod, meta, segoff_v, y)


def _rope_tables(t):
    rows = jnp.arange(t, dtype=F32) // GRID_W
    cols = jnp.arange(t, dtype=F32) % GRID_W

    def tables(rot_dim):
        a = rot_dim // 2
        inv = 1.0 / (ROPE_BASE ** (jnp.arange(0, a, 2, dtype=F32) / a))
        ar = rows[:, None] * inv
        ac = cols[:, None] * inv
        cos = jnp.concatenate([jnp.cos(ar), jnp.cos(ar), jnp.cos(ac), jnp.cos(ac)], axis=-1)
        sin = jnp.concatenate([-jnp.sin(ar), jnp.sin(ar), -jnp.sin(ac), jnp.sin(ac)], axis=-1)
        return cos, sin

    ca, sa = tables(HEAD_DIM)
    cos_a = jnp.tile(ca, (1, LANES // HEAD_DIM))
    sin_a = jnp.tile(sa, (1, LANES // HEAD_DIM))
    cc, sc = tables(C_ROPE)
    ones = jnp.ones((t, C_NOPE), F32)
    tail = C_SLOT - C_NOPE - C_ROPE
    cos_c = jnp.concatenate([ones, cc, jnp.ones((t, tail), F32)], axis=-1)
    sin_c = jnp.concatenate([0 * ones, sc, jnp.zeros((t, tail), F32)], axis=-1)
    return cos_a, sin_a, cos_c, sin_c


def _head_perm():
    order = []
    for j in range(A_HEADS // 2):
        order += [j, A_HEADS // 2 + j]
    return np.concatenate([np.arange(h * HEAD_DIM, (h + 1) * HEAD_DIM) for h in order])


def _segment_mean_matrix(widths, total):
    m = np.zeros((total, total), np.float32)
    o = 0
    while o < total:
        for w, used in widths:
            if used:
                m[o:o + w, o:o + w] = 1.0 / w
            o += w
    return jnp.asarray(m, BF16)


def _slot_vec(nope, rope):
    z = jnp.zeros((C_SLOT - C_NOPE - C_ROPE,), F32)
    n = jnp.zeros((C_NOPE,), F32) if nope is None else nope
    r = jnp.zeros((C_ROPE,), F32) if rope is None else rope
    return jnp.tile(jnp.concatenate([n, r, z]), C_HEADS)[None, :]


def _layer_consts(i, p, perm, tabs):
    w_in = p["w_in"][i]
    o_kr = A_Q + 2 * A_KV + 2 * B_CH + C_Q_RANK + C_KV_RANK
    d = w_in.shape[0]
    kr_cols = jnp.concatenate([jnp.zeros((d, C_NOPE), F32), w_in[:, o_kr:o_kr + C_ROPE],
                               jnp.zeros((d, C_SLOT - C_NOPE - C_ROPE), F32)], axis=1)
    win = jnp.concatenate([w_in[:, :A_Q][:, perm], w_in[:, A_Q:o_kr], kr_cols], axis=1).astype(BF16)

    w_uq = p["c_w_uq"][i].reshape(C_Q_RANK, C_HEADS, C_NOPE + C_ROPE)
    wuq = jnp.pad(w_uq, ((0, 0), (0, 0), (0, C_SLOT - C_NOPE - C_ROPE))).reshape(C_Q_RANK, C_HEADS * C_SLOT)
    w_ukv = p["c_w_ukv"][i].reshape(C_KV_RANK, C_HEADS, C_NOPE + C_V)
    wukvk = jnp.pad(w_ukv[:, :, :C_NOPE], ((0, 0), (0, 0), (0, C_SLOT - C_NOPE))).reshape(C_KV_RANK, -1)
    wukvv = w_ukv[:, :, C_NOPE:].reshape(C_KV_RANK, C_HEADS * C_V)

    sa = _segment_mean_matrix([(HEAD_DIM, True)], MXU_TILE)
    sc = _segment_mean_matrix([(C_NOPE, True), (C_ROPE, True), (C_SLOT - C_NOPE - C_ROPE, False)], MXU_TILE)
    gq = jnp.tile(p["a_q_norm_g"][i] * (A_SCALE * LOG2E), A_HEADS)[None, :]
    gk = jnp.tile(p["a_k_norm_g"][i], A_KV_HEADS)[None, :]
    gqc = _slot_vec(p["c_q_nope_norm_g"][i], p["c_q_rope_norm_g"][i]) * (MLA_SCALE * LOG2E)
    gkn = _slot_vec(p["c_k_nope_norm_g"][i], None)
    gkr = _slot_vec(None, p["c_k_rope_norm_g"][i])[:, :C_SLOT]
    return (p["mix_norm_g"][i][None, :], win) + tabs + (
        sa, sc, gq, gk, p["c_q_rank_norm_g"][i][None, :], p["c_kv_rank_norm_g"][i][None, :],
        wuq.astype(BF16), wukvk.astype(BF16), wukvv.astype(BF16), gqc, gkn, gkr)


def kernel(x, c, ctx, c_ctx, ada_w, ada_b, mix_norm_g, ffn_norm_g, w_in, w_out, a_q_norm_g, a_k_norm_g, a_sink, b_conv_w, b_conv_b, b_ln_g, b_ln_b, c_q_rank_norm_g, c_kv_rank_norm_g, c_w_uq, c_w_ukv, c_q_nope_norm_g, c_k_nope_norm_g, c_q_rope_norm_g, c_k_rope_norm_g, dense_w1, dense_w3, dense_w2, moe_router_w, moe_router_b, moe_w1, moe_w3, moe_w2):
    p = dict(w_in=w_in, c_w_uq=c_w_uq, c_w_ukv=c_w_ukv, a_q_norm_g=a_q_norm_g, a_k_norm_g=a_k_norm_g,
             c_q_nope_norm_g=c_q_nope_norm_g, c_k_nope_norm_g=c_k_nope_norm_g,
             c_q_rope_norm_g=c_q_rope_norm_g, c_k_rope_norm_g=c_k_rope_norm_g,
             c_q_rank_norm_g=c_q_rank_norm_g, c_kv_rank_norm_g=c_kv_rank_norm_g, mix_norm_g=mix_norm_g)
    bsz, t, d = x.shape
    ctx_len = ctx.shape[1]
    depth = ada_w.shape[0]
    n_x, n_c = bsz * t, bsz * ctx_len
    tm_pre = min(4 * PRE_SUB, t, n_c)
    tm_tok = 512
    tq = min(2048, t)
    assert t % tm_pre == 0 and n_c % tm_pre == 0 and t % tm_tok == 0 and n_c % tm_tok == 0 and t % tq == 0
    assert t % GRID_W == 0 and t >= 3 * BLOCK and ctx_len % BLOCK == 0

    ada_rows = ((bsz + 1 + 7) // 8) * 8
    c_pad = jnp.concatenate([c, c_ctx[None, :], jnp.zeros((ada_rows - bsz - 1, d), F32)], axis=0)
    mods = _ada_call(c_pad, ada_w, ada_b).reshape(depth, ada_rows, 6, d)
    mods = jnp.pad(mods, ((0, 0), (0, 0), (0, MOD_ROWS - 6), (0, 0)))

    tabs_x = _rope_tables(t)
    ones = jnp.ones((tm_pre, LANES), F32)
    tabs_c = (ones, 0 * ones, ones, 0 * ones)
    perm = _head_perm()

    x2 = x.reshape(n_x, d)
    c2 = ctx.reshape(n_c, d)
    for i in range(depth):
        last = i == depth - 1
        mod_x = mods[i, :bsz]
        mod_c = mods[i, bsz:bsz + 1]
        consts_x = _layer_consts(i, p, perm, tabs_x)
        consts_c = _layer_consts(i, p, perm, tabs_c)

        qa_x, ka_x, va_x, u_x, qc_x, kc_x, vc_x = _pre_call(x2, mod_x, t // tm_pre, t // tm_pre, consts_x, tm_pre)
        if last:
            ka_c, va_c, kc_c, vc_c = _pre_call(c2, mod_c, n_c // tm_pre, 1, consts_c, tm_pre, kv_only=True)
        else:
            qa_c, ka_c, va_c, u_c, qc_c, kc_c, vc_c = _pre_call(c2, mod_c, n_c // tm_pre, 1, consts_c, tm_pre)

        def r3(a, length):
            return a.reshape(bsz, length, a.shape[-1])

        sink_row = jnp.repeat(a_sink[i] * LOG2E, BLOCK)[None, :]
        conv_w = jnp.pad(b_conv_w[i], ((0, -B_WIDTH % SUBLANES), (0, 0)))
        conv_p = (conv_w, b_conv_b[i][None, :], b_ln_g[i][None, :], b_ln_b[i][None, :])
        w_o = jnp.concatenate([w_out[i][:A_Q][perm], w_out[i][A_Q:]], axis=0).astype(BF16)

        o_a, o_b = _attn_a_call(qa_x, r3(ka_x, t), va_x, r3(ka_c, ctx_len), va_c, sink_row, r3(u_x, t), conv_p, bsz, t)
        o_c = _mla_call(qc_x, r3(kc_x, t), vc_x, r3(kc_c, ctx_len), vc_c, bsz, t, tq)
        if not last:
            oc_a, oc_b = _attn_a_call(qa_c, None, None, r3(ka_c, ctx_len), va_c, sink_row, r3(u_c, ctx_len), conv_p,
                                      bsz, ctx_len)
            oc_c = _mla_call(qc_c, None, None, r3(kc_c, ctx_len), vc_c, bsz, ctx_len, ctx_len)

        j = i // 2
        g_ffn = ffn_norm_g[i][None, :]
        if i % 2 == 0:
            w = (dense_w1[j].astype(BF16), dense_w3[j].astype(BF16), dense_w2[j].astype(BF16))

            def mix(a2, mod, tiles_per_mod, outs, w=w, g_ffn=g_ffn, w_o=w_o):
                return _ffn_call(a2, mod, tiles_per_mod, *outs, w_o, g_ffn, *w, tm_tok)
        else:
            w = tuple(a.astype(BF16).reshape(-1, a.shape[-1]) for a in (moe_w1[j], moe_w3[j], moe_w2[j]))
            wr = jnp.pad(moe_router_w[j], ((0, 0), (0, LANES - N_EXPERTS)))
            br = jnp.pad(moe_router_b[j], (0, LANES - N_EXPERTS))[None, :]

            def mix(a2, mod, tiles_per_mod, outs, w=w, g_ffn=g_ffn, wr=wr, br=br, w_o=w_o):
                return _moe_call(a2, mod, tiles_per_mod * (tm_tok // MOE_TM), outs, w_o, g_ffn, wr, br, *w)
        x2 = mix(x2, mod_x, t // tm_tok, (o_a, o_b, o_c))
        if not last:
            c2 = mix(c2, mod_c, n_c // tm_tok, (oc_a, oc_b, oc_c))
    return x2.reshape(bsz, t, d)
```

```python
import functools

import jax
import jax.numpy as jnp
import numpy as np
from jax import lax
from jax.experimental import pallas as pl
from jax.experimental.pallas import tpu as pltpu

F32 = jnp.float32
BF16 = jnp.bfloat16

GRID_W = 64
HEAD_DIM = 64
A_HEADS = 8
A_KV_HEADS = 2
A_WINDOW = 128
BLOCK = 128
B_CH = 256
B_WIDTH = 31
C_HEADS = 4
C_Q_RANK = 384
C_KV_RANK = 256
C_NOPE = 64
C_ROPE = 32
C_V = 64
A_Q = A_HEADS * HEAD_DIM
A_KV = A_KV_HEADS * HEAD_DIM
N_EXPERTS = 8
ROPE_BASE = 10000.0
EPS = 1e-6
NEG = -1e30
A_SCALE = HEAD_DIM ** -0.5
MLA_SCALE = (C_NOPE + C_ROPE) ** -0.5
LOG2E = 1.4426950408889634

LANES = 128
MXU_TILE = 256
MOD_ROWS = 8
C_SLOT = 128
VMEM_LIMIT = 56 * 1024 * 1024


def _cparams(sem):
    return pltpu.CompilerParams(dimension_semantics=sem, vmem_limit_bytes=VMEM_LIMIT)


def _dot(a, b):
    return jnp.dot(a, b, preferred_element_type=F32)


def _dot_nt(a, b):
    return lax.dot_general(a, b, (((1,), (1,)), ((), ())), preferred_element_type=F32)


def _ada_kernel(c_ref, w_ref, b_ref, o_ref):
    c = c_ref[...]
    a = c * jax.nn.sigmoid(c)
    o_ref[0] = _dot(a.astype(BF16), w_ref[0].astype(BF16)) + b_ref[0]


def _ada_call(c_pad, ada_w, ada_b):
    depth, d, n6 = ada_w.shape
    rows = c_pad.shape[0]
    tn = 1536
    return pl.pallas_call(
        _ada_kernel,
        grid=(depth, n6 // tn),
        in_specs=[
            pl.BlockSpec((rows, d), lambda i, j: (0, 0)),
            pl.BlockSpec((1, d, tn), lambda i, j: (i, 0, j)),
            pl.BlockSpec((1, 1, tn), lambda i, j: (i, 0, j)),
        ],
        out_specs=pl.BlockSpec((1, rows, tn), lambda i, j: (i, 0, j)),
        out_shape=jax.ShapeDtypeStruct((depth, rows, n6), F32),
        compiler_params=_cparams(("parallel", "parallel")),
        name="ada_proj",
    )(c_pad, ada_w, ada_b.reshape(depth, 1, n6))


def _rope_chunk(c, cos, sin, half):
    lane = lax.broadcasted_iota(jnp.int32, c.shape, 1)
    lo = (lane & (2 * half - 1)) < half
    partner = jnp.where(lo, pltpu.roll(c, LANES - half, 1), pltpu.roll(c, half, 1))
    return c * cos + partner * sin


PRE_SUB = 256


def _segment_mean(sq, s_ref):
    w = s_ref.shape[0]
    return jnp.concatenate([_dot(sq[:, c:c + w].astype(BF16), s_ref[...]) for c in range(0, sq.shape[1], w)], axis=1)


def _pre_kernel(x_ref, mod_ref, gmix_ref, win_ref, cosa_ref, sina_ref, cosc_ref, sinc_ref,
                sa_ref, sc_ref, gq_ref, gk_ref, gcq_ref, gckv_ref, wuq_ref, wukvk_ref, wukvv_ref,
                gqc_ref, gkn_ref, gkr_ref, *out_refs, kv_only):
    if kv_only:
        ka_ref, va_ref, kc_ref, vc_ref = out_refs
        o_ka, o_ckv = 0, 2 * A_KV
    else:
        qa_ref, ka_ref, va_ref, u_ref, qc_ref, kc_ref, vc_ref = out_refs
        o_ka, o_ckv = A_Q, A_Q + 2 * A_KV + 2 * B_CH + C_Q_RANK
    for sub in range(x_ref.shape[0] // PRE_SUB):
        rows = slice(sub * PRE_SUB, (sub + 1) * PRE_SUB)
        x = x_ref[rows, :]
        ms = jnp.mean(x * x, axis=-1, keepdims=True)
        y = x * lax.rsqrt(ms + EPS)
        shift = mod_ref[0, 0:1, :]
        scale = mod_ref[0, 1:2, :]
        h = (y * gmix_ref[...]) * (1.0 + scale) + shift
        p = _dot(h.astype(BF16), win_ref[...])

        cosa, sina = cosa_ref[rows, :], sina_ref[rows, :]
        cosc, sinc = cosc_ref[rows, :], sinc_ref[rows, :]

        if not kv_only:
            qa = p[:, 0:A_Q]
            ssq = _segment_mean(qa * qa, sa_ref)
            qa = qa * lax.rsqrt(ssq + EPS) * gq_ref[...]
            for j in range(A_Q // LANES):
                sl = slice(j * LANES, (j + 1) * LANES)
                qa_ref[rows, sl] = _rope_chunk(qa[:, sl], cosa, sina, HEAD_DIM // 4).astype(BF16)

            o = A_Q + 2 * A_KV
            u_ref[rows, :] = p[:, o:o + B_CH] * jax.nn.sigmoid(p[:, o + B_CH:o + 2 * B_CH])

            o = o + 2 * B_CH
            cq = p[:, o:o + C_Q_RANK]
            cq = cq * lax.rsqrt(jnp.mean(cq * cq, axis=-1, keepdims=True) + EPS) * gcq_ref[...]
            qc = _dot(cq.astype(BF16), wuq_ref[...])
            ssq = _segment_mean(qc * qc, sc_ref)
            qc = qc * lax.rsqrt(ssq + EPS) * gqc_ref[...]
            for j in range(C_HEADS):
                sl = slice(j * C_SLOT, (j + 1) * C_SLOT)
                qc_ref[rows, sl] = _rope_chunk(qc[:, sl], cosc, sinc, C_ROPE // 4).astype(BF16)

        ka = p[:, o_ka:o_ka + A_KV]
        ssk = _dot((ka * ka).astype(BF16), sa_ref[0:A_KV, 0:A_KV])
        ka = ka * lax.rsqrt(ssk + EPS) * gk_ref[...]
        ka_ref[rows, :] = _rope_chunk(ka, cosa, sina, HEAD_DIM // 4).astype(BF16)
        va_ref[:, rows] = p[:, o_ka + A_KV:o_ka + 2 * A_KV].T.astype(BF16)

        o = o_ckv
        ckv = p[:, o:o + C_KV_RANK]
        ckv = (ckv * lax.rsqrt(jnp.mean(ckv * ckv, axis=-1, keepdims=True) + EPS) * gckv_ref[...]).astype(BF16)
        kn = _dot(ckv, wukvk_ref[...])
        vc_ref[:, rows] = _dot(ckv, wukvv_ref[...]).T.astype(BF16)
        ssk = _segment_mean(kn * kn, sc_ref)
        kn = kn * lax.rsqrt(ssk + EPS) * gkn_ref[...]
        o = o + C_KV_RANK
        kr = p[:, o:o + C_SLOT]
        kr = kr * lax.rsqrt(jnp.sum(kr * kr, axis=-1, keepdims=True) * (1.0 / C_ROPE) + EPS) * gkr_ref[...]
        kr = _rope_chunk(kr, cosc, sinc, C_ROPE // 4)
        for j in range(C_HEADS):
            sl = slice(j * C_SLOT, (j + 1) * C_SLOT)
            kc_ref[rows, sl] = (kn[:, sl] + kr).astype(BF16)


def _pre_call(x2d, mod, tiles_per_mod, tab_tiles, consts, tm, kv_only=False):
    n, d = x2d.shape
    (gmix, win, cosa, sina, cosc, sinc, sa, sc, gq, gk, gcq, gckv, wuq, wukvk, wukvv, gqc, gkn, gkr) = consts
    if kv_only:
        o_ckv = A_Q + 2 * A_KV + 2 * B_CH + C_Q_RANK
        win = jnp.concatenate([win[:, A_Q:A_Q + 2 * A_KV], win[:, o_ckv:]], axis=1)

    def const(a):
        return pl.BlockSpec(a.shape, lambda i: (0,) * a.ndim)

    def tab(a):
        return pl.BlockSpec((tm, LANES), lambda i: (i % tab_tiles, 0))

    in_specs = [
        pl.BlockSpec((tm, d), lambda i: (i, 0)),
        pl.BlockSpec((1, MOD_ROWS, d), lambda i: (i // tiles_per_mod, 0, 0)),
        const(gmix), const(win), tab(cosa), tab(sina), tab(cosc), tab(sinc),
        const(sa), const(sc), const(gq), const(gk), const(gcq), const(gckv),
        const(wuq), const(wukvk), const(wukvv), const(gqc), const(gkn), const(gkr),
    ]
    widths = (A_Q, A_KV, A_KV, B_CH, C_HEADS * C_SLOT, C_HEADS * C_SLOT, C_HEADS * C_V)
    dtypes = (BF16, BF16, BF16, F32, BF16, BF16, BF16)
    out_specs = [pl.BlockSpec((tm, w), lambda i: (i, 0)) for w in widths]
    out_shape = [jax.ShapeDtypeStruct((n, w), dt) for w, dt in zip(widths, dtypes)]
    out_specs[2] = pl.BlockSpec((A_KV, tm), lambda i: (0, i))
    out_shape[2] = jax.ShapeDtypeStruct((A_KV, n), BF16)
    out_specs[6] = pl.BlockSpec((C_HEADS * C_V, tm), lambda i: (0, i))
    out_shape[6] = jax.ShapeDtypeStruct((C_HEADS * C_V, n), BF16)
    if kv_only:
        keep = (1, 2, 5, 6)
        out_specs = [out_specs[k] for k in keep]
        out_shape = [out_shape[k] for k in keep]
    return pl.pallas_call(
        functools.partial(_pre_kernel, kv_only=kv_only),
        grid=(n // tm,),
        in_specs=in_specs,
        out_specs=out_specs,
        out_shape=out_shape,
        compiler_params=_cparams(("parallel",)),
        name="pre_attn",
    )(x2d, mod, gmix, win, cosa, sina, cosc, sinc, sa, sc, gq, gk, gcq, gckv, wuq, wukvk, wukvv,
      gqc, gkn, gkr)


CONV_PAD = 16
CONV_CHUNK = 128
SUBLANES = 8


def _conv_chunk(pad_ref, c, w_ref, b_ref, g_ref, beta_ref):
    off = CONV_PAD - B_WIDTH // 2
    nq = (off + B_WIDTH - 1) // SUBLANES + 1
    win = CONV_CHUNK + (nq - 1) * SUBLANES
    base = c * CONV_CHUNK
    acc = jnp.zeros((CONV_CHUNK, B_CH), F32)
    for r in range(SUBLANES):
        taps = [k for k in range(B_WIDTH) if (off + k) % SUBLANES == r]
        if not taps:
            continue
        w_r = pad_ref[base + r:base + r + win, :]
        part = None
        for k in taps:
            q = (off + k) // SUBLANES
            term = w_r[q * SUBLANES:q * SUBLANES + CONV_CHUNK, :] * w_ref[k:k + 1, :]
            part = term if part is None else part + term
        acc = acc + part
    y = acc + b_ref[...]
    mu = jnp.mean(y, axis=-1, keepdims=True)
    yc = y - mu
    var = jnp.mean(yc * yc, axis=-1, keepdims=True)
    z = yc * lax.rsqrt(var + EPS) * g_ref[...] + beta_ref[...]
    return (z * jax.nn.sigmoid(z)).astype(BF16)


ATTN_A_QBLOCKS = 16


ONES_ROWS = 16


def _attn_a_kernel(*refs, t, has_local, qblocks):
    u_ref, uprev_ref, unext_ref, cw_ref, cb_ref, cg_ref, cbeta_ref, o_ref, ob_ref, pad_ref = refs[-10:]
    if has_local:
        q_ref, k_ref, vt_ref, kc_ref, vct_ref, sink_ref, bias_ref = refs[:-10]
    else:
        q_ref, kc_ref, vct_ref, sink_ref = refs[:-10]
    step = pl.program_id(1)
    tq = q_ref.shape[0]
    pad_ref[0:CONV_PAD, :] = jnp.where(step > 0, uprev_ref[0], 0.0)
    pad_ref[CONV_PAD:CONV_PAD + tq, :] = u_ref[0]
    pad_ref[CONV_PAD + tq:CONV_PAD + tq + CONV_PAD, :] = jnp.where(step < pl.num_programs(1) - 1, unext_ref[0], 0.0)
    nchunk = A_Q // LANES
    span = 3 * BLOCK
    lane = lax.broadcasted_iota(jnp.int32, (BLOCK, LANES), 1)
    row = lax.broadcasted_iota(jnp.int32, (A_KV, BLOCK), 0)
    zero = jnp.zeros((BLOCK, LANES), BF16)
    sink = sink_ref[...]
    kc = kc_ref[0]
    vct = vct_ref[...]
    for blk in range(qblocks):
        rows = slice(blk * BLOCK, (blk + 1) * BLOCK)
        chunks = [q_ref[rows, j * LANES:(j + 1) * LANES] for j in range(nchunk)]
        qs = jnp.concatenate([jnp.where(lane < HEAD_DIM, c, zero) for c in chunks]
                             + [jnp.where(lane >= HEAD_DIM, c, zero) for c in chunks], axis=0)
        if has_local:
            n = pl.program_id(1) * qblocks + blk
            start = pl.multiple_of(jnp.clip((n - 1) * BLOCK, 0, t - span), BLOCK)
            keys = jnp.concatenate([k_ref[0, pl.ds(start, span), :], kc], axis=0)
            vt = jnp.concatenate([vt_ref[:, pl.ds(start, span)], vct], axis=1)
        else:
            keys, vt = kc, vct
        s = _dot_nt(keys, qs)
        if has_local:
            s = jnp.concatenate([s[:span] + bias_ref[n - start // BLOCK], s[span:]], axis=0)
        m = jnp.maximum(jnp.max(s, axis=0, keepdims=True), sink)
        e = jnp.exp2(s - m).astype(BF16)
        vte = jnp.concatenate([vt, jnp.ones((ONES_ROWS, vt.shape[1]), BF16)], axis=0)
        acc = _dot(vte, e)
        den = acc[A_KV:A_KV + 1] + jnp.exp2(sink - m)
        out = acc[:A_KV] * (1.0 / den)
        for j in range(nchunk):
            x = jnp.where(row < HEAD_DIM, out[:, j * BLOCK:(j + 1) * BLOCK],
                          out[:, (nchunk + j) * BLOCK:(nchunk + j + 1) * BLOCK])
            o_ref[rows, j * LANES:(j + 1) * LANES] = x.T.astype(BF16)
        ob_ref[rows, :] = _conv_chunk(pad_ref, blk, cw_ref, cb_ref, cg_ref, cbeta_ref)


def _window_bias():
    r = np.arange(BLOCK)[None, :]
    c = np.arange(3 * BLOCK)[:, None]
    pats = [np.where(np.abs(c - p * BLOCK - r) <= A_WINDOW, 0.0, NEG) for p in range(3)]
    return jnp.asarray(np.stack([np.tile(p, (1, A_HEADS)) for p in pats]), F32)


def _attn_a_call(qa, ka, vat, kac, vact, sink_row, u3, conv_p, bsz, t):
    has_local = ka is not None
    qblocks = min(ATTN_A_QBLOCKS, t // BLOCK)
    tq = qblocks * BLOCK
    nq = t // tq
    ctx_len = kac.shape[1]
    in_specs = [pl.BlockSpec((tq, A_Q), lambda b, n: (b * nq + n, 0))]
    args = [qa]
    if has_local:
        in_specs += [pl.BlockSpec((1, t, A_KV), lambda b, n: (b, 0, 0)),
                     pl.BlockSpec((A_KV, t), lambda b, n: (0, b))]
        args += [ka, vat]
    in_specs += [pl.BlockSpec((1, ctx_len, A_KV), lambda b, n: (b, 0, 0)),
                 pl.BlockSpec((A_KV, ctx_len), lambda b, n: (0, b))]
    in_specs += [pl.BlockSpec(sink_row.shape, lambda b, n: (0, 0))]
    args += [kac, vact, sink_row]
    if has_local:
        bias = _window_bias()
        in_specs += [pl.BlockSpec(bias.shape, lambda b, n: (0, 0, 0))]
        args += [bias]
    halo_per_step = tq // CONV_PAD
    n_halo = t // CONV_PAD
    in_specs += [
        pl.BlockSpec((1, tq, B_CH), lambda b, n: (b, n, 0)),
        pl.BlockSpec((1, CONV_PAD, B_CH), lambda b, n: (b, jnp.maximum(n * halo_per_step - 1, 0), 0)),
        pl.BlockSpec((1, CONV_PAD, B_CH), lambda b, n: (b, jnp.minimum((n + 1) * halo_per_step, n_halo - 1), 0)),
    ] + [pl.BlockSpec(a.shape, lambda b, n: (0, 0)) for a in conv_p]
    args += [u3, u3, u3, *conv_p]
    return pl.pallas_call(
        functools.partial(_attn_a_kernel, t=t, has_local=has_local, qblocks=qblocks),
        grid=(bsz, nq),
        in_specs=in_specs,
        out_specs=[pl.BlockSpec((tq, A_Q), lambda b, n: (b * nq + n, 0)),
                   pl.BlockSpec((tq, B_CH), lambda b, n: (b * nq + n, 0))],
        out_shape=[jax.ShapeDtypeStruct((bsz * t, A_Q), BF16), jax.ShapeDtypeStruct((bsz * t, B_CH), BF16)],
        scratch_shapes=[pltpu.VMEM((tq + 2 * CONV_PAD, B_CH), F32)],
        compiler_params=_cparams(("parallel", "parallel")),
        name="attn_ab_local" if has_local else "attn_ab_ctx",
    )(*args)


def _mla_kernel(*refs, has_local):
    if has_local:
        q_ref, kx_ref, vxt_ref, kc_ref, vct_ref, o_ref = refs
    else:
        q_ref, kc_ref, vct_ref, o_ref = refs
    def scores(h):
        sl = slice(h * C_SLOT, (h + 1) * C_SLOT)
        q = q_ref[:, sl]
        s_c = _dot_nt(kc_ref[0, :, sl], q)
        s_x = _dot_nt(kx_ref[0, :, sl], q) if has_local else None
        return s_c, s_x

    outs = []
    nxt = scores(0)
    for h in range(C_HEADS):
        vs = slice(h * C_V, (h + 1) * C_V)
        s_c, s_x = nxt
        if h + 1 < C_HEADS:
            nxt = scores(h + 1)
        m = jnp.max(s_c, axis=0, keepdims=True)
        if has_local:
            m = jnp.maximum(m, jnp.max(s_x, axis=0, keepdims=True))
        e_c = jnp.exp2(s_c - m).astype(BF16)
        vt = jnp.concatenate([vct_ref[vs, :], jnp.ones((ONES_ROWS, e_c.shape[0]), BF16)], axis=0)
        acc = _dot(vt, e_c)
        if has_local:
            e_x = jnp.exp2(s_x - m).astype(BF16)
            vt = jnp.concatenate([vxt_ref[vs, :], jnp.ones((ONES_ROWS, e_x.shape[0]), BF16)], axis=0)
            acc = acc + _dot(vt, e_x)
        outs.append(acc[:C_V] * (1.0 / acc[C_V:C_V + 1]))
    o_ref[...] = jnp.concatenate(outs, axis=0).T.astype(BF16)


def _mla_call(qc, kx, vxt, kcc, vcct, bsz, t, tq):
    has_local = kx is not None
    nq = t // tq
    ctx_len = kcc.shape[1]
    wq = C_HEADS * C_SLOT
    wv = C_HEADS * C_V
    in_specs = [pl.BlockSpec((tq, wq), lambda b, n: (b * nq + n, 0))]
    args = [qc]
    if has_local:
        in_specs += [pl.BlockSpec((1, t, wq), lambda b, n: (b, 0, 0)),
                     pl.BlockSpec((wv, t), lambda b, n: (0, b))]
        args += [kx, vxt]
    in_specs += [pl.BlockSpec((1, ctx_len, wq), lambda b, n: (b, 0, 0)),
                 pl.BlockSpec((wv, ctx_len), lambda b, n: (0, b))]
    args += [kcc, vcct]
    return pl.pallas_call(
        functools.partial(_mla_kernel, has_local=has_local),
        grid=(bsz, nq),
        in_specs=in_specs,
        out_specs=pl.BlockSpec((tq, wv), lambda b, n: (b * nq + n, 0)),
        out_shape=jax.ShapeDtypeStruct((bsz * t, wv), BF16),
        compiler_params=_cparams(("parallel", "parallel")),
        name="mla_local" if has_local else "mla_ctx",
    )(*args)


def _mixer_residual(x_ref, mod_ref, oa_ref, ob_ref, oc_ref, wo_ref):
    y = _dot(oa_ref[...], wo_ref[0:A_Q, :])
    y = y + _dot(ob_ref[...], wo_ref[A_Q:A_Q + B_CH, :])
    y = y + _dot(oc_ref[...], wo_ref[A_Q + B_CH:, :])
    return x_ref[...] + mod_ref[0, 2:3, :] * y


def _mixer_specs(tm, d, tiles_per_mod, oa, ob, oc, w_out):
    return [
        pl.BlockSpec((tm, d), lambda i: (i, 0)),
        pl.BlockSpec((1, MOD_ROWS, d), lambda i: (i // tiles_per_mod, 0, 0)),
        pl.BlockSpec((tm, oa.shape[1]), lambda i: (i, 0)),
        pl.BlockSpec((tm, ob.shape[1]), lambda i: (i, 0)),
        pl.BlockSpec((tm, oc.shape[1]), lambda i: (i, 0)),
        pl.BlockSpec(w_out.shape, lambda i: (0, 0), pipeline_mode=pl.Buffered(1)),
    ]


def _split_bf16(a):
    hi = a.astype(BF16)
    lo = (a - hi.astype(F32)).astype(BF16)
    return hi, lo


def _ffn_input(x, mod_ref, g_ref):
    ms = jnp.mean(x * x, axis=-1, keepdims=True)
    y = x * lax.rsqrt(ms + EPS)
    return (y * g_ref[...]) * (1.0 + mod_ref[0, 4:5, :]) + mod_ref[0, 3:4, :]


def _ff_chunks(ff):
    tiles = ff // MXU_TILE
    if ff % MXU_TILE or tiles < 3:
        return (ff,)
    third = (tiles // 3) * MXU_TILE
    return (third, third, ff - 2 * third)


def _swiglu(h, w1_ref, w3_ref, w2_ref):
    y = None
    o = 0
    for tf in _ff_chunks(w1_ref.shape[1]):
        a = _dot(h, w1_ref[:, o:o + tf])
        b = _dot(h, w3_ref[:, o:o + tf])
        g = (a * jax.nn.sigmoid(a) * b).astype(BF16)
        yc = _dot(g, w2_ref[o:o + tf, :])
        y = yc if y is None else y + yc
        o += tf
    return y


def _ffn_kernel(x_ref, mod_ref, oa_ref, ob_ref, oc_ref, wo_ref, g_ref, w1_ref, w3_ref, w2_ref, o_ref):
    x = _mixer_residual(x_ref, mod_ref, oa_ref, ob_ref, oc_ref, wo_ref)
    h = _ffn_input(x, mod_ref, g_ref).astype(BF16)
    o_ref[...] = x + mod_ref[0, 5:6, :] * _swiglu(h, w1_ref, w3_ref, w2_ref)


def _ffn_call(x2d, mod, tiles_per_mod, oa, ob, oc, w_out, g, w1, w3, w2, tm):
    n, d = x2d.shape

    def resident(a):
        return pl.BlockSpec(a.shape, lambda i: (0, 0), pipeline_mode=pl.Buffered(1))

    return pl.pallas_call(
        _ffn_kernel,
        grid=(n // tm,),
        in_specs=_mixer_specs(tm, d, tiles_per_mod, oa, ob, oc, w_out) + [
            pl.BlockSpec(g.shape, lambda i: (0, 0)), resident(w1), resident(w3), resident(w2)],
        out_specs=pl.BlockSpec((tm, d), lambda i: (i, 0)),
        out_shape=jax.ShapeDtypeStruct((n, d), F32),
        compiler_params=_cparams(("parallel",)),
        name="ffn_dense",
    )(x2d, mod, oa, ob, oc, w_out, g, w1, w3, w2)


MOE_TM = 512
SEG_ALIGN = 16
SEG_PIECES = (512, 256, 128, 64, 32, 16)
CBUF_ROWS = 2 * MOE_TM + N_EXPERTS * SEG_ALIGN
META_I1, META_I2, META_G1, META_G2, META_R1, META_R2 = range(6)
META_ROWS = 8


def _router_kernel(x_ref, mod_ref, oa_ref, ob_ref, oc_ref, wo_ref, g_ref, wr_ref, br_ref, utri_ref,
                   x1_ref, h_ref, meta_ref, metat_ref, cnt_ref):
    x = _mixer_residual(x_ref, mod_ref, oa_ref, ob_ref, oc_ref, wo_ref)
    x1_ref[...] = x
    h = _ffn_input(x, mod_ref, g_ref)
    h_ref[...] = h.astype(BF16)
    h_hi, h_lo = _split_bf16(h)
    w_hi, w_lo = _split_bf16(wr_ref[...])
    logits = _dot(h_hi, w_hi) + (_dot(h_lo, w_hi) + _dot(h_hi, w_lo)) + br_ref[...]
    lt = logits.T[0:N_EXPERTS, :]
    row = lax.broadcasted_iota(jnp.int32, lt.shape, 0).astype(F32)
    m1 = jnp.max(lt, axis=0, keepdims=True)
    i1 = jnp.min(jnp.where(lt == m1, row, float(N_EXPERTS)), axis=0, keepdims=True)
    rest = jnp.where(row == i1, NEG, lt)
    m2 = jnp.max(rest, axis=0, keepdims=True)
    i2 = jnp.min(jnp.where(rest == m2, row, float(N_EXPERTS)), axis=0, keepdims=True)
    e2 = jnp.exp(m2 - m1)
    den = 1.0 + e2
    sel1 = jnp.where(row == i1, 1.0, 0.0)
    sel2 = jnp.where(row == i2, 1.0, 0.0)
    sel = sel1 + sel2
    before = _dot(sel.astype(BF16), utri_ref[...])
    r1 = jnp.sum(before * sel1, axis=0, keepdims=True)
    r2 = jnp.sum(before * sel2, axis=0, keepdims=True)
    cnt_ref[0] = jnp.broadcast_to(jnp.sum(sel, axis=1, keepdims=True), (N_EXPERTS, LANES))
    zero = jnp.zeros_like(m1)
    metat = jnp.concatenate([i1, i2, 1.0 / den, e2 / den, r1, r2, zero, zero], axis=0)
    metat_ref[...] = metat
    pad = jnp.zeros((LANES - META_ROWS, metat.shape[1]), F32)
    meta_ref[...] = jnp.concatenate([metat, pad], axis=0).T


def _router_call(x2d, mod, tiles_per_mod, oa, ob, oc, w_out, g, wr, br, utri):
    n, d = x2d.shape
    tm = MOE_TM
    nt = n // tm
    return pl.pallas_call(
        _router_kernel,
        grid=(nt,),
        in_specs=_mixer_specs(tm, d, tiles_per_mod, oa, ob, oc, w_out) + [
            pl.BlockSpec(g.shape, lambda i: (0, 0)),
            pl.BlockSpec(wr.shape, lambda i: (0, 0)),
            pl.BlockSpec(br.shape, lambda i: (0, 0)),
            pl.BlockSpec(utri.shape, lambda i: (0, 0)),
        ],
        out_specs=[
            pl.BlockSpec((tm, d), lambda i: (i, 0)),
            pl.BlockSpec((tm, d), lambda i: (i, 0)),
            pl.BlockSpec((tm, LANES), lambda i: (i, 0)),
            pl.BlockSpec((META_ROWS, tm), lambda i: (0, i)),
            pl.BlockSpec((1, N_EXPERTS, LANES), lambda i: (i, 0, 0)),
        ],
        out_shape=[
            jax.ShapeDtypeStruct((n, d), F32),
            jax.ShapeDtypeStruct((n, d), BF16),
            jax.ShapeDtypeStruct((n, LANES), F32),
            jax.ShapeDtypeStruct((META_ROWS, n), F32),
            jax.ShapeDtypeStruct((nt, N_EXPERTS, LANES), F32),
        ],
        compiler_params=_cparams(("parallel",)),
        name="moe_router",
    )(x2d, mod, oa, ob, oc, w_out, g, wr, br, utri)


def _pair_slots(meta, segoff_row):
    lane = lax.broadcasted_iota(jnp.int32, meta.shape, 1).astype(F32)
    i1 = meta[:, META_I1:META_I1 + 1]
    i2 = meta[:, META_I2:META_I2 + 1]
    s1 = jnp.sum(jnp.where(lane == i1, segoff_row, 0.0), axis=-1, keepdims=True) + meta[:, META_R1:META_R1 + 1]
    s2 = jnp.sum(jnp.where(lane == i2, segoff_row, 0.0), axis=-1, keepdims=True) + meta[:, META_R2:META_R2 + 1]
    return s1, s2


def _segment_copies(src, dst, src_off, dst_off, length, sem):
    out = []
    for size in SEG_PIECES:
        done = (length // (2 * size)) * (2 * size)
        s = pl.multiple_of(src_off + done, SEG_ALIGN)
        t = pl.multiple_of(dst_off + done, SEG_ALIGN)
        cp = pltpu.make_async_copy(src.at[pl.ds(s, size)], dst.at[pl.ds(t, size)], sem)
        out.append(((length & size) != 0, cp))
    return out


def _start_copies(copies):
    for pred, cp in copies:
        @pl.when(pred)
        def _(cp=cp):
            cp.start()


def _wait_copies(copies):
    for pred, cp in copies:
        @pl.when(pred)
        def _(cp=cp):
            cp.wait()


def _compact_kernel(segoff_s, base_s, len_s, fill_s, h_ref, meta_ref, metat_ref, xs_out, gs_out,
                    cbuf2, gbuf2, zx, zg, sems):
    i = pl.program_id(0)
    last = pl.num_programs(0) - 1
    slot_i = i % 2
    cbuf = cbuf2.at[slot_i]
    gbuf = gbuf2.at[slot_i]

    def copies_of(tile, slot):
        out = []
        for e in range(N_EXPERTS):
            k = tile * N_EXPERTS + e
            out += _segment_copies(cbuf2.at[slot], xs_out, segoff_s[k], base_s[k], len_s[k], sems.at[0, slot])
            out += _segment_copies(gbuf2.at[slot], gs_out, segoff_s[k], base_s[k], len_s[k], sems.at[1, slot])
        return out

    @pl.when(i >= 2)
    def _():
        _wait_copies(copies_of(i - 2, slot_i))

    mt = metat_ref[...]
    s1 = mt[META_R1:META_R1 + 1]
    s2 = mt[META_R2:META_R2 + 1]
    for e in range(N_EXPERTS):
        off = segoff_s[i * N_EXPERTS + e].astype(F32)
        s1 = s1 + jnp.where(mt[META_I1:META_I1 + 1] == e, off, 0.0)
        s2 = s2 + jnp.where(mt[META_I2:META_I2 + 1] == e, off, 0.0)
    row = lax.broadcasted_iota(jnp.int32, (CBUF_ROWS, MOE_TM), 0).astype(F32)
    p1 = jnp.where(row == s1, 1.0, 0.0)
    p2 = jnp.where(row == s2, 1.0, 0.0)
    cbuf[...] = _dot((p1 + p2).astype(BF16), h_ref[...]).astype(BF16)
    meta = meta_ref[...]
    lane = lax.broadcasted_iota(jnp.int32, meta.shape, 1)
    g1 = meta[:, META_G1:META_G1 + 1]
    g2 = meta[:, META_G2:META_G2 + 1]
    g1_hi = g1.astype(BF16).astype(F32)
    g2_hi = g2.astype(BF16).astype(F32)
    a1 = jnp.where(lane == 0, g1_hi, jnp.where(lane == 1, g1 - g1_hi, 0.0)).astype(BF16)
    a2 = jnp.where(lane == 0, g2_hi, jnp.where(lane == 1, g2 - g2_hi, 0.0)).astype(BF16)
    gbuf[...] = _dot(p1.astype(BF16), a1) + _dot(p2.astype(BF16), a2)
    _start_copies(copies_of(i, slot_i))

    @pl.when(i == last)
    def _():
        @pl.when(i >= 1)
        def _():
            _wait_copies(copies_of(i - 1, 1 - slot_i))
        _wait_copies(copies_of(i, slot_i))
        zx[...] = jnp.zeros_like(zx)
        zg[...] = jnp.zeros_like(zg)
        fills = []
        for e in range(N_EXPERTS):
            fills += _segment_copies(zx, xs_out, 0, fill_s[e], fill_s[N_EXPERTS + e], sems.at[0, 0])
            fills += _segment_copies(zg, gs_out, 0, fill_s[e], fill_s[N_EXPERTS + e], sems.at[1, 0])
        _start_copies(fills)
        _wait_copies(fills)

        def fill_copies(j):
            r = pl.multiple_of(fill_s[2 * N_EXPERTS] + j * MOE_TM, MOE_TM)
            return (pltpu.make_async_copy(zx, xs_out.at[pl.ds(r, MOE_TM)], sems.at[0, 0]),
                    pltpu.make_async_copy(zg, gs_out.at[pl.ds(r, MOE_TM)], sems.at[1, 0]))

        def start_tile(j, carry):
            for cp in fill_copies(j):
                cp.start()
            return carry

        def wait_tile(j, carry):
            for cp in fill_copies(j):
                cp.wait()
            return carry

        lax.fori_loop(0, fill_s[2 * N_EXPERTS + 1], start_tile, 0)
        lax.fori_loop(0, fill_s[2 * N_EXPERTS + 1], wait_tile, 0)


def _compact_call(sched, fill, h, meta, metat, rows):
    n, d = h.shape
    nt = n // MOE_TM
    grid_spec = pltpu.PrefetchScalarGridSpec(
        num_scalar_prefetch=4,
        grid=(nt,),
        in_specs=[
            pl.BlockSpec((MOE_TM, d), lambda i, *_: (i, 0)),
            pl.BlockSpec((MOE_TM, LANES), lambda i, *_: (i, 0)),
            pl.BlockSpec((META_ROWS, MOE_TM), lambda i, *_: (0, i)),
        ],
        out_specs=[pl.BlockSpec(memory_space=pl.ANY), pl.BlockSpec(memory_space=pl.ANY)],
        scratch_shapes=[pltpu.VMEM((2, CBUF_ROWS, d), BF16), pltpu.VMEM((2, CBUF_ROWS, LANES), F32),
                        pltpu.VMEM((MOE_TM, d), BF16), pltpu.VMEM((MOE_TM, LANES), F32),
                        pltpu.SemaphoreType.DMA((2, 2))],
    )
    return pl.pallas_call(
        _compact_kernel,
        grid_spec=grid_spec,
        out_shape=[jax.ShapeDtypeStruct((rows, d), BF16), jax.ShapeDtypeStruct((rows, LANES), F32)],
        compiler_params=_cparams(("arbitrary",)),
        name="moe_compact",
    )(*sched, fill, h, meta, metat)


def _expert_kernel(exp_s, blk_s, valid_s, xs_ref, gs_ref, w1_ref, w3_ref, w2_ref, y_ref):
    del exp_s, blk_s
    j = pl.program_id(0)

    @pl.when(valid_s[j] != 0)
    def _():
        gate = gs_ref[:, 0:1] + gs_ref[:, 1:2]
        y_ref[...] = (_swiglu(xs_ref[...], w1_ref, w3_ref, w2_ref) * gate).astype(BF16)

    @pl.when(valid_s[j] == 0)
    def _():
        y_ref[...] = jnp.zeros_like(y_ref)


def _expert_call(tile_sched, xs, gs, w1, w3, w2):
    rows, d = xs.shape
    ff = w1.shape[1]
    grid_spec = pltpu.PrefetchScalarGridSpec(
        num_scalar_prefetch=3,
        grid=(rows // MOE_TM,),
        in_specs=[
            pl.BlockSpec((MOE_TM, d), lambda j, e_s, b_s, v_s: (b_s[j], 0)),
            pl.BlockSpec((MOE_TM, LANES), lambda j, e_s, b_s, v_s: (b_s[j], 0)),
            pl.BlockSpec((d, ff), lambda j, e_s, b_s, v_s: (e_s[j], 0)),
            pl.BlockSpec((d, ff), lambda j, e_s, b_s, v_s: (e_s[j], 0)),
            pl.BlockSpec((ff, d), lambda j, e_s, b_s, v_s: (e_s[j], 0)),
        ],
        out_specs=pl.BlockSpec((MOE_TM, d), lambda j, e_s, b_s, v_s: (j, 0)),
    )
    return pl.pallas_call(
        _expert_kernel,
        grid_spec=grid_spec,
        out_shape=jax.ShapeDtypeStruct((rows, d), BF16),
        compiler_params=_cparams(("arbitrary",)),
        name="moe_experts",
    )(*tile_sched, xs, gs, w1, w3, w2)


def _combine_kernel(segoff_s, base_s, len_s, x_ref, mod_ref, meta_ref, segoff_ref, y_hbm, o_ref, ybuf2, sems):
    i = pl.program_id(0)
    last = pl.num_programs(0) - 1
    slot_i = i % 2

    def copies_of(tile, slot):
        out = []
        for e in range(N_EXPERTS):
            k = tile * N_EXPERTS + e
            out += _segment_copies(y_hbm, ybuf2.at[slot], base_s[k], segoff_s[k], len_s[k], sems.at[slot])
        return out

    @pl.when(i == 0)
    def _():
        ybuf2[...] = jnp.zeros_like(ybuf2)
        _start_copies(copies_of(i, slot_i))

    @pl.when(i < last)
    def _():
        _start_copies(copies_of(i + 1, 1 - slot_i))

    _wait_copies(copies_of(i, slot_i))
    s1, s2 = _pair_slots(meta_ref[...], segoff_ref[0])
    slot = lax.broadcasted_iota(jnp.int32, (MOE_TM, CBUF_ROWS), 1).astype(F32)
    pick = (jnp.where(slot == s1, 1.0, 0.0) + jnp.where(slot == s2, 1.0, 0.0)).astype(BF16)
    o_ref[...] = x_ref[...] + mod_ref[0, 5:6, :] * _dot(pick, ybuf2[slot_i])


def _combine_call(sched, x2d, mod, tiles_per_mod, meta, segoff_v, y):
    n, d = x2d.shape
    grid_spec = pltpu.PrefetchScalarGridSpec(
        num_scalar_prefetch=3,
        grid=(n // MOE_TM,),
        in_specs=[
            pl.BlockSpec((MOE_TM, d), lambda i, *_: (i, 0)),
            pl.BlockSpec((1, MOD_ROWS, d), lambda i, *_: (i // tiles_per_mod, 0, 0)),
            pl.BlockSpec((MOE_TM, LANES), lambda i, *_: (i, 0)),
            pl.BlockSpec((1, 1, LANES), lambda i, *_: (i, 0, 0)),
            pl.BlockSpec(memory_space=pl.ANY),
        ],
        out_specs=pl.BlockSpec((MOE_TM, d), lambda i, *_: (i, 0)),
        scratch_shapes=[pltpu.VMEM((2, CBUF_ROWS, d), BF16), pltpu.SemaphoreType.DMA((2,))],
    )
    return pl.pallas_call(
        _combine_kernel,
        grid_spec=grid_spec,
        out_shape=jax.ShapeDtypeStruct((n, d), F32),
        compiler_params=_cparams(("arbitrary",)),
        name="moe_combine",
    )(*sched, x2d, mod, meta, segoff_v, y)


def _moe_call(x2d, mod, tiles_per_mod, outs, w_out, g, wr, br, w1, w3, w2):
    n, d = x2d.shape
    nt = n // MOE_TM
    utri = jnp.asarray(np.triu(np.ones((MOE_TM, MOE_TM), np.float32), 1), BF16)
    x2d, h, meta, metat, counts = _router_call(x2d, mod, tiles_per_mod, *outs, w_out, g, wr, br, utri)

    cnt = counts[:, :, 0].astype(jnp.int32)
    seg_len = (cnt + SEG_ALIGN - 1) // SEG_ALIGN * SEG_ALIGN
    segoff = jnp.cumsum(seg_len, axis=1) - seg_len
    region = (jnp.sum(seg_len, axis=0) + MOE_TM - 1) // MOE_TM * MOE_TM
    region_start = jnp.cumsum(region) - region
    base = region_start[None, :] + jnp.cumsum(seg_len, axis=0) - seg_len
    rows_max = 2 * n + nt * N_EXPERTS * (SEG_ALIGN - 1) + N_EXPERTS * (MOE_TM - 1)
    n_sorted_tiles = (rows_max + MOE_TM - 1) // MOE_TM
    tile_end = jnp.cumsum(region // MOE_TM)
    total_tiles = tile_end[-1]
    jt = jnp.arange(n_sorted_tiles, dtype=jnp.int32)
    blk = jnp.minimum(jt, total_tiles - 1)
    tile_expert = jnp.sum((blk[:, None] >= tile_end[None, :]).astype(jnp.int32), axis=1)
    tile_sched = (tile_expert.astype(jnp.int32), blk.astype(jnp.int32), (jt < total_tiles).astype(jnp.int32))
    sched = tuple(a.reshape(-1).astype(jnp.int32) for a in (segoff, base, seg_len))
    segoff_v = jnp.pad(segoff.astype(F32), ((0, 0), (0, LANES - N_EXPERTS)))[:, None, :]

    total = jnp.sum(seg_len, axis=0)
    fill = jnp.concatenate([region_start + total, region - total,
                            jnp.stack([total_tiles * MOE_TM, n_sorted_tiles - total_tiles])]).astype(jnp.int32)
    xs, gs = _compact_call(sched, fill, h, meta, metat, n_sorted_tiles * MOE_TM)
    y = _expert_call(tile_sched, xs, gs, w1, w3, w2)
    return _combine_call(sched, x2d, mod, tiles_per_mod, meta, segoff_v, y)


def _rope_tables(t):
    rows = jnp.arange(t, dtype=F32) // GRID_W
    cols = jnp.arange(t, dtype=F32) % GRID_W

    def tables(rot_dim):
        a = rot_dim // 2
        inv = 1.0 / (ROPE_BASE ** (jnp.arange(0, a, 2, dtype=F32) / a))
        ar = rows[:, None] * inv
        ac = cols[:, None] * inv
        cos = jnp.concatenate([jnp.cos(ar), jnp.cos(ar), jnp.cos(ac), jnp.cos(ac)], axis=-1)
        sin = jnp.concatenate([-jnp.sin(ar), jnp.sin(ar), -jnp.sin(ac), jnp.sin(ac)], axis=-1)
        return cos, sin

    ca, sa = tables(HEAD_DIM)
    cos_a = jnp.tile(ca, (1, LANES // HEAD_DIM))
    sin_a = jnp.tile(sa, (1, LANES // HEAD_DIM))
    cc, sc = tables(C_ROPE)
    ones = jnp.ones((t, C_NOPE), F32)
    tail = C_SLOT - C_NOPE - C_ROPE
    cos_c = jnp.concatenate([ones, cc, jnp.ones((t, tail), F32)], axis=-1)
    sin_c = jnp.concatenate([0 * ones, sc, jnp.zeros((t, tail), F32)], axis=-1)
    return cos_a, sin_a, cos_c, sin_c


def _head_perm():
    order = []
    for j in range(A_HEADS // 2):
        order += [j, A_HEADS // 2 + j]
    return np.concatenate([np.arange(h * HEAD_DIM, (h + 1) * HEAD_DIM) for h in order])


def _segment_mean_matrix(widths, total):
    m = np.zeros((total, total), np.float32)
    o = 0
    while o < total:
        for w, used in widths:
            if used:
                m[o:o + w, o:o + w] = 1.0 / w
            o += w
    return jnp.asarray(m, BF16)


def _slot_vec(nope, rope):
    z = jnp.zeros((C_SLOT - C_NOPE - C_ROPE,), F32)
    n = jnp.zeros((C_NOPE,), F32) if nope is None else nope
    r = jnp.zeros((C_ROPE,), F32) if rope is None else rope
    return jnp.tile(jnp.concatenate([n, r, z]), C_HEADS)[None, :]


def _layer_consts(i, p, perm, tabs):
    w_in = p["w_in"][i]
    o_kr = A_Q + 2 * A_KV + 2 * B_CH + C_Q_RANK + C_KV_RANK
    d = w_in.shape[0]
    kr_cols = jnp.concatenate([jnp.zeros((d, C_NOPE), F32), w_in[:, o_kr:o_kr + C_ROPE],
                               jnp.zeros((d, C_SLOT - C_NOPE - C_ROPE), F32)], axis=1)
    win = jnp.concatenate([w_in[:, :A_Q][:, perm], w_in[:, A_Q:o_kr], kr_cols], axis=1).astype(BF16)

    w_uq = p["c_w_uq"][i].reshape(C_Q_RANK, C_HEADS, C_NOPE + C_ROPE)
    wuq = jnp.pad(w_uq, ((0, 0), (0, 0), (0, C_SLOT - C_NOPE - C_ROPE))).reshape(C_Q_RANK, C_HEADS * C_SLOT)
    w_ukv = p["c_w_ukv"][i].reshape(C_KV_RANK, C_HEADS, C_NOPE + C_V)
    wukvk = jnp.pad(w_ukv[:, :, :C_NOPE], ((0, 0), (0, 0), (0, C_SLOT - C_NOPE))).reshape(C_KV_RANK, -1)
    wukvv = w_ukv[:, :, C_NOPE:].reshape(C_KV_RANK, C_HEADS * C_V)

    sa = _segment_mean_matrix([(HEAD_DIM, True)], MXU_TILE)
    sc = _segment_mean_matrix([(C_NOPE, True), (C_ROPE, True), (C_SLOT - C_NOPE - C_ROPE, False)], MXU_TILE)
    gq = jnp.tile(p["a_q_norm_g"][i] * (A_SCALE * LOG2E), A_HEADS)[None, :]
    gk = jnp.tile(p["a_k_norm_g"][i], A_KV_HEADS)[None, :]
    gqc = _slot_vec(p["c_q_nope_norm_g"][i], p["c_q_rope_norm_g"][i]) * (MLA_SCALE * LOG2E)
    gkn = _slot_vec(p["c_k_nope_norm_g"][i], None)
    gkr = _slot_vec(None, p["c_k_rope_norm_g"][i])[:, :C_SLOT]
    return (p["mix_norm_g"][i][None, :], win) + tabs + (
        sa, sc, gq, gk, p["c_q_rank_norm_g"][i][None, :], p["c_kv_rank_norm_g"][i][None, :],
        wuq.astype(BF16), wukvk.astype(BF16), wukvv.astype(BF16), gqc, gkn, gkr)


def kernel(x, c, ctx, c_ctx, ada_w, ada_b, mix_norm_g, ffn_norm_g, w_in, w_out, a_q_norm_g, a_k_norm_g, a_sink, b_conv_w, b_conv_b, b_ln_g, b_ln_b, c_q_rank_norm_g, c_kv_rank_norm_g, c_w_uq, c_w_ukv, c_q_nope_norm_g, c_k_nope_norm_g, c_q_rope_norm_g, c_k_rope_norm_g, dense_w1, dense_w3, dense_w2, moe_router_w, moe_router_b, moe_w1, moe_w3, moe_w2):
    p = dict(w_in=w_in, c_w_uq=c_w_uq, c_w_ukv=c_w_ukv, a_q_norm_g=a_q_norm_g, a_k_norm_g=a_k_norm_g,
             c_q_nope_norm_g=c_q_nope_norm_g, c_k_nope_norm_g=c_k_nope_norm_g,
             c_q_rope_norm_g=c_q_rope_norm_g, c_k_rope_norm_g=c_k_rope_norm_g,
             c_q_rank_norm_g=c_q_rank_norm_g, c_kv_rank_norm_g=c_kv_rank_norm_g, mix_norm_g=mix_norm_g)
    bsz, t, d = x.shape
    ctx_len = ctx.shape[1]
    depth = ada_w.shape[0]
    n_x, n_c = bsz * t, bsz * ctx_len
    tm_pre = min(4 * PRE_SUB, t, n_c)
    tm_tok = 512
    tq = min(2048, t)
    assert t % tm_pre == 0 and n_c % tm_pre == 0 and t % tm_tok == 0 and n_c % tm_tok == 0 and t % tq == 0
    assert t % GRID_W == 0 and t >= 3 * BLOCK and ctx_len % BLOCK == 0

    ada_rows = ((bsz + 1 + 7) // 8) * 8
    c_pad = jnp.concatenate([c, c_ctx[None, :], jnp.zeros((ada_rows - bsz - 1, d), F32)], axis=0)
    mods = _ada_call(c_pad, ada_w, ada_b).reshape(depth, ada_rows, 6, d)
    mods = jnp.pad(mods, ((0, 0), (0, 0), (0, MOD_ROWS - 6), (0, 0)))

    tabs_x = _rope_tables(t)
    ones = jnp.ones((tm_pre, LANES), F32)
    tabs_c = (ones, 0 * ones, ones, 0 * ones)
    perm = _head_perm()

    x2 = x.reshape(n_x, d)
    c2 = ctx.reshape(n_c, d)
    for i in range(depth):
        last = i == depth - 1
        mod_x = mods[i, :bsz]
        mod_c = mods[i, bsz:bsz + 1]
        consts_x = _layer_consts(i, p, perm, tabs_x)
        consts_c = _layer_consts(i, p, perm, tabs_c)

        qa_x, ka_x, va_x, u_x, qc_x, kc_x, vc_x = _pre_call(x2, mod_x, t // tm_pre, t // tm_pre, consts_x, tm_pre)
        if last:
            ka_c, va_c, kc_c, vc_c = _pre_call(c2, mod_c, n_c // tm_pre, 1, consts_c, tm_pre, kv_only=True)
        else:
            qa_c, ka_c, va_c, u_c, qc_c, kc_c, vc_c = _pre_call(c2, mod_c, n_c // tm_pre, 1, consts_c, tm_pre)

        def r3(a, length):
            return a.reshape(bsz, length, a.shape[-1])

        sink_row = jnp.repeat(a_sink[i] * LOG2E, BLOCK)[None, :]
        conv_w = jnp.pad(b_conv_w[i], ((0, -B_WIDTH % SUBLANES), (0, 0)))
        conv_p = (conv_w, b_conv_b[i][None, :], b_ln_g[i][None, :], b_ln_b[i][None, :])
        w_o = jnp.concatenate([w_out[i][:A_Q][perm], w_out[i][A_Q:]], axis=0).astype(BF16)

        o_a, o_b = _attn_a_call(qa_x, r3(ka_x, t), va_x, r3(ka_c, ctx_len), va_c, sink_row, r3(u_x, t), conv_p, bsz, t)
        o_c = _mla_call(qc_x, r3(kc_x, t), vc_x, r3(kc_c, ctx_len), vc_c, bsz, t, tq)
        if not last:
            oc_a, oc_b = _attn_a_call(qa_c, None, None, r3(ka_c, ctx_len), va_c, sink_row, r3(u_c, ctx_len), conv_p,
                                      bsz, ctx_len)
            oc_c = _mla_call(qc_c, None, None, r3(kc_c, ctx_len), vc_c, bsz, ctx_len, ctx_len)

        j = i // 2
        g_ffn = ffn_norm_g[i][None, :]
        if i % 2 == 0:
            w = (dense_w1[j].astype(BF16), dense_w3[j].astype(BF16), dense_w2[j].astype(BF16))

            def mix(a2, mod, tiles_per_mod, outs, w=w, g_ffn=g_ffn, w_o=w_o):
                return _ffn_call(a2, mod, tiles_per_mod, *outs, w_o, g_ffn, *w, tm_tok)
        else:
            w = tuple(a.astype(BF16).reshape(-1, a.shape[-1]) for a in (moe_w1[j], moe_w3[j], moe_w2[j]))
            wr = jnp.pad(moe_router_w[j], ((0, 0), (0, LANES - N_EXPERTS)))
            br = jnp.pad(moe_router_b[j], (0, LANES - N_EXPERTS))[None, :]

            def mix(a2, mod, tiles_per_mod, outs, w=w, g_ffn=g_ffn, wr=wr, br=br, w_o=w_o):
                return _moe_call(a2, mod, tiles_per_mod * (tm_tok // MOE_TM), outs, w_o, g_ffn, wr, br, *w)
        x2 = mix(x2, mod_x, t // tm_tok, (o_a, o_b, o_c))
        if not last:
            c2 = mix(c2, mod_c, n_c // tm_tok, (oc_a, oc_b, oc_c))
    return x2.reshape(bsz, t, d)
```

```python
import functools

import jax
import jax.numpy as jnp
import numpy as np
from jax import lax
from jax.experimental import pallas as pl
from jax.experimental.pallas import tpu as pltpu

F32 = jnp.float32
BF16 = jnp.bfloat16

GRID_W = 64
HEAD_DIM = 64
A_HEADS = 8
A_KV_HEADS = 2
A_WINDOW = 128
BLOCK = 128
B_CH = 256
B_WIDTH = 31
C_HEADS = 4
C_Q_RANK = 384
C_KV_RANK = 256
C_NOPE = 64
C_ROPE = 32
C_V = 64
A_Q = A_HEADS * HEAD_DIM
A_KV = A_KV_HEADS * HEAD_DIM
N_EXPERTS = 8
ROPE_BASE = 10000.0
EPS = 1e-6
NEG = -1e30
A_SCALE = HEAD_DIM ** -0.5
MLA_SCALE = (C_NOPE + C_ROPE) ** -0.5
LOG2E = 1.4426950408889634

LANES = 128
MXU_TILE = 256
MOD_ROWS = 8
C_SLOT = 128
VMEM_LIMIT = 56 * 1024 * 1024


def _cparams(sem):
    return pltpu.CompilerParams(dimension_semantics=sem, vmem_limit_bytes=VMEM_LIMIT)


def _dot(a, b):
    return jnp.dot(a, b, preferred_element_type=F32)


def _dot_nt(a, b):
    return lax.dot_general(a, b, (((1,), (1,)), ((), ())), preferred_element_type=F32)


def _ada_kernel(c_ref, w_ref, b_ref, o_ref):
    c = c_ref[...]
    a = c * jax.nn.sigmoid(c)
    o_ref[0] = _dot(a.astype(BF16), w_ref[0].astype(BF16)) + b_ref[0]


def _ada_call(c_pad, ada_w, ada_b):
    depth, d, n6 = ada_w.shape
    rows = c_pad.shape[0]
    tn = 1536
    return pl.pallas_call(
        _ada_kernel,
        grid=(depth, n6 // tn),
        in_specs=[
            pl.BlockSpec((rows, d), lambda i, j: (0, 0)),
            pl.BlockSpec((1, d, tn), lambda i, j: (i, 0, j)),
            pl.BlockSpec((1, 1, tn), lambda i, j: (i, 0, j)),
        ],
        out_specs=pl.BlockSpec((1, rows, tn), lambda i, j: (i, 0, j)),
        out_shape=jax.ShapeDtypeStruct((depth, rows, n6), F32),
        compiler_params=_cparams(("parallel", "parallel")),
        name="ada_proj",
    )(c_pad, ada_w, ada_b.reshape(depth, 1, n6))


def _rope_chunk(c, cos, sin, half):
    lane = lax.broadcasted_iota(jnp.int32, c.shape, 1)
    lo = (lane & (2 * half - 1)) < half
    partner = jnp.where(lo, pltpu.roll(c, LANES - half, 1), pltpu.roll(c, half, 1))
    return c * cos + partner * sin


PRE_SUB = 256


def _segment_mean(sq, s_ref):
    w = s_ref.shape[0]
    return jnp.concatenate([_dot(sq[:, c:c + w].astype(BF16), s_ref[...]) for c in range(0, sq.shape[1], w)], axis=1)


def _pre_kernel(x_ref, mod_ref, gmix_ref, win_ref, cosa_ref, sina_ref, cosc_ref, sinc_ref,
                sa_ref, sc_ref, gq_ref, gk_ref, gcq_ref, gckv_ref, wuq_ref, wukvk_ref, wukvv_ref,
                gqc_ref, gkn_ref, gkr_ref, *out_refs, kv_only):
    if kv_only:
        ka_ref, va_ref, kc_ref, vc_ref = out_refs
        o_ka, o_ckv = 0, 2 * A_KV
    else:
        qa_ref, ka_ref, va_ref, u_ref, qc_ref, kc_ref, vc_ref = out_refs
        o_ka, o_ckv = A_Q, A_Q + 2 * A_KV + 2 * B_CH + C_Q_RANK
    for sub in range(x_ref.shape[0] // PRE_SUB):
        rows = slice(sub * PRE_SUB, (sub + 1) * PRE_SUB)
        x = x_ref[rows, :]
        ms = jnp.mean(x * x, axis=-1, keepdims=True)
        y = x * lax.rsqrt(ms + EPS)
        shift = mod_ref[0, 0:1, :]
        scale = mod_ref[0, 1:2, :]
        h = (y * gmix_ref[...]) * (1.0 + scale) + shift
        p = _dot(h.astype(BF16), win_ref[...])

        cosa, sina = cosa_ref[rows, :], sina_ref[rows, :]
        cosc, sinc = cosc_ref[rows, :], sinc_ref[rows, :]

        if not kv_only:
            qa = p[:, 0:A_Q]
            ssq = _segment_mean(qa * qa, sa_ref)
            qa = qa * lax.rsqrt(ssq + EPS) * gq_ref[...]
            for j in range(A_Q // LANES):
                sl = slice(j * LANES, (j + 1) * LANES)
                qa_ref[rows, sl] = _rope_chunk(qa[:, sl], cosa, sina, HEAD_DIM // 4).astype(BF16)

            o = A_Q + 2 * A_KV
            u_ref[rows, :] = p[:, o:o + B_CH] * jax.nn.sigmoid(p[:, o + B_CH:o + 2 * B_CH])

            o = o + 2 * B_CH
            cq = p[:, o:o + C_Q_RANK]
            cq = cq * lax.rsqrt(jnp.mean(cq * cq, axis=-1, keepdims=True) + EPS) * gcq_ref[...]
            qc = _dot(cq.astype(BF16), wuq_ref[...])
            ssq = _segment_mean(qc * qc, sc_ref)
            qc = qc * lax.rsqrt(ssq + EPS) * gqc_ref[...]
            for j in range(C_HEADS):
                sl = slice(j * C_SLOT, (j + 1) * C_SLOT)
                qc_ref[rows, sl] = _rope_chunk(qc[:, sl], cosc, sinc, C_ROPE // 4).astype(BF16)

        ka = p[:, o_ka:o_ka + A_KV]
        ssk = _dot((ka * ka).astype(BF16), sa_ref[0:A_KV, 0:A_KV])
        ka = ka * lax.rsqrt(ssk + EPS) * gk_ref[...]
        ka_ref[rows, :] = _rope_chunk(ka, cosa, sina, HEAD_DIM // 4).astype(BF16)
        va_ref[:, rows] = p[:, o_ka + A_KV:o_ka + 2 * A_KV].T.astype(BF16)

        o = o_ckv
        ckv = p[:, o:o + C_KV_RANK]
        ckv = (ckv * lax.rsqrt(jnp.mean(ckv * ckv, axis=-1, keepdims=True) + EPS) * gckv_ref[...]).astype(BF16)
        kn = _dot(ckv, wukvk_ref[...])
        vc_ref[:, rows] = _dot(ckv, wukvv_ref[...]).T.astype(BF16)
        ssk = _segment_mean(kn * kn, sc_ref)
        kn = kn * lax.rsqrt(ssk + EPS) * gkn_ref[...]
        o = o + C_KV_RANK
        kr = p[:, o:o + C_SLOT]
        kr = kr * lax.rsqrt(jnp.sum(kr * kr, axis=-1, keepdims=True) * (1.0 / C_ROPE) + EPS) * gkr_ref[...]
        kr = _rope_chunk(kr, cosc, sinc, C_ROPE // 4)
        for j in range(C_HEADS):
            sl = slice(j * C_SLOT, (j + 1) * C_SLOT)
            kc_ref[rows, sl] = (kn[:, sl] + kr).astype(BF16)


def _pre_call(x2d, mod, tiles_per_mod, tab_tiles, consts, tm, kv_only=False):
    n, d = x2d.shape
    (gmix, win, cosa, sina, cosc, sinc, sa, sc, gq, gk, gcq, gckv, wuq, wukvk, wukvv, gqc, gkn, gkr) = consts
    if kv_only:
        o_ckv = A_Q + 2 * A_KV + 2 * B_CH + C_Q_RANK
        win = jnp.concatenate([win[:, A_Q:A_Q + 2 * A_KV], win[:, o_ckv:]], axis=1)

    def const(a):
        return pl.BlockSpec(a.shape, lambda i: (0,) * a.ndim)

    def tab(a):
        return pl.BlockSpec((tm, LANES), lambda i: (i % tab_tiles, 0))

    in_specs = [
        pl.BlockSpec((tm, d), lambda i: (i, 0)),
        pl.BlockSpec((1, MOD_ROWS, d), lambda i: (i // tiles_per_mod, 0, 0)),
        const(gmix), const(win), tab(cosa), tab(sina), tab(cosc), tab(sinc),
        const(sa), const(sc), const(gq), const(gk), const(gcq), const(gckv),
        const(wuq), const(wukvk), const(wukvv), const(gqc), const(gkn), const(gkr),
    ]
    widths = (A_Q, A_KV, A_KV, B_CH, C_HEADS * C_SLOT, C_HEADS * C_SLOT, C_HEADS * C_V)
    dtypes = (BF16, BF16, BF16, F32, BF16, BF16, BF16)
    out_specs = [pl.BlockSpec((tm, w), lambda i: (i, 0)) for w in widths]
    out_shape = [jax.ShapeDtypeStruct((n, w), dt) for w, dt in zip(widths, dtypes)]
    out_specs[2] = pl.BlockSpec((A_KV, tm), lambda i: (0, i))
    out_shape[2] = jax.ShapeDtypeStruct((A_KV, n), BF16)
    out_specs[6] = pl.BlockSpec((C_HEADS * C_V, tm), lambda i: (0, i))
    out_shape[6] = jax.ShapeDtypeStruct((C_HEADS * C_V, n), BF16)
    if kv_only:
        keep = (1, 2, 5, 6)
        out_specs = [out_specs[k] for k in keep]
        out_shape = [out_shape[k] for k in keep]
    return pl.pallas_call(
        functools.partial(_pre_kernel, kv_only=kv_only),
        grid=(n // tm,),
        in_specs=in_specs,
        out_specs=out_specs,
        out_shape=out_shape,
        compiler_params=_cparams(("parallel",)),
        name="pre_attn",
    )(x2d, mod, gmix, win, cosa, sina, cosc, sinc, sa, sc, gq, gk, gcq, gckv, wuq, wukvk, wukvv,
      gqc, gkn, gkr)


CONV_PAD = 16
CONV_CHUNK = 128
SUBLANES = 8


def _conv_chunk(pad_ref, c, w_ref, b_ref, g_ref, beta_ref):
    off = CONV_PAD - B_WIDTH // 2
    nq = (off + B_WIDTH - 1) // SUBLANES + 1
    win = CONV_CHUNK + (nq - 1) * SUBLANES
    base = c * CONV_CHUNK
    acc = jnp.zeros((CONV_CHUNK, B_CH), F32)
    for r in range(SUBLANES):
        taps = [k for k in range(B_WIDTH) if (off + k) % SUBLANES == r]
        if not taps:
            continue
        w_r = pad_ref[base + r:base + r + win, :]
        part = None
        for k in taps:
            q = (off + k) // SUBLANES
            term = w_r[q * SUBLANES:q * SUBLANES + CONV_CHUNK, :] * w_ref[k:k + 1, :]
            part = term if part is None else part + term
        acc = acc + part
    y = acc + b_ref[...]
    mu = jnp.mean(y, axis=-1, keepdims=True)
    yc = y - mu
    var = jnp.mean(yc * yc, axis=-1, keepdims=True)
    z = yc * lax.rsqrt(var + EPS) * g_ref[...] + beta_ref[...]
    return (z * jax.nn.sigmoid(z)).astype(BF16)


ATTN_A_QBLOCKS = 16


ONES_ROWS = 16


def _attn_a_kernel(*refs, t, has_local, qblocks):
    u_ref, uprev_ref, unext_ref, cw_ref, cb_ref, cg_ref, cbeta_ref, o_ref, ob_ref, pad_ref = refs[-10:]
    if has_local:
        q_ref, k_ref, vt_ref, kc_ref, vct_ref, sink_ref, bias_ref = refs[:-10]
    else:
        q_ref, kc_ref, vct_ref, sink_ref = refs[:-10]
    step = pl.program_id(1)
    tq = q_ref.shape[0]
    pad_ref[0:CONV_PAD, :] = jnp.where(step > 0, uprev_ref[0], 0.0)
    pad_ref[CONV_PAD:CONV_PAD + tq, :] = u_ref[0]
    pad_ref[CONV_PAD + tq:CONV_PAD + tq + CONV_PAD, :] = jnp.where(step < pl.num_programs(1) - 1, unext_ref[0], 0.0)
    nchunk = A_Q // LANES
    span = 3 * BLOCK
    lane = lax.broadcasted_iota(jnp.int32, (BLOCK, LANES), 1)
    row = lax.broadcasted_iota(jnp.int32, (A_KV, BLOCK), 0)
    zero = jnp.zeros((BLOCK, LANES), BF16)
    sink = sink_ref[...]
    kc = kc_ref[0]
    vct = vct_ref[...]
    for blk in range(qblocks):
        rows = slice(blk * BLOCK, (blk + 1) * BLOCK)
        chunks = [q_ref[rows, j * LANES:(j + 1) * LANES] for j in range(nchunk)]
        qs = jnp.concatenate([jnp.where(lane < HEAD_DIM, c, zero) for c in chunks]
                             + [jnp.where(lane >= HEAD_DIM, c, zero) for c in chunks], axis=0)
        if has_local:
            n = pl.program_id(1) * qblocks + blk
            start = pl.multiple_of(jnp.clip((n - 1) * BLOCK, 0, t - span), BLOCK)
            keys = jnp.concatenate([k_ref[0, pl.ds(start, span), :], kc], axis=0)
            vt = jnp.concatenate([vt_ref[:, pl.ds(start, span)], vct], axis=1)
        else:
            keys, vt = kc, vct
        s = _dot_nt(keys, qs)
        if has_local:
            s = jnp.concatenate([s[:span] + bias_ref[n - start // BLOCK], s[span:]], axis=0)
        m = jnp.maximum(jnp.max(s, axis=0, keepdims=True), sink)
        e = jnp.exp2(s - m).astype(BF16)
        vte = jnp.concatenate([vt, jnp.ones((ONES_ROWS, vt.shape[1]), BF16)], axis=0)
        acc = _dot(vte, e)
        den = acc[A_KV:A_KV + 1] + jnp.exp2(sink - m)
        out = acc[:A_KV] * (1.0 / den)
        for j in range(nchunk):
            x = jnp.where(row < HEAD_DIM, out[:, j * BLOCK:(j + 1) * BLOCK],
                          out[:, (nchunk + j) * BLOCK:(nchunk + j + 1) * BLOCK])
            o_ref[rows, j * LANES:(j + 1) * LANES] = x.T.astype(BF16)
        ob_ref[rows, :] = _conv_chunk(pad_ref, blk, cw_ref, cb_ref, cg_ref, cbeta_ref)


def _window_bias():
    r = np.arange(BLOCK)[None, :]
    c = np.arange(3 * BLOCK)[:, None]
    pats = [np.where(np.abs(c - p * BLOCK - r) <= A_WINDOW, 0.0, NEG) for p in range(3)]
    return jnp.asarray(np.stack([np.tile(p, (1, A_HEADS)) for p in pats]), F32)


def _attn_a_call(qa, ka, vat, kac, vact, sink_row, u3, conv_p, bsz, t):
    has_local = ka is not None
    qblocks = min(ATTN_A_QBLOCKS, t // BLOCK)
    tq = qblocks * BLOCK
    nq = t // tq
    ctx_len = kac.shape[1]
    in_specs = [pl.BlockSpec((tq, A_Q), lambda b, n: (b * nq + n, 0))]
    args = [qa]
    if has_local:
        in_specs += [pl.BlockSpec((1, t, A_KV), lambda b, n: (b, 0, 0)),
                     pl.BlockSpec((A_KV, t), lambda b, n: (0, b))]
        args += [ka, vat]
    in_specs += [pl.BlockSpec((1, ctx_len, A_KV), lambda b, n: (b, 0, 0)),
                 pl.BlockSpec((A_KV, ctx_len), lambda b, n: (0, b))]
    in_specs += [pl.BlockSpec(sink_row.shape, lambda b, n: (0, 0))]
    args += [kac, vact, sink_row]
    if has_local:
        bias = _window_bias()
        in_specs += [pl.BlockSpec(bias.shape, lambda b, n: (0, 0, 0))]
        args += [bias]
    halo_per_step = tq // CONV_PAD
    n_halo = t // CONV_PAD
    in_specs += [
        pl.BlockSpec((1, tq, B_CH), lambda b, n: (b, n, 0)),
        pl.BlockSpec((1, CONV_PAD, B_CH), lambda b, n: (b, jnp.maximum(n * halo_per_step - 1, 0), 0)),
        pl.BlockSpec((1, CONV_PAD, B_CH), lambda b, n: (b, jnp.minimum((n + 1) * halo_per_step, n_halo - 1), 0)),
    ] + [pl.BlockSpec(a.shape, lambda b, n: (0, 0)) for a in conv_p]
    args += [u3, u3, u3, *conv_p]
    return pl.pallas_call(
        functools.partial(_attn_a_kernel, t=t, has_local=has_local, qblocks=qblocks),
        grid=(bsz, nq),
        in_specs=in_specs,
        out_specs=[pl.BlockSpec((tq, A_Q), lambda b, n: (b * nq + n, 0)),
                   pl.BlockSpec((tq, B_CH), lambda b, n: (b * nq + n, 0))],
        out_shape=[jax.ShapeDtypeStruct((bsz * t, A_Q), BF16), jax.ShapeDtypeStruct((bsz * t, B_CH), BF16)],
        scratch_shapes=[pltpu.VMEM((tq + 2 * CONV_PAD, B_CH), F32)],
        compiler_params=_cparams(("parallel", "parallel")),
        name="attn_ab_local" if has_local else "attn_ab_ctx",
    )(*args)


def _mla_kernel(*refs, has_local):
    if has_local:
        q_ref, kx_ref, vxt_ref, kc_ref, vct_ref, o_ref = refs
    else:
        q_ref, kc_ref, vct_ref, o_ref = refs
    def scores(h):
        sl = slice(h * C_SLOT, (h + 1) * C_SLOT)
        q = q_ref[:, sl]
        s_c = _dot_nt(kc_ref[0, :, sl], q)
        s_x = _dot_nt(kx_ref[0, :, sl], q) if has_local else None
        return s_c, s_x

    outs = []
    nxt = scores(0)
    for h in range(C_HEADS):
        vs = slice(h * C_V, (h + 1) * C_V)
        s_c, s_x = nxt
        if h + 1 < C_HEADS:
            nxt = scores(h + 1)
        m = jnp.max(s_c, axis=0, keepdims=True)
        if has_local:
            m = jnp.maximum(m, jnp.max(s_x, axis=0, keepdims=True))
        e_c = jnp.exp2(s_c - m).astype(BF16)
        vt = jnp.concatenate([vct_ref[vs, :], jnp.ones((ONES_ROWS, e_c.shape[0]), BF16)], axis=0)
        acc = _dot(vt, e_c)
        if has_local:
            e_x = jnp.exp2(s_x - m).astype(BF16)
            vt = jnp.concatenate([vxt_ref[vs, :], jnp.ones((ONES_ROWS, e_x.shape[0]), BF16)], axis=0)
            acc = acc + _dot(vt, e_x)
        outs.append(acc[:C_V] * (1.0 / acc[C_V:C_V + 1]))
    o_ref[...] = jnp.concatenate(outs, axis=0).T.astype(BF16)


def _mla_call(qc, kx, vxt, kcc, vcct, bsz, t, tq):
    has_local = kx is not None
    nq = t // tq
    ctx_len = kcc.shape[1]
    wq = C_HEADS * C_SLOT
    wv = C_HEADS * C_V
    in_specs = [pl.BlockSpec((tq, wq), lambda b, n: (b * nq + n, 0))]
    args = [qc]
    if has_local:
        in_specs += [pl.BlockSpec((1, t, wq), lambda b, n: (b, 0, 0)),
                     pl.BlockSpec((wv, t), lambda b, n: (0, b))]
        args += [kx, vxt]
    in_specs += [pl.BlockSpec((1, ctx_len, wq), lambda b, n: (b, 0, 0)),
                 pl.BlockSpec((wv, ctx_len), lambda b, n: (0, b))]
    args += [kcc, vcct]
    return pl.pallas_call(
        functools.partial(_mla_kernel, has_local=has_local),
        grid=(bsz, nq),
        in_specs=in_specs,
        out_specs=pl.BlockSpec((tq, wv), lambda b, n: (b * nq + n, 0)),
        out_shape=jax.ShapeDtypeStruct((bsz * t, wv), BF16),
        compiler_params=_cparams(("parallel", "parallel")),
        name="mla_local" if has_local else "mla_ctx",
    )(*args)


def _mixer_residual(x_ref, mod_ref, oa_ref, ob_ref, oc_ref, wo_ref):
    y = _dot(oa_ref[...], wo_ref[0:A_Q, :])
    y = y + _dot(ob_ref[...], wo_ref[A_Q:A_Q + B_CH, :])
    y = y + _dot(oc_ref[...], wo_ref[A_Q + B_CH:, :])
    return x_ref[...] + mod_ref[0, 2:3, :] * y


def _mixer_specs(tm, d, tiles_per_mod, oa, ob, oc, w_out):
    return [
        pl.BlockSpec((tm, d), lambda i: (i, 0)),
        pl.BlockSpec((1, MOD_ROWS, d), lambda i: (i // tiles_per_mod, 0, 0)),
        pl.BlockSpec((tm, oa.shape[1]), lambda i: (i, 0)),
        pl.BlockSpec((tm, ob.shape[1]), lambda i: (i, 0)),
        pl.BlockSpec((tm, oc.shape[1]), lambda i: (i, 0)),
        pl.BlockSpec(w_out.shape, lambda i: (0, 0), pipeline_mode=pl.Buffered(1)),
    ]


def _split_bf16(a):
    hi = a.astype(BF16)
    lo = (a - hi.astype(F32)).astype(BF16)
    return hi, lo


def _ffn_input(x, mod_ref, g_ref):
    ms = jnp.mean(x * x, axis=-1, keepdims=True)
    y = x * lax.rsqrt(ms + EPS)
    return (y * g_ref[...]) * (1.0 + mod_ref[0, 4:5, :]) + mod_ref[0, 3:4, :]


def _ff_chunks(ff):
    tiles = ff // MXU_TILE
    if ff % MXU_TILE or tiles < 3:
        return (ff,)
    third = (tiles // 3) * MXU_TILE
    return (third, third, ff - 2 * third)


def _swiglu(h, w1_ref, w3_ref, w2_ref):
    y = None
    o = 0
    for tf in _ff_chunks(w1_ref.shape[1]):
        a = _dot(h, w1_ref[:, o:o + tf])
        b = _dot(h, w3_ref[:, o:o + tf])
        g = (a * jax.nn.sigmoid(a) * b).astype(BF16)
        yc = _dot(g, w2_ref[o:o + tf, :])
        y = yc if y is None else y + yc
        o += tf
    return y


def _ffn_kernel(x_ref, mod_ref, oa_ref, ob_ref, oc_ref, wo_ref, g_ref, w1_ref, w3_ref, w2_ref, o_ref):
    x = _mixer_residual(x_ref, mod_ref, oa_ref, ob_ref, oc_ref, wo_ref)
    h = _ffn_input(x, mod_ref, g_ref).astype(BF16)
    o_ref[...] = x + mod_ref[0, 5:6, :] * _swiglu(h, w1_ref, w3_ref, w2_ref)


def _ffn_call(x2d, mod, tiles_per_mod, oa, ob, oc, w_out, g, w1, w3, w2, tm):
    n, d = x2d.shape

    def resident(a):
        return pl.BlockSpec(a.shape, lambda i: (0, 0), pipeline_mode=pl.Buffered(1))

    return pl.pallas_call(
        _ffn_kernel,
        grid=(n // tm,),
        in_specs=_mixer_specs(tm, d, tiles_per_mod, oa, ob, oc, w_out) + [
            pl.BlockSpec(g.shape, lambda i: (0, 0)), resident(w1), resident(w3), resident(w2)],
        out_specs=pl.BlockSpec((tm, d), lambda i: (i, 0)),
        out_shape=jax.ShapeDtypeStruct((n, d), F32),
        compiler_params=_cparams(("parallel",)),
        name="ffn_dense",
    )(x2d, mod, oa, ob, oc, w_out, g, w1, w3, w2)


MOE_TM = 512
SEG_ALIGN = 16
SEG_PIECES = (512, 256, 128, 64, 32, 16)
CBUF_ROWS = 2 * MOE_TM + N_EXPERTS * SEG_ALIGN
META_I1, META_I2, META_G1, META_G2, META_R1, META_R2 = range(6)
META_ROWS = 8


def _router_kernel(x_ref, mod_ref, oa_ref, ob_ref, oc_ref, wo_ref, g_ref, wr_ref, br_ref, utri_ref,
                   x1_ref, h_ref, meta_ref, metat_ref, cnt_ref):
    x = _mixer_residual(x_ref, mod_ref, oa_ref, ob_ref, oc_ref, wo_ref)
    x1_ref[...] = x
    h = _ffn_input(x, mod_ref, g_ref)
    h_ref[...] = h.astype(BF16)
    h_hi, h_lo = _split_bf16(h)
    w_hi, w_lo = _split_bf16(wr_ref[...])
    logits = _dot(h_hi, w_hi) + (_dot(h_lo, w_hi) + _dot(h_hi, w_lo)) + br_ref[...]
    lt = logits.T[0:N_EXPERTS, :]
    row = lax.broadcasted_iota(jnp.int32, lt.shape, 0).astype(F32)
    m1 = jnp.max(lt, axis=0, keepdims=True)
    i1 = jnp.min(jnp.where(lt == m1, row, float(N_EXPERTS)), axis=0, keepdims=True)
    rest = jnp.where(row == i1, NEG, lt)
    m2 = jnp.max(rest, axis=0, keepdims=True)
    i2 = jnp.min(jnp.where(rest == m2, row, float(N_EXPERTS)), axis=0, keepdims=True)
    e2 = jnp.exp(m2 - m1)
    den = 1.0 + e2
    sel1 = jnp.where(row == i1, 1.0, 0.0)
    sel2 = jnp.where(row == i2, 1.0, 0.0)
    sel = sel1 + sel2
    before = _dot(sel.astype(BF16), utri_ref[...])
    r1 = jnp.sum(before * sel1, axis=0, keepdims=True)
    r2 = jnp.sum(before * sel2, axis=0, keepdims=True)
    cnt_ref[0] = jnp.broadcast_to(jnp.sum(sel, axis=1, keepdims=True), (N_EXPERTS, LANES))
    zero = jnp.zeros_like(m1)
    metat = jnp.concatenate([i1, i2, 1.0 / den, e2 / den, r1, r2, zero, zero], axis=0)
    metat_ref[...] = metat
    pad = jnp.zeros((LANES - META_ROWS, metat.shape[1]), F32)
    meta_ref[...] = jnp.concatenate([metat, pad], axis=0).T


def _router_call(x2d, mod, tiles_per_mod, oa, ob, oc, w_out, g, wr, br, utri):
    n, d = x2d.shape
    tm = MOE_TM
    nt = n // tm
    return pl.pallas_call(
        _router_kernel,
        grid=(nt,),
        in_specs=_mixer_specs(tm, d, tiles_per_mod, oa, ob, oc, w_out) + [
            pl.BlockSpec(g.shape, lambda i: (0, 0)),
            pl.BlockSpec(wr.shape, lambda i: (0, 0)),
            pl.BlockSpec(br.shape, lambda i: (0, 0)),
            pl.BlockSpec(utri.shape, lambda i: (0, 0)),
        ],
        out_specs=[
            pl.BlockSpec((tm, d), lambda i: (i, 0)),
            pl.BlockSpec((tm, d), lambda i: (i, 0)),
            pl.BlockSpec((tm, LANES), lambda i: (i, 0)),
            pl.BlockSpec((META_ROWS, tm), lambda i: (0, i)),
            pl.BlockSpec((1, N_EXPERTS, LANES), lambda i: (i, 0, 0)),
        ],
        out_shape=[
            jax.ShapeDtypeStruct((n, d), F32),
            jax.ShapeDtypeStruct((n, d), BF16),
            jax.ShapeDtypeStruct((n, LANES), F32),
            jax.ShapeDtypeStruct((META_ROWS, n), F32),
            jax.ShapeDtypeStruct((nt, N_EXPERTS, LANES), F32),
        ],
        compiler_params=_cparams(("parallel",)),
        name="moe_router",
    )(x2d, mod, oa, ob, oc, w_out, g, wr, br, utri)


def _pair_slots(meta, segoff_row):
    lane = lax.broadcasted_iota(jnp.int32, meta.shape, 1).astype(F32)
    i1 = meta[:, META_I1:META_I1 + 1]
    i2 = meta[:, META_I2:META_I2 + 1]
    s1 = jnp.sum(jnp.where(lane == i1, segoff_row, 0.0), axis=-1, keepdims=True) + meta[:, META_R1:META_R1 + 1]
    s2 = jnp.sum(jnp.where(lane == i2, segoff_row, 0.0), axis=-1, keepdims=True) + meta[:, META_R2:META_R2 + 1]
    return s1, s2


def _segment_copies(src, dst, src_off, dst_off, length, sem):
    out = []
    for size in SEG_PIECES:
        done = (length // (2 * size)) * (2 * size)
        s = pl.multiple_of(src_off + done, SEG_ALIGN)
        t = pl.multiple_of(dst_off + done, SEG_ALIGN)
        cp = pltpu.make_async_copy(src.at[pl.ds(s, size)], dst.at[pl.ds(t, size)], sem)
        out.append(((length & size) != 0, cp))
    return out


def _start_copies(copies):
    for pred, cp in copies:
        @pl.when(pred)
        def _(cp=cp):
            cp.start()


def _wait_copies(copies):
    for pred, cp in copies:
        @pl.when(pred)
        def _(cp=cp):
            cp.wait()


def _compact_kernel(segoff_s, base_s, len_s, fill_s, h_ref, meta_ref, metat_ref, xs_out, gs_out,
                    cbuf2, gbuf2, zx, zg, sems):
    i = pl.program_id(0)
    last = pl.num_programs(0) - 1
    slot_i = i % 2
    cbuf = cbuf2.at[slot_i]
    gbuf = gbuf2.at[slot_i]

    def copies_of(tile, slot):
        out = []
        for e in range(N_EXPERTS):
            k = tile * N_EXPERTS + e
            out += _segment_copies(cbuf2.at[slot], xs_out, segoff_s[k], base_s[k], len_s[k], sems.at[0, slot])
            out += _segment_copies(gbuf2.at[slot], gs_out, segoff_s[k], base_s[k], len_s[k], sems.at[1, slot])
        return out

    @pl.when(i >= 2)
    def _():
        _wait_copies(copies_of(i - 2, slot_i))

    mt = metat_ref[...]
    s1 = mt[META_R1:META_R1 + 1]
    s2 = mt[META_R2:META_R2 + 1]
    for e in range(N_EXPERTS):
        off = segoff_s[i * N_EXPERTS + e].astype(F32)
        s1 = s1 + jnp.where(mt[META_I1:META_I1 + 1] == e, off, 0.0)
        s2 = s2 + jnp.where(mt[META_I2:META_I2 + 1] == e, off, 0.0)
    row = lax.broadcasted_iota(jnp.int32, (CBUF_ROWS, MOE_TM), 0).astype(F32)
    p1 = jnp.where(row == s1, 1.0, 0.0)
    p2 = jnp.where(row == s2, 1.0, 0.0)
    perm = (p1 + p2).astype(BF16)
    cbuf[...] = _dot(perm, h_ref[...]).astype(BF16)
    meta = meta_ref[...]
    lane = lax.broadcasted_iota(jnp.int32, meta.shape, 1)
    g1 = meta[:, META_G1:META_G1 + 1]
    g2 = meta[:, META_G2:META_G2 + 1]
    g1_hi = g1.astype(BF16).astype(F32)
    g2_hi = g2.astype(BF16).astype(F32)
    fields = (g1_hi, g1 - g1_hi, g2_hi, g2 - g2_hi, meta[:, META_I1:META_I1 + 1])
    a = jnp.zeros_like(meta)
    for k, col in enumerate(fields):
        a = jnp.where(lane == k, col, a)
    gbuf[...] = _dot(perm, a.astype(BF16))
    _start_copies(copies_of(i, slot_i))

    @pl.when(i == last)
    def _():
        @pl.when(i >= 1)
        def _():
            _wait_copies(copies_of(i - 1, 1 - slot_i))
        _wait_copies(copies_of(i, slot_i))
        zx[...] = jnp.zeros_like(zx)
        zg[...] = jnp.zeros_like(zg)
        fills = []
        for e in range(N_EXPERTS):
            fills += _segment_copies(zx, xs_out, 0, fill_s[e], fill_s[N_EXPERTS + e], sems.at[0, 0])
            fills += _segment_copies(zg, gs_out, 0, fill_s[e], fill_s[N_EXPERTS + e], sems.at[1, 0])
        _start_copies(fills)
        _wait_copies(fills)

        def fill_copies(j):
            r = pl.multiple_of(fill_s[2 * N_EXPERTS] + j * MOE_TM, MOE_TM)
            return (pltpu.make_async_copy(zx, xs_out.at[pl.ds(r, MOE_TM)], sems.at[0, 0]),
                    pltpu.make_async_copy(zg, gs_out.at[pl.ds(r, MOE_TM)], sems.at[1, 0]))

        def start_tile(j, carry):
            for cp in fill_copies(j):
                cp.start()
            return carry

        def wait_tile(j, carry):
            for cp in fill_copies(j):
                cp.wait()
            return carry

        lax.fori_loop(0, fill_s[2 * N_EXPERTS + 1], start_tile, 0)
        lax.fori_loop(0, fill_s[2 * N_EXPERTS + 1], wait_tile, 0)


def _compact_call(sched, fill, h, meta, metat, rows):
    n, d = h.shape
    nt = n // MOE_TM
    grid_spec = pltpu.PrefetchScalarGridSpec(
        num_scalar_prefetch=4,
        grid=(nt,),
        in_specs=[
            pl.BlockSpec((MOE_TM, d), lambda i, *_: (i, 0)),
            pl.BlockSpec((MOE_TM, LANES), lambda i, *_: (i, 0)),
            pl.BlockSpec((META_ROWS, MOE_TM), lambda i, *_: (0, i)),
        ],
        out_specs=[pl.BlockSpec(memory_space=pl.ANY), pl.BlockSpec(memory_space=pl.ANY)],
        scratch_shapes=[pltpu.VMEM((2, CBUF_ROWS, d), BF16), pltpu.VMEM((2, CBUF_ROWS, LANES), F32),
                        pltpu.VMEM((MOE_TM, d), BF16), pltpu.VMEM((MOE_TM, LANES), F32),
                        pltpu.SemaphoreType.DMA((2, 2))],
    )
    return pl.pallas_call(
        _compact_kernel,
        grid_spec=grid_spec,
        out_shape=[jax.ShapeDtypeStruct((rows, d), BF16), jax.ShapeDtypeStruct((rows, LANES), F32)],
        compiler_params=_cparams(("arbitrary",)),
        name="moe_compact",
    )(*sched, fill, h, meta, metat)


def _expert_kernel(exp_s, blk_s, valid_s, xs_ref, gs_ref, w1_ref, w3_ref, w2_ref, y_ref):
    del blk_s
    j = pl.program_id(0)

    @pl.when(valid_s[j] != 0)
    def _():
        first = gs_ref[:, 4:5] == exp_s[j].astype(F32)
        gate = jnp.where(first, gs_ref[:, 0:1] + gs_ref[:, 1:2], gs_ref[:, 2:3] + gs_ref[:, 3:4])
        y_ref[...] = (_swiglu(xs_ref[...], w1_ref, w3_ref, w2_ref) * gate).astype(BF16)

    @pl.when(valid_s[j] == 0)
    def _():
        y_ref[...] = jnp.zeros_like(y_ref)


def _expert_call(tile_sched, xs, gs, w1, w3, w2):
    rows, d = xs.shape
    ff = w1.shape[1]
    grid_spec = pltpu.PrefetchScalarGridSpec(
        num_scalar_prefetch=3,
        grid=(rows // MOE_TM,),
        in_specs=[
            pl.BlockSpec((MOE_TM, d), lambda j, e_s, b_s, v_s: (b_s[j], 0)),
            pl.BlockSpec((MOE_TM, LANES), lambda j, e_s, b_s, v_s: (b_s[j], 0)),
            pl.BlockSpec((d, ff), lambda j, e_s, b_s, v_s: (e_s[j], 0)),
            pl.BlockSpec((d, ff), lambda j, e_s, b_s, v_s: (e_s[j], 0)),
            pl.BlockSpec((ff, d), lambda j, e_s, b_s, v_s: (e_s[j], 0)),
        ],
        out_specs=pl.BlockSpec((MOE_TM, d), lambda j, e_s, b_s, v_s: (j, 0)),
    )
    return pl.pallas_call(
        _expert_kernel,
        grid_spec=grid_spec,
        out_shape=jax.ShapeDtypeStruct((rows, d), BF16),
        compiler_params=_cparams(("arbitrary",)),
        name="moe_experts",
    )(*tile_sched, xs, gs, w1, w3, w2)


def _combine_kernel(segoff_s, base_s, len_s, x_ref, mod_ref, meta_ref, segoff_ref, y_hbm, o_ref, ybuf2, sems):
    i = pl.program_id(0)
    last = pl.num_programs(0) - 1
    slot_i = i % 2

    def copies_of(tile, slot):
        out = []
        for e in range(N_EXPERTS):
            k = tile * N_EXPERTS + e
            out += _segment_copies(y_hbm, ybuf2.at[slot], base_s[k], segoff_s[k], len_s[k], sems.at[slot])
        return out

    @pl.when(i == 0)
    def _():
        ybuf2[...] = jnp.zeros_like(ybuf2)
        _start_copies(copies_of(i, slot_i))

    @pl.when(i < last)
    def _():
        _start_copies(copies_of(i + 1, 1 - slot_i))

    _wait_copies(copies_of(i, slot_i))
    s1, s2 = _pair_slots(meta_ref[...], segoff_ref[0])
    slot = lax.broadcasted_iota(jnp.int32, (MOE_TM, CBUF_ROWS), 1).astype(F32)
    pick = (jnp.where(slot == s1, 1.0, 0.0) + jnp.where(slot == s2, 1.0, 0.0)).astype(BF16)
    o_ref[...] = x_ref[...] + mod_ref[0, 5:6, :] * _dot(pick, ybuf2[slot_i])


def _combine_call(sched, x2d, mod, tiles_per_mod, meta, segoff_v, y):
    n, d = x2d.shape
    grid_spec = pltpu.PrefetchScalarGridSpec(
        num_scalar_prefetch=3,
        grid=(n // MOE_TM,),
        in_specs=[
            pl.BlockSpec((MOE_TM, d), lambda i, *_: (i, 0)),
            pl.BlockSpec((1, MOD_ROWS, d), lambda i, *_: (i // tiles_per_mod, 0, 0)),
            pl.BlockSpec((MOE_TM, LANES), lambda i, *_: (i, 0)),
            pl.BlockSpec((1, 1, LANES), lambda i, *_: (i, 0, 0)),
            pl.BlockSpec(memory_space=pl.ANY),
        ],
        out_specs=pl.BlockSpec((MOE_TM, d), lambda i, *_: (i, 0)),
        scratch_shapes=[pltpu.VMEM((2, CBUF_ROWS, d), BF16), pltpu.SemaphoreType.DMA((2,))],
    )
    return pl.pallas_call(
        _combine_kernel,
        grid_spec=grid_spec,
        out_shape=jax.ShapeDtypeStruct((n, d), F32),
        compiler_params=_cparams(("arbitrary",)),
        name="moe_combine",
    )(*sched, x2d, mod, meta, segoff_v, y)


def _moe_call(x2d, mod, tiles_per_mod, outs, w_out, g, wr, br, w1, w3, w2):
    n, d = x2d.shape
    nt = n // MOE_TM
    utri = jnp.asarray(np.triu(np.ones((MOE_TM, MOE_TM), np.float32), 1), BF16)
    x2d, h, meta, metat, counts = _router_call(x2d, mod, tiles_per_mod, *outs, w_out, g, wr, br, utri)

    cnt = counts[:, :, 0].astype(jnp.int32)
    seg_len = (cnt + SEG_ALIGN - 1) // SEG_ALIGN * SEG_ALIGN
    segoff = jnp.cumsum(seg_len, axis=1) - seg_len
    region = (jnp.sum(seg_len, axis=0) + MOE_TM - 1) // MOE_TM * MOE_TM
    region_start = jnp.cumsum(region) - region
    base = region_start[None, :] + jnp.cumsum(seg_len, axis=0) - seg_len
    rows_max = 2 * n + nt * N_EXPERTS * (SEG_ALIGN - 1) + N_EXPERTS * (MOE_TM - 1)
    n_sorted_tiles = (rows_max + MOE_TM - 1) // MOE_TM
    tile_end = jnp.cumsum(region // MOE_TM)
    total_tiles = tile_end[-1]
    jt = jnp.arange(n_sorted_tiles, dtype=jnp.int32)
    blk = jnp.minimum(jt, total_tiles - 1)
    tile_expert = jnp.sum((blk[:, None] >= tile_end[None, :]).astype(jnp.int32), axis=1)
    tile_sched = (tile_expert.astype(jnp.int32), blk.astype(jnp.int32), (jt < total_tiles).astype(jnp.int32))
    sched = tuple(a.reshape(-1).astype(jnp.int32) for a in (segoff, base, seg_len))
    segoff_v = jnp.pad(segoff.astype(F32), ((0, 0), (0, LANES - N_EXPERTS)))[:, None, :]

    total = jnp.sum(seg_len, axis=0)
    fill = jnp.concatenate([region_start + total, region - total,
                            jnp.stack([total_tiles * MOE_TM, n_sorted_tiles - total_tiles])]).astype(jnp.int32)
    xs, gs = _compact_call(sched, fill, h, meta, metat, n_sorted_tiles * MOE_TM)
    y = _expert_call(tile_sched, xs, gs, w1, w3, w2)
    return _combine_call(sched, x2d, mod, tiles_per_mod, meta, segoff_v, y)


def _rope_tables(t):
    rows = jnp.arange(t, dtype=F32) // GRID_W
    cols = jnp.arange(t, dtype=F32) % GRID_W

    def tables(rot_dim):
        a = rot_dim // 2
        inv = 1.0 / (ROPE_BASE ** (jnp.arange(0, a, 2, dtype=F32) / a))
        ar = rows[:, None] * inv
        ac = cols[:, None] * inv
        cos = jnp.concatenate([jnp.cos(ar), jnp.cos(ar), jnp.cos(ac), jnp.cos(ac)], axis=-1)
        sin = jnp.concatenate([-jnp.sin(ar), jnp.sin(ar), -jnp.sin(ac), jnp.sin(ac)], axis=-1)
        return cos, sin

    ca, sa = tables(HEAD_DIM)
    cos_a = jnp.tile(ca, (1, LANES // HEAD_DIM))
    sin_a = jnp.tile(sa, (1, LANES // HEAD_DIM))
    cc, sc = tables(C_ROPE)
    ones = jnp.ones((t, C_NOPE), F32)
    tail = C_SLOT - C_NOPE - C_ROPE
    cos_c = jnp.concatenate([ones, cc, jnp.ones((t, tail), F32)], axis=-1)
    sin_c = jnp.concatenate([0 * ones, sc, jnp.zeros((t, tail), F32)], axis=-1)
    return cos_a, sin_a, cos_c, sin_c


def _head_perm():
    order = []
    for j in range(A_HEADS // 2):
        order += [j, A_HEADS // 2 + j]
    return np.concatenate([np.arange(h * HEAD_DIM, (h + 1) * HEAD_DIM) for h in order])


def _segment_mean_matrix(widths, total):
    m = np.zeros((total, total), np.float32)
    o = 0
    while o < total:
        for w, used in widths:
            if used:
                m[o:o + w, o:o + w] = 1.0 / w
            o += w
    return jnp.asarray(m, BF16)


def _slot_vec(nope, rope):
    z = jnp.zeros((C_SLOT - C_NOPE - C_ROPE,), F32)
    n = jnp.zeros((C_NOPE,), F32) if nope is None else nope
    r = jnp.zeros((C_ROPE,), F32) if rope is None else rope
    return jnp.tile(jnp.concatenate([n, r, z]), C_HEADS)[None, :]


def _layer_consts(i, p, perm, tabs):
    w_in = p["w_in"][i]
    o_kr = A_Q + 2 * A_KV + 2 * B_CH + C_Q_RANK + C_KV_RANK
    d = w_in.shape[0]
    kr_cols = jnp.concatenate([jnp.zeros((d, C_NOPE), F32), w_in[:, o_kr:o_kr + C_ROPE],
                               jnp.zeros((d, C_SLOT - C_NOPE - C_ROPE), F32)], axis=1)
    win = jnp.concatenate([w_in[:, :A_Q][:, perm], w_in[:, A_Q:o_kr], kr_cols], axis=1).astype(BF16)

    w_uq = p["c_w_uq"][i].reshape(C_Q_RANK, C_HEADS, C_NOPE + C_ROPE)
    wuq = jnp.pad(w_uq, ((0, 0), (0, 0), (0, C_SLOT - C_NOPE - C_ROPE))).reshape(C_Q_RANK, C_HEADS * C_SLOT)
    w_ukv = p["c_w_ukv"][i].reshape(C_KV_RANK, C_HEADS, C_NOPE + C_V)
    wukvk = jnp.pad(w_ukv[:, :, :C_NOPE], ((0, 0), (0, 0), (0, C_SLOT - C_NOPE))).reshape(C_KV_RANK, -1)
    wukvv = w_ukv[:, :, C_NOPE:].reshape(C_KV_RANK, C_HEADS * C_V)

    sa = _segment_mean_matrix([(HEAD_DIM, True)], MXU_TILE)
    sc = _segment_mean_matrix([(C_NOPE, True), (C_ROPE, True), (C_SLOT - C_NOPE - C_ROPE, False)], MXU_TILE)
    gq = jnp.tile(p["a_q_norm_g"][i] * (A_SCALE * LOG2E), A_HEADS)[None, :]
    gk = jnp.tile(p["a_k_norm_g"][i], A_KV_HEADS)[None, :]
    gqc = _slot_vec(p["c_q_nope_norm_g"][i], p["c_q_rope_norm_g"][i]) * (MLA_SCALE * LOG2E)
    gkn = _slot_vec(p["c_k_nope_norm_g"][i], None)
    gkr = _slot_vec(None, p["c_k_rope_norm_g"][i])[:, :C_SLOT]
    return (p["mix_norm_g"][i][None, :], win) + tabs + (
        sa, sc, gq, gk, p["c_q_rank_norm_g"][i][None, :], p["c_kv_rank_norm_g"][i][None, :],
        wuq.astype(BF16), wukvk.astype(BF16), wukvv.astype(BF16), gqc, gkn, gkr)


def kernel(x, c, ctx, c_ctx, ada_w, ada_b, mix_norm_g, ffn_norm_g, w_in, w_out, a_q_norm_g, a_k_norm_g, a_sink, b_conv_w, b_conv_b, b_ln_g, b_ln_b, c_q_rank_norm_g, c_kv_rank_norm_g, c_w_uq, c_w_ukv, c_q_nope_norm_g, c_k_nope_norm_g, c_q_rope_norm_g, c_k_rope_norm_g, dense_w1, dense_w3, dense_w2, moe_router_w, moe_router_b, moe_w1, moe_w3, moe_w2):
    p = dict(w_in=w_in, c_w_uq=c_w_uq, c_w_ukv=c_w_ukv, a_q_norm_g=a_q_norm_g, a_k_norm_g=a_k_norm_g,
             c_q_nope_norm_g=c_q_nope_norm_g, c_k_nope_norm_g=c_k_nope_norm_g,
             c_q_rope_norm_g=c_q_rope_norm_g, c_k_rope_norm_g=c_k_rope_norm_g,
             c_q_rank_norm_g=c_q_rank_norm_g, c_kv_rank_norm_g=c_kv_rank_norm_g, mix_norm_g=mix_norm_g)
    bsz, t, d = x.shape
    ctx_len = ctx.shape[1]
    depth = ada_w.shape[0]
    n_x, n_c = bsz * t, bsz * ctx_len
    tm_pre = min(4 * PRE_SUB, t, n_c)
    tm_tok = 512
    tq = min(2048, t)
    assert t % tm_pre == 0 and n_c % tm_pre == 0 and t % tm_tok == 0 and n_c % tm_tok == 0 and t % tq == 0
    assert t % GRID_W == 0 and t >= 3 * BLOCK and ctx_len % BLOCK == 0

    ada_rows = ((bsz + 1 + 7) // 8) * 8
    c_pad = jnp.concatenate([c, c_ctx[None, :], jnp.zeros((ada_rows - bsz - 1, d), F32)], axis=0)
    mods = _ada_call(c_pad, ada_w, ada_b).reshape(depth, ada_rows, 6, d)
    mods = jnp.pad(mods, ((0, 0), (0, 0), (0, MOD_ROWS - 6), (0, 0)))

    tabs_x = _rope_tables(t)
    ones = jnp.ones((tm_pre, LANES), F32)
    tabs_c = (ones, 0 * ones, ones, 0 * ones)
    perm = _head_perm()

    x2 = x.reshape(n_x, d)
    c2 = ctx.reshape(n_c, d)
    for i in range(depth):
        last = i == depth - 1
        mod_x = mods[i, :bsz]
        mod_c = mods[i, bsz:bsz + 1]
        consts_x = _layer_consts(i, p, perm, tabs_x)
        consts_c = _layer_consts(i, p, perm, tabs_c)

        qa_x, ka_x, va_x, u_x, qc_x, kc_x, vc_x = _pre_call(x2, mod_x, t // tm_pre, t // tm_pre, consts_x, tm_pre)
        if last:
            ka_c, va_c, kc_c, vc_c = _pre_call(c2, mod_c, n_c // tm_pre, 1, consts_c, tm_pre, kv_only=True)
        else:
            qa_c, ka_c, va_c, u_c, qc_c, kc_c, vc_c = _pre_call(c2, mod_c, n_c // tm_pre, 1, consts_c, tm_pre)

        def r3(a, length):
            return a.reshape(bsz, length, a.shape[-1])

        sink_row = jnp.repeat(a_sink[i] * LOG2E, BLOCK)[None, :]
        conv_w = jnp.pad(b_conv_w[i], ((0, -B_WIDTH % SUBLANES), (0, 0)))
        conv_p = (conv_w, b_conv_b[i][None, :], b_ln_g[i][None, :], b_ln_b[i][None, :])
        w_o = jnp.concatenate([w_out[i][:A_Q][perm], w_out[i][A_Q:]], axis=0).astype(BF16)

        o_a, o_b = _attn_a_call(qa_x, r3(ka_x, t), va_x, r3(ka_c, ctx_len), va_c, sink_row, r3(u_x, t), conv_p, bsz, t)
        o_c = _mla_call(qc_x, r3(kc_x, t), vc_x, r3(kc_c, ctx_len), vc_c, bsz, t, tq)
        if not last:
            oc_a, oc_b = _attn_a_call(qa_c, None, None, r3(ka_c, ctx_len), va_c, sink_row, r3(u_c, ctx_len), conv_p,
                                      bsz, ctx_len)
            oc_c = _mla_call(qc_c, None, None, r3(kc_c, ctx_len), vc_c, bsz, ctx_len, ctx_len)

        j = i // 2
        g_ffn = ffn_norm_g[i][None, :]
        if i % 2 == 0:
            w = (dense_w1[j].astype(BF16), dense_w3[j].astype(BF16), dense_w2[j].astype(BF16))

            def mix(a2, mod, tiles_per_mod, outs, w=w, g_ffn=g_ffn, w_o=w_o):
                return _ffn_call(a2, mod, tiles_per_mod, *outs, w_o, g_ffn, *w, tm_tok)
        else:
            w = tuple(a.astype(BF16).reshape(-1, a.shape[-1]) for a in (moe_w1[j], moe_w3[j], moe_w2[j]))
            wr = jnp.pad(moe_router_w[j], ((0, 0), (0, LANES - N_EXPERTS)))
            br = jnp.pad(moe_router_b[j], (0, LANES - N_EXPERTS))[None, :]

            def mix(a2, mod, tiles_per_mod, outs, w=w, g_ffn=g_ffn, wr=wr, br=br, w_o=w_o):
                return _moe_call(a2, mod, tiles_per_mod * (tm_tok // MOE_TM), outs, w_o, g_ffn, wr, br, *w)
        x2 = mix(x2, mod_x, t // tm_tok, (o_a, o_b, o_c))
        if not last:
            c2 = mix(c2, mod_c, n_c // tm_tok, (oc_a, oc_b, oc_c))
    return x2.reshape(bsz, t, d)
```

```python
import functools

import jax
import jax.numpy as jnp
import numpy as np
from jax import lax
from jax.experimental import pallas as pl
from jax.experimental.pallas import tpu as pltpu

F32 = jnp.float32
BF16 = jnp.bfloat16

GRID_W = 64
HEAD_DIM = 64
A_HEADS = 8
A_KV_HEADS = 2
A_WINDOW = 128
BLOCK = 128
B_CH = 256
B_WIDTH = 31
C_HEADS = 4
C_Q_RANK = 384
C_KV_RANK = 256
C_NOPE = 64
C_ROPE = 32
C_V = 64
A_Q = A_HEADS * HEAD_DIM
A_KV = A_KV_HEADS * HEAD_DIM
N_EXPERTS = 8
ROPE_BASE = 10000.0
EPS = 1e-6
NEG = -1e30
A_SCALE = HEAD_DIM ** -0.5
MLA_SCALE = (C_NOPE + C_ROPE) ** -0.5
LOG2E = 1.4426950408889634

LANES = 128
MXU_TILE = 256
MOD_ROWS = 8
C_SLOT = 128
VMEM_LIMIT = 56 * 1024 * 1024


def _cparams(sem):
    return pltpu.CompilerParams(dimension_semantics=sem, vmem_limit_bytes=VMEM_LIMIT)


def _dot(a, b):
    return jnp.dot(a, b, preferred_element_type=F32)


def _dot_nt(a, b):
    return lax.dot_general(a, b, (((1,), (1,)), ((), ())), preferred_element_type=F32)


def _ada_kernel(c_ref, w_ref, b_ref, o_ref):
    c = c_ref[...]
    a = c * jax.nn.sigmoid(c)
    o_ref[0] = _dot(a.astype(BF16), w_ref[0].astype(BF16)) + b_ref[0]


def _ada_call(c_pad, ada_w, ada_b):
    depth, d, n6 = ada_w.shape
    rows = c_pad.shape[0]
    tn = 1536
    return pl.pallas_call(
        _ada_kernel,
        grid=(depth, n6 // tn),
        in_specs=[
            pl.BlockSpec((rows, d), lambda i, j: (0, 0)),
            pl.BlockSpec((1, d, tn), lambda i, j: (i, 0, j)),
            pl.BlockSpec((1, 1, tn), lambda i, j: (i, 0, j)),
        ],
        out_specs=pl.BlockSpec((1, rows, tn), lambda i, j: (i, 0, j)),
        out_shape=jax.ShapeDtypeStruct((depth, rows, n6), F32),
        compiler_params=_cparams(("parallel", "parallel")),
        name="ada_proj",
    )(c_pad, ada_w, ada_b.reshape(depth, 1, n6))


def _rope_chunk(c, cos, sin, half):
    lane = lax.broadcasted_iota(jnp.int32, c.shape, 1)
    lo = (lane & (2 * half - 1)) < half
    partner = jnp.where(lo, pltpu.roll(c, LANES - half, 1), pltpu.roll(c, half, 1))
    return c * cos + partner * sin


PRE_SUB = 256


def _segment_mean(sq, s_ref):
    w = s_ref.shape[0]
    return jnp.concatenate([_dot(sq[:, c:c + w].astype(BF16), s_ref[...]) for c in range(0, sq.shape[1], w)], axis=1)


def _pre_kernel(x_ref, mod_ref, gmix_ref, win_ref, cosa_ref, sina_ref, cosc_ref, sinc_ref,
                sa_ref, sc_ref, gq_ref, gk_ref, gcq_ref, gckv_ref, wuq_ref, wukvk_ref, wukvv_ref,
                gqc_ref, gkn_ref, gkr_ref, *out_refs, kv_only):
    if kv_only:
        ka_ref, va_ref, kc_ref, vc_ref = out_refs
        o_ka, o_ckv = 0, 2 * A_KV
    else:
        qa_ref, ka_ref, va_ref, u_ref, qc_ref, kc_ref, vc_ref = out_refs
        o_ka, o_ckv = A_Q, A_Q + 2 * A_KV + 2 * B_CH + C_Q_RANK
    for sub in range(x_ref.shape[0] // PRE_SUB):
        rows = slice(sub * PRE_SUB, (sub + 1) * PRE_SUB)
        x = x_ref[rows, :]
        ms = jnp.mean(x * x, axis=-1, keepdims=True)
        y = x * lax.rsqrt(ms + EPS)
        shift = mod_ref[0, 0:1, :]
        scale = mod_ref[0, 1:2, :]
        h = (y * gmix_ref[...]) * (1.0 + scale) + shift
        p = _dot(h.astype(BF16), win_ref[...])

        cosa, sina = cosa_ref[rows, :], sina_ref[rows, :]
        cosc, sinc = cosc_ref[rows, :], sinc_ref[rows, :]

        if not kv_only:
            qa = p[:, 0:A_Q]
            ssq = _segment_mean(qa * qa, sa_ref)
            qa = qa * lax.rsqrt(ssq + EPS) * gq_ref[...]
            for j in range(A_Q // LANES):
                sl = slice(j * LANES, (j + 1) * LANES)
                qa_ref[rows, sl] = _rope_chunk(qa[:, sl], cosa, sina, HEAD_DIM // 4).astype(BF16)

            o = A_Q + 2 * A_KV
            u_ref[rows, :] = p[:, o:o + B_CH] * jax.nn.sigmoid(p[:, o + B_CH:o + 2 * B_CH])

            o = o + 2 * B_CH
            cq = p[:, o:o + C_Q_RANK]
            cq = cq * lax.rsqrt(jnp.mean(cq * cq, axis=-1, keepdims=True) + EPS) * gcq_ref[...]
            qc = _dot(cq.astype(BF16), wuq_ref[...])
            ssq = _segment_mean(qc * qc, sc_ref)
            qc = qc * lax.rsqrt(ssq + EPS) * gqc_ref[...]
            for j in range(C_HEADS):
                sl = slice(j * C_SLOT, (j + 1) * C_SLOT)
                qc_ref[rows, sl] = _rope_chunk(qc[:, sl], cosc, sinc, C_ROPE // 4).astype(BF16)

        ka = p[:, o_ka:o_ka + A_KV]
        ssk = _dot((ka * ka).astype(BF16), sa_ref[0:A_KV, 0:A_KV])
        ka = ka * lax.rsqrt(ssk + EPS) * gk_ref[...]
        ka_ref[rows, :] = _rope_chunk(ka, cosa, sina, HEAD_DIM // 4).astype(BF16)
        va_ref[:, rows] = p[:, o_ka + A_KV:o_ka + 2 * A_KV].T.astype(BF16)

        o = o_ckv
        ckv = p[:, o:o + C_KV_RANK]
        ckv = (ckv * lax.rsqrt(jnp.mean(ckv * ckv, axis=-1, keepdims=True) + EPS) * gckv_ref[...]).astype(BF16)
        kn = _dot(ckv, wukvk_ref[...])
        vc_ref[:, rows] = _dot(ckv, wukvv_ref[...]).T.astype(BF16)
        ssk = _segment_mean(kn * kn, sc_ref)
        kn = kn * lax.rsqrt(ssk + EPS) * gkn_ref[...]
        o = o + C_KV_RANK
        kr = p[:, o:o + C_SLOT]
        kr = kr * lax.rsqrt(jnp.sum(kr * kr, axis=-1, keepdims=True) * (1.0 / C_ROPE) + EPS) * gkr_ref[...]
        kr = _rope_chunk(kr, cosc, sinc, C_ROPE // 4)
        for j in range(C_HEADS):
            sl = slice(j * C_SLOT, (j + 1) * C_SLOT)
            kc_ref[rows, sl] = (kn[:, sl] + kr).astype(BF16)


def _pre_call(x2d, mod, tiles_per_mod, tab_tiles, consts, tm, kv_only=False):
    n, d = x2d.shape
    (gmix, win, cosa, sina, cosc, sinc, sa, sc, gq, gk, gcq, gckv, wuq, wukvk, wukvv, gqc, gkn, gkr) = consts
    if kv_only:
        o_ckv = A_Q + 2 * A_KV + 2 * B_CH + C_Q_RANK
        win = jnp.concatenate([win[:, A_Q:A_Q + 2 * A_KV], win[:, o_ckv:]], axis=1)

    def const(a):
        return pl.BlockSpec(a.shape, lambda i: (0,) * a.ndim)

    def tab(a):
        return pl.BlockSpec((tm, LANES), lambda i: (i % tab_tiles, 0))

    in_specs = [
        pl.BlockSpec((tm, d), lambda i: (i, 0)),
        pl.BlockSpec((1, MOD_ROWS, d), lambda i: (i // tiles_per_mod, 0, 0)),
        const(gmix), const(win), tab(cosa), tab(sina), tab(cosc), tab(sinc),
        const(sa), const(sc), const(gq), const(gk), const(gcq), const(gckv),
        const(wuq), const(wukvk), const(wukvv), const(gqc), const(gkn), const(gkr),
    ]
    widths = (A_Q, A_KV, A_KV, B_CH, C_HEADS * C_SLOT, C_HEADS * C_SLOT, C_HEADS * C_V)
    dtypes = (BF16, BF16, BF16, F32, BF16, BF16, BF16)
    out_specs = [pl.BlockSpec((tm, w), lambda i: (i, 0)) for w in widths]
    out_shape = [jax.ShapeDtypeStruct((n, w), dt) for w, dt in zip(widths, dtypes)]
    out_specs[2] = pl.BlockSpec((A_KV, tm), lambda i: (0, i))
    out_shape[2] = jax.ShapeDtypeStruct((A_KV, n), BF16)
    out_specs[6] = pl.BlockSpec((C_HEADS * C_V, tm), lambda i: (0, i))
    out_shape[6] = jax.ShapeDtypeStruct((C_HEADS * C_V, n), BF16)
    if kv_only:
        keep = (1, 2, 5, 6)
        out_specs = [out_specs[k] for k in keep]
        out_shape = [out_shape[k] for k in keep]
    return pl.pallas_call(
        functools.partial(_pre_kernel, kv_only=kv_only),
        grid=(n // tm,),
        in_specs=in_specs,
        out_specs=out_specs,
        out_shape=out_shape,
        compiler_params=_cparams(("parallel",)),
        name="pre_attn",
    )(x2d, mod, gmix, win, cosa, sina, cosc, sinc, sa, sc, gq, gk, gcq, gckv, wuq, wukvk, wukvv,
      gqc, gkn, gkr)


CONV_PAD = 16
CONV_CHUNK = 128
SUBLANES = 8


def _conv_chunk(pad_ref, c, w_ref, b_ref, g_ref, beta_ref):
    off = CONV_PAD - B_WIDTH // 2
    nq = (off + B_WIDTH - 1) // SUBLANES + 1
    win = CONV_CHUNK + (nq - 1) * SUBLANES
    base = c * CONV_CHUNK
    acc = jnp.zeros((CONV_CHUNK, B_CH), F32)
    for r in range(SUBLANES):
        taps = [k for k in range(B_WIDTH) if (off + k) % SUBLANES == r]
        if not taps:
            continue
        w_r = pad_ref[base + r:base + r + win, :]
        part = None
        for k in taps:
            q = (off + k) // SUBLANES
            term = w_r[q * SUBLANES:q * SUBLANES + CONV_CHUNK, :] * w_ref[k:k + 1, :]
            part = term if part is None else part + term
        acc = acc + part
    y = acc + b_ref[...]
    mu = jnp.mean(y, axis=-1, keepdims=True)
    yc = y - mu
    var = jnp.mean(yc * yc, axis=-1, keepdims=True)
    z = yc * lax.rsqrt(var + EPS) * g_ref[...] + beta_ref[...]
    return (z * jax.nn.sigmoid(z)).astype(BF16)


ATTN_A_QBLOCKS = 16


ONES_ROWS = 16


def _attn_a_kernel(*refs, t, has_local, qblocks):
    u_ref, uprev_ref, unext_ref, cw_ref, cb_ref, cg_ref, cbeta_ref, o_ref, ob_ref, pad_ref = refs[-10:]
    if has_local:
        q_ref, k_ref, vt_ref, kc_ref, vct_ref, sink_ref, bias_ref = refs[:-10]
    else:
        q_ref, kc_ref, vct_ref, sink_ref = refs[:-10]
    step = pl.program_id(1)
    tq = q_ref.shape[0]
    pad_ref[0:CONV_PAD, :] = jnp.where(step > 0, uprev_ref[0], 0.0)
    pad_ref[CONV_PAD:CONV_PAD + tq, :] = u_ref[0]
    pad_ref[CONV_PAD + tq:CONV_PAD + tq + CONV_PAD, :] = jnp.where(step < pl.num_programs(1) - 1, unext_ref[0], 0.0)
    nchunk = A_Q // LANES
    span = 3 * BLOCK
    lane = lax.broadcasted_iota(jnp.int32, (BLOCK, LANES), 1)
    row = lax.broadcasted_iota(jnp.int32, (A_KV, BLOCK), 0)
    zero = jnp.zeros((BLOCK, LANES), BF16)
    sink = sink_ref[...]
    kc = kc_ref[0]
    vct = vct_ref[...]
    for blk in range(qblocks):
        rows = slice(blk * BLOCK, (blk + 1) * BLOCK)
        chunks = [q_ref[rows, j * LANES:(j + 1) * LANES] for j in range(nchunk)]
        qs = jnp.concatenate([jnp.where(lane < HEAD_DIM, c, zero) for c in chunks]
                             + [jnp.where(lane >= HEAD_DIM, c, zero) for c in chunks], axis=0)
        if has_local:
            n = pl.program_id(1) * qblocks + blk
            start = pl.multiple_of(jnp.clip((n - 1) * BLOCK, 0, t - span), BLOCK)
            keys = jnp.concatenate([k_ref[0, pl.ds(start, span), :], kc], axis=0)
            vt = jnp.concatenate([vt_ref[:, pl.ds(start, span)], vct], axis=1)
        else:
            keys, vt = kc, vct
        s = _dot_nt(keys, qs)
        if has_local:
            s = jnp.concatenate([s[:span] + bias_ref[n - start // BLOCK], s[span:]], axis=0)
        m = jnp.maximum(jnp.max(s, axis=0, keepdims=True), sink)
        e = jnp.exp2(s - m).astype(BF16)
        vte = jnp.concatenate([vt, jnp.ones((ONES_ROWS, vt.shape[1]), BF16)], axis=0)
        acc = _dot(vte, e)
        den = acc[A_KV:A_KV + 1] + jnp.exp2(sink - m)
        out = acc[:A_KV] * (1.0 / den)
        for j in range(nchunk):
            x = jnp.where(row < HEAD_DIM, out[:, j * BLOCK:(j + 1) * BLOCK],
                          out[:, (nchunk + j) * BLOCK:(nchunk + j + 1) * BLOCK])
            o_ref[rows, j * LANES:(j + 1) * LANES] = x.T.astype(BF16)
        ob_ref[rows, :] = _conv_chunk(pad_ref, blk, cw_ref, cb_ref, cg_ref, cbeta_ref)


def _window_bias():
    r = np.arange(BLOCK)[None, :]
    c = np.arange(3 * BLOCK)[:, None]
    pats = [np.where(np.abs(c - p * BLOCK - r) <= A_WINDOW, 0.0, NEG) for p in range(3)]
    return jnp.asarray(np.stack([np.tile(p, (1, A_HEADS)) for p in pats]), F32)


def _attn_a_call(qa, ka, vat, kac, vact, sink_row, u3, conv_p, bsz, t):
    has_local = ka is not None
    qblocks = min(ATTN_A_QBLOCKS, t // BLOCK)
    tq = qblocks * BLOCK
    nq = t // tq
    ctx_len = kac.shape[1]
    in_specs = [pl.BlockSpec((tq, A_Q), lambda b, n: (b * nq + n, 0))]
    args = [qa]
    if has_local:
        in_specs += [pl.BlockSpec((1, t, A_KV), lambda b, n: (b, 0, 0)),
                     pl.BlockSpec((A_KV, t), lambda b, n: (0, b))]
        args += [ka, vat]
    in_specs += [pl.BlockSpec((1, ctx_len, A_KV), lambda b, n: (b, 0, 0)),
                 pl.BlockSpec((A_KV, ctx_len), lambda b, n: (0, b))]
    in_specs += [pl.BlockSpec(sink_row.shape, lambda b, n: (0, 0))]
    args += [kac, vact, sink_row]
    if has_local:
        bias = _window_bias()
        in_specs += [pl.BlockSpec(bias.shape, lambda b, n: (0, 0, 0))]
        args += [bias]
    halo_per_step = tq // CONV_PAD
    n_halo = t // CONV_PAD
    in_specs += [
        pl.BlockSpec((1, tq, B_CH), lambda b, n: (b, n, 0)),
        pl.BlockSpec((1, CONV_PAD, B_CH), lambda b, n: (b, jnp.maximum(n * halo_per_step - 1, 0), 0)),
        pl.BlockSpec((1, CONV_PAD, B_CH), lambda b, n: (b, jnp.minimum((n + 1) * halo_per_step, n_halo - 1), 0)),
    ] + [pl.BlockSpec(a.shape, lambda b, n: (0, 0)) for a in conv_p]
    args += [u3, u3, u3, *conv_p]
    return pl.pallas_call(
        functools.partial(_attn_a_kernel, t=t, has_local=has_local, qblocks=qblocks),
        grid=(bsz, nq),
        in_specs=in_specs,
        out_specs=[pl.BlockSpec((tq, A_Q), lambda b, n: (b * nq + n, 0)),
                   pl.BlockSpec((tq, B_CH), lambda b, n: (b * nq + n, 0))],
        out_shape=[jax.ShapeDtypeStruct((bsz * t, A_Q), BF16), jax.ShapeDtypeStruct((bsz * t, B_CH), BF16)],
        scratch_shapes=[pltpu.VMEM((tq + 2 * CONV_PAD, B_CH), F32)],
        compiler_params=_cparams(("parallel", "parallel")),
        name="attn_ab_local" if has_local else "attn_ab_ctx",
    )(*args)


def _mla_kernel(*refs, has_local):
    if has_local:
        q_ref, kx_ref, vxt_ref, kc_ref, vct_ref, o_ref = refs
    else:
        q_ref, kc_ref, vct_ref, o_ref = refs
    def scores(h):
        sl = slice(h * C_SLOT, (h + 1) * C_SLOT)
        q = q_ref[:, sl]
        s_c = _dot_nt(kc_ref[0, :, sl], q)
        s_x = _dot_nt(kx_ref[0, :, sl], q) if has_local else None
        return s_c, s_x

    outs = []
    nxt = scores(0)
    for h in range(C_HEADS):
        vs = slice(h * C_V, (h + 1) * C_V)
        s_c, s_x = nxt
        if h + 1 < C_HEADS:
            nxt = scores(h + 1)
        m = jnp.max(s_c, axis=0, keepdims=True)
        if has_local:
            m = jnp.maximum(m, jnp.max(s_x, axis=0, keepdims=True))
        e_c = jnp.exp2(s_c - m).astype(BF16)
        vt = jnp.concatenate([vct_ref[vs, :], jnp.ones((ONES_ROWS, e_c.shape[0]), BF16)], axis=0)
        acc = _dot(vt, e_c)
        if has_local:
            e_x = jnp.exp2(s_x - m).astype(BF16)
            vt = jnp.concatenate([vxt_ref[vs, :], jnp.ones((ONES_ROWS, e_x.shape[0]), BF16)], axis=0)
            acc = acc + _dot(vt, e_x)
        outs.append(acc[:C_V] * (1.0 / acc[C_V:C_V + 1]))
    o_ref[...] = jnp.concatenate(outs, axis=0).T.astype(BF16)


def _mla_call(qc, kx, vxt, kcc, vcct, bsz, t, tq):
    has_local = kx is not None
    nq = t // tq
    ctx_len = kcc.shape[1]
    wq = C_HEADS * C_SLOT
    wv = C_HEADS * C_V
    in_specs = [pl.BlockSpec((tq, wq), lambda b, n: (b * nq + n, 0))]
    args = [qc]
    if has_local:
        in_specs += [pl.BlockSpec((1, t, wq), lambda b, n: (b, 0, 0)),
                     pl.BlockSpec((wv, t), lambda b, n: (0, b))]
        args += [kx, vxt]
    in_specs += [pl.BlockSpec((1, ctx_len, wq), lambda b, n: (b, 0, 0)),
                 pl.BlockSpec((wv, ctx_len), lambda b, n: (0, b))]
    args += [kcc, vcct]
    return pl.pallas_call(
        functools.partial(_mla_kernel, has_local=has_local),
        grid=(bsz, nq),
        in_specs=in_specs,
        out_specs=pl.BlockSpec((tq, wv), lambda b, n: (b * nq + n, 0)),
        out_shape=jax.ShapeDtypeStruct((bsz * t, wv), BF16),
        compiler_params=_cparams(("parallel", "parallel")),
        name="mla_local" if has_local else "mla_ctx",
    )(*args)


def _mixer_residual(x_ref, mod_ref, oa_ref, ob_ref, oc_ref, wo_ref):
    y = _dot(oa_ref[...], wo_ref[0:A_Q, :])
    y = y + _dot(ob_ref[...], wo_ref[A_Q:A_Q + B_CH, :])
    y = y + _dot(oc_ref[...], wo_ref[A_Q + B_CH:, :])
    return x_ref[...] + mod_ref[0, 2:3, :] * y


def _mixer_specs(tm, d, tiles_per_mod, oa, ob, oc, w_out):
    return [
        pl.BlockSpec((tm, d), lambda i: (i, 0)),
        pl.BlockSpec((1, MOD_ROWS, d), lambda i: (i // tiles_per_mod, 0, 0)),
        pl.BlockSpec((tm, oa.shape[1]), lambda i: (i, 0)),
        pl.BlockSpec((tm, ob.shape[1]), lambda i: (i, 0)),
        pl.BlockSpec((tm, oc.shape[1]), lambda i: (i, 0)),
        pl.BlockSpec(w_out.shape, lambda i: (0, 0), pipeline_mode=pl.Buffered(1)),
    ]


def _split_bf16(a):
    hi = a.astype(BF16)
    lo = (a - hi.astype(F32)).astype(BF16)
    return hi, lo


def _ffn_input(x, mod_ref, g_ref):
    ms = jnp.mean(x * x, axis=-1, keepdims=True)
    y = x * lax.rsqrt(ms + EPS)
    return (y * g_ref[...]) * (1.0 + mod_ref[0, 4:5, :]) + mod_ref[0, 3:4, :]


def _ff_chunks(ff):
    tiles = ff // MXU_TILE
    if ff % MXU_TILE or tiles < 3:
        return (ff,)
    third = (tiles // 3) * MXU_TILE
    return (third, third, ff - 2 * third)


def _swiglu(h, w1_ref, w3_ref, w2_ref):
    y = None
    o = 0
    for tf in _ff_chunks(w1_ref.shape[1]):
        a = _dot(h, w1_ref[:, o:o + tf])
        b = _dot(h, w3_ref[:, o:o + tf])
        g = (a * jax.nn.sigmoid(a) * b).astype(BF16)
        yc = _dot(g, w2_ref[o:o + tf, :])
        y = yc if y is None else y + yc
        o += tf
    return y


def _ffn_kernel(x_ref, mod_ref, oa_ref, ob_ref, oc_ref, wo_ref, g_ref, w1_ref, w3_ref, w2_ref, o_ref):
    x = _mixer_residual(x_ref, mod_ref, oa_ref, ob_ref, oc_ref, wo_ref)
    h = _ffn_input(x, mod_ref, g_ref).astype(BF16)
    o_ref[...] = x + mod_ref[0, 5:6, :] * _swiglu(h, w1_ref, w3_ref, w2_ref)


def _ffn_call(x2d, mod, tiles_per_mod, oa, ob, oc, w_out, g, w1, w3, w2, tm):
    n, d = x2d.shape

    def resident(a):
        return pl.BlockSpec(a.shape, lambda i: (0, 0), pipeline_mode=pl.Buffered(1))

    return pl.pallas_call(
        _ffn_kernel,
        grid=(n // tm,),
        in_specs=_mixer_specs(tm, d, tiles_per_mod, oa, ob, oc, w_out) + [
            pl.BlockSpec(g.shape, lambda i: (0, 0)), resident(w1), resident(w3), resident(w2)],
        out_specs=pl.BlockSpec((tm, d), lambda i: (i, 0)),
        out_shape=jax.ShapeDtypeStruct((n, d), F32),
        compiler_params=_cparams(("parallel",)),
        name="ffn_dense",
    )(x2d, mod, oa, ob, oc, w_out, g, w1, w3, w2)


MOE_TM = 512
SEG_ALIGN = 16
SEG_PIECES = (512, 256, 128, 64, 32, 16)
CBUF_ROWS = 2 * MOE_TM + N_EXPERTS * SEG_ALIGN
META_I1, META_I2, META_G1, META_G2, META_R1, META_R2 = range(6)
META_ROWS = 8


def _router_kernel(x_ref, mod_ref, oa_ref, ob_ref, oc_ref, wo_ref, g_ref, wr_ref, br_ref, utri_ref,
                   x1_ref, h_ref, meta_ref, metat_ref, cnt_ref):
    x = _mixer_residual(x_ref, mod_ref, oa_ref, ob_ref, oc_ref, wo_ref)
    x1_ref[...] = x
    h = _ffn_input(x, mod_ref, g_ref)
    h_ref[...] = h.astype(BF16)
    h_hi, h_lo = _split_bf16(h)
    w_hi, w_lo = _split_bf16(wr_ref[...])
    logits = _dot(h_hi, w_hi) + (_dot(h_lo, w_hi) + _dot(h_hi, w_lo)) + br_ref[...]
    lt = logits.T[0:N_EXPERTS, :]
    row = lax.broadcasted_iota(jnp.int32, lt.shape, 0).astype(F32)
    m1 = jnp.max(lt, axis=0, keepdims=True)
    i1 = jnp.min(jnp.where(lt == m1, row, float(N_EXPERTS)), axis=0, keepdims=True)
    rest = jnp.where(row == i1, NEG, lt)
    m2 = jnp.max(rest, axis=0, keepdims=True)
    i2 = jnp.min(jnp.where(rest == m2, row, float(N_EXPERTS)), axis=0, keepdims=True)
    e2 = jnp.exp(m2 - m1)
    den = 1.0 + e2
    sel1 = jnp.where(row == i1, 1.0, 0.0)
    sel2 = jnp.where(row == i2, 1.0, 0.0)
    sel = sel1 + sel2
    before = _dot(sel.astype(BF16), utri_ref[...])
    r1 = jnp.sum(before * sel1, axis=0, keepdims=True)
    r2 = jnp.sum(before * sel2, axis=0, keepdims=True)
    cnt_ref[0] = jnp.broadcast_to(jnp.sum(sel, axis=1, keepdims=True), (N_EXPERTS, LANES))
    zero = jnp.zeros_like(m1)
    metat = jnp.concatenate([i1, i2, 1.0 / den, e2 / den, r1, r2, zero, zero], axis=0)
    metat_ref[...] = metat
    pad = jnp.zeros((LANES - META_ROWS, metat.shape[1]), F32)
    meta_ref[...] = jnp.concatenate([metat, pad], axis=0).T


def _router_call(x2d, mod, tiles_per_mod, oa, ob, oc, w_out, g, wr, br, utri):
    n, d = x2d.shape
    tm = MOE_TM
    nt = n // tm
    return pl.pallas_call(
        _router_kernel,
        grid=(nt,),
        in_specs=_mixer_specs(tm, d, tiles_per_mod, oa, ob, oc, w_out) + [
            pl.BlockSpec(g.shape, lambda i: (0, 0)),
            pl.BlockSpec(wr.shape, lambda i: (0, 0)),
            pl.BlockSpec(br.shape, lambda i: (0, 0)),
            pl.BlockSpec(utri.shape, lambda i: (0, 0)),
        ],
        out_specs=[
            pl.BlockSpec((tm, d), lambda i: (i, 0)),
            pl.BlockSpec((tm, d), lambda i: (i, 0)),
            pl.BlockSpec((tm, LANES), lambda i: (i, 0)),
            pl.BlockSpec((META_ROWS, tm), lambda i: (0, i)),
            pl.BlockSpec((1, N_EXPERTS, LANES), lambda i: (i, 0, 0)),
        ],
        out_shape=[
            jax.ShapeDtypeStruct((n, d), F32),
            jax.ShapeDtypeStruct((n, d), BF16),
            jax.ShapeDtypeStruct((n, LANES), F32),
            jax.ShapeDtypeStruct((META_ROWS, n), F32),
            jax.ShapeDtypeStruct((nt, N_EXPERTS, LANES), F32),
        ],
        compiler_params=_cparams(("parallel",)),
        name="moe_router",
    )(x2d, mod, oa, ob, oc, w_out, g, wr, br, utri)


def _pair_slots(meta, segoff_row):
    lane = lax.broadcasted_iota(jnp.int32, meta.shape, 1).astype(F32)
    i1 = meta[:, META_I1:META_I1 + 1]
    i2 = meta[:, META_I2:META_I2 + 1]
    s1 = jnp.sum(jnp.where(lane == i1, segoff_row, 0.0), axis=-1, keepdims=True) + meta[:, META_R1:META_R1 + 1]
    s2 = jnp.sum(jnp.where(lane == i2, segoff_row, 0.0), axis=-1, keepdims=True) + meta[:, META_R2:META_R2 + 1]
    return s1, s2


def _segment_copies(src, dst, src_off, dst_off, length, sem):
    out = []
    for size in SEG_PIECES:
        done = (length // (2 * size)) * (2 * size)
        s = pl.multiple_of(src_off + done, SEG_ALIGN)
        t = pl.multiple_of(dst_off + done, SEG_ALIGN)
        cp = pltpu.make_async_copy(src.at[pl.ds(s, size)], dst.at[pl.ds(t, size)], sem)
        out.append(((length & size) != 0, cp))
    return out


def _start_copies(copies):
    for k, (pred, cp) in enumerate(copies):
        @pl.when(pred)
        def _(cp=cp, k=k):
            cp.start(priority=k % 2)


def _wait_copies(copies):
    for pred, cp in copies:
        @pl.when(pred)
        def _(cp=cp):
            cp.wait()


def _compact_kernel(segoff_s, base_s, len_s, fill_s, h_ref, meta_ref, metat_ref, xs_out, gs_out,
                    cbuf2, gbuf2, zx, zg, sems):
    i = pl.program_id(0)
    last = pl.num_programs(0) - 1
    slot_i = i % 2
    cbuf = cbuf2.at[slot_i]
    gbuf = gbuf2.at[slot_i]

    def copies_of(tile, slot):
        out = []
        for e in range(N_EXPERTS):
            k = tile * N_EXPERTS + e
            out += _segment_copies(cbuf2.at[slot], xs_out, segoff_s[k], base_s[k], len_s[k], sems.at[0, slot])
            out += _segment_copies(gbuf2.at[slot], gs_out, segoff_s[k], base_s[k], len_s[k], sems.at[1, slot])
        return out

    @pl.when(i >= 2)
    def _():
        _wait_copies(copies_of(i - 2, slot_i))

    mt = metat_ref[...]
    s1 = mt[META_R1:META_R1 + 1]
    s2 = mt[META_R2:META_R2 + 1]
    for e in range(N_EXPERTS):
        off = segoff_s[i * N_EXPERTS + e].astype(F32)
        s1 = s1 + jnp.where(mt[META_I1:META_I1 + 1] == e, off, 0.0)
        s2 = s2 + jnp.where(mt[META_I2:META_I2 + 1] == e, off, 0.0)
    row = lax.broadcasted_iota(jnp.int32, (CBUF_ROWS, MOE_TM), 0).astype(F32)
    p1 = jnp.where(row == s1, 1.0, 0.0)
    p2 = jnp.where(row == s2, 1.0, 0.0)
    cbuf[...] = _dot((p1 + p2).astype(BF16), h_ref[...]).astype(BF16)
    meta = meta_ref[...]
    lane = lax.broadcasted_iota(jnp.int32, meta.shape, 1)
    g1 = meta[:, META_G1:META_G1 + 1]
    g2 = meta[:, META_G2:META_G2 + 1]
    g1_hi = g1.astype(BF16).astype(F32)
    g2_hi = g2.astype(BF16).astype(F32)
    a1 = jnp.where(lane == 0, g1_hi, jnp.where(lane == 1, g1 - g1_hi, 0.0)).astype(BF16)
    a2 = jnp.where(lane == 0, g2_hi, jnp.where(lane == 1, g2 - g2_hi, 0.0)).astype(BF16)
    gbuf[...] = _dot(p1.astype(BF16), a1) + _dot(p2.astype(BF16), a2)
    _start_copies(copies_of(i, slot_i))

    @pl.when(i == last)
    def _():
        @pl.when(i >= 1)
        def _():
            _wait_copies(copies_of(i - 1, 1 - slot_i))
        _wait_copies(copies_of(i, slot_i))
        zx[...] = jnp.zeros_like(zx)
        zg[...] = jnp.zeros_like(zg)
        fills = []
        for e in range(N_EXPERTS):
            fills += _segment_copies(zx, xs_out, 0, fill_s[e], fill_s[N_EXPERTS + e], sems.at[0, 0])
            fills += _segment_copies(zg, gs_out, 0, fill_s[e], fill_s[N_EXPERTS + e], sems.at[1, 0])
        _start_copies(fills)
        _wait_copies(fills)

        def fill_copies(j):
            r = pl.multiple_of(fill_s[2 * N_EXPERTS] + j * MOE_TM, MOE_TM)
            return (pltpu.make_async_copy(zx, xs_out.at[pl.ds(r, MOE_TM)], sems.at[0, 0]),
                    pltpu.make_async_copy(zg, gs_out.at[pl.ds(r, MOE_TM)], sems.at[1, 0]))

        def start_tile(j, carry):
            for cp in fill_copies(j):
                cp.start()
            return carry

        def wait_tile(j, carry):
            for cp in fill_copies(j):
                cp.wait()
            return carry

        lax.fori_loop(0, fill_s[2 * N_EXPERTS + 1], start_tile, 0)
        lax.fori_loop(0, fill_s[2 * N_EXPERTS + 1], wait_tile, 0)


def _compact_call(sched, fill, h, meta, metat, rows):
    n, d = h.shape
    nt = n // MOE_TM
    grid_spec = pltpu.PrefetchScalarGridSpec(
        num_scalar_prefetch=4,
        grid=(nt,),
        in_specs=[
            pl.BlockSpec((MOE_TM, d), lambda i, *_: (i, 0)),
            pl.BlockSpec((MOE_TM, LANES), lambda i, *_: (i, 0)),
            pl.BlockSpec((META_ROWS, MOE_TM), lambda i, *_: (0, i)),
        ],
        out_specs=[pl.BlockSpec(memory_space=pl.ANY), pl.BlockSpec(memory_space=pl.ANY)],
        scratch_shapes=[pltpu.VMEM((2, CBUF_ROWS, d), BF16), pltpu.VMEM((2, CBUF_ROWS, LANES), F32),
                        pltpu.VMEM((MOE_TM, d), BF16), pltpu.VMEM((MOE_TM, LANES), F32),
                        pltpu.SemaphoreType.DMA((2, 2))],
    )
    return pl.pallas_call(
        _compact_kernel,
        grid_spec=grid_spec,
        out_shape=[jax.ShapeDtypeStruct((rows, d), BF16), jax.ShapeDtypeStruct((rows, LANES), F32)],
        compiler_params=_cparams(("arbitrary",)),
        name="moe_compact",
    )(*sched, fill, h, meta, metat)


def _expert_kernel(exp_s, blk_s, valid_s, xs_ref, gs_ref, w1_ref, w3_ref, w2_ref, y_ref):
    del exp_s, blk_s
    j = pl.program_id(0)

    @pl.when(valid_s[j] != 0)
    def _():
        gate = gs_ref[:, 0:1] + gs_ref[:, 1:2]
        y_ref[...] = (_swiglu(xs_ref[...], w1_ref, w3_ref, w2_ref) * gate).astype(BF16)

    @pl.when(valid_s[j] == 0)
    def _():
        y_ref[...] = jnp.zeros_like(y_ref)


def _expert_call(tile_sched, xs, gs, w1, w3, w2):
    rows, d = xs.shape
    ff = w1.shape[1]
    grid_spec = pltpu.PrefetchScalarGridSpec(
        num_scalar_prefetch=3,
        grid=(rows // MOE_TM,),
        in_specs=[
            pl.BlockSpec((MOE_TM, d), lambda j, e_s, b_s, v_s: (b_s[j], 0)),
            pl.BlockSpec((MOE_TM, LANES), lambda j, e_s, b_s, v_s: (b_s[j], 0)),
            pl.BlockSpec((d, ff), lambda j, e_s, b_s, v_s: (e_s[j], 0)),
            pl.BlockSpec((d, ff), lambda j, e_s, b_s, v_s: (e_s[j], 0)),
            pl.BlockSpec((ff, d), lambda j, e_s, b_s, v_s: (e_s[j], 0)),
        ],
        out_specs=pl.BlockSpec((MOE_TM, d), lambda j, e_s, b_s, v_s: (j, 0)),
    )
    return pl.pallas_call(
        _expert_kernel,
        grid_spec=grid_spec,
        out_shape=jax.ShapeDtypeStruct((rows, d), BF16),
        compiler_params=_cparams(("arbitrary",)),
        name="moe_experts",
    )(*tile_sched, xs, gs, w1, w3, w2)


def _combine_kernel(segoff_s, base_s, len_s, x_ref, mod_ref, meta_ref, segoff_ref, y_hbm, o_ref, ybuf2, sems):
    i = pl.program_id(0)
    last = pl.num_programs(0) - 1
    slot_i = i % 2

    def copies_of(tile, slot):
        out = []
        for e in range(N_EXPERTS):
            k = tile * N_EXPERTS + e
            out += _segment_copies(y_hbm, ybuf2.at[slot], base_s[k], segoff_s[k], len_s[k], sems.at[slot])
        return out

    @pl.when(i == 0)
    def _():
        ybuf2[...] = jnp.zeros_like(ybuf2)
        _start_copies(copies_of(i, slot_i))

    @pl.when(i < last)
    def _():
        _start_copies(copies_of(i + 1, 1 - slot_i))

    _wait_copies(copies_of(i, slot_i))
    s1, s2 = _pair_slots(meta_ref[...], segoff_ref[0])
    slot = lax.broadcasted_iota(jnp.int32, (MOE_TM, CBUF_ROWS), 1).astype(F32)
    pick = (jnp.where(slot == s1, 1.0, 0.0) + jnp.where(slot == s2, 1.0, 0.0)).astype(BF16)
    o_ref[...] = x_ref[...] + mod_ref[0, 5:6, :] * _dot(pick, ybuf2[slot_i])


def _combine_call(sched, x2d, mod, tiles_per_mod, meta, segoff_v, y):
    n, d = x2d.shape
    grid_spec = pltpu.PrefetchScalarGridSpec(
        num_scalar_prefetch=3,
        grid=(n // MOE_TM,),
        in_specs=[
            pl.BlockSpec((MOE_TM, d), lambda i, *_: (i, 0)),
            pl.BlockSpec((1, MOD_ROWS, d), lambda i, *_: (i // tiles_per_mod, 0, 0)),
            pl.BlockSpec((MOE_TM, LANES), lambda i, *_: (i, 0)),
            pl.BlockSpec((1, 1, LANES), lambda i, *_: (i, 0, 0)),
            pl.BlockSpec(memory_space=pl.ANY),
        ],
        out_specs=pl.BlockSpec((MOE_TM, d), lambda i, *_: (i, 0)),
        scratch_shapes=[pltpu.VMEM((2, CBUF_ROWS, d), BF16), pltpu.SemaphoreType.DMA((2,))],
    )
    return pl.pallas_call(
        _combine_kernel,
        grid_spec=grid_spec,
        out_shape=jax.ShapeDtypeStruct((n, d), F32),
        compiler_params=_cparams(("arbitrary",)),
        name="moe_combine",
    )(*sched, x2d, mod, meta, segoff_v, y)


def _moe_call(x2d, mod, tiles_per_mod, outs, w_out, g, wr, br, w1, w3, w2):
    n, d = x2d.shape
    nt = n // MOE_TM
    utri = jnp.asarray(np.triu(np.ones((MOE_TM, MOE_TM), np.float32), 1), BF16)
    x2d, h, meta, metat, counts = _router_call(x2d, mod, tiles_per_mod, *outs, w_out, g, wr, br, utri)

    cnt = counts[:, :, 0].astype(jnp.int32)
    seg_len = (cnt + SEG_ALIGN - 1) // SEG_ALIGN * SEG_ALIGN
    segoff = jnp.cumsum(seg_len, axis=1) - seg_len
    region = (jnp.sum(seg_len, axis=0) + MOE_TM - 1) // MOE_TM * MOE_TM
    region_start = jnp.cumsum(region) - region
    base = region_start[None, :] + jnp.cumsum(seg_len, axis=0) - seg_len
    rows_max = 2 * n + nt * N_EXPERTS * (SEG_ALIGN - 1) + N_EXPERTS * (MOE_TM - 1)
    n_sorted_tiles = (rows_max + MOE_TM - 1) // MOE_TM
    tile_end = jnp.cumsum(region // MOE_TM)
    total_tiles = tile_end[-1]
    jt = jnp.arange(n_sorted_tiles, dtype=jnp.int32)
    blk = jnp.minimum(jt, total_tiles - 1)
    tile_expert = jnp.sum((blk[:, None] >= tile_end[None, :]).astype(jnp.int32), axis=1)
    tile_sched = (tile_expert.astype(jnp.int32), blk.astype(jnp.int32), (jt < total_tiles).astype(jnp.int32))
    sched = tuple(a.reshape(-1).astype(jnp.int32) for a in (segoff, base, seg_len))
    segoff_v = jnp.pad(segoff.astype(F32), ((0, 0), (0, LANES - N_EXPERTS)))[:, None, :]

    total = jnp.sum(seg_len, axis=0)
    fill = jnp.concatenate([region_start + total, region - total,
                            jnp.stack([total_tiles * MOE_TM, n_sorted_tiles - total_tiles])]).astype(jnp.int32)
    xs, gs = _compact_call(sched, fill, h, meta, metat, n_sorted_tiles * MOE_TM)
    y = _expert_call(tile_sched, xs, gs, w1, w3, w2)
    return _combine_call(sched, x2d, mod, tiles_per_mod, meta, segoff_v, y)


def _rope_tables(t):
    rows = jnp.arange(t, dtype=F32) // GRID_W
    cols = jnp.arange(t, dtype=F32) % GRID_W

    def tables(rot_dim):
        a = rot_dim // 2
        inv = 1.0 / (ROPE_BASE ** (jnp.arange(0, a, 2, dtype=F32) / a))
        ar = rows[:, None] * inv
        ac = cols[:, None] * inv
        cos = jnp.concatenate([jnp.cos(ar), jnp.cos(ar), jnp.cos(ac), jnp.cos(ac)], axis=-1)
        sin = jnp.concatenate([-jnp.sin(ar), jnp.sin(ar), -jnp.sin(ac), jnp.sin(ac)], axis=-1)
        return cos, sin

    ca, sa = tables(HEAD_DIM)
    cos_a = jnp.tile(ca, (1, LANES // HEAD_DIM))
    sin_a = jnp.tile(sa, (1, LANES // HEAD_DIM))
    cc, sc = tables(C_ROPE)
    ones = jnp.ones((t, C_NOPE), F32)
    tail = C_SLOT - C_NOPE - C_ROPE
    cos_c = jnp.concatenate([ones, cc, jnp.ones((t, tail), F32)], axis=-1)
    sin_c = jnp.concatenate([0 * ones, sc, jnp.zeros((t, tail), F32)], axis=-1)
    return cos_a, sin_a, cos_c, sin_c


def _head_perm():
    order = []
    for j in range(A_HEADS // 2):
        order += [j, A_HEADS // 2 + j]
    return np.concatenate([np.arange(h * HEAD_DIM, (h + 1) * HEAD_DIM) for h in order])


def _segment_mean_matrix(widths, total):
    m = np.zeros((total, total), np.float32)
    o = 0
    while o < total:
        for w, used in widths:
            if used:
                m[o:o + w, o:o + w] = 1.0 / w
            o += w
    return jnp.asarray(m, BF16)


def _slot_vec(nope, rope):
    z = jnp.zeros((C_SLOT - C_NOPE - C_ROPE,), F32)
    n = jnp.zeros((C_NOPE,), F32) if nope is None else nope
    r = jnp.zeros((C_ROPE,), F32) if rope is None else rope
    return jnp.tile(jnp.concatenate([n, r, z]), C_HEADS)[None, :]


def _layer_consts(i, p, perm, tabs):
    w_in = p["w_in"][i]
    o_kr = A_Q + 2 * A_KV + 2 * B_CH + C_Q_RANK + C_KV_RANK
    d = w_in.shape[0]
    kr_cols = jnp.concatenate([jnp.zeros((d, C_NOPE), F32), w_in[:, o_kr:o_kr + C_ROPE],
                               jnp.zeros((d, C_SLOT - C_NOPE - C_ROPE), F32)], axis=1)
    win = jnp.concatenate([w_in[:, :A_Q][:, perm], w_in[:, A_Q:o_kr], kr_cols], axis=1).astype(BF16)

    w_uq = p["c_w_uq"][i].reshape(C_Q_RANK, C_HEADS, C_NOPE + C_ROPE)
    wuq = jnp.pad(w_uq, ((0, 0), (0, 0), (0, C_SLOT - C_NOPE - C_ROPE))).reshape(C_Q_RANK, C_HEADS * C_SLOT)
    w_ukv = p["c_w_ukv"][i].reshape(C_KV_RANK, C_HEADS, C_NOPE + C_V)
    wukvk = jnp.pad(w_ukv[:, :, :C_NOPE], ((0, 0), (0, 0), (0, C_SLOT - C_NOPE))).reshape(C_KV_RANK, -1)
    wukvv = w_ukv[:, :, C_NOPE:].reshape(C_KV_RANK, C_HEADS * C_V)

    sa = _segment_mean_matrix([(HEAD_DIM, True)], MXU_TILE)
    sc = _segment_mean_matrix([(C_NOPE, True), (C_ROPE, True), (C_SLOT - C_NOPE - C_ROPE, False)], MXU_TILE)
    gq = jnp.tile(p["a_q_norm_g"][i] * (A_SCALE * LOG2E), A_HEADS)[None, :]
    gk = jnp.tile(p["a_k_norm_g"][i], A_KV_HEADS)[None, :]
    gqc = _slot_vec(p["c_q_nope_norm_g"][i], p["c_q_rope_norm_g"][i]) * (MLA_SCALE * LOG2E)
    gkn = _slot_vec(p["c_k_nope_norm_g"][i], None)
    gkr = _slot_vec(None, p["c_k_rope_norm_g"][i])[:, :C_SLOT]
    return (p["mix_norm_g"][i][None, :], win) + tabs + (
        sa, sc, gq, gk, p["c_q_rank_norm_g"][i][None, :], p["c_kv_rank_norm_g"][i][None, :],
        wuq.astype(BF16), wukvk.astype(BF16), wukvv.astype(BF16), gqc, gkn, gkr)


def kernel(x, c, ctx, c_ctx, ada_w, ada_b, mix_norm_g, ffn_norm_g, w_in, w_out, a_q_norm_g, a_k_norm_g, a_sink, b_conv_w, b_conv_b, b_ln_g, b_ln_b, c_q_rank_norm_g, c_kv_rank_norm_g, c_w_uq, c_w_ukv, c_q_nope_norm_g, c_k_nope_norm_g, c_q_rope_norm_g, c_k_rope_norm_g, dense_w1, dense_w3, dense_w2, moe_router_w, moe_router_b, moe_w1, moe_w3, moe_w2):
    p = dict(w_in=w_in, c_w_uq=c_w_uq, c_w_ukv=c_w_ukv, a_q_norm_g=a_q_norm_g, a_k_norm_g=a_k_norm_g,
             c_q_nope_norm_g=c_q_nope_norm_g, c_k_nope_norm_g=c_k_nope_norm_g,
             c_q_rope_norm_g=c_q_rope_norm_g, c_k_rope_norm_g=c_k_rope_norm_g,
             c_q_rank_norm_g=c_q_rank_norm_g, c_kv_rank_norm_g=c_kv_rank_norm_g, mix_norm_g=mix_norm_g)
    bsz, t, d = x.shape
    ctx_len = ctx.shape[1]
    depth = ada_w.shape[0]
    n_x, n_c = bsz * t, bsz * ctx_len
    tm_pre = min(4 * PRE_SUB, t, n_c)
    tm_tok = 512
    tq = min(2048, t)
    assert t % tm_pre == 0 and n_c % tm_pre == 0 and t % tm_tok == 0 and n_c % tm_tok == 0 and t % tq == 0
    assert t % GRID_W == 0 and t >= 3 * BLOCK and ctx_len % BLOCK == 0

    ada_rows = ((bsz + 1 + 7) // 8) * 8
    c_pad = jnp.concatenate([c, c_ctx[None, :], jnp.zeros((ada_rows - bsz - 1, d), F32)], axis=0)
    mods = _ada_call(c_pad, ada_w, ada_b).reshape(depth, ada_rows, 6, d)
    mods = jnp.pad(mods, ((0, 0), (0, 0), (0, MOD_ROWS - 6), (0, 0)))

    tabs_x = _rope_tables(t)
    ones = jnp.ones((tm_pre, LANES), F32)
    tabs_c = (ones, 0 * ones, ones, 0 * ones)
    perm = _head_perm()

    x2 = x.reshape(n_x, d)
    c2 = ctx.reshape(n_c, d)
    for i in range(depth):
        last = i == depth - 1
        mod_x = mods[i, :bsz]
        mod_c = mods[i, bsz:bsz + 1]
        consts_x = _layer_consts(i, p, perm, tabs_x)
        consts_c = _layer_consts(i, p, perm, tabs_c)

        qa_x, ka_x, va_x, u_x, qc_x, kc_x, vc_x = _pre_call(x2, mod_x, t // tm_pre, t // tm_pre, consts_x, tm_pre)
        if last:
            ka_c, va_c, kc_c, vc_c = _pre_call(c2, mod_c, n_c // tm_pre, 1, consts_c, tm_pre, kv_only=True)
        else:
            qa_c, ka_c, va_c, u_c, qc_c, kc_c, vc_c = _pre_call(c2, mod_c, n_c // tm_pre, 1, consts_c, tm_pre)

        def r3(a, length):
            return a.reshape(bsz, length, a.shape[-1])

        sink_row = jnp.repeat(a_sink[i] * LOG2E, BLOCK)[None, :]
        conv_w = jnp.pad(b_conv_w[i], ((0, -B_WIDTH % SUBLANES), (0, 0)))
        conv_p = (conv_w, b_conv_b[i][None, :], b_ln_g[i][None, :], b_ln_b[i][None, :])
        w_o = jnp.concatenate([w_out[i][:A_Q][perm], w_out[i][A_Q:]], axis=0).astype(BF16)

        o_a, o_b = _attn_a_call(qa_x, r3(ka_x, t), va_x, r3(ka_c, ctx_len), va_c, sink_row, r3(u_x, t), conv_p, bsz, t)
        o_c = _mla_call(qc_x, r3(kc_x, t), vc_x, r3(kc_c, ctx_len), vc_c, bsz, t, tq)
        if not last:
            oc_a, oc_b = _attn_a_call(qa_c, None, None, r3(ka_c, ctx_len), va_c, sink_row, r3(u_c, ctx_len), conv_p,
                                      bsz, ctx_len)
            oc_c = _mla_call(qc_c, None, None, r3(kc_c, ctx_len), vc_c, bsz, ctx_len, ctx_len)

        j = i // 2
        g_ffn = ffn_norm_g[i][None, :]
        if i % 2 == 0:
            w = (dense_w1[j].astype(BF16), dense_w3[j].astype(BF16), dense_w2[j].astype(BF16))

            def mix(a2, mod, tiles_per_mod, outs, w=w, g_ffn=g_ffn, w_o=w_o):
                return _ffn_call(a2, mod, tiles_per_mod, *outs, w_o, g_ffn, *w, tm_tok)
        else:
            w = tuple(a.astype(BF16).reshape(-1, a.shape[-1]) for a in (moe_w1[j], moe_w3[j], moe_w2[j]))
            wr = jnp.pad(moe_router_w[j], ((0, 0), (0, LANES - N_EXPERTS)))
            br = jnp.pad(moe_router_b[j], (0, LANES - N_EXPERTS))[None, :]

            def mix(a2, mod, tiles_per_mod, outs, w=w, g_ffn=g_ffn, wr=wr, br=br, w_o=w_o):
                return _moe_call(a2, mod, tiles_per_mod * (tm_tok // MOE_TM), outs, w_o, g_ffn, wr, br, *w)
        x2 = mix(x2, mod_x, t // tm_tok, (o_a, o_b, o_c))
        if not last:
            c2 = mix(c2, mod_c, n_c // tm_tok, (oc_a, oc_b, oc_c))
    return x2.reshape(bsz, t, d)
```
